```python
import jax, jax.numpy as jnp
from jax import lax
import numpy as np

D_MODEL = 1024
BATCH = 8
SEQ = 2048
DEPTH = 1

GLA_HEADS = 4
GLA_DK = 64
GLA_DV = 128
GLA_LOWRANK = 16
GLA_GATE_NORM = 16.0
GLA_CHUNK = 64
SWA_HEADS = 8
SWA_KV_HEADS = 2
SWA_HD = 64
WINDOW = 128
ROPE_THETA = 500000.0
ROPE_DIMS = SWA_HD // 4
N_GROUPS = 4
EXPERTS_PER_GROUP = 8
TOP_K_IN_GROUP = 2
D_EXPERT = 256

EPS = 1e-6
GLA_WIDTH = GLA_HEADS * GLA_DV
SWA_WIDTH = SWA_HEADS * SWA_HD
MIX_WIDTH = GLA_WIDTH + SWA_WIDTH
IN_SIZES = (GLA_HEADS * GLA_DK,
            GLA_HEADS * GLA_DK,
            GLA_WIDTH,
            GLA_WIDTH,
            GLA_LOWRANK,
            SWA_WIDTH,
            SWA_KV_HEADS * SWA_HD,
            SWA_KV_HEADS * SWA_HD)
IN_WIDTH = sum(IN_SIZES)

kernel_name = "hymba_gla_swa_sink_hier_moe_adaln"


def rmsnorm(x, g):
    x32 = x.astype(jnp.float32)
    y = x32 * lax.rsqrt(jnp.mean(x32 * x32, axis=-1, keepdims=True) + EPS)
    return y.astype(x.dtype) * g


def rope_tables(positions):
    inv_freq = ROPE_THETA ** (-jnp.arange(0, ROPE_DIMS, 2, dtype=jnp.float32) / ROPE_DIMS)
    ang = positions.astype(jnp.float32)[..., None] * inv_freq
    return jnp.cos(ang)[:, :, None, :], jnp.sin(ang)[:, :, None, :]


def partial_rope(x, cos, sin):
    half = ROPE_DIMS // 2
    x32 = x.astype(jnp.float32)
    x1, x2, rest = x32[..., :half], x32[..., half:ROPE_DIMS], x32[..., ROPE_DIMS:]
    out = jnp.concatenate([x1 * cos - x2 * sin, x2 * cos + x1 * sin, rest], axis=-1)
    return out.astype(x.dtype)


def gla_mixer(q, k, v, g_out, a_low, w_a2, b_a2, gn_gain):
    B, S = q.shape[:2]
    nc, C, H, dk, dv = S // GLA_CHUNK, GLA_CHUNK, GLA_HEADS, GLA_DK, GLA_DV
    f32 = jnp.float32
    log_a = jax.nn.log_sigmoid((a_low @ w_a2 + b_a2).astype(f32)) / GLA_GATE_NORM
    log_a = log_a.reshape(B, nc, C, H, dk)
    qc = q.astype(f32).reshape(B, nc, C, H, dk) * (dk ** -0.5)
    kc = k.astype(f32).reshape(B, nc, C, H, dk)
    vc = v.astype(f32).reshape(B, nc, C, H, dv)
    b = jnp.cumsum(log_a, axis=2)
    b_last = b[:, :, -1:]
    q_dec = qc * jnp.exp(b)
    k_dec = kc * jnp.exp(-b)
    causal = jnp.tril(jnp.ones((C, C), dtype=bool))
    scores = jnp.einsum('bnthd,bnshd->bnhts', q_dec, k_dec)
    scores = jnp.where(causal, scores, 0.0)
    o_intra = jnp.einsum('bnhts,bnshe->bnthe', scores, vc)
    kv = jnp.einsum('bnshd,bnshe->bnhde', kc * jnp.exp(b_last - b), vc)
    decay = jnp.exp(b_last[:, :, 0])

    def step(state, inp):
        kv_n, dec_n = inp
        return dec_n[..., None] * state + kv_n, state

    s0 = jnp.zeros((B, H, dk, dv), f32)
    _, s_prev = lax.scan(step, s0, (jnp.moveaxis(kv, 1, 0), jnp.moveaxis(decay, 1, 0)))
    s_prev = jnp.moveaxis(s_prev, 0, 1)
    o_inter = jnp.einsum('bnthd,bnhde->bnthe', q_dec, s_prev)
    o = (o_intra + o_inter).reshape(B, S, H, dv)
    o = rmsnorm(o, gn_gain)
    o = o * jax.nn.silu(g_out.astype(f32).reshape(B, S, H, dv))
    return o.reshape(B, S, H * dv).astype(q.dtype)


def swa_sink_mixer(q, k, v, sinks, cos, sin):
    B, S = q.shape[:2]
    W, hd, Hkv = WINDOW, SWA_HD, SWA_KV_HEADS
    G = SWA_HEADS // Hkv
    nb = S // W
    q = partial_rope(q.reshape(B, S, SWA_HEADS, hd), cos, sin)
    k = partial_rope(k.reshape(B, S, Hkv, hd), cos, sin)
    qb = q.reshape(B, nb, W, Hkv, G, hd) * (hd ** -0.5)
    kb = k.reshape(B, nb, W, Hkv, hd)
    vb = v.reshape(B, nb, W, Hkv, hd)
    pad = ((0, 0), (1, 0), (0, 0), (0, 0), (0, 0))
    kk = jnp.concatenate([jnp.pad(kb, pad)[:, :-1], kb], axis=2)
    vv = jnp.concatenate([jnp.pad(vb, pad)[:, :-1], vb], axis=2)
    scores = jnp.einsum('bnqkgd,bnskd->bnkgqs', qb, kk).astype(jnp.float32)
    q_pos = jnp.arange(W)[:, None] + W
    k_pos = jnp.arange(2 * W)[None, :]
    rel = q_pos - k_pos
    band = (rel >= 0) & (rel < W)
    blk = jnp.arange(nb)[:, None, None]
    valid = band[None] & ((blk > 0) | (k_pos[None] >= W))
    scores = jnp.where(valid[None, :, None, None], scores, -jnp.inf)
    sink = jnp.broadcast_to(sinks.astype(jnp.float32).reshape(Hkv, G)[None, None, :, :, None, None],
                            scores.shape[:-1] + (1,))
    probs = jax.nn.softmax(jnp.concatenate([scores, sink], axis=-1), axis=-1)[..., :-1]
    out = jnp.einsum('bnkgqs,bnskd->bnqkgd', probs.astype(vv.dtype), vv)
    return out.reshape(B, S, SWA_HEADS * hd)


def hier_moe(h, w_grp, b_grp, w_exp, b_exp, w_gate, w_up, w_down):
    T = h.shape[0]
    grp_logits = (h @ w_grp + b_grp).astype(jnp.float32)
    grp_prob = jax.nn.softmax(grp_logits, axis=-1)
    grp_idx = jnp.argmax(grp_logits, axis=-1)
    grp_gate = jnp.take_along_axis(grp_prob, grp_idx[:, None], axis=1)[:, 0]
    exp_logits = (h @ w_exp + b_exp).astype(jnp.float32).reshape(T, N_GROUPS, EXPERTS_PER_GROUP)
    sel_logits = jnp.take_along_axis(exp_logits, grp_idx[:, None, None], axis=1)[:, 0]
    top_vals, top_idx = lax.top_k(sel_logits, TOP_K_IN_GROUP)
    top_w = jax.nn.softmax(top_vals, axis=-1) * grp_gate[:, None]
    within = jnp.sum(jax.nn.one_hot(top_idx, EXPERTS_PER_GROUP, dtype=jnp.float32) * top_w[..., None], axis=1)
    combine = (jax.nn.one_hot(grp_idx, N_GROUPS, dtype=jnp.float32)[:, :, None] * within[:, None, :]).astype(h.dtype)
    out = jnp.zeros_like(h)
    for gi in range(N_GROUPS):
        a = jnp.einsum('td,edf->tef', h, w_gate[gi])
        u = jnp.einsum('td,edf->tef', h, w_up[gi])
        act = jax.nn.silu(a) * u * combine[:, gi, :, None]
        out = out + jnp.einsum('tef,efd->td', act, w_down[gi])
    return out


def setup_inputs(seed: int = 0) -> dict:
    key = jax.random.key(seed)
    ks = jax.random.split(key, 24)
    D, L = D_MODEL, DEPTH
    nrm = lambda k, shape, s: jax.random.normal(k, shape, jnp.float32) * s
    return {
        "x": nrm(ks[0], (BATCH, SEQ, D), 1.0),
        "c": nrm(ks[1], (BATCH, D), 1.0),
        "positions": (jnp.arange(SEQ, dtype=jnp.int32)[None, :]
                      + jax.random.randint(ks[2], (BATCH, 1), 0, 1024, dtype=jnp.int32)),
        "ada_w": nrm(ks[3], (L, D, 6 * D), D ** -0.5),
        "ada_b": nrm(ks[4], (L, 6 * D), 0.02),
        "norm_mix_g": 1.0 + nrm(ks[5], (L, D), 0.02),
        "w_in": nrm(ks[6], (L, D, IN_WIDTH), D ** -0.5),
        "gla_w_a2": nrm(ks[7], (L, GLA_LOWRANK, GLA_HEADS * GLA_DK), GLA_LOWRANK ** -0.5),
        "gla_b_a2": nrm(ks[8], (L, GLA_HEADS * GLA_DK), 0.02),
        "gla_norm_g": 1.0 + nrm(ks[9], (L, GLA_HEADS, GLA_DV), 0.02),
        "swa_sinks": nrm(ks[10], (L, SWA_HEADS), 0.5),
        "w_out": nrm(ks[11], (L, MIX_WIDTH, D), MIX_WIDTH ** -0.5),
        "norm_ffn_g": 1.0 + nrm(ks[12], (L, D), 0.02),
        "w_grp": nrm(ks[13], (L, D, N_GROUPS), D ** -0.5),
        "b_grp": nrm(ks[14], (L, N_GROUPS), 0.01),
        "w_exp": nrm(ks[15], (L, D, N_GROUPS * EXPERTS_PER_GROUP), D ** -0.5),
        "b_exp": nrm(ks[16], (L, N_GROUPS * EXPERTS_PER_GROUP), 0.01),
        "w_gate": nrm(ks[17], (L, N_GROUPS, EXPERTS_PER_GROUP, D, D_EXPERT), D ** -0.5),
        "w_up": nrm(ks[18], (L, N_GROUPS, EXPERTS_PER_GROUP, D, D_EXPERT), D ** -0.5),
        "w_down": nrm(ks[19], (L, N_GROUPS, EXPERTS_PER_GROUP, D_EXPERT, D), D_EXPERT ** -0.5),
        "final_norm_g": 1.0 + nrm(ks[20], (D,), 0.02),
    }


def reference(x, c, positions, ada_w, ada_b, norm_mix_g, w_in, gla_w_a2, gla_b_a2, gla_norm_g,
              swa_sinks, w_out, norm_ffn_g, w_grp, b_grp, w_exp, b_exp, w_gate, w_up, w_down,
              final_norm_g):
    B, S, D = x.shape
    cos, sin = rope_tables(positions)
    c_act = jax.nn.silu(c)
    split_pts = [int(p) for p in np.cumsum(IN_SIZES)[:-1]]
    for l in range(DEPTH):
        mod = (c_act @ ada_w[l] + ada_b[l])[:, None, :]
        sh_a, sc_a, gt_a, sh_f, sc_f, gt_f = jnp.split(mod, 6, axis=-1)
        h = rmsnorm(x, norm_mix_g[l]) * (1.0 + sc_a) + sh_a
        proj = h @ w_in[l]
        g_q, g_k, g_v, g_g, g_a, s_q, s_k, s_v = jnp.split(proj, split_pts, axis=-1)
        o_gla = gla_mixer(g_q, g_k, g_v, g_g, g_a, gla_w_a2[l], gla_b_a2[l], gla_norm_g[l])
        o_swa = swa_sink_mixer(s_q, s_k, s_v, swa_sinks[l], cos, sin)
        mix = jnp.concatenate([o_gla, o_swa], axis=-1) @ w_out[l]
        x = x + gt_a * mix
        h = rmsnorm(x, norm_ffn_g[l]) * (1.0 + sc_f) + sh_f
        y = hier_moe(h.reshape(B * S, D), w_grp[l], b_grp[l], w_exp[l], b_exp[l],
                     w_gate[l], w_up[l], w_down[l]).reshape(B, S, D)
        x = x + gt_f * y
    return rmsnorm(x, final_norm_g)
```

```python
import functools

import numpy as np
import jax
import jax.numpy as jnp
from jax import lax
from jax.experimental import pallas as pl
from jax.experimental.pallas import tpu as pltpu

f32 = jnp.float32
bf16 = jnp.bfloat16

GLA_HEADS = 4
GLA_DK = 64
GLA_DV = 128
GLA_LOWRANK = 16
GLA_GATE_NORM = 16.0
GLA_CHUNK = 64
SWA_HEADS = 8
SWA_KV_HEADS = 2
SWA_HD = 64
WINDOW = 128
ROPE_THETA = 500000.0
ROPE_DIMS = SWA_HD // 4
N_GROUPS = 4
EXPERTS_PER_GROUP = 8
D_EXPERT = 256
EPS = 1e-6

LANES = 128
VMEM_LIMIT = 52 * 1024 * 1024

GLA_QK = GLA_HEADS * GLA_DK
GLA_W = GLA_HEADS * GLA_DV
SWA_W = SWA_HEADS * SWA_HD
SWA_KV = SWA_KV_HEADS * SWA_HD
SWA_HEAD_ORDER = (0, 4, 1, 5, 2, 6, 3, 7)
ROUTE_ROWS = 8
ROUTER_ROWS = 128

TN = (((0,), (0,)), ((), ()))
NT = (((1,), (1,)), ((), ()))


def _cparams(n_axes):
    return pltpu.CompilerParams(dimension_semantics=("arbitrary",) * n_axes, vmem_limit_bytes=VMEM_LIMIT)


def _split_bf16(v):
    hi = v.astype(bf16)
    lo = (v - hi.astype(f32)).astype(bf16)
    return hi, lo


def _split_stack_bf16(v):
    hi = v.astype(bf16).astype(f32)
    return jnp.concatenate([hi, v - hi], axis=0).astype(bf16)


def _rmsnorm_rows(v):
    return v * lax.rsqrt(jnp.mean(v * v, axis=-1, keepdims=True) + EPS)


def _silu(v):
    return v * jax.nn.sigmoid(v)


def _adaln_kernel(c_ref, w_ref, b_ref, o_ref):
    ca = _silu(c_ref[...])
    o_ref[...] = jnp.dot(ca, w_ref[...], precision=lax.Precision.HIGHEST, preferred_element_type=f32) + b_ref[...]


def _adaln(c, w, b):
    bsz, d = c.shape
    n = w.shape[1]
    return pl.pallas_call(
        _adaln_kernel,
        grid=(n // d,),
        in_specs=[pl.BlockSpec((bsz, d), lambda j: (0, 0)),
                  pl.BlockSpec((d, d), lambda j: (0, j)),
                  pl.BlockSpec((1, d), lambda j: (0, j))],
        out_specs=pl.BlockSpec((bsz, d), lambda j: (0, j)),
        out_shape=jax.ShapeDtypeStruct((bsz, n), f32),
        compiler_params=_cparams(1),
        name="adaln",
    )(c, w, b.reshape(1, n))


def _rope_tab_kernel(pos_ref, invf_ref, o_ref):
    half = ROPE_DIMS // 2
    for b in range(pos_ref.shape[0]):
        ang = pos_ref[b:b + 1, :].astype(f32) * invf_ref[...]
        o_ref[b, 0:half, :] = jnp.cos(ang)
        o_ref[b, half:2 * half, :] = jnp.sin(ang)


def _rope_tab(positions):
    bsz, s = positions.shape
    half = ROPE_DIMS // 2
    inv_freq = (np.float32(ROPE_THETA) ** (-np.arange(0, ROPE_DIMS, 2, dtype=np.float32) / np.float32(ROPE_DIMS)))
    invf = jnp.asarray(inv_freq.astype(np.float32).reshape(half, 1))
    return pl.pallas_call(
        _rope_tab_kernel,
        out_shape=jax.ShapeDtypeStruct((bsz, 2 * half, s), f32),
        name="rope_tab",
    )(positions, invf)


def _rope_expand_matrix():
    half = ROPE_DIMS // 2
    e = np.zeros((2 * half, 3 * LANES), np.float32)
    for j in range(LANES):
        jj = j % SWA_HD
        if jj < half:
            e[jj, j] = 1.0
            e[half + jj, LANES + j] = -1.0
        elif jj < 2 * half:
            e[jj - half, j] = 1.0
            e[half + jj - half, 2 * LANES + j] = 1.0
    return jnp.asarray(np.concatenate([e, e], axis=0), dtype=bf16)


IN_QK0, IN_VG0, IN_SQ0, IN_SKV0, IN_A0, IN_END = 0, 512, 1536, 2048, 2304, 2432


def _inproj_kernel(x_ref, mod_ref, g_ref, w_ref, tab_ref, e_ref, wa_hi_ref, wa_lo_ref, ba_ref,
                   qk_ref, vg_ref, la_ref, sq_ref, skv_ref):
    x = x_ref[...]
    h = (_rmsnorm_rows(x) * g_ref[...]) * (1.0 + mod_ref[1:2, :]) + mod_ref[0:1, :]
    hb = h.astype(bf16)

    def proj(lo, hi):
        return jnp.dot(hb, w_ref[:, lo:hi], preferred_element_type=f32)

    qk = proj(IN_QK0, IN_VG0)
    lane = lax.broadcasted_iota(jnp.int32, qk.shape, 1)
    qk_ref[...] = jnp.where(lane < GLA_QK, qk * (GLA_DK ** -0.5), qk).astype(bf16)
    vg_ref[...] = proj(IN_VG0, IN_SQ0).astype(bf16)

    al_hi, al_lo = _split_bf16(proj(IN_A0, IN_END))
    z = (jnp.dot(al_hi, wa_hi_ref[...], preferred_element_type=f32)
         + jnp.dot(al_lo, wa_hi_ref[...], preferred_element_type=f32)
         + jnp.dot(al_hi, wa_lo_ref[...], preferred_element_type=f32)) + ba_ref[...]
    la_ref[...] = (jnp.minimum(z, 0.0) - jnp.log1p(jnp.exp(-jnp.abs(z)))) * (1.0 / GLA_GATE_NORM)

    tabs = lax.dot_general(_split_stack_bf16(tab_ref[...]), e_ref[...], TN, preferred_element_type=f32)
    lane1 = lax.broadcasted_iota(jnp.int32, (1, LANES), 1)
    cos_t = tabs[:, 0:LANES] + jnp.where((lane1 & (SWA_HD - 1)) < ROPE_DIMS, 0.0, 1.0)
    sa_t = tabs[:, LANES:2 * LANES]
    sb_t = tabs[:, 2 * LANES:3 * LANES]

    def rope(v):
        return (v * cos_t + pltpu.roll(v, LANES - ROPE_DIMS // 2, 1) * sa_t
                + pltpu.roll(v, ROPE_DIMS // 2, 1) * sb_t)

    sq = proj(IN_SQ0, IN_SKV0)
    for p in range(SWA_W // LANES):
        sq_ref[:, p * LANES:(p + 1) * LANES] = (rope(sq[:, p * LANES:(p + 1) * LANES]) * (SWA_HD ** -0.5)).astype(bf16)
    skv = proj(IN_SKV0, IN_A0)
    skv_ref[:, 0:LANES] = rope(skv[:, 0:LANES]).astype(bf16)
    skv_ref[:, LANES:2 * LANES] = skv[:, LANES:2 * LANES].astype(bf16)


def _inproj(x2, mod3, g, w, tab, e_mat, wa_hi, wa_lo, ba, seq, tm):
    t, d = x2.shape
    per_b = seq // tm
    row = lambda i: (i, 0)
    const = lambda i: (0, 0)
    return pl.pallas_call(
        _inproj_kernel,
        grid=(t // tm,),
        in_specs=[pl.BlockSpec((tm, d), row),
                  pl.BlockSpec((None, 6, d), lambda i: (i // per_b, 0, 0)),
                  pl.BlockSpec((1, d), const),
                  pl.BlockSpec(w.shape, const),
                  pl.BlockSpec((None, ROPE_DIMS, tm), lambda i: (i // per_b, 0, i % per_b)),
                  pl.BlockSpec(e_mat.shape, const),
                  pl.BlockSpec(wa_hi.shape, const),
                  pl.BlockSpec(wa_lo.shape, const),
                  pl.BlockSpec(ba.shape, const)],
        out_specs=[pl.BlockSpec((tm, 2 * GLA_QK), row),
                   pl.BlockSpec((tm, 2 * GLA_W), row),
                   pl.BlockSpec((tm, GLA_QK), row),
                   pl.BlockSpec((tm, SWA_W), row),
                   pl.BlockSpec((tm, 2 * SWA_KV), row)],
        out_shape=[jax.ShapeDtypeStruct((t, 2 * GLA_QK), bf16),
                   jax.ShapeDtypeStruct((t, 2 * GLA_W), bf16),
                   jax.ShapeDtypeStruct((t, GLA_QK), f32),
                   jax.ShapeDtypeStruct((t, SWA_W), bf16),
                   jax.ShapeDtypeStruct((t, 2 * SWA_KV), bf16)],
        compiler_params=_cparams(1),
        name="inproj",
    )(x2, mod3, g, w, tab, e_mat, wa_hi, wa_lo, ba)


def _gla_kernel(qk_ref, vg_ref, la_ref, gn_ref, o_ref, st_ref, *, chunks):
    c_len = GLA_CHUNK

    @pl.when(pl.program_id(1) == 0)
    def _():
        st_ref[...] = jnp.zeros_like(st_ref)

    r_i = lax.broadcasted_iota(jnp.int32, (c_len, c_len), 0)
    c_i = lax.broadcasted_iota(jnp.int32, (c_len, c_len), 1)
    causal = r_i >= c_i
    tri = jnp.where(causal, 1.0, 0.0).astype(bf16)
    ones = jnp.ones((c_len, LANES), bf16)
    lane = lax.broadcasted_iota(jnp.int32, (c_len, LANES), 1)
    first = lane < GLA_DK

    for c in range(chunks):
        rows = slice(c * c_len, (c + 1) * c_len)
        la_hi, la_lo = _split_bf16(la_ref[rows, :])
        b = jnp.dot(tri, la_hi, preferred_element_type=f32) + jnp.dot(tri, la_lo, preferred_element_type=f32)
        b_tot_t = (lax.dot_general(la_hi, ones, TN, preferred_element_type=f32)
                   + lax.dot_general(la_lo, ones, TN, preferred_element_type=f32))
        b_last = b[c_len - 1:c_len, :]
        q = qk_ref[rows, 0:GLA_QK].astype(f32)
        k = qk_ref[rows, GLA_QK:2 * GLA_QK].astype(f32)
        q_dec = q * jnp.exp(b)
        k_dec = (k * jnp.exp(-b)).astype(bf16)
        k_rem = (k * jnp.exp(b_last - b)).astype(bf16)
        decay = jnp.exp(b_tot_t)

        for p in range(GLA_HEADS // 2):
            ls = slice(p * LANES, (p + 1) * LANES)
            ss = slice(p * LANES, (p + 1) * LANES)
            s_prev = st_ref[ss, :]
            s_prev_b = s_prev.astype(bf16)
            qd = q_dec[:, ls]
            kv_halves = []
            for hh in range(2):
                h = 2 * p + hh
                vs = slice(h * GLA_DV, (h + 1) * GLA_DV)
                qm = jnp.where(first if hh == 0 else ~first, qd, 0.0).astype(bf16)
                v = vg_ref[rows, vs]
                scores = lax.dot_general(qm, k_dec[:, ls], NT, preferred_element_type=f32)
                scores = jnp.where(causal, scores, 0.0).astype(bf16)
                o = (jnp.dot(scores, v, preferred_element_type=f32)
                     + jnp.dot(qm, s_prev_b, preferred_element_type=f32))
                o = _rmsnorm_rows(o) * gn_ref[:, vs]
                gate = vg_ref[rows, GLA_W + h * GLA_DV:GLA_W + (h + 1) * GLA_DV].astype(f32)
                o_ref[rows, vs] = (o * _silu(gate)).astype(bf16)
                kv = lax.dot_general(k_rem[:, ls], v, TN, preferred_element_type=f32)
                kv_halves.append(kv[hh * GLA_DK:(hh + 1) * GLA_DK, :])
            st_ref[ss, :] = decay[ss, :] * s_prev + jnp.concatenate(kv_halves, axis=0)


def _gla(qk, vg, la, gn, bsz, seq, tc):
    t = qk.shape[0]
    nt = seq // tc
    row = lambda b, j: (b * nt + j, 0)
    return pl.pallas_call(
        functools.partial(_gla_kernel, chunks=tc // GLA_CHUNK),
        grid=(bsz, nt),
        in_specs=[pl.BlockSpec((tc, 2 * GLA_QK), row),
                  pl.BlockSpec((tc, 2 * GLA_W), row),
                  pl.BlockSpec((tc, GLA_QK), row),
                  pl.BlockSpec((1, GLA_W), lambda b, j: (0, 0))],
        out_specs=pl.BlockSpec((tc, GLA_W), row),
        out_shape=jax.ShapeDtypeStruct((t, GLA_W), bf16),
        scratch_shapes=[pltpu.VMEM((GLA_HEADS * GLA_DK, GLA_DV), f32)],
        compiler_params=_cparams(2),
        name="gla",
    )(qk, vg, la, gn)


def _swa_kernel(sink_ref, q_ref, kvp_ref, kvc_ref, o_ref):
    w = WINDOW
    j = pl.program_id(1)
    k2 = jnp.concatenate([kvp_ref[:, 0:LANES], kvc_ref[:, 0:LANES]], axis=0)
    v2 = jnp.concatenate([kvp_ref[:, LANES:2 * LANES], kvc_ref[:, LANES:2 * LANES]], axis=0)
    t_i = lax.broadcasted_iota(jnp.int32, (w, 2 * w), 0)
    s_i = lax.broadcasted_iota(jnp.int32, (w, 2 * w), 1)
    rel = t_i + w - s_i
    valid = (rel >= 0) & (rel < w) & ((j > 0) | (s_i >= w))
    lane = lax.broadcasted_iota(jnp.int32, (w, LANES), 1)
    first = lane < SWA_HD
    for p in range(SWA_W // LANES):
        qp = q_ref[:, p * LANES:(p + 1) * LANES].astype(f32)
        halves = []
        for hh in range(2):
            head = SWA_HEAD_ORDER[2 * p + hh]
            qm = jnp.where(first if hh == 0 else ~first, qp, 0.0).astype(bf16)
            s = lax.dot_general(qm, k2, NT, preferred_element_type=f32)
            s = jnp.where(valid, s, -jnp.inf)
            sink = sink_ref[0, head]
            m = jnp.maximum(jnp.max(s, axis=-1, keepdims=True), sink)
            pr = jnp.exp(s - m)
            denom = jnp.sum(pr, axis=-1, keepdims=True) + jnp.exp(sink - m)
            o = jnp.dot(pr.astype(bf16), v2, preferred_element_type=f32)
            halves.append(o / denom)
        o_ref[:, p * LANES:(p + 1) * LANES] = jnp.where(first, halves[0], halves[1]).astype(bf16)


def _swa(sinks, sq, skv, bsz, seq):
    t = sq.shape[0]
    nb = seq // WINDOW
    return pl.pallas_call(
        _swa_kernel,
        grid=(bsz, nb),
        in_specs=[pl.BlockSpec(memory_space=pltpu.SMEM),
                  pl.BlockSpec((WINDOW, SWA_W), lambda b, j: (b * nb + j, 0)),
                  pl.BlockSpec((WINDOW, 2 * SWA_KV), lambda b, j: (b * nb + jnp.maximum(j - 1, 0), 0)),
                  pl.BlockSpec((WINDOW, 2 * SWA_KV), lambda b, j: (b * nb + j, 0))],
        out_specs=pl.BlockSpec((WINDOW, SWA_W), lambda b, j: (b * nb + j, 0)),
        out_shape=jax.ShapeDtypeStruct((t, SWA_W), bf16),
        compiler_params=_cparams(2),
        name="swa",
    )(sinks, sq, skv, skv)


def _outproj_kernel(og_ref, os_ref, x_ref, mod_ref, wo_ref, g2_ref, wr_hi_ref, wr_lo_ref, br_ref,
                    x1_ref, h2_ref, route_ref):
    mix = (jnp.dot(og_ref[...], wo_ref[0:GLA_W, :], preferred_element_type=f32)
           + jnp.dot(os_ref[...], wo_ref[GLA_W:GLA_W + SWA_W, :], preferred_element_type=f32))
    x1 = x_ref[...] + mod_ref[2:3, :] * mix
    x1_ref[...] = x1
    h = (_rmsnorm_rows(x1) * g2_ref[...]) * (1.0 + mod_ref[4:5, :]) + mod_ref[3:4, :]
    h_hi, h_lo = _split_bf16(h)
    h2_ref[...] = h_hi

    lt = (lax.dot_general(wr_hi_ref[...], h_hi, NT, preferred_element_type=f32)
          + lax.dot_general(wr_hi_ref[...], h_lo, NT, preferred_element_type=f32)
          + lax.dot_general(wr_lo_ref[...], h_hi, NT, preferred_element_type=f32)) + br_ref[...]
    tm = lt.shape[1]
    e_g = EXPERTS_PER_GROUP
    row = lax.broadcasted_iota(jnp.int32, (e_g, tm), 0)
    neg = -jnp.inf
    gl = jnp.where(row < N_GROUPS, lt[0:e_g, :], neg)
    g_max = jnp.max(gl, axis=0, keepdims=True)
    g_gate = 1.0 / jnp.sum(jnp.exp(gl - g_max), axis=0, keepdims=True)
    g_idx = jnp.min(jnp.where(gl == g_max, row, e_g), axis=0, keepdims=True)
    sel = lt[e_g * N_GROUPS:e_g * (N_GROUPS + 1), :]
    for g in range(N_GROUPS - 2, -1, -1):
        sel = jnp.where(g_idx == g, lt[e_g * (g + 1):e_g * (g + 2), :], sel)
    t1 = jnp.max(sel, axis=0, keepdims=True)
    i1 = jnp.min(jnp.where(sel == t1, row, e_g), axis=0, keepdims=True)
    sel2 = jnp.where(row == i1, neg, sel)
    t2 = jnp.max(sel2, axis=0, keepdims=True)
    i2 = jnp.min(jnp.where(sel2 == t2, row, e_g), axis=0, keepdims=True)
    ex = jnp.exp(t2 - t1)
    w1 = g_gate / (1.0 + ex)
    w2 = g_gate * ex / (1.0 + ex)
    e1 = (g_idx * e_g + i1).astype(f32)
    e2 = (g_idx * e_g + i2).astype(f32)
    route_ref[...] = jnp.where(row == 0, e1, jnp.where(row == 1, e2, jnp.where(row == 2, w1,
                                                                                 jnp.where(row == 3, w2, 0.0))))


def _outproj(og, osw, x2, mod3, wo, g2, wr_hi, wr_lo, br, seq, tm):
    t, d = x2.shape
    per_b = seq // tm
    row = lambda i: (i, 0)
    const = lambda i: (0, 0)
    return pl.pallas_call(
        _outproj_kernel,
        grid=(t // tm,),
        in_specs=[pl.BlockSpec((tm, GLA_W), row),
                  pl.BlockSpec((tm, SWA_W), row),
                  pl.BlockSpec((tm, d), row),
                  pl.BlockSpec((None, 6, d), lambda i: (i // per_b, 0, 0)),
                  pl.BlockSpec(wo.shape, const),
                  pl.BlockSpec((1, d), const),
                  pl.BlockSpec(wr_hi.shape, const),
                  pl.BlockSpec(wr_lo.shape, const),
                  pl.BlockSpec(br.shape, const)],
        out_specs=[pl.BlockSpec((tm, d), row),
                   pl.BlockSpec((tm, d), row),
                   pl.BlockSpec((ROUTE_ROWS, tm), lambda i: (0, i))],
        out_shape=[jax.ShapeDtypeStruct((t, d), f32),
                   jax.ShapeDtypeStruct((t, d), bf16),
                   jax.ShapeDtypeStruct((ROUTE_ROWS, t), f32)],
        compiler_params=_cparams(1),
        name="outproj",
    )(og, osw, x2, mod3, wo, g2, wr_hi, wr_lo, br)


def _moe_kernel(h_ref, route_ref, x1_ref, mod_ref, wg_ref, wu_ref, wd_ref, fg_ref, ex_ref, o_ref, acc_ref):
    g = pl.program_id(1)

    @pl.when(g == 0)
    def _():
        acc_ref[...] = jnp.zeros_like(acc_ref)

    h = h_ref[...]
    tm = h.shape[0]
    e_g = EXPERTS_PER_GROUP
    eid = (lax.broadcasted_iota(jnp.int32, (e_g, tm), 0) + g * e_g).astype(f32)
    comb = (jnp.where(eid == route_ref[0:1, :], route_ref[2:3, :], 0.0)
            + jnp.where(eid == route_ref[1:2, :], route_ref[3:4, :], 0.0))
    c_st = _split_stack_bf16(comb)
    acc = acc_ref[...]
    for e in range(e_g):
        scale = lax.dot_general(c_st, ex_ref[e], TN, preferred_element_type=f32)
        a = jnp.dot(h, wg_ref[e], preferred_element_type=f32)
        u = jnp.dot(h, wu_ref[e], preferred_element_type=f32)
        act = (_silu(a) * u * scale).astype(bf16)
        acc = acc + jnp.dot(act, wd_ref[e], preferred_element_type=f32)
    acc_ref[...] = acc

    @pl.when(g == N_GROUPS - 1)
    def _():
        x2 = x1_ref[...] + mod_ref[5:6, :] * acc_ref[...]
        o_ref[...] = _rmsnorm_rows(x2) * fg_ref[...]


def _moe(h2, route, x1, mod3, wg, wu, wd, fg, seq, tm):
    t, d = x1.shape
    per_b = seq // tm
    e_g = EXPERTS_PER_GROUP
    ex = np.zeros((e_g, 2 * e_g, D_EXPERT), np.float32)
    for e in range(e_g):
        ex[e, e, :] = 1.0
        ex[e, e_g + e, :] = 1.0
    ex = jnp.asarray(ex, dtype=bf16)
    row = lambda i, g: (i, 0)
    return pl.pallas_call(
        _moe_kernel,
        grid=(t // tm, N_GROUPS),
        in_specs=[pl.BlockSpec((tm, d), row),
                  pl.BlockSpec((ROUTE_ROWS, tm), lambda i, g: (0, i)),
                  pl.BlockSpec((tm, d), row),
                  pl.BlockSpec((None, 6, d), lambda i, g: (i // per_b, 0, 0)),
                  pl.BlockSpec((None, e_g, d, D_EXPERT), lambda i, g: (g, 0, 0, 0)),
                  pl.BlockSpec((None, e_g, d, D_EXPERT), lambda i, g: (g, 0, 0, 0)),
                  pl.BlockSpec((None, e_g, D_EXPERT, d), lambda i, g: (g, 0, 0, 0)),
                  pl.BlockSpec((1, d), lambda i, g: (0, 0)),
                  pl.BlockSpec(ex.shape, lambda i, g: (0, 0, 0))],
        out_specs=pl.BlockSpec((tm, d), row),
        out_shape=jax.ShapeDtypeStruct((t, d), f32),
        scratch_shapes=[pltpu.VMEM((tm, d), f32)],
        compiler_params=_cparams(2),
        name="moe",
    )(h2, route, x1, mod3, wg, wu, wd, fg, ex)


def _prep_w_in(w_in_l):
    sizes = (GLA_QK, GLA_QK, GLA_W, GLA_W, GLA_LOWRANK, SWA_W, SWA_KV, SWA_KV)
    off = np.concatenate([[0], np.cumsum(sizes)])
    gq, gk, gv, gg, ga, sq, sk, sv = [np.arange(off[i], off[i + 1]) for i in range(len(sizes))]
    sq_perm = np.concatenate([sq[h * SWA_HD:(h + 1) * SWA_HD] for h in SWA_HEAD_ORDER])
    cols = np.concatenate([gq, gk, gv, gg, sq_perm, sk, sv, ga])
    w = jnp.take(w_in_l, jnp.asarray(cols), axis=1)
    w = jnp.pad(w, ((0, 0), (0, IN_END - w.shape[1])))
    return w.astype(bf16)


def _prep_w_out(w_out_l):
    rows = np.concatenate([np.arange(GLA_W)]
                          + [GLA_W + np.arange(h * SWA_HD, (h + 1) * SWA_HD) for h in SWA_HEAD_ORDER])
    return jnp.take(w_out_l, jnp.asarray(rows), axis=0).astype(bf16)


def _prep_router(w_grp_l, b_grp_l, w_exp_l, b_exp_l):
    d = w_grp_l.shape[0]
    n_e = N_GROUPS * EXPERTS_PER_GROUP
    wr = jnp.zeros((ROUTER_ROWS, d), f32)
    wr = wr.at[0:N_GROUPS, :].set(w_grp_l.T).at[EXPERTS_PER_GROUP:EXPERTS_PER_GROUP + n_e, :].set(w_exp_l.T)
    br = jnp.zeros((ROUTER_ROWS, 1), f32)
    br = br.at[0:N_GROUPS, 0].set(b_grp_l).at[EXPERTS_PER_GROUP:EXPERTS_PER_GROUP + n_e, 0].set(b_exp_l)
    wr_hi = wr.astype(bf16)
    wr_lo = (wr - wr_hi.astype(f32)).astype(bf16)
    return wr_hi, wr_lo, br


def _layer(x2, mod3, tab, e_mat, bsz, seq, norm_mix_g, w_in, gla_w_a2, gla_b_a2, gla_norm_g, swa_sinks, w_out,
           norm_ffn_g, w_grp, b_grp, w_exp, b_exp, w_gate, w_up, w_down, out_g, tm, tc, tm_moe):
    d = x2.shape[1]
    wa = jnp.pad(gla_w_a2, ((0, LANES - GLA_LOWRANK), (0, 0)))
    wa_hi = wa.astype(bf16)
    wa_lo = (wa - wa_hi.astype(f32)).astype(bf16)
    qk, vg, la, sq, skv = _inproj(x2, mod3, norm_mix_g.reshape(1, d), _prep_w_in(w_in), tab, e_mat,
                                  wa_hi, wa_lo, gla_b_a2.reshape(1, GLA_QK), seq, tm)
    o_gla = _gla(qk, vg, la, gla_norm_g.reshape(1, GLA_W), bsz, seq, tc)
    o_swa = _swa(swa_sinks.reshape(1, SWA_HEADS), sq, skv, bsz, seq)
    wr_hi, wr_lo, br = _prep_router(w_grp, b_grp, w_exp, b_exp)
    x1, h2, route = _outproj(o_gla, o_swa, x2, mod3, _prep_w_out(w_out), norm_ffn_g.reshape(1, d),
                             wr_hi, wr_lo, br, seq, tm)
    return _moe(h2, route, x1, mod3, w_gate.astype(bf16), w_up.astype(bf16), w_down.astype(bf16),
                out_g.reshape(1, d), seq, tm_moe)


def kernel(x, c, positions, ada_w, ada_b, norm_mix_g, w_in, gla_w_a2, gla_b_a2, gla_norm_g, swa_sinks, w_out,
           norm_ffn_g, w_grp, b_grp, w_exp, b_exp, w_gate, w_up, w_down, final_norm_g):
    bsz, seq, d = x.shape
    depth = ada_w.shape[0]
    assert depth == 1, "the final rmsnorm is fused into the last layer's MoE kernel"
    tab = _rope_tab(positions)
    e_mat = _rope_expand_matrix()
    x2 = x.reshape(bsz * seq, d)
    for l in range(depth):
        mod3 = _adaln(c, ada_w[l], ada_b[l]).reshape(bsz, 6, d)
        x2 = _layer(x2, mod3, tab, e_mat, bsz, seq, norm_mix_g[l], w_in[l], gla_w_a2[l], gla_b_a2[l],
                    gla_norm_g[l], swa_sinks[l], w_out[l], norm_ffn_g[l], w_grp[l], b_grp[l], w_exp[l], b_exp[l],
                    w_gate[l], w_up[l], w_down[l], final_norm_g, tm=512, tc=256, tm_moe=512)
    return x2.reshape(bsz, seq, d)
```

```python
import functools

import numpy as np
import jax
import jax.numpy as jnp
from jax import lax
from jax.experimental import pallas as pl
from jax.experimental.pallas import tpu as pltpu

f32 = jnp.float32
bf16 = jnp.bfloat16

GLA_HEADS = 4
GLA_DK = 64
GLA_DV = 128
GLA_LOWRANK = 16
GLA_GATE_NORM = 16.0
GLA_CHUNK = 64
SWA_HEADS = 8
SWA_KV_HEADS = 2
SWA_HD = 64
WINDOW = 128
ROPE_THETA = 500000.0
ROPE_DIMS = SWA_HD // 4
N_GROUPS = 4
EXPERTS_PER_GROUP = 8
D_EXPERT = 256
EPS = 1e-6

LANES = 128
VMEM_LIMIT = 52 * 1024 * 1024

GLA_QK = GLA_HEADS * GLA_DK
GLA_W = GLA_HEADS * GLA_DV
SWA_W = SWA_HEADS * SWA_HD
SWA_KV = SWA_KV_HEADS * SWA_HD
SWA_HEAD_ORDER = (0, 4, 1, 5, 2, 6, 3, 7)
ROUTE_ROWS = 8
ROUTER_ROWS = 128
EXPERT_BITS = 3
N_BUCKETS = N_GROUPS << (2 * EXPERT_BITS)
N_PAIRS = N_GROUPS * EXPERTS_PER_GROUP * (EXPERTS_PER_GROUP - 1) // 2
MOE_TM = 128

TN = (((0,), (0,)), ((), ()))
NT = (((1,), (1,)), ((), ()))


def _cparams(n_axes):
    return pltpu.CompilerParams(dimension_semantics=("arbitrary",) * n_axes, vmem_limit_bytes=VMEM_LIMIT)


def _split_bf16(v):
    hi = v.astype(bf16)
    lo = (v - hi.astype(f32)).astype(bf16)
    return hi, lo


def _split_stack_bf16(v):
    hi = v.astype(bf16).astype(f32)
    return jnp.concatenate([hi, v - hi], axis=0).astype(bf16)


def _rmsnorm_rows(v):
    return v * lax.rsqrt(jnp.mean(v * v, axis=-1, keepdims=True) + EPS)


def _silu(v):
    return v * jax.nn.sigmoid(v)


def _adaln_kernel(c_ref, w_ref, b_ref, o_ref):
    ca = _silu(c_ref[...])
    o_ref[...] = jnp.dot(ca, w_ref[...], precision=lax.Precision.HIGHEST, preferred_element_type=f32) + b_ref[...]


def _adaln(c, w, b):
    bsz, d = c.shape
    n = w.shape[1]
    return pl.pallas_call(
        _adaln_kernel,
        grid=(n // d,),
        in_specs=[pl.BlockSpec((bsz, d), lambda j: (0, 0)),
                  pl.BlockSpec((d, d), lambda j: (0, j)),
                  pl.BlockSpec((1, d), lambda j: (0, j))],
        out_specs=pl.BlockSpec((bsz, d), lambda j: (0, j)),
        out_shape=jax.ShapeDtypeStruct((bsz, n), f32),
        compiler_params=_cparams(1),
        name="adaln",
    )(c, w, b.reshape(1, n))


def _rope_tab_kernel(pos_ref, invf_ref, o_ref):
    half = ROPE_DIMS // 2
    for b in range(pos_ref.shape[0]):
        ang = pos_ref[b:b + 1, :].astype(f32) * invf_ref[...]
        o_ref[b, 0:half, :] = jnp.cos(ang)
        o_ref[b, half:2 * half, :] = jnp.sin(ang)


def _rope_tab(positions):
    bsz, s = positions.shape
    half = ROPE_DIMS // 2
    inv_freq = (np.float32(ROPE_THETA) ** (-np.arange(0, ROPE_DIMS, 2, dtype=np.float32) / np.float32(ROPE_DIMS)))
    invf = jnp.asarray(inv_freq.astype(np.float32).reshape(half, 1))
    return pl.pallas_call(
        _rope_tab_kernel,
        out_shape=jax.ShapeDtypeStruct((bsz, 2 * half, s), f32),
        name="rope_tab",
    )(positions, invf)


def _rope_expand_matrix():
    half = ROPE_DIMS // 2
    e = np.zeros((2 * half, 3 * LANES), np.float32)
    for j in range(LANES):
        jj = j % SWA_HD
        if jj < half:
            e[jj, j] = 1.0
            e[half + jj, LANES + j] = -1.0
        elif jj < 2 * half:
            e[jj - half, j] = 1.0
            e[half + jj - half, 2 * LANES + j] = 1.0
    return jnp.asarray(np.concatenate([e, e], axis=0), dtype=bf16)


IN_QK0, IN_VG0, IN_SQ0, IN_SKV0, IN_A0, IN_END = 0, 512, 1536, 2048, 2304, 2432


def _inproj_kernel(x_ref, mod_ref, g_ref, w_ref, tab_ref, e_ref, wa_hi_ref, wa_lo_ref, ba_ref,
                   qk_ref, vg_ref, la_ref, sq_ref, skv_ref):
    x = x_ref[...]
    h = (_rmsnorm_rows(x) * g_ref[...]) * (1.0 + mod_ref[1:2, :]) + mod_ref[0:1, :]
    hb = h.astype(bf16)

    def proj(lo, hi):
        return jnp.dot(hb, w_ref[:, lo:hi], preferred_element_type=f32)

    qk = proj(IN_QK0, IN_VG0)
    lane = lax.broadcasted_iota(jnp.int32, qk.shape, 1)
    qk_ref[...] = jnp.where(lane < GLA_QK, qk * (GLA_DK ** -0.5), qk).astype(bf16)
    vg_ref[...] = proj(IN_VG0, IN_SQ0).astype(bf16)

    al_hi, al_lo = _split_bf16(proj(IN_A0, IN_END))
    z = (jnp.dot(al_hi, wa_hi_ref[...], preferred_element_type=f32)
         + jnp.dot(al_lo, wa_hi_ref[...], preferred_element_type=f32)
         + jnp.dot(al_hi, wa_lo_ref[...], preferred_element_type=f32)) + ba_ref[...]
    la_ref[...] = (jnp.minimum(z, 0.0) - jnp.log1p(jnp.exp(-jnp.abs(z)))) * (1.0 / GLA_GATE_NORM)

    tabs = lax.dot_general(_split_stack_bf16(tab_ref[...]), e_ref[...], TN, preferred_element_type=f32)
    lane1 = lax.broadcasted_iota(jnp.int32, (1, LANES), 1)
    cos_t = tabs[:, 0:LANES] + jnp.where((lane1 & (SWA_HD - 1)) < ROPE_DIMS, 0.0, 1.0)
    sa_t = tabs[:, LANES:2 * LANES]
    sb_t = tabs[:, 2 * LANES:3 * LANES]

    def rope(v):
        return (v * cos_t + pltpu.roll(v, LANES - ROPE_DIMS // 2, 1) * sa_t
                + pltpu.roll(v, ROPE_DIMS // 2, 1) * sb_t)

    sq = proj(IN_SQ0, IN_SKV0)
    for p in range(SWA_W // LANES):
        sq_ref[:, p * LANES:(p + 1) * LANES] = (rope(sq[:, p * LANES:(p + 1) * LANES]) * (SWA_HD ** -0.5)).astype(bf16)
    skv = proj(IN_SKV0, IN_A0)
    skv_ref[:, 0:LANES] = rope(skv[:, 0:LANES]).astype(bf16)
    skv_ref[:, LANES:2 * LANES] = skv[:, LANES:2 * LANES].astype(bf16)


def _inproj(x2, mod3, g, w, tab, e_mat, wa_hi, wa_lo, ba, seq, tm):
    t, d = x2.shape
    per_b = seq // tm
    row = lambda i: (i, 0)
    const = lambda i: (0, 0)
    return pl.pallas_call(
        _inproj_kernel,
        grid=(t // tm,),
        in_specs=[pl.BlockSpec((tm, d), row),
                  pl.BlockSpec((None, 6, d), lambda i: (i // per_b, 0, 0)),
                  pl.BlockSpec((1, d), const),
                  pl.BlockSpec(w.shape, const),
                  pl.BlockSpec((None, ROPE_DIMS, tm), lambda i: (i // per_b, 0, i % per_b)),
                  pl.BlockSpec(e_mat.shape, const),
                  pl.BlockSpec(wa_hi.shape, const),
                  pl.BlockSpec(wa_lo.shape, const),
                  pl.BlockSpec(ba.shape, const)],
        out_specs=[pl.BlockSpec((tm, 2 * GLA_QK), row),
                   pl.BlockSpec((tm, 2 * GLA_W), row),
                   pl.BlockSpec((tm, GLA_QK), row),
                   pl.BlockSpec((tm, SWA_W), row),
                   pl.BlockSpec((tm, 2 * SWA_KV), row)],
        out_shape=[jax.ShapeDtypeStruct((t, 2 * GLA_QK), bf16),
                   jax.ShapeDtypeStruct((t, 2 * GLA_W), bf16),
                   jax.ShapeDtypeStruct((t, GLA_QK), f32),
                   jax.ShapeDtypeStruct((t, SWA_W), bf16),
                   jax.ShapeDtypeStruct((t, 2 * SWA_KV), bf16)],
        compiler_params=_cparams(1),
        name="inproj",
    )(x2, mod3, g, w, tab, e_mat, wa_hi, wa_lo, ba)


def _gla_kernel(qk_ref, vg_ref, la_ref, gn_ref, o_ref, st_ref, *, chunks):
    c_len = GLA_CHUNK

    @pl.when(pl.program_id(1) == 0)
    def _():
        st_ref[...] = jnp.zeros_like(st_ref)

    r_i = lax.broadcasted_iota(jnp.int32, (c_len, c_len), 0)
    c_i = lax.broadcasted_iota(jnp.int32, (c_len, c_len), 1)
    causal = r_i >= c_i
    tri = jnp.where(causal, 1.0, 0.0).astype(bf16)
    ones = jnp.ones((c_len, LANES), bf16)
    lane = lax.broadcasted_iota(jnp.int32, (c_len, LANES), 1)
    first = lane < GLA_DK

    for c in range(chunks):
        rows = slice(c * c_len, (c + 1) * c_len)
        la_hi, la_lo = _split_bf16(la_ref[rows, :])
        b = jnp.dot(tri, la_hi, preferred_element_type=f32) + jnp.dot(tri, la_lo, preferred_element_type=f32)
        b_tot_t = (lax.dot_general(la_hi, ones, TN, preferred_element_type=f32)
                   + lax.dot_general(la_lo, ones, TN, preferred_element_type=f32))
        b_last = b[c_len - 1:c_len, :]
        q = qk_ref[rows, 0:GLA_QK].astype(f32)
        k = qk_ref[rows, GLA_QK:2 * GLA_QK].astype(f32)
        q_dec = q * jnp.exp(b)
        k_dec = (k * jnp.exp(-b)).astype(bf16)
        k_rem = (k * jnp.exp(b_last - b)).astype(bf16)
        decay = jnp.exp(b_tot_t)

        for p in range(GLA_HEADS // 2):
            ls = slice(p * LANES, (p + 1) * LANES)
            ss = slice(p * LANES, (p + 1) * LANES)
            s_prev = st_ref[ss, :]
            s_prev_b = s_prev.astype(bf16)
            qd = q_dec[:, ls]
            kv_halves = []
            for hh in range(2):
                h = 2 * p + hh
                vs = slice(h * GLA_DV, (h + 1) * GLA_DV)
                qm = jnp.where(first if hh == 0 else ~first, qd, 0.0).astype(bf16)
                v = vg_ref[rows, vs]
                scores = lax.dot_general(qm, k_dec[:, ls], NT, preferred_element_type=f32)
                scores = jnp.where(causal, scores, 0.0).astype(bf16)
                o = (jnp.dot(scores, v, preferred_element_type=f32)
                     + jnp.dot(qm, s_prev_b, preferred_element_type=f32))
                o = _rmsnorm_rows(o) * gn_ref[:, vs]
                gate = vg_ref[rows, GLA_W + h * GLA_DV:GLA_W + (h + 1) * GLA_DV].astype(f32)
                o_ref[rows, vs] = (o * _silu(gate)).astype(bf16)
                kv = lax.dot_general(k_rem[:, ls], v, TN, preferred_element_type=f32)
                kv_halves.append(kv[hh * GLA_DK:(hh + 1) * GLA_DK, :])
            st_ref[ss, :] = decay[ss, :] * s_prev + jnp.concatenate(kv_halves, axis=0)


def _gla(qk, vg, la, gn, bsz, seq, tc):
    t = qk.shape[0]
    nt = seq // tc
    row = lambda b, j: (b * nt + j, 0)
    return pl.pallas_call(
        functools.partial(_gla_kernel, chunks=tc // GLA_CHUNK),
        grid=(bsz, nt),
        in_specs=[pl.BlockSpec((tc, 2 * GLA_QK), row),
                  pl.BlockSpec((tc, 2 * GLA_W), row),
                  pl.BlockSpec((tc, GLA_QK), row),
                  pl.BlockSpec((1, GLA_W), lambda b, j: (0, 0))],
        out_specs=pl.BlockSpec((tc, GLA_W), row),
        out_shape=jax.ShapeDtypeStruct((t, GLA_W), bf16),
        scratch_shapes=[pltpu.VMEM((GLA_HEADS * GLA_DK, GLA_DV), f32)],
        compiler_params=_cparams(2),
        name="gla",
    )(qk, vg, la, gn)


def _swa_kernel(sink_ref, q_ref, kvp_ref, kvc_ref, o_ref):
    w = WINDOW
    j = pl.program_id(1)
    k2 = jnp.concatenate([kvp_ref[:, 0:LANES], kvc_ref[:, 0:LANES]], axis=0)
    v2 = jnp.concatenate([kvp_ref[:, LANES:2 * LANES], kvc_ref[:, LANES:2 * LANES]], axis=0)
    t_i = lax.broadcasted_iota(jnp.int32, (w, 2 * w), 0)
    s_i = lax.broadcasted_iota(jnp.int32, (w, 2 * w), 1)
    rel = t_i + w - s_i
    valid = (rel >= 0) & (rel < w) & ((j > 0) | (s_i >= w))
    lane = lax.broadcasted_iota(jnp.int32, (w, LANES), 1)
    first = lane < SWA_HD
    for p in range(SWA_W // LANES):
        qp = q_ref[:, p * LANES:(p + 1) * LANES].astype(f32)
        halves = []
        for hh in range(2):
            head = SWA_HEAD_ORDER[2 * p + hh]
            qm = jnp.where(first if hh == 0 else ~first, qp, 0.0).astype(bf16)
            s = lax.dot_general(qm, k2, NT, preferred_element_type=f32)
            s = jnp.where(valid, s, -jnp.inf)
            sink = sink_ref[0, head]
            m = jnp.maximum(jnp.max(s, axis=-1, keepdims=True), sink)
            pr = jnp.exp(s - m)
            denom = jnp.sum(pr, axis=-1, keepdims=True) + jnp.exp(sink - m)
            o = jnp.dot(pr.astype(bf16), v2, preferred_element_type=f32)
            halves.append(o / denom)
        o_ref[:, p * LANES:(p + 1) * LANES] = jnp.where(first, halves[0], halves[1]).astype(bf16)


def _swa(sinks, sq, skv, bsz, seq):
    t = sq.shape[0]
    nb = seq // WINDOW
    return pl.pallas_call(
        _swa_kernel,
        grid=(bsz, nb),
        in_specs=[pl.BlockSpec(memory_space=pltpu.SMEM),
                  pl.BlockSpec((WINDOW, SWA_W), lambda b, j: (b * nb + j, 0)),
                  pl.BlockSpec((WINDOW, 2 * SWA_KV), lambda b, j: (b * nb + jnp.maximum(j - 1, 0), 0)),
                  pl.BlockSpec((WINDOW, 2 * SWA_KV), lambda b, j: (b * nb + j, 0))],
        out_specs=pl.BlockSpec((WINDOW, SWA_W), lambda b, j: (b * nb + j, 0)),
        out_shape=jax.ShapeDtypeStruct((t, SWA_W), bf16),
        compiler_params=_cparams(2),
        name="swa",
    )(sinks, sq, skv, skv)


def _to_slabs(v):
    return v.reshape(v.shape[0], v.shape[1] // LANES, LANES).astype(bf16)


def _from_slabs(ref, lo, hi):
    return jnp.concatenate([ref[:, c, :] for c in range(lo, hi)], axis=1)


def _outproj_kernel(og_ref, os_ref, x_ref, mod_ref, wo_ref, g2_ref, wr_hi_ref, wr_lo_ref, br_ref,
                    x1_ref, hrow_ref, route_ref, cnt_ref, run_ref):
    @pl.when(pl.program_id(0) == 0)
    def _():
        run_ref[...] = jnp.zeros_like(run_ref)

    mix = (jnp.dot(og_ref[...], wo_ref[0:GLA_W, :], preferred_element_type=f32)
           + jnp.dot(os_ref[...], wo_ref[GLA_W:GLA_W + SWA_W, :], preferred_element_type=f32))
    x1 = x_ref[...] + mod_ref[2:3, :] * mix
    x1_ref[...] = x1
    h = (_rmsnorm_rows(x1) * g2_ref[...]) * (1.0 + mod_ref[4:5, :]) + mod_ref[3:4, :]
    h_hi, h_lo = _split_bf16(h)
    hrow_ref[...] = _to_slabs(h)

    lt = (lax.dot_general(wr_hi_ref[...], h_hi, NT, preferred_element_type=f32)
          + lax.dot_general(wr_hi_ref[...], h_lo, NT, preferred_element_type=f32)
          + lax.dot_general(wr_lo_ref[...], h_hi, NT, preferred_element_type=f32)) + br_ref[...]
    tm = lt.shape[1]
    e_g = EXPERTS_PER_GROUP
    row = lax.broadcasted_iota(jnp.int32, (e_g, tm), 0)
    neg = -jnp.inf
    gl = jnp.where(row < N_GROUPS, lt[0:e_g, :], neg)
    g_max = jnp.max(gl, axis=0, keepdims=True)
    g_gate = 1.0 / jnp.sum(jnp.exp(gl - g_max), axis=0, keepdims=True)
    g_idx = jnp.min(jnp.where(gl == g_max, row, e_g), axis=0, keepdims=True)
    sel = lt[e_g * N_GROUPS:e_g * (N_GROUPS + 1), :]
    for g in range(N_GROUPS - 2, -1, -1):
        sel = jnp.where(g_idx == g, lt[e_g * (g + 1):e_g * (g + 2), :], sel)
    t1 = jnp.max(sel, axis=0, keepdims=True)
    i1 = jnp.min(jnp.where(sel == t1, row, e_g), axis=0, keepdims=True)
    sel2 = jnp.where(row == i1, neg, sel)
    t2 = jnp.max(sel2, axis=0, keepdims=True)
    i2 = jnp.min(jnp.where(sel2 == t2, row, e_g), axis=0, keepdims=True)
    ex = jnp.exp(t2 - t1)
    w1 = g_gate / (1.0 + ex)
    w2 = g_gate * ex / (1.0 + ex)
    first_lo = i1 < i2
    w_lo = jnp.where(first_lo, w1, w2)
    w_hi = jnp.where(first_lo, w2, w1)
    bucket = ((g_idx << (2 * EXPERT_BITS)) | (jnp.minimum(i1, i2) << EXPERT_BITS) | jnp.maximum(i1, i2))
    onehot = lax.broadcasted_iota(jnp.int32, (N_BUCKETS, tm), 0) == bucket
    oh_b = jnp.where(onehot, 1.0, 0.0).astype(bf16)
    t_r = lax.broadcasted_iota(jnp.int32, (tm, tm), 0)
    t_c = lax.broadcasted_iota(jnp.int32, (tm, tm), 1)
    earlier = jnp.where(t_r < t_c, 1.0, 0.0).astype(bf16)
    run = run_ref[...]
    prefix = jnp.dot(oh_b, earlier, preferred_element_type=f32) + run
    rank = jnp.sum(jnp.where(onehot, prefix, 0.0), axis=0, keepdims=True)
    run = run + jnp.dot(oh_b, jnp.ones((tm, tm), bf16), preferred_element_type=f32)
    run_ref[...] = run
    cnt_ref[...] = run[:, 0:LANES]
    route_ref[...] = jnp.where(row == 0, bucket.astype(f32), jnp.where(row == 1, rank, jnp.where(
        row == 2, w_lo, jnp.where(row == 3, w_hi, 0.0))))


def _outproj(og, osw, x2, mod3, wo, g2, wr_hi, wr_lo, br, seq, tm):
    t, d = x2.shape
    per_b = seq // tm
    row = lambda i: (i, 0)
    const = lambda i: (0, 0)
    return pl.pallas_call(
        _outproj_kernel,
        grid=(t // tm,),
        in_specs=[pl.BlockSpec((tm, GLA_W), row),
                  pl.BlockSpec((tm, SWA_W), row),
                  pl.BlockSpec((tm, d), row),
                  pl.BlockSpec((None, 6, d), lambda i: (i // per_b, 0, 0)),
                  pl.BlockSpec(wo.shape, const),
                  pl.BlockSpec((1, d), const),
                  pl.BlockSpec(wr_hi.shape, const),
                  pl.BlockSpec(wr_lo.shape, const),
                  pl.BlockSpec(br.shape, const)],
        out_specs=[pl.BlockSpec((tm, d), row),
                   pl.BlockSpec((tm, d // LANES, LANES), lambda i: (i, 0, 0)),
                   pl.BlockSpec((ROUTE_ROWS, tm), lambda i: (0, i)),
                   pl.BlockSpec((N_BUCKETS, LANES), const)],
        out_shape=[jax.ShapeDtypeStruct((t, d), f32),
                   jax.ShapeDtypeStruct((t, d // LANES, LANES), bf16),
                   jax.ShapeDtypeStruct((ROUTE_ROWS, t), f32),
                   jax.ShapeDtypeStruct((N_BUCKETS, LANES), f32)],
        scratch_shapes=[pltpu.VMEM((N_BUCKETS, tm), f32)],
        compiler_params=_cparams(1),
        name="outproj",
    )(og, osw, x2, mod3, wo, g2, wr_hi, wr_lo, br)


def _plan_kernel(cnt_ref, route_ref, dest_ref, tb_ref, nt_ref):
    nb = N_BUCKETS
    tiles = jnp.floor((cnt_ref[...] + (MOE_TM - 1)) * (1.0 / MOE_TM))
    b_r = lax.broadcasted_iota(jnp.int32, (nb, nb), 0)
    b_c = lax.broadcasted_iota(jnp.int32, (nb, nb), 1)
    before = jnp.where(b_c < b_r, 1.0, 0.0).astype(bf16)
    t_start = jnp.dot(before, tiles.astype(bf16), preferred_element_type=f32)
    t_end = t_start + tiles
    tile_i = lax.broadcasted_iota(jnp.int32, (nb, 2 * LANES), 1).astype(f32)
    ended = jnp.where(jnp.concatenate([t_end, t_end], axis=1) <= tile_i, 1.0, 0.0)
    tb_ref[...] = jnp.sum(ended, axis=0, keepdims=True).astype(jnp.int32)
    nt_ref[...] = t_end[nb - 1:nb, :].astype(jnp.int32)
    tl = route_ref.shape[1]
    onehot = lax.broadcasted_iota(jnp.int32, (nb, tl), 0) == route_ref[0:1, :].astype(jnp.int32)
    start = lax.dot_general(t_start.astype(bf16), jnp.where(onehot, 1.0, 0.0).astype(bf16), TN,
                            preferred_element_type=f32)
    dest_ref[...] = (start[0:1, :] * MOE_TM + route_ref[1:2, :]).astype(jnp.int32)


def _plan(cnt, route, tl):
    t = route.shape[1]
    return pl.pallas_call(
        _plan_kernel,
        grid=(t // tl,),
        in_specs=[pl.BlockSpec(cnt.shape, lambda i: (0, 0)),
                  pl.BlockSpec((ROUTE_ROWS, tl), lambda i: (0, i))],
        out_specs=[pl.BlockSpec((1, tl), lambda i: (0, i)),
                   pl.BlockSpec((1, 2 * LANES), lambda i: (0, 0)),
                   pl.BlockSpec((1, LANES), lambda i: (0, 0))],
        out_shape=[jax.ShapeDtypeStruct((1, t), jnp.int32),
                   jax.ShapeDtypeStruct((1, 2 * LANES), jnp.int32),
                   jax.ShapeDtypeStruct((1, LANES), jnp.int32)],
        compiler_params=_cparams(1),
        name="plan",
    )(cnt, route)


def _dispatch_kernel(dest_ref, tb_ref, nt_ref, src_ref, out_ref, zero_ref, sem_z, sem_r, *, rows, n_tiles_max):
    i = pl.program_id(0)

    def last_tile(k):
        return (k >= nt_ref[0]) | (tb_ref[k] != tb_ref[k + 1])

    def zero_copy(k):
        return pltpu.make_async_copy(zero_ref, out_ref.at[pl.ds(k * MOE_TM, MOE_TM)], sem_z)

    @pl.when(i == 0)
    def _():
        zero_ref[...] = jnp.zeros_like(zero_ref)

        def start(k, c):
            @pl.when(last_tile(k))
            def _():
                zero_copy(k).start()
            return c

        def wait(k, c):
            @pl.when(last_tile(k))
            def _():
                zero_copy(k).wait()
            return c

        lax.fori_loop(0, n_tiles_max, start, 0)
        lax.fori_loop(0, n_tiles_max, wait, 0)

    base = i * rows

    def row_copy(r):
        return pltpu.make_async_copy(src_ref.at[base + r], out_ref.at[dest_ref[base + r]], sem_r)

    def issue(r, c):
        row_copy(r).start()
        return c

    def drain(r, c):
        row_copy(r).wait()
        return c

    lax.fori_loop(0, rows, issue, 0, unroll=8)
    lax.fori_loop(0, rows, drain, 0, unroll=8)


def _dispatch(dest, tb, nt, hrow, n_tiles_max, rows):
    t = hrow.shape[0]
    slab = hrow.shape[1:]
    grid_spec = pltpu.PrefetchScalarGridSpec(
        num_scalar_prefetch=3, grid=(t // rows,),
        in_specs=[pl.BlockSpec(memory_space=pl.ANY)],
        out_specs=pl.BlockSpec(memory_space=pl.ANY),
        scratch_shapes=[pltpu.VMEM((MOE_TM,) + slab, hrow.dtype), pltpu.SemaphoreType.DMA(()),
                        pltpu.SemaphoreType.DMA(())])
    return pl.pallas_call(
        functools.partial(_dispatch_kernel, rows=rows, n_tiles_max=n_tiles_max),
        grid_spec=grid_spec,
        out_shape=jax.ShapeDtypeStruct((n_tiles_max * MOE_TM,) + slab, hrow.dtype),
        compiler_params=_cparams(1),
        name="dispatch",
    )(dest, tb, nt, hrow)


def _moe_kernel(tb_ref, nt_ref, hs_ref, wg_ref, wu_ref, wd_ref, y_ref):
    i = pl.program_id(0)

    @pl.when(i < nt_ref[0])
    def _():
        e_g = EXPERTS_PER_GROUP
        bucket = tb_ref[i]
        n_slab = hs_ref.shape[1]
        h = _from_slabs(hs_ref, 0, n_slab)
        for k, e in enumerate(((bucket >> EXPERT_BITS) & (e_g - 1), bucket & (e_g - 1))):
            a = jnp.dot(h, wg_ref[e], preferred_element_type=f32)
            u = jnp.dot(h, wu_ref[e], preferred_element_type=f32)
            y = jnp.dot((_silu(a) * u).astype(bf16), wd_ref[e], preferred_element_type=f32)
            y_ref[:, k * n_slab:(k + 1) * n_slab, :] = _to_slabs(y)

    @pl.when(i >= nt_ref[0])
    def _():
        y_ref[...] = jnp.zeros_like(y_ref)


def _moe(tb, nt, hs, wg, wu, wd):
    n_tiles_max = hs.shape[0] // MOE_TM
    e_g = EXPERTS_PER_GROUP
    d = wg.shape[2]
    n_slab = hs.shape[1]
    last = lambda i, tb, nt: jnp.minimum(i, nt[0] - 1)
    group = lambda i, tb, nt: (tb[last(i, tb, nt)] >> (2 * EXPERT_BITS), 0, 0, 0)
    grid_spec = pltpu.PrefetchScalarGridSpec(
        num_scalar_prefetch=2, grid=(n_tiles_max,),
        in_specs=[pl.BlockSpec((MOE_TM, n_slab, LANES), lambda i, tb, nt: (last(i, tb, nt), 0, 0)),
                  pl.BlockSpec((None, e_g, d, D_EXPERT), group),
                  pl.BlockSpec((None, e_g, d, D_EXPERT), group),
                  pl.BlockSpec((None, e_g, D_EXPERT, d), group)],
        out_specs=pl.BlockSpec((MOE_TM, 2 * n_slab, LANES), lambda i, tb, nt: (i, 0, 0)))
    return pl.pallas_call(
        _moe_kernel,
        grid_spec=grid_spec,
        out_shape=jax.ShapeDtypeStruct((hs.shape[0], 2 * n_slab, LANES), bf16),
        compiler_params=_cparams(1),
        name="moe",
    )(tb, nt, hs, wg, wu, wd)


def _final_kernel(dest_ref, x1_ref, mod_ref, fg_ref, route_ref, ones_ref, ys_ref, o_ref, ybuf_ref, sem):
    i = pl.program_id(0)
    tm = x1_ref.shape[0]

    def row_copy(tile, slot, r):
        return pltpu.make_async_copy(ys_ref.at[dest_ref[tile * tm + r]], ybuf_ref.at[slot, r], sem.at[slot])

    def gather(tile, slot):
        def issue(r, c):
            row_copy(tile, slot, r).start()
            return c
        lax.fori_loop(0, tm, issue, 0, unroll=8)

    @pl.when(i == 0)
    def _():
        gather(0, 0)

    @pl.when(i + 1 < pl.num_programs(0))
    def _():
        gather(i + 1, (i + 1) % 2)

    slot = i % 2

    def drain(r, c):
        row_copy(i, slot, r).wait()
        return c

    lax.fori_loop(0, tm, drain, 0, unroll=8)

    def pieces(v):
        p1 = v.astype(bf16).astype(f32)
        p2 = (v - p1).astype(bf16).astype(f32)
        return p1, p2, v - p1 - p2

    rows = ones_ref.shape[0]
    r_i = lax.broadcasted_iota(jnp.int32, (rows, tm), 0)
    stack = jnp.zeros((rows, tm), f32)
    for k, piece in enumerate(pieces(route_ref[2:3, :]) + pieces(route_ref[3:4, :])):
        stack = jnp.where(r_i == k, piece, stack)
    w_cols = lax.dot_general(stack.astype(bf16), ones_ref[...], TN, preferred_element_type=f32)
    n_slab = ybuf_ref.shape[2] // 2
    buf = ybuf_ref.at[slot]
    y = jnp.concatenate(
        [buf[:, c, :].astype(f32) * w_cols[:, 0:LANES] + buf[:, n_slab + c, :].astype(f32) * w_cols[:, LANES:2 * LANES]
         for c in range(n_slab)], axis=1)
    x2 = x1_ref[...] + mod_ref[5:6, :] * y
    o_ref[...] = _rmsnorm_rows(x2) * fg_ref[...]


def _final(dest, x1, mod3, fg, route, ys, seq, tm):
    t, d = x1.shape
    per_b = seq // tm
    ones = np.zeros((2 * EXPERTS_PER_GROUP, 2 * LANES), np.float32)
    ones[0:3, 0:LANES] = 1.0
    ones[3:6, LANES:2 * LANES] = 1.0
    ones = jnp.asarray(ones, dtype=bf16)
    grid_spec = pltpu.PrefetchScalarGridSpec(
        num_scalar_prefetch=1, grid=(t // tm,),
        in_specs=[pl.BlockSpec((tm, d), lambda i, dest: (i, 0)),
                  pl.BlockSpec((None, 6, d), lambda i, dest: (i // per_b, 0, 0)),
                  pl.BlockSpec((1, d), lambda i, dest: (0, 0)),
                  pl.BlockSpec((ROUTE_ROWS, tm), lambda i, dest: (0, i)),
                  pl.BlockSpec(ones.shape, lambda i, dest: (0, 0)),
                  pl.BlockSpec(memory_space=pl.ANY)],
        out_specs=pl.BlockSpec((tm, d), lambda i, dest: (i, 0)),
        scratch_shapes=[pltpu.VMEM((2, tm) + ys.shape[1:], ys.dtype), pltpu.SemaphoreType.DMA((2,))])
    return pl.pallas_call(
        _final_kernel,
        grid_spec=grid_spec,
        out_shape=jax.ShapeDtypeStruct((t, d), f32),
        compiler_params=_cparams(1),
        name="final",
    )(dest, x1, mod3, fg, route, ones, ys)


def _prep_w_in(w_in_l):
    sizes = (GLA_QK, GLA_QK, GLA_W, GLA_W, GLA_LOWRANK, SWA_W, SWA_KV, SWA_KV)
    off = np.concatenate([[0], np.cumsum(sizes)])
    gq, gk, gv, gg, ga, sq, sk, sv = [np.arange(off[i], off[i + 1]) for i in range(len(sizes))]
    sq_perm = np.concatenate([sq[h * SWA_HD:(h + 1) * SWA_HD] for h in SWA_HEAD_ORDER])
    cols = np.concatenate([gq, gk, gv, gg, sq_perm, sk, sv, ga])
    w = jnp.take(w_in_l, jnp.asarray(cols), axis=1)
    w = jnp.pad(w, ((0, 0), (0, IN_END - w.shape[1])))
    return w.astype(bf16)


def _prep_w_out(w_out_l):
    rows = np.concatenate([np.arange(GLA_W)]
                          + [GLA_W + np.arange(h * SWA_HD, (h + 1) * SWA_HD) for h in SWA_HEAD_ORDER])
    return jnp.take(w_out_l, jnp.asarray(rows), axis=0).astype(bf16)


def _prep_router(w_grp_l, b_grp_l, w_exp_l, b_exp_l):
    d = w_grp_l.shape[0]
    n_e = N_GROUPS * EXPERTS_PER_GROUP
    wr = jnp.zeros((ROUTER_ROWS, d), f32)
    wr = wr.at[0:N_GROUPS, :].set(w_grp_l.T).at[EXPERTS_PER_GROUP:EXPERTS_PER_GROUP + n_e, :].set(w_exp_l.T)
    br = jnp.zeros((ROUTER_ROWS, 1), f32)
    br = br.at[0:N_GROUPS, 0].set(b_grp_l).at[EXPERTS_PER_GROUP:EXPERTS_PER_GROUP + n_e, 0].set(b_exp_l)
    wr_hi = wr.astype(bf16)
    wr_lo = (wr - wr_hi.astype(f32)).astype(bf16)
    return wr_hi, wr_lo, br


def _layer(x2, mod3, tab, e_mat, bsz, seq, norm_mix_g, w_in, gla_w_a2, gla_b_a2, gla_norm_g, swa_sinks, w_out,
           norm_ffn_g, w_grp, b_grp, w_exp, b_exp, w_gate, w_up, w_down, out_g, tm, tc, tm_moe):
    d = x2.shape[1]
    wa = jnp.pad(gla_w_a2, ((0, LANES - GLA_LOWRANK), (0, 0)))
    wa_hi = wa.astype(bf16)
    wa_lo = (wa - wa_hi.astype(f32)).astype(bf16)
    qk, vg, la, sq, skv = _inproj(x2, mod3, norm_mix_g.reshape(1, d), _prep_w_in(w_in), tab, e_mat,
                                  wa_hi, wa_lo, gla_b_a2.reshape(1, GLA_QK), seq, tm)
    o_gla = _gla(qk, vg, la, gla_norm_g.reshape(1, GLA_W), bsz, seq, tc)
    o_swa = _swa(swa_sinks.reshape(1, SWA_HEADS), sq, skv, bsz, seq)
    wr_hi, wr_lo, br = _prep_router(w_grp, b_grp, w_exp, b_exp)
    x1, hrow, route, cnt = _outproj(o_gla, o_swa, x2, mod3, _prep_w_out(w_out), norm_ffn_g.reshape(1, d),
                                    wr_hi, wr_lo, br, seq, tm)
    t = x2.shape[0]
    n_tiles_max = t // MOE_TM + N_PAIRS
    assert n_tiles_max <= 2 * LANES and t % MOE_TM == 0
    dest, tile_bucket, n_tiles = _plan(cnt, route, min(t, 2048))
    dest = dest.reshape(t)
    tile_bucket = tile_bucket.reshape(2 * LANES)
    n_tiles = n_tiles.reshape(LANES)[0:1]
    hs = _dispatch(dest, tile_bucket, n_tiles, hrow, n_tiles_max, min(t, 1024))
    ys = _moe(tile_bucket, n_tiles, hs, w_gate.astype(bf16), w_up.astype(bf16), w_down.astype(bf16))
    return _final(dest, x1, mod3, out_g.reshape(1, d), route, ys, seq, tm_moe)


def kernel(x, c, positions, ada_w, ada_b, norm_mix_g, w_in, gla_w_a2, gla_b_a2, gla_norm_g, swa_sinks, w_out,
           norm_ffn_g, w_grp, b_grp, w_exp, b_exp, w_gate, w_up, w_down, final_norm_g):
    bsz, seq, d = x.shape
    depth = ada_w.shape[0]
    assert depth == 1, "the final rmsnorm is fused into the last layer's MoE kernel"
    tab = _rope_tab(positions)
    e_mat = _rope_expand_matrix()
    x2 = x.reshape(bsz * seq, d)
    for l in range(depth):
        mod3 = _adaln(c, ada_w[l], ada_b[l]).reshape(bsz, 6, d)
        x2 = _layer(x2, mod3, tab, e_mat, bsz, seq, norm_mix_g[l], w_in[l], gla_w_a2[l], gla_b_a2[l],
                    gla_norm_g[l], swa_sinks[l], w_out[l], norm_ffn_g[l], w_grp[l], b_grp[l], w_exp[l], b_exp[l],
                    w_gate[l], w_up[l], w_down[l], final_norm_g, tm=512, tc=256, tm_moe=512)
    return x2.reshape(bsz, seq, d)
```

```python
import functools

import numpy as np
import jax
import jax.numpy as jnp
from jax import lax
from jax.experimental import pallas as pl
from jax.experimental.pallas import tpu as pltpu

f32 = jnp.float32
bf16 = jnp.bfloat16

GLA_HEADS = 4
GLA_DK = 64
GLA_DV = 128
GLA_LOWRANK = 16
GLA_GATE_NORM = 16.0
GLA_CHUNK = 64
SWA_HEADS = 8
SWA_KV_HEADS = 2
SWA_HD = 64
WINDOW = 128
ROPE_THETA = 500000.0
ROPE_DIMS = SWA_HD // 4
N_GROUPS = 4
EXPERTS_PER_GROUP = 8
D_EXPERT = 256
EPS = 1e-6

LANES = 128
VMEM_LIMIT = 52 * 1024 * 1024

GLA_QK = GLA_HEADS * GLA_DK
GLA_W = GLA_HEADS * GLA_DV
SWA_W = SWA_HEADS * SWA_HD
SWA_KV = SWA_KV_HEADS * SWA_HD
SWA_HEAD_ORDER = (0, 4, 1, 5, 2, 6, 3, 7)
ROUTE_ROWS = 8
ROUTER_ROWS = 128
EXPERT_BITS = 3
N_BUCKETS = N_GROUPS << (2 * EXPERT_BITS)
N_PAIRS = N_GROUPS * EXPERTS_PER_GROUP * (EXPERTS_PER_GROUP - 1) // 2
MOE_TM = 128

TN = (((0,), (0,)), ((), ()))
NT = (((1,), (1,)), ((), ()))


def _cparams(n_axes):
    return pltpu.CompilerParams(dimension_semantics=("arbitrary",) * n_axes, vmem_limit_bytes=VMEM_LIMIT)


def _split_bf16(v):
    hi = v.astype(bf16)
    lo = (v - hi.astype(f32)).astype(bf16)
    return hi, lo


def _split_stack_bf16(v):
    hi = v.astype(bf16).astype(f32)
    return jnp.concatenate([hi, v - hi], axis=0).astype(bf16)


def _rmsnorm_rows(v):
    return v * lax.rsqrt(jnp.mean(v * v, axis=-1, keepdims=True) + EPS)


def _silu(v):
    return v * jax.nn.sigmoid(v)


def _adaln_kernel(c_ref, w_ref, b_ref, o_ref):
    ca = _silu(c_ref[...])
    o_ref[...] = jnp.dot(ca, w_ref[...], precision=lax.Precision.HIGHEST, preferred_element_type=f32) + b_ref[...]


def _adaln(c, w, b):
    bsz, d = c.shape
    n = w.shape[1]
    return pl.pallas_call(
        _adaln_kernel,
        grid=(n // d,),
        in_specs=[pl.BlockSpec((bsz, d), lambda j: (0, 0)),
                  pl.BlockSpec((d, d), lambda j: (0, j)),
                  pl.BlockSpec((1, d), lambda j: (0, j))],
        out_specs=pl.BlockSpec((bsz, d), lambda j: (0, j)),
        out_shape=jax.ShapeDtypeStruct((bsz, n), f32),
        compiler_params=_cparams(1),
        name="adaln",
    )(c, w, b.reshape(1, n))


def _rope_tab_kernel(pos_ref, invf_ref, o_ref):
    half = ROPE_DIMS // 2
    for b in range(pos_ref.shape[0]):
        ang = pos_ref[b:b + 1, :].astype(f32) * invf_ref[...]
        o_ref[b, 0:half, :] = jnp.cos(ang)
        o_ref[b, half:2 * half, :] = jnp.sin(ang)


def _rope_tab(positions):
    bsz, s = positions.shape
    half = ROPE_DIMS // 2
    inv_freq = (np.float32(ROPE_THETA) ** (-np.arange(0, ROPE_DIMS, 2, dtype=np.float32) / np.float32(ROPE_DIMS)))
    invf = jnp.asarray(inv_freq.astype(np.float32).reshape(half, 1))
    return pl.pallas_call(
        _rope_tab_kernel,
        out_shape=jax.ShapeDtypeStruct((bsz, 2 * half, s), f32),
        name="rope_tab",
    )(positions, invf)


def _rope_expand_matrix():
    half = ROPE_DIMS // 2
    e = np.zeros((2 * half, 3 * LANES), np.float32)
    for j in range(LANES):
        jj = j % SWA_HD
        if jj < half:
            e[jj, j] = 1.0
            e[half + jj, LANES + j] = -1.0
        elif jj < 2 * half:
            e[jj - half, j] = 1.0
            e[half + jj - half, 2 * LANES + j] = 1.0
    return jnp.asarray(np.concatenate([e, e], axis=0), dtype=bf16)


IN_QK0, IN_VG0, IN_SQ0, IN_SKV0, IN_A0, IN_END = 0, 512, 1536, 2048, 2304, 2432


def _inproj_kernel(x_ref, mod_ref, g_ref, w_ref, tab_ref, e_ref, wa_hi_ref, wa_lo_ref, ba_ref,
                   qk_ref, vg_ref, la_ref, sq_ref, skv_ref):
    x = x_ref[...]
    h = (_rmsnorm_rows(x) * g_ref[...]) * (1.0 + mod_ref[1:2, :]) + mod_ref[0:1, :]
    hb = h.astype(bf16)

    def proj(lo, hi):
        return jnp.dot(hb, w_ref[:, lo:hi], preferred_element_type=f32)

    qk = proj(IN_QK0, IN_VG0)
    lane = lax.broadcasted_iota(jnp.int32, qk.shape, 1)
    qk_ref[...] = jnp.where(lane < GLA_QK, qk * (GLA_DK ** -0.5), qk).astype(bf16)
    vg_ref[...] = proj(IN_VG0, IN_SQ0).astype(bf16)

    al_hi, al_lo = _split_bf16(proj(IN_A0, IN_END))
    z = (jnp.dot(al_hi, wa_hi_ref[...], preferred_element_type=f32)
         + jnp.dot(al_lo, wa_hi_ref[...], preferred_element_type=f32)
         + jnp.dot(al_hi, wa_lo_ref[...], preferred_element_type=f32)) + ba_ref[...]
    la_ref[...] = (jnp.minimum(z, 0.0) - jnp.log1p(jnp.exp(-jnp.abs(z)))) * (1.0 / GLA_GATE_NORM)

    tabs = lax.dot_general(_split_stack_bf16(tab_ref[...]), e_ref[...], TN, preferred_element_type=f32)
    lane1 = lax.broadcasted_iota(jnp.int32, (1, LANES), 1)
    cos_t = tabs[:, 0:LANES] + jnp.where((lane1 & (SWA_HD - 1)) < ROPE_DIMS, 0.0, 1.0)
    sa_t = tabs[:, LANES:2 * LANES]
    sb_t = tabs[:, 2 * LANES:3 * LANES]

    def rope(v):
        return (v * cos_t + pltpu.roll(v, LANES - ROPE_DIMS // 2, 1) * sa_t
                + pltpu.roll(v, ROPE_DIMS // 2, 1) * sb_t)

    sq = proj(IN_SQ0, IN_SKV0)
    for p in range(SWA_W // LANES):
        sq_ref[:, p * LANES:(p + 1) * LANES] = (rope(sq[:, p * LANES:(p + 1) * LANES]) * (SWA_HD ** -0.5)).astype(bf16)
    skv = proj(IN_SKV0, IN_A0)
    skv_ref[:, 0:LANES] = rope(skv[:, 0:LANES]).astype(bf16)
    skv_ref[:, LANES:2 * LANES] = skv[:, LANES:2 * LANES].astype(bf16)


def _inproj(x2, mod3, g, w, tab, e_mat, wa_hi, wa_lo, ba, seq, tm):
    t, d = x2.shape
    per_b = seq // tm
    row = lambda i: (i, 0)
    const = lambda i: (0, 0)
    return pl.pallas_call(
        _inproj_kernel,
        grid=(t // tm,),
        in_specs=[pl.BlockSpec((tm, d), row),
                  pl.BlockSpec((None, 6, d), lambda i: (i // per_b, 0, 0)),
                  pl.BlockSpec((1, d), const),
                  pl.BlockSpec(w.shape, const),
                  pl.BlockSpec((None, ROPE_DIMS, tm), lambda i: (i // per_b, 0, i % per_b)),
                  pl.BlockSpec(e_mat.shape, const),
                  pl.BlockSpec(wa_hi.shape, const),
                  pl.BlockSpec(wa_lo.shape, const),
                  pl.BlockSpec(ba.shape, const)],
        out_specs=[pl.BlockSpec((tm, 2 * GLA_QK), row),
                   pl.BlockSpec((tm, 2 * GLA_W), row),
                   pl.BlockSpec((tm, GLA_QK), row),
                   pl.BlockSpec((tm, SWA_W), row),
                   pl.BlockSpec((tm, 2 * SWA_KV), row)],
        out_shape=[jax.ShapeDtypeStruct((t, 2 * GLA_QK), bf16),
                   jax.ShapeDtypeStruct((t, 2 * GLA_W), bf16),
                   jax.ShapeDtypeStruct((t, GLA_QK), f32),
                   jax.ShapeDtypeStruct((t, SWA_W), bf16),
                   jax.ShapeDtypeStruct((t, 2 * SWA_KV), bf16)],
        compiler_params=_cparams(1),
        name="inproj",
    )(x2, mod3, g, w, tab, e_mat, wa_hi, wa_lo, ba)


def _gla_kernel(qk_ref, vg_ref, la_ref, gn_ref, o_ref, st_ref, *, chunks):
    c_len = GLA_CHUNK

    @pl.when(pl.program_id(1) == 0)
    def _():
        st_ref[...] = jnp.zeros_like(st_ref)

    r_i = lax.broadcasted_iota(jnp.int32, (c_len, c_len), 0)
    c_i = lax.broadcasted_iota(jnp.int32, (c_len, c_len), 1)
    causal = r_i >= c_i
    tri = jnp.where(causal, 1.0, 0.0).astype(bf16)
    ones = jnp.ones((c_len, LANES), bf16)
    lane = lax.broadcasted_iota(jnp.int32, (c_len, LANES), 1)
    first = lane < GLA_DK

    for c in range(chunks):
        rows = slice(c * c_len, (c + 1) * c_len)
        la_hi, la_lo = _split_bf16(la_ref[rows, :])
        b = jnp.dot(tri, la_hi, preferred_element_type=f32) + jnp.dot(tri, la_lo, preferred_element_type=f32)
        b_tot_t = (lax.dot_general(la_hi, ones, TN, preferred_element_type=f32)
                   + lax.dot_general(la_lo, ones, TN, preferred_element_type=f32))
        b_last = b[c_len - 1:c_len, :]
        q = qk_ref[rows, 0:GLA_QK].astype(f32)
        k = qk_ref[rows, GLA_QK:2 * GLA_QK].astype(f32)
        q_dec = q * jnp.exp(b)
        k_dec = (k * jnp.exp(-b)).astype(bf16)
        k_rem = (k * jnp.exp(b_last - b)).astype(bf16)
        decay = jnp.exp(b_tot_t)

        for p in range(GLA_HEADS // 2):
            ls = slice(p * LANES, (p + 1) * LANES)
            ss = slice(p * LANES, (p + 1) * LANES)
            s_prev = st_ref[ss, :]
            s_prev_b = s_prev.astype(bf16)
            qd = q_dec[:, ls]
            kv_halves = []
            for hh in range(2):
                h = 2 * p + hh
                vs = slice(h * GLA_DV, (h + 1) * GLA_DV)
                qm = jnp.where(first if hh == 0 else ~first, qd, 0.0).astype(bf16)
                v = vg_ref[rows, vs]
                scores = lax.dot_general(qm, k_dec[:, ls], NT, preferred_element_type=f32)
                scores = jnp.where(causal, scores, 0.0).astype(bf16)
                o = (jnp.dot(scores, v, preferred_element_type=f32)
                     + jnp.dot(qm, s_prev_b, preferred_element_type=f32))
                o = _rmsnorm_rows(o) * gn_ref[:, vs]
                gate = vg_ref[rows, GLA_W + h * GLA_DV:GLA_W + (h + 1) * GLA_DV].astype(f32)
                o_ref[rows, vs] = (o * _silu(gate)).astype(bf16)
                kv = lax.dot_general(k_rem[:, ls], v, TN, preferred_element_type=f32)
                kv_halves.append(kv[hh * GLA_DK:(hh + 1) * GLA_DK, :])
            st_ref[ss, :] = decay[ss, :] * s_prev + jnp.concatenate(kv_halves, axis=0)


def _gla(qk, vg, la, gn, bsz, seq, tc):
    t = qk.shape[0]
    nt = seq // tc
    row = lambda b, j: (b * nt + j, 0)
    return pl.pallas_call(
        functools.partial(_gla_kernel, chunks=tc // GLA_CHUNK),
        grid=(bsz, nt),
        in_specs=[pl.BlockSpec((tc, 2 * GLA_QK), row),
                  pl.BlockSpec((tc, 2 * GLA_W), row),
                  pl.BlockSpec((tc, GLA_QK), row),
                  pl.BlockSpec((1, GLA_W), lambda b, j: (0, 0))],
        out_specs=pl.BlockSpec((tc, GLA_W), row),
        out_shape=jax.ShapeDtypeStruct((t, GLA_W), bf16),
        scratch_shapes=[pltpu.VMEM((GLA_HEADS * GLA_DK, GLA_DV), f32)],
        compiler_params=_cparams(2),
        name="gla",
    )(qk, vg, la, gn)


def _swa_kernel(sink_ref, q_ref, kvp_ref, kvc_ref, o_ref):
    w = WINDOW
    j = pl.program_id(1)
    k2 = jnp.concatenate([kvp_ref[:, 0:LANES], kvc_ref[:, 0:LANES]], axis=0)
    v2 = jnp.concatenate([kvp_ref[:, LANES:2 * LANES], kvc_ref[:, LANES:2 * LANES]], axis=0)
    t_i = lax.broadcasted_iota(jnp.int32, (w, 2 * w), 0)
    s_i = lax.broadcasted_iota(jnp.int32, (w, 2 * w), 1)
    rel = t_i + w - s_i
    valid = (rel >= 0) & (rel < w) & ((j > 0) | (s_i >= w))
    lane = lax.broadcasted_iota(jnp.int32, (w, LANES), 1)
    first = lane < SWA_HD
    for p in range(SWA_W // LANES):
        qp = q_ref[:, p * LANES:(p + 1) * LANES].astype(f32)
        halves = []
        for hh in range(2):
            head = SWA_HEAD_ORDER[2 * p + hh]
            qm = jnp.where(first if hh == 0 else ~first, qp, 0.0).astype(bf16)
            s = lax.dot_general(qm, k2, NT, preferred_element_type=f32)
            s = jnp.where(valid, s, -jnp.inf)
            sink = sink_ref[0, head]
            m = jnp.maximum(jnp.max(s, axis=-1, keepdims=True), sink)
            pr = jnp.exp(s - m)
            denom = jnp.sum(pr, axis=-1, keepdims=True) + jnp.exp(sink - m)
            o = jnp.dot(pr.astype(bf16), v2, preferred_element_type=f32)
            halves.append(o / denom)
        o_ref[:, p * LANES:(p + 1) * LANES] = jnp.where(first, halves[0], halves[1]).astype(bf16)


def _swa(sinks, sq, skv, bsz, seq):
    t = sq.shape[0]
    nb = seq // WINDOW
    return pl.pallas_call(
        _swa_kernel,
        grid=(bsz, nb),
        in_specs=[pl.BlockSpec(memory_space=pltpu.SMEM),
                  pl.BlockSpec((WINDOW, SWA_W), lambda b, j: (b * nb + j, 0)),
                  pl.BlockSpec((WINDOW, 2 * SWA_KV), lambda b, j: (b * nb + jnp.maximum(j - 1, 0), 0)),
                  pl.BlockSpec((WINDOW, 2 * SWA_KV), lambda b, j: (b * nb + j, 0))],
        out_specs=pl.BlockSpec((WINDOW, SWA_W), lambda b, j: (b * nb + j, 0)),
        out_shape=jax.ShapeDtypeStruct((t, SWA_W), bf16),
        compiler_params=_cparams(2),
        name="swa",
    )(sinks, sq, skv, skv)


def _to_slabs(v):
    return v.reshape(v.shape[0], v.shape[1] // LANES, LANES).astype(bf16)


def _from_slabs(ref, lo, hi):
    return jnp.concatenate([ref[:, c, :] for c in range(lo, hi)], axis=1)


def _outproj_kernel(og_ref, os_ref, x_ref, mod_ref, wo_ref, g2_ref, wr_hi_ref, wr_lo_ref, br_ref,
                    x1_ref, hrow_ref, route_ref, cnt_ref, run_ref):
    @pl.when(pl.program_id(0) == 0)
    def _():
        run_ref[...] = jnp.zeros_like(run_ref)

    mix = (jnp.dot(og_ref[...], wo_ref[0:GLA_W, :], preferred_element_type=f32)
           + jnp.dot(os_ref[...], wo_ref[GLA_W:GLA_W + SWA_W, :], preferred_element_type=f32))
    x1 = x_ref[...] + mod_ref[2:3, :] * mix
    x1_ref[...] = x1
    h = (_rmsnorm_rows(x1) * g2_ref[...]) * (1.0 + mod_ref[4:5, :]) + mod_ref[3:4, :]
    h_hi, h_lo = _split_bf16(h)
    hrow_ref[...] = _to_slabs(h)

    lt = (lax.dot_general(wr_hi_ref[...], h_hi, NT, preferred_element_type=f32)
          + lax.dot_general(wr_hi_ref[...], h_lo, NT, preferred_element_type=f32)
          + lax.dot_general(wr_lo_ref[...], h_hi, NT, preferred_element_type=f32)) + br_ref[...]
    tm = lt.shape[1]
    e_g = EXPERTS_PER_GROUP
    row = lax.broadcasted_iota(jnp.int32, (e_g, tm), 0)
    neg = -jnp.inf
    gl = jnp.where(row < N_GROUPS, lt[0:e_g, :], neg)
    g_max = jnp.max(gl, axis=0, keepdims=True)
    g_gate = 1.0 / jnp.sum(jnp.exp(gl - g_max), axis=0, keepdims=True)
    g_idx = jnp.min(jnp.where(gl == g_max, row, e_g), axis=0, keepdims=True)
    sel = lt[e_g * N_GROUPS:e_g * (N_GROUPS + 1), :]
    for g in range(N_GROUPS - 2, -1, -1):
        sel = jnp.where(g_idx == g, lt[e_g * (g + 1):e_g * (g + 2), :], sel)
    t1 = jnp.max(sel, axis=0, keepdims=True)
    i1 = jnp.min(jnp.where(sel == t1, row, e_g), axis=0, keepdims=True)
    sel2 = jnp.where(row == i1, neg, sel)
    t2 = jnp.max(sel2, axis=0, keepdims=True)
    i2 = jnp.min(jnp.where(sel2 == t2, row, e_g), axis=0, keepdims=True)
    ex = jnp.exp(t2 - t1)
    w1 = g_gate / (1.0 + ex)
    w2 = g_gate * ex / (1.0 + ex)
    first_lo = i1 < i2
    w_lo = jnp.where(first_lo, w1, w2)
    w_hi = jnp.where(first_lo, w2, w1)
    bucket = ((g_idx << (2 * EXPERT_BITS)) | (jnp.minimum(i1, i2) << EXPERT_BITS) | jnp.maximum(i1, i2))
    onehot = lax.broadcasted_iota(jnp.int32, (N_BUCKETS, tm), 0) == bucket
    oh_b = jnp.where(onehot, 1.0, 0.0).astype(bf16)
    t_r = lax.broadcasted_iota(jnp.int32, (tm, tm), 0)
    t_c = lax.broadcasted_iota(jnp.int32, (tm, tm), 1)
    earlier = jnp.where(t_r < t_c, 1.0, 0.0).astype(bf16)
    run = run_ref[...]
    prefix = jnp.dot(oh_b, earlier, preferred_element_type=f32) + run
    rank = jnp.sum(jnp.where(onehot, prefix, 0.0), axis=0, keepdims=True)
    run = run + jnp.dot(oh_b, jnp.ones((tm, tm), bf16), preferred_element_type=f32)
    run_ref[...] = run
    cnt_ref[...] = run[:, 0:LANES]
    route_ref[...] = jnp.where(row == 0, bucket.astype(f32), jnp.where(row == 1, rank, jnp.where(
        row == 2, w_lo, jnp.where(row == 3, w_hi, 0.0))))


def _outproj(og, osw, x2, mod3, wo, g2, wr_hi, wr_lo, br, seq, tm):
    t, d = x2.shape
    per_b = seq // tm
    row = lambda i: (i, 0)
    const = lambda i: (0, 0)
    return pl.pallas_call(
        _outproj_kernel,
        grid=(t // tm,),
        in_specs=[pl.BlockSpec((tm, GLA_W), row),
                  pl.BlockSpec((tm, SWA_W), row),
                  pl.BlockSpec((tm, d), row),
                  pl.BlockSpec((None, 6, d), lambda i: (i // per_b, 0, 0)),
                  pl.BlockSpec(wo.shape, const),
                  pl.BlockSpec((1, d), const),
                  pl.BlockSpec(wr_hi.shape, const),
                  pl.BlockSpec(wr_lo.shape, const),
                  pl.BlockSpec(br.shape, const)],
        out_specs=[pl.BlockSpec((tm, d), row),
                   pl.BlockSpec((tm, d // LANES, LANES), lambda i: (i, 0, 0)),
                   pl.BlockSpec((ROUTE_ROWS, tm), lambda i: (0, i)),
                   pl.BlockSpec((N_BUCKETS, LANES), const)],
        out_shape=[jax.ShapeDtypeStruct((t, d), f32),
                   jax.ShapeDtypeStruct((t, d // LANES, LANES), bf16),
                   jax.ShapeDtypeStruct((ROUTE_ROWS, t), f32),
                   jax.ShapeDtypeStruct((N_BUCKETS, LANES), f32)],
        scratch_shapes=[pltpu.VMEM((N_BUCKETS, tm), f32)],
        compiler_params=_cparams(1),
        name="outproj",
    )(og, osw, x2, mod3, wo, g2, wr_hi, wr_lo, br)


def _plan_kernel(cnt_ref, route_ref, dest_ref, tb_ref, nt_ref):
    nb = N_BUCKETS
    tiles = jnp.floor((cnt_ref[...] + (MOE_TM - 1)) * (1.0 / MOE_TM))
    b_r = lax.broadcasted_iota(jnp.int32, (nb, nb), 0)
    b_c = lax.broadcasted_iota(jnp.int32, (nb, nb), 1)
    before = jnp.where(b_c < b_r, 1.0, 0.0).astype(bf16)
    t_start = jnp.dot(before, tiles.astype(bf16), preferred_element_type=f32)
    t_end = t_start + tiles
    tile_i = lax.broadcasted_iota(jnp.int32, (nb, 2 * LANES), 1).astype(f32)
    ended = jnp.where(jnp.concatenate([t_end, t_end], axis=1) <= tile_i, 1.0, 0.0)
    tb_ref[...] = jnp.sum(ended, axis=0, keepdims=True).astype(jnp.int32)
    nt_ref[...] = t_end[nb - 1:nb, :].astype(jnp.int32)
    tl = route_ref.shape[1]
    onehot = lax.broadcasted_iota(jnp.int32, (nb, tl), 0) == route_ref[0:1, :].astype(jnp.int32)
    start = lax.dot_general(t_start.astype(bf16), jnp.where(onehot, 1.0, 0.0).astype(bf16), TN,
                            preferred_element_type=f32)
    dest_ref[...] = (start[0:1, :] * MOE_TM + route_ref[1:2, :]).astype(jnp.int32)


def _plan(cnt, route, tl):
    t = route.shape[1]
    return pl.pallas_call(
        _plan_kernel,
        grid=(t // tl,),
        in_specs=[pl.BlockSpec(cnt.shape, lambda i: (0, 0)),
                  pl.BlockSpec((ROUTE_ROWS, tl), lambda i: (0, i))],
        out_specs=[pl.BlockSpec((1, tl), lambda i: (0, i)),
                   pl.BlockSpec((1, 2 * LANES), lambda i: (0, 0)),
                   pl.BlockSpec((1, LANES), lambda i: (0, 0))],
        out_shape=[jax.ShapeDtypeStruct((1, t), jnp.int32),
                   jax.ShapeDtypeStruct((1, 2 * LANES), jnp.int32),
                   jax.ShapeDtypeStruct((1, LANES), jnp.int32)],
        compiler_params=_cparams(1),
        name="plan",
    )(cnt, route)


def _dispatch_kernel(dest_ref, tb_ref, nt_ref, src_ref, out_ref, zero_ref, sem_z, sem_r, *, rows, n_tiles_max):
    i = pl.program_id(0)

    def last_tile(k):
        return (k >= nt_ref[0]) | (tb_ref[k] != tb_ref[k + 1])

    def zero_copy(k):
        return pltpu.make_async_copy(zero_ref, out_ref.at[pl.ds(k * MOE_TM, MOE_TM)], sem_z)

    @pl.when(i == 0)
    def _():
        zero_ref[...] = jnp.zeros_like(zero_ref)

        def start(k, c):
            @pl.when(last_tile(k))
            def _():
                zero_copy(k).start()
            return c

        def wait(k, c):
            @pl.when(last_tile(k))
            def _():
                zero_copy(k).wait()
            return c

        lax.fori_loop(0, n_tiles_max, start, 0)
        lax.fori_loop(0, n_tiles_max, wait, 0)

    base = i * rows

    def row_copy(r):
        return pltpu.make_async_copy(src_ref.at[r], out_ref.at[dest_ref[base + r]], sem_r)

    def issue(r, c):
        row_copy(r).start()
        return c

    def drain(r, c):
        row_copy(r).wait()
        return c

    lax.fori_loop(0, rows, issue, 0, unroll=8)
    lax.fori_loop(0, rows, drain, 0, unroll=8)


def _dispatch(dest, tb, nt, hrow, n_tiles_max, rows):
    t = hrow.shape[0]
    slab = hrow.shape[1:]
    grid_spec = pltpu.PrefetchScalarGridSpec(
        num_scalar_prefetch=3, grid=(t // rows,),
        in_specs=[pl.BlockSpec((rows,) + slab, lambda i, dest, tb, nt: (i, 0, 0))],
        out_specs=pl.BlockSpec(memory_space=pl.ANY),
        scratch_shapes=[pltpu.VMEM((MOE_TM,) + slab, hrow.dtype), pltpu.SemaphoreType.DMA(()),
                        pltpu.SemaphoreType.DMA(())])
    return pl.pallas_call(
        functools.partial(_dispatch_kernel, rows=rows, n_tiles_max=n_tiles_max),
        grid_spec=grid_spec,
        out_shape=jax.ShapeDtypeStruct((n_tiles_max * MOE_TM,) + slab, hrow.dtype),
        compiler_params=_cparams(1),
        name="dispatch",
    )(dest, tb, nt, hrow)


def _moe_kernel(tb_ref, nt_ref, hs_ref, wg_ref, wu_ref, wd_ref, y_ref, h32_ref, y32_ref):
    i = pl.program_id(0)

    @pl.when(i < nt_ref[0])
    def _():
        e_g = EXPERTS_PER_GROUP
        bucket = tb_ref[i]
        n_slab = hs_ref.shape[1]
        tm = hs_ref.shape[0]
        h32_ref[...] = hs_ref[...].astype(f32).reshape(tm * n_slab, LANES)
        h = jnp.concatenate([h32_ref[pl.ds(c, tm, stride=n_slab), :] for c in range(n_slab)], axis=1).astype(bf16)
        for k, e in enumerate(((bucket >> EXPERT_BITS) & (e_g - 1), bucket & (e_g - 1))):
            a = jnp.dot(h, wg_ref[e], preferred_element_type=f32)
            u = jnp.dot(h, wu_ref[e], preferred_element_type=f32)
            y = jnp.dot((_silu(a) * u).astype(bf16), wd_ref[e], preferred_element_type=f32)
            for c in range(n_slab):
                y32_ref[pl.ds(k * n_slab + c, tm, stride=2 * n_slab), :] = y[:, c * LANES:(c + 1) * LANES]
        y_ref[...] = y32_ref[...].reshape(tm, 2 * n_slab, LANES).astype(bf16)

    @pl.when(i >= nt_ref[0])
    def _():
        y_ref[...] = jnp.zeros_like(y_ref)


def _moe(tb, nt, hs, wg, wu, wd):
    n_tiles_max = hs.shape[0] // MOE_TM
    e_g = EXPERTS_PER_GROUP
    d = wg.shape[2]
    n_slab = hs.shape[1]
    last = lambda i, tb, nt: jnp.minimum(i, nt[0] - 1)
    group = lambda i, tb, nt: (tb[last(i, tb, nt)] >> (2 * EXPERT_BITS), 0, 0, 0)
    grid_spec = pltpu.PrefetchScalarGridSpec(
        num_scalar_prefetch=2, grid=(n_tiles_max,),
        in_specs=[pl.BlockSpec((MOE_TM, n_slab, LANES), lambda i, tb, nt: (last(i, tb, nt), 0, 0)),
                  pl.BlockSpec((None, e_g, d, D_EXPERT), group),
                  pl.BlockSpec((None, e_g, d, D_EXPERT), group),
                  pl.BlockSpec((None, e_g, D_EXPERT, d), group)],
        out_specs=pl.BlockSpec((MOE_TM, 2 * n_slab, LANES), lambda i, tb, nt: (i, 0, 0)),
        scratch_shapes=[pltpu.VMEM((MOE_TM * n_slab, LANES), f32), pltpu.VMEM((MOE_TM * 2 * n_slab, LANES), f32)])
    return pl.pallas_call(
        _moe_kernel,
        grid_spec=grid_spec,
        out_shape=jax.ShapeDtypeStruct((hs.shape[0], 2 * n_slab, LANES), bf16),
        compiler_params=_cparams(1),
        name="moe",
    )(tb, nt, hs, wg, wu, wd)


def _final_kernel(dest_ref, x1_ref, mod_ref, fg_ref, route_ref, ones_ref, ys_ref, o_ref, ybuf_ref, y32_ref, sem):
    i = pl.program_id(0)
    tm = x1_ref.shape[0]

    def row_copy(tile, slot, r):
        return pltpu.make_async_copy(ys_ref.at[dest_ref[tile * tm + r]], ybuf_ref.at[slot, r], sem.at[slot])

    def gather(tile, slot):
        def issue(r, c):
            row_copy(tile, slot, r).start()
            return c
        lax.fori_loop(0, tm, issue, 0, unroll=8)

    @pl.when(i == 0)
    def _():
        gather(0, 0)

    @pl.when(i + 1 < pl.num_programs(0))
    def _():
        gather(i + 1, (i + 1) % 2)

    slot = i % 2

    def drain(r, c):
        row_copy(i, slot, r).wait()
        return c

    lax.fori_loop(0, tm, drain, 0, unroll=8)

    def pieces(v):
        p1 = v.astype(bf16).astype(f32)
        p2 = (v - p1).astype(bf16).astype(f32)
        return p1, p2, v - p1 - p2

    rows = ones_ref.shape[0]
    r_i = lax.broadcasted_iota(jnp.int32, (rows, tm), 0)
    stack = jnp.zeros((rows, tm), f32)
    for k, piece in enumerate(pieces(route_ref[2:3, :]) + pieces(route_ref[3:4, :])):
        stack = jnp.where(r_i == k, piece, stack)
    w_cols = lax.dot_general(stack.astype(bf16), ones_ref[...], TN, preferred_element_type=f32)
    n_slab = ybuf_ref.shape[2] // 2
    y_pair = ybuf_ref[slot].astype(f32)
    w_lo = jnp.broadcast_to(w_cols[:, None, 0:LANES], (tm, n_slab, LANES))
    w_hi = jnp.broadcast_to(w_cols[:, None, LANES:2 * LANES], (tm, n_slab, LANES))
    y_slab = y_pair[:, 0:n_slab, :] * w_lo + y_pair[:, n_slab:2 * n_slab, :] * w_hi
    y32_ref[...] = y_slab.reshape(tm * n_slab, LANES)
    y = jnp.concatenate([y32_ref[pl.ds(c, tm, stride=n_slab), :] for c in range(n_slab)], axis=1)
    x2 = x1_ref[...] + mod_ref[5:6, :] * y
    o_ref[...] = _rmsnorm_rows(x2) * fg_ref[...]


def _final(dest, x1, mod3, fg, route, ys, seq, tm):
    t, d = x1.shape
    per_b = seq // tm
    ones = np.zeros((2 * EXPERTS_PER_GROUP, 2 * LANES), np.float32)
    ones[0:3, 0:LANES] = 1.0
    ones[3:6, LANES:2 * LANES] = 1.0
    ones = jnp.asarray(ones, dtype=bf16)
    grid_spec = pltpu.PrefetchScalarGridSpec(
        num_scalar_prefetch=1, grid=(t // tm,),
        in_specs=[pl.BlockSpec((tm, d), lambda i, dest: (i, 0)),
                  pl.BlockSpec((None, 6, d), lambda i, dest: (i // per_b, 0, 0)),
                  pl.BlockSpec((1, d), lambda i, dest: (0, 0)),
                  pl.BlockSpec((ROUTE_ROWS, tm), lambda i, dest: (0, i)),
                  pl.BlockSpec(ones.shape, lambda i, dest: (0, 0)),
                  pl.BlockSpec(memory_space=pl.ANY)],
        out_specs=pl.BlockSpec((tm, d), lambda i, dest: (i, 0)),
        scratch_shapes=[pltpu.VMEM((2, tm) + ys.shape[1:], ys.dtype), pltpu.VMEM((tm * d // LANES, LANES), f32),
                        pltpu.SemaphoreType.DMA((2,))])
    return pl.pallas_call(
        _final_kernel,
        grid_spec=grid_spec,
        out_shape=jax.ShapeDtypeStruct((t, d), f32),
        compiler_params=_cparams(1),
        name="final",
    )(dest, x1, mod3, fg, route, ones, ys)


def _prep_w_in(w_in_l):
    sizes = (GLA_QK, GLA_QK, GLA_W, GLA_W, GLA_LOWRANK, SWA_W, SWA_KV, SWA_KV)
    off = np.concatenate([[0], np.cumsum(sizes)])
    gq, gk, gv, gg, ga, sq, sk, sv = [np.arange(off[i], off[i + 1]) for i in range(len(sizes))]
    sq_perm = np.concatenate([sq[h * SWA_HD:(h + 1) * SWA_HD] for h in SWA_HEAD_ORDER])
    cols = np.concatenate([gq, gk, gv, gg, sq_perm, sk, sv, ga])
    w = jnp.take(w_in_l, jnp.asarray(cols), axis=1)
    w = jnp.pad(w, ((0, 0), (0, IN_END - w.shape[1])))
    return w.astype(bf16)


def _prep_w_out(w_out_l):
    rows = np.concatenate([np.arange(GLA_W)]
                          + [GLA_W + np.arange(h * SWA_HD, (h + 1) * SWA_HD) for h in SWA_HEAD_ORDER])
    return jnp.take(w_out_l, jnp.asarray(rows), axis=0).astype(bf16)


def _prep_router(w_grp_l, b_grp_l, w_exp_l, b_exp_l):
    d = w_grp_l.shape[0]
    n_e = N_GROUPS * EXPERTS_PER_GROUP
    wr = jnp.zeros((ROUTER_ROWS, d), f32)
    wr = wr.at[0:N_GROUPS, :].set(w_grp_l.T).at[EXPERTS_PER_GROUP:EXPERTS_PER_GROUP + n_e, :].set(w_exp_l.T)
    br = jnp.zeros((ROUTER_ROWS, 1), f32)
    br = br.at[0:N_GROUPS, 0].set(b_grp_l).at[EXPERTS_PER_GROUP:EXPERTS_PER_GROUP + n_e, 0].set(b_exp_l)
    wr_hi = wr.astype(bf16)
    wr_lo = (wr - wr_hi.astype(f32)).astype(bf16)
    return wr_hi, wr_lo, br


def _layer(x2, mod3, tab, e_mat, bsz, seq, norm_mix_g, w_in, gla_w_a2, gla_b_a2, gla_norm_g, swa_sinks, w_out,
           norm_ffn_g, w_grp, b_grp, w_exp, b_exp, w_gate, w_up, w_down, out_g, tm, tc, tm_moe):
    d = x2.shape[1]
    wa = jnp.pad(gla_w_a2, ((0, LANES - GLA_LOWRANK), (0, 0)))
    wa_hi = wa.astype(bf16)
    wa_lo = (wa - wa_hi.astype(f32)).astype(bf16)
    qk, vg, la, sq, skv = _inproj(x2, mod3, norm_mix_g.reshape(1, d), _prep_w_in(w_in), tab, e_mat,
                                  wa_hi, wa_lo, gla_b_a2.reshape(1, GLA_QK), seq, tm)
    o_gla = _gla(qk, vg, la, gla_norm_g.reshape(1, GLA_W), bsz, seq, tc)
    o_swa = _swa(swa_sinks.reshape(1, SWA_HEADS), sq, skv, bsz, seq)
    wr_hi, wr_lo, br = _prep_router(w_grp, b_grp, w_exp, b_exp)
    x1, hrow, route, cnt = _outproj(o_gla, o_swa, x2, mod3, _prep_w_out(w_out), norm_ffn_g.reshape(1, d),
                                    wr_hi, wr_lo, br, seq, tm)
    t = x2.shape[0]
    n_tiles_max = t // MOE_TM + N_PAIRS
    assert n_tiles_max <= 2 * LANES and t % MOE_TM == 0
    dest, tile_bucket, n_tiles = _plan(cnt, route, min(t, 2048))
    dest = dest.reshape(t)
    tile_bucket = tile_bucket.reshape(2 * LANES)
    n_tiles = n_tiles.reshape(LANES)[0:1]
    hs = _dispatch(dest, tile_bucket, n_tiles, hrow, n_tiles_max, min(t, 1024))
    ys = _moe(tile_bucket, n_tiles, hs, w_gate.astype(bf16), w_up.astype(bf16), w_down.astype(bf16))
    return _final(dest, x1, mod3, out_g.reshape(1, d), route, ys, seq, tm_moe)


def kernel(x, c, positions, ada_w, ada_b, norm_mix_g, w_in, gla_w_a2, gla_b_a2, gla_norm_g, swa_sinks, w_out,
           norm_ffn_g, w_grp, b_grp, w_exp, b_exp, w_gate, w_up, w_down, final_norm_g):
    bsz, seq, d = x.shape
    depth = ada_w.shape[0]
    assert depth == 1, "the final rmsnorm is fused into the last layer's MoE kernel"
    tab = _rope_tab(positions)
    e_mat = _rope_expand_matrix()
    x2 = x.reshape(bsz * seq, d)
    for l in range(depth):
        mod3 = _adaln(c, ada_w[l], ada_b[l]).reshape(bsz, 6, d)
        x2 = _layer(x2, mod3, tab, e_mat, bsz, seq, norm_mix_g[l], w_in[l], gla_w_a2[l], gla_b_a2[l],
                    gla_norm_g[l], swa_sinks[l], w_out[l], norm_ffn_g[l], w_grp[l], b_grp[l], w_exp[l], b_exp[l],
                    w_gate[l], w_up[l], w_down[l], final_norm_g, tm=512, tc=256, tm_moe=512)
    return x2.reshape(bsz, seq, d)
```

```python
import functools

import numpy as np
import jax
import jax.numpy as jnp
from jax import lax
from jax.experimental import pallas as pl
from jax.experimental.pallas import tpu as pltpu

f32 = jnp.float32
bf16 = jnp.bfloat16

GLA_HEADS = 4
GLA_DK = 64
GLA_DV = 128
GLA_LOWRANK = 16
GLA_GATE_NORM = 16.0
GLA_CHUNK = 64
SWA_HEADS = 8
SWA_KV_HEADS = 2
SWA_HD = 64
WINDOW = 128
ROPE_THETA = 500000.0
ROPE_DIMS = SWA_HD // 4
N_GROUPS = 4
EXPERTS_PER_GROUP = 8
D_EXPERT = 256
EPS = 1e-6

LANES = 128
VMEM_LIMIT = 52 * 1024 * 1024

GLA_QK = GLA_HEADS * GLA_DK
GLA_W = GLA_HEADS * GLA_DV
SWA_W = SWA_HEADS * SWA_HD
SWA_KV = SWA_KV_HEADS * SWA_HD
SWA_HEAD_ORDER = (0, 4, 1, 5, 2, 6, 3, 7)
ROUTE_ROWS = 8
ROUTER_ROWS = 128
EXPERT_BITS = 3
N_BUCKETS = N_GROUPS << (2 * EXPERT_BITS)
N_PAIRS = N_GROUPS * EXPERTS_PER_GROUP * (EXPERTS_PER_GROUP - 1) // 2
MOE_TM = 128

TN = (((0,), (0,)), ((), ()))
NT = (((1,), (1,)), ((), ()))


def _cparams(n_axes):
    return pltpu.CompilerParams(dimension_semantics=("arbitrary",) * n_axes, vmem_limit_bytes=VMEM_LIMIT)


def _split_bf16(v):
    hi = v.astype(bf16)
    lo = (v - hi.astype(f32)).astype(bf16)
    return hi, lo


def _split_stack_bf16(v):
    hi = v.astype(bf16).astype(f32)
    return jnp.concatenate([hi, v - hi], axis=0).astype(bf16)


def _rmsnorm_rows(v):
    return v * lax.rsqrt(jnp.mean(v * v, axis=-1, keepdims=True) + EPS)


def _silu(v):
    return v * jax.nn.sigmoid(v)


def _adaln_kernel(c_ref, w_ref, b_ref, o_ref):
    ca = _silu(c_ref[...])
    o_ref[...] = jnp.dot(ca, w_ref[...], precision=lax.Precision.HIGHEST, preferred_element_type=f32) + b_ref[...]


def _adaln(c, w, b):
    bsz, d = c.shape
    n = w.shape[1]
    return pl.pallas_call(
        _adaln_kernel,
        grid=(n // d,),
        in_specs=[pl.BlockSpec((bsz, d), lambda j: (0, 0)),
                  pl.BlockSpec((d, d), lambda j: (0, j)),
                  pl.BlockSpec((1, d), lambda j: (0, j))],
        out_specs=pl.BlockSpec((bsz, d), lambda j: (0, j)),
        out_shape=jax.ShapeDtypeStruct((bsz, n), f32),
        compiler_params=_cparams(1),
        name="adaln",
    )(c, w, b.reshape(1, n))


def _rope_tab_kernel(pos_ref, invf_ref, o_ref):
    half = ROPE_DIMS // 2
    for b in range(pos_ref.shape[0]):
        ang = pos_ref[b:b + 1, :].astype(f32) * invf_ref[...]
        o_ref[b, 0:half, :] = jnp.cos(ang)
        o_ref[b, half:2 * half, :] = jnp.sin(ang)


def _rope_tab(positions):
    bsz, s = positions.shape
    half = ROPE_DIMS // 2
    inv_freq = (np.float32(ROPE_THETA) ** (-np.arange(0, ROPE_DIMS, 2, dtype=np.float32) / np.float32(ROPE_DIMS)))
    invf = jnp.asarray(inv_freq.astype(np.float32).reshape(half, 1))
    return pl.pallas_call(
        _rope_tab_kernel,
        out_shape=jax.ShapeDtypeStruct((bsz, 2 * half, s), f32),
        name="rope_tab",
    )(positions, invf)


def _rope_expand_matrix():
    half = ROPE_DIMS // 2
    e = np.zeros((2 * half, 3 * LANES), np.float32)
    for j in range(LANES):
        jj = j % SWA_HD
        if jj < half:
            e[jj, j] = 1.0
            e[half + jj, LANES + j] = -1.0
        elif jj < 2 * half:
            e[jj - half, j] = 1.0
            e[half + jj - half, 2 * LANES + j] = 1.0
    return jnp.asarray(np.concatenate([e, e], axis=0), dtype=bf16)


IN_QK0, IN_VG0, IN_SQ0, IN_SKV0, IN_A0, IN_END = 0, 512, 1536, 2048, 2304, 2432


def _inproj_kernel(x_ref, mod_ref, g_ref, w_ref, tab_ref, e_ref, wa_hi_ref, wa_lo_ref, ba_ref,
                   qk_ref, vg_ref, la_ref, sq_ref, skv_ref):
    x = x_ref[...]
    h = (_rmsnorm_rows(x) * g_ref[...]) * (1.0 + mod_ref[1:2, :]) + mod_ref[0:1, :]
    hb = h.astype(bf16)

    def proj(lo, hi):
        return jnp.dot(hb, w_ref[:, lo:hi], preferred_element_type=f32)

    qk = proj(IN_QK0, IN_VG0)
    lane = lax.broadcasted_iota(jnp.int32, qk.shape, 1)
    qk_ref[...] = jnp.where(lane < GLA_QK, qk * (GLA_DK ** -0.5), qk).astype(bf16)
    vg_ref[...] = proj(IN_VG0, IN_SQ0).astype(bf16)

    al_hi, al_lo = _split_bf16(proj(IN_A0, IN_END))
    z = (jnp.dot(al_hi, wa_hi_ref[...], preferred_element_type=f32)
         + jnp.dot(al_lo, wa_hi_ref[...], preferred_element_type=f32)
         + jnp.dot(al_hi, wa_lo_ref[...], preferred_element_type=f32)) + ba_ref[...]
    la_ref[...] = (jnp.minimum(z, 0.0) - jnp.log1p(jnp.exp(-jnp.abs(z)))) * (1.0 / GLA_GATE_NORM)

    tabs = lax.dot_general(_split_stack_bf16(tab_ref[...]), e_ref[...], TN, preferred_element_type=f32)
    lane1 = lax.broadcasted_iota(jnp.int32, (1, LANES), 1)
    cos_t = tabs[:, 0:LANES] + jnp.where((lane1 & (SWA_HD - 1)) < ROPE_DIMS, 0.0, 1.0)
    sa_t = tabs[:, LANES:2 * LANES]
    sb_t = tabs[:, 2 * LANES:3 * LANES]

    def rope(v):
        return (v * cos_t + pltpu.roll(v, LANES - ROPE_DIMS // 2, 1) * sa_t
                + pltpu.roll(v, ROPE_DIMS // 2, 1) * sb_t)

    sq = proj(IN_SQ0, IN_SKV0)
    for p in range(SWA_W // LANES):
        sq_ref[:, p * LANES:(p + 1) * LANES] = (rope(sq[:, p * LANES:(p + 1) * LANES]) * (SWA_HD ** -0.5)).astype(bf16)
    skv = proj(IN_SKV0, IN_A0)
    skv_ref[:, 0:LANES] = rope(skv[:, 0:LANES]).astype(bf16)
    skv_ref[:, LANES:2 * LANES] = skv[:, LANES:2 * LANES].astype(bf16)


def _inproj(x2, mod3, g, w, tab, e_mat, wa_hi, wa_lo, ba, seq, tm):
    t, d = x2.shape
    per_b = seq // tm
    row = lambda i: (i, 0)
    const = lambda i: (0, 0)
    return pl.pallas_call(
        _inproj_kernel,
        grid=(t // tm,),
        in_specs=[pl.BlockSpec((tm, d), row),
                  pl.BlockSpec((None, 6, d), lambda i: (i // per_b, 0, 0)),
                  pl.BlockSpec((1, d), const),
                  pl.BlockSpec(w.shape, const),
                  pl.BlockSpec((None, ROPE_DIMS, tm), lambda i: (i // per_b, 0, i % per_b)),
                  pl.BlockSpec(e_mat.shape, const),
                  pl.BlockSpec(wa_hi.shape, const),
                  pl.BlockSpec(wa_lo.shape, const),
                  pl.BlockSpec(ba.shape, const)],
        out_specs=[pl.BlockSpec((tm, 2 * GLA_QK), row),
                   pl.BlockSpec((tm, 2 * GLA_W), row),
                   pl.BlockSpec((tm, GLA_QK), row),
                   pl.BlockSpec((tm, SWA_W), row),
                   pl.BlockSpec((tm, 2 * SWA_KV), row)],
        out_shape=[jax.ShapeDtypeStruct((t, 2 * GLA_QK), bf16),
                   jax.ShapeDtypeStruct((t, 2 * GLA_W), bf16),
                   jax.ShapeDtypeStruct((t, GLA_QK), f32),
                   jax.ShapeDtypeStruct((t, SWA_W), bf16),
                   jax.ShapeDtypeStruct((t, 2 * SWA_KV), bf16)],
        compiler_params=_cparams(1),
        name="inproj",
    )(x2, mod3, g, w, tab, e_mat, wa_hi, wa_lo, ba)


def _gla_kernel(qk_ref, vg_ref, la_ref, gn_ref, o_ref, st_ref, *, chunks):
    c_len = GLA_CHUNK

    @pl.when(pl.program_id(1) == 0)
    def _():
        st_ref[...] = jnp.zeros_like(st_ref)

    r_i = lax.broadcasted_iota(jnp.int32, (c_len, c_len), 0)
    c_i = lax.broadcasted_iota(jnp.int32, (c_len, c_len), 1)
    causal = r_i >= c_i
    tri = jnp.where(causal, 1.0, 0.0).astype(bf16)
    ones = jnp.ones((c_len, LANES), bf16)
    lane = lax.broadcasted_iota(jnp.int32, (c_len, LANES), 1)
    first = lane < GLA_DK

    for c in range(chunks):
        rows = slice(c * c_len, (c + 1) * c_len)
        la_hi, la_lo = _split_bf16(la_ref[rows, :])
        b = jnp.dot(tri, la_hi, preferred_element_type=f32) + jnp.dot(tri, la_lo, preferred_element_type=f32)
        b_tot_t = (lax.dot_general(la_hi, ones, TN, preferred_element_type=f32)
                   + lax.dot_general(la_lo, ones, TN, preferred_element_type=f32))
        b_last = b[c_len - 1:c_len, :]
        q = qk_ref[rows, 0:GLA_QK].astype(f32)
        k = qk_ref[rows, GLA_QK:2 * GLA_QK].astype(f32)
        q_dec = q * jnp.exp(b)
        k_dec = (k * jnp.exp(-b)).astype(bf16)
        k_rem = (k * jnp.exp(b_last - b)).astype(bf16)
        decay = jnp.exp(b_tot_t)

        for p in range(GLA_HEADS // 2):
            ls = slice(p * LANES, (p + 1) * LANES)
            ss = slice(p * LANES, (p + 1) * LANES)
            s_prev = st_ref[ss, :]
            s_prev_b = s_prev.astype(bf16)
            qd = q_dec[:, ls]
            kv_halves = []
            for hh in range(2):
                h = 2 * p + hh
                vs = slice(h * GLA_DV, (h + 1) * GLA_DV)
                qm = jnp.where(first if hh == 0 else ~first, qd, 0.0).astype(bf16)
                v = vg_ref[rows, vs]
                scores = lax.dot_general(qm, k_dec[:, ls], NT, preferred_element_type=f32)
                scores = jnp.where(causal, scores, 0.0).astype(bf16)
                o = (jnp.dot(scores, v, preferred_element_type=f32)
                     + jnp.dot(qm, s_prev_b, preferred_element_type=f32))
                o = _rmsnorm_rows(o) * gn_ref[:, vs]
                gate = vg_ref[rows, GLA_W + h * GLA_DV:GLA_W + (h + 1) * GLA_DV].astype(f32)
                o_ref[rows, vs] = (o * _silu(gate)).astype(bf16)
                kv = lax.dot_general(k_rem[:, ls], v, TN, preferred_element_type=f32)
                kv_halves.append(kv[hh * GLA_DK:(hh + 1) * GLA_DK, :])
            st_ref[ss, :] = decay[ss, :] * s_prev + jnp.concatenate(kv_halves, axis=0)


def _gla(qk, vg, la, gn, bsz, seq, tc):
    t = qk.shape[0]
    nt = seq // tc
    row = lambda b, j: (b * nt + j, 0)
    return pl.pallas_call(
        functools.partial(_gla_kernel, chunks=tc // GLA_CHUNK),
        grid=(bsz, nt),
        in_specs=[pl.BlockSpec((tc, 2 * GLA_QK), row),
                  pl.BlockSpec((tc, 2 * GLA_W), row),
                  pl.BlockSpec((tc, GLA_QK), row),
                  pl.BlockSpec((1, GLA_W), lambda b, j: (0, 0))],
        out_specs=pl.BlockSpec((tc, GLA_W), row),
        out_shape=jax.ShapeDtypeStruct((t, GLA_W), bf16),
        scratch_shapes=[pltpu.VMEM((GLA_HEADS * GLA_DK, GLA_DV), f32)],
        compiler_params=_cparams(2),
        name="gla",
    )(qk, vg, la, gn)


def _swa_kernel(sink_ref, q_ref, kvp_ref, kvc_ref, o_ref):
    w = WINDOW
    j = pl.program_id(1)
    k2 = jnp.concatenate([kvp_ref[:, 0:LANES], kvc_ref[:, 0:LANES]], axis=0)
    v2 = jnp.concatenate([kvp_ref[:, LANES:2 * LANES], kvc_ref[:, LANES:2 * LANES]], axis=0)
    t_i = lax.broadcasted_iota(jnp.int32, (w, 2 * w), 0)
    s_i = lax.broadcasted_iota(jnp.int32, (w, 2 * w), 1)
    rel = t_i + w - s_i
    valid = (rel >= 0) & (rel < w) & ((j > 0) | (s_i >= w))
    lane = lax.broadcasted_iota(jnp.int32, (w, LANES), 1)
    first = lane < SWA_HD
    for p in range(SWA_W // LANES):
        qp = q_ref[:, p * LANES:(p + 1) * LANES].astype(f32)
        halves = []
        for hh in range(2):
            head = SWA_HEAD_ORDER[2 * p + hh]
            qm = jnp.where(first if hh == 0 else ~first, qp, 0.0).astype(bf16)
            s = lax.dot_general(qm, k2, NT, preferred_element_type=f32)
            s = jnp.where(valid, s, -jnp.inf)
            sink = sink_ref[0, head]
            m = jnp.maximum(jnp.max(s, axis=-1, keepdims=True), sink)
            pr = jnp.exp(s - m)
            denom = jnp.sum(pr, axis=-1, keepdims=True) + jnp.exp(sink - m)
            o = jnp.dot(pr.astype(bf16), v2, preferred_element_type=f32)
            halves.append(o / denom)
        o_ref[:, p * LANES:(p + 1) * LANES] = jnp.where(first, halves[0], halves[1]).astype(bf16)


def _swa(sinks, sq, skv, bsz, seq):
    t = sq.shape[0]
    nb = seq // WINDOW
    return pl.pallas_call(
        _swa_kernel,
        grid=(bsz, nb),
        in_specs=[pl.BlockSpec(memory_space=pltpu.SMEM),
                  pl.BlockSpec((WINDOW, SWA_W), lambda b, j: (b * nb + j, 0)),
                  pl.BlockSpec((WINDOW, 2 * SWA_KV), lambda b, j: (b * nb + jnp.maximum(j - 1, 0), 0)),
                  pl.BlockSpec((WINDOW, 2 * SWA_KV), lambda b, j: (b * nb + j, 0))],
        out_specs=pl.BlockSpec((WINDOW, SWA_W), lambda b, j: (b * nb + j, 0)),
        out_shape=jax.ShapeDtypeStruct((t, SWA_W), bf16),
        compiler_params=_cparams(2),
        name="swa",
    )(sinks, sq, skv, skv)


def _to_slabs(v):
    return v.reshape(v.shape[0], v.shape[1] // LANES, LANES).astype(bf16)


def _slab_rows(ref, first, count, tm, pitch):
    return jnp.concatenate([ref[pl.ds(first + c, tm, stride=pitch), :] for c in range(count)], axis=1)


def _bf16_pieces(v):
    p1 = v.astype(bf16).astype(f32)
    p2 = (v - p1).astype(bf16).astype(f32)
    return p1, p2, v - p1 - p2


def _outproj_kernel(og_ref, os_ref, x_ref, mod_ref, wo_ref, g2_ref, wr_hi_ref, wr_lo_ref, br_ref, eye_ref,
                    x1_ref, hrow_ref, route_ref, cnt_ref, run_ref):
    @pl.when(pl.program_id(0) == 0)
    def _():
        run_ref[...] = jnp.zeros_like(run_ref)

    mix = (jnp.dot(og_ref[...], wo_ref[0:GLA_W, :], preferred_element_type=f32)
           + jnp.dot(os_ref[...], wo_ref[GLA_W:GLA_W + SWA_W, :], preferred_element_type=f32))
    x1 = x_ref[...] + mod_ref[2:3, :] * mix
    x1_ref[...] = x1
    h = (_rmsnorm_rows(x1) * g2_ref[...]) * (1.0 + mod_ref[4:5, :]) + mod_ref[3:4, :]
    h_hi, h_lo = _split_bf16(h)
    n_slab = h.shape[1] // LANES
    hrow_ref[:, 0:n_slab, :] = _to_slabs(h)

    lt = (lax.dot_general(wr_hi_ref[...], h_hi, NT, preferred_element_type=f32)
          + lax.dot_general(wr_hi_ref[...], h_lo, NT, preferred_element_type=f32)
          + lax.dot_general(wr_lo_ref[...], h_hi, NT, preferred_element_type=f32)) + br_ref[...]
    tm = lt.shape[1]
    e_g = EXPERTS_PER_GROUP
    row = lax.broadcasted_iota(jnp.int32, (e_g, tm), 0)
    neg = -jnp.inf
    gl = jnp.where(row < N_GROUPS, lt[0:e_g, :], neg)
    g_max = jnp.max(gl, axis=0, keepdims=True)
    g_gate = 1.0 / jnp.sum(jnp.exp(gl - g_max), axis=0, keepdims=True)
    g_idx = jnp.min(jnp.where(gl == g_max, row, e_g), axis=0, keepdims=True)
    sel = lt[e_g * N_GROUPS:e_g * (N_GROUPS + 1), :]
    for g in range(N_GROUPS - 2, -1, -1):
        sel = jnp.where(g_idx == g, lt[e_g * (g + 1):e_g * (g + 2), :], sel)
    t1 = jnp.max(sel, axis=0, keepdims=True)
    i1 = jnp.min(jnp.where(sel == t1, row, e_g), axis=0, keepdims=True)
    sel2 = jnp.where(row == i1, neg, sel)
    t2 = jnp.max(sel2, axis=0, keepdims=True)
    i2 = jnp.min(jnp.where(sel2 == t2, row, e_g), axis=0, keepdims=True)
    ex = jnp.exp(t2 - t1)
    w1 = g_gate / (1.0 + ex)
    w2 = g_gate * ex / (1.0 + ex)
    first_lo = i1 < i2
    w_lo = jnp.where(first_lo, w1, w2)
    w_hi = jnp.where(first_lo, w2, w1)
    bucket = ((g_idx << (2 * EXPERT_BITS)) | (jnp.minimum(i1, i2) << EXPERT_BITS) | jnp.maximum(i1, i2))
    onehot = lax.broadcasted_iota(jnp.int32, (N_BUCKETS, tm), 0) == bucket
    oh_b = jnp.where(onehot, 1.0, 0.0).astype(bf16)
    t_r = lax.broadcasted_iota(jnp.int32, (tm, tm), 0)
    t_c = lax.broadcasted_iota(jnp.int32, (tm, tm), 1)
    earlier = jnp.where(t_r < t_c, 1.0, 0.0).astype(bf16)
    run = run_ref[...]
    prefix = jnp.dot(oh_b, earlier, preferred_element_type=f32) + run
    rank = jnp.sum(jnp.where(onehot, prefix, 0.0), axis=0, keepdims=True)
    run = run + jnp.dot(oh_b, jnp.ones((tm, tm), bf16), preferred_element_type=f32)
    run_ref[...] = run
    cnt_ref[...] = run[:, 0:LANES]
    route_ref[...] = jnp.where(row == 0, bucket.astype(f32), jnp.where(row == 1, rank, jnp.where(
        row == 2, w_lo, jnp.where(row == 3, w_hi, 0.0))))

    r_i = lax.broadcasted_iota(jnp.int32, (eye_ref.shape[0], tm), 0)
    stack = jnp.zeros(r_i.shape, f32)
    for k, piece in enumerate(_bf16_pieces(w_lo) + _bf16_pieces(w_hi)):
        stack = jnp.where(r_i == k, piece, stack)
    w_rows = lax.dot_general(stack.astype(bf16), eye_ref[...], TN, preferred_element_type=f32)
    hrow_ref[:, n_slab:2 * n_slab, :] = jnp.broadcast_to(w_rows[:, None, :], (tm, n_slab, LANES)).astype(bf16)


def _outproj(og, osw, x2, mod3, wo, g2, wr_hi, wr_lo, br, seq, tm):
    t, d = x2.shape
    eye = jnp.asarray(np.eye(2 * EXPERTS_PER_GROUP, LANES, dtype=np.float32), dtype=bf16)
    per_b = seq // tm
    row = lambda i: (i, 0)
    const = lambda i: (0, 0)
    return pl.pallas_call(
        _outproj_kernel,
        grid=(t // tm,),
        in_specs=[pl.BlockSpec((tm, GLA_W), row),
                  pl.BlockSpec((tm, SWA_W), row),
                  pl.BlockSpec((tm, d), row),
                  pl.BlockSpec((None, 6, d), lambda i: (i // per_b, 0, 0)),
                  pl.BlockSpec(wo.shape, const),
                  pl.BlockSpec((1, d), const),
                  pl.BlockSpec(wr_hi.shape, const),
                  pl.BlockSpec(wr_lo.shape, const),
                  pl.BlockSpec(br.shape, const),
                  pl.BlockSpec(eye.shape, const)],
        out_specs=[pl.BlockSpec((tm, d), row),
                   pl.BlockSpec((tm, 2 * d // LANES, LANES), lambda i: (i, 0, 0)),
                   pl.BlockSpec((ROUTE_ROWS, tm), lambda i: (0, i)),
                   pl.BlockSpec((N_BUCKETS, LANES), const)],
        out_shape=[jax.ShapeDtypeStruct((t, d), f32),
                   jax.ShapeDtypeStruct((t, 2 * d // LANES, LANES), bf16),
                   jax.ShapeDtypeStruct((ROUTE_ROWS, t), f32),
                   jax.ShapeDtypeStruct((N_BUCKETS, LANES), f32)],
        scratch_shapes=[pltpu.VMEM((N_BUCKETS, tm), f32)],
        compiler_params=_cparams(1),
        name="outproj",
    )(og, osw, x2, mod3, wo, g2, wr_hi, wr_lo, br, eye)


def _plan_kernel(cnt_ref, route_ref, dest_ref, tb_ref, nt_ref):
    nb = N_BUCKETS
    tiles = jnp.floor((cnt_ref[...] + (MOE_TM - 1)) * (1.0 / MOE_TM))
    b_r = lax.broadcasted_iota(jnp.int32, (nb, nb), 0)
    b_c = lax.broadcasted_iota(jnp.int32, (nb, nb), 1)
    before = jnp.where(b_c < b_r, 1.0, 0.0).astype(bf16)
    t_start = jnp.dot(before, tiles.astype(bf16), preferred_element_type=f32)
    t_end = t_start + tiles
    tile_i = lax.broadcasted_iota(jnp.int32, (nb, 2 * LANES), 1).astype(f32)
    ended = jnp.where(jnp.concatenate([t_end, t_end], axis=1) <= tile_i, 1.0, 0.0)
    tb_ref[...] = jnp.sum(ended, axis=0, keepdims=True).astype(jnp.int32)
    nt_ref[...] = t_end[nb - 1:nb, :].astype(jnp.int32)
    tl = route_ref.shape[1]
    onehot = lax.broadcasted_iota(jnp.int32, (nb, tl), 0) == route_ref[0:1, :].astype(jnp.int32)
    start = lax.dot_general(t_start.astype(bf16), jnp.where(onehot, 1.0, 0.0).astype(bf16), TN,
                            preferred_element_type=f32)
    dest_ref[...] = (start[0:1, :] * MOE_TM + route_ref[1:2, :]).astype(jnp.int32)


def _plan(cnt, route, tl):
    t = route.shape[1]
    return pl.pallas_call(
        _plan_kernel,
        grid=(t // tl,),
        in_specs=[pl.BlockSpec(cnt.shape, lambda i: (0, 0)),
                  pl.BlockSpec((ROUTE_ROWS, tl), lambda i: (0, i))],
        out_specs=[pl.BlockSpec((1, tl), lambda i: (0, i)),
                   pl.BlockSpec((1, 2 * LANES), lambda i: (0, 0)),
                   pl.BlockSpec((1, LANES), lambda i: (0, 0))],
        out_shape=[jax.ShapeDtypeStruct((1, t), jnp.int32),
                   jax.ShapeDtypeStruct((1, 2 * LANES), jnp.int32),
                   jax.ShapeDtypeStruct((1, LANES), jnp.int32)],
        compiler_params=_cparams(1),
        name="plan",
    )(cnt, route)


def _dispatch_kernel(dest_ref, tb_ref, nt_ref, src_ref, out_ref, zero_ref, sem_z, sem_r, *, rows, n_tiles_max):
    i = pl.program_id(0)

    def last_tile(k):
        return (k >= nt_ref[0]) | (tb_ref[k] != tb_ref[k + 1])

    def zero_copy(k):
        return pltpu.make_async_copy(zero_ref, out_ref.at[pl.ds(k * MOE_TM, MOE_TM)], sem_z)

    @pl.when(i == 0)
    def _():
        zero_ref[...] = jnp.zeros_like(zero_ref)

        def start(k, c):
            @pl.when(last_tile(k))
            def _():
                zero_copy(k).start()
            return c

        def wait(k, c):
            @pl.when(last_tile(k))
            def _():
                zero_copy(k).wait()
            return c

        lax.fori_loop(0, n_tiles_max, start, 0)
        lax.fori_loop(0, n_tiles_max, wait, 0)

    base = i * rows

    def row_copy(r):
        return pltpu.make_async_copy(src_ref.at[r], out_ref.at[dest_ref[base + r]], sem_r)

    def issue(r, c):
        row_copy(r).start()
        return c

    def drain(r, c):
        row_copy(r).wait()
        return c

    lax.fori_loop(0, rows, issue, 0, unroll=8)
    lax.fori_loop(0, rows, drain, 0, unroll=8)


def _dispatch(dest, tb, nt, hrow, n_tiles_max, rows):
    t = hrow.shape[0]
    slab = hrow.shape[1:]
    grid_spec = pltpu.PrefetchScalarGridSpec(
        num_scalar_prefetch=3, grid=(t // rows,),
        in_specs=[pl.BlockSpec((rows,) + slab, lambda i, dest, tb, nt: (i, 0, 0))],
        out_specs=pl.BlockSpec(memory_space=pl.ANY),
        scratch_shapes=[pltpu.VMEM((MOE_TM,) + slab, hrow.dtype), pltpu.SemaphoreType.DMA(()),
                        pltpu.SemaphoreType.DMA(())])
    return pl.pallas_call(
        functools.partial(_dispatch_kernel, rows=rows, n_tiles_max=n_tiles_max),
        grid_spec=grid_spec,
        out_shape=jax.ShapeDtypeStruct((n_tiles_max * MOE_TM,) + slab, hrow.dtype),
        compiler_params=_cparams(1),
        name="dispatch",
    )(dest, tb, nt, hrow)


def _moe_kernel(tb_ref, nt_ref, hs_ref, wg_ref, wu_ref, wd_ref, y_ref, h32_ref, y32_ref):
    i = pl.program_id(0)

    @pl.when(i < nt_ref[0])
    def _():
        e_g = EXPERTS_PER_GROUP
        bucket = tb_ref[i]
        tm, pitch = hs_ref.shape[0], hs_ref.shape[1]
        n_slab = pitch // 2
        h32_ref[...] = hs_ref[...].astype(f32).reshape(tm * pitch, LANES)
        h = _slab_rows(h32_ref, 0, n_slab, tm, pitch).astype(bf16)
        w_rows = h32_ref[pl.ds(n_slab, tm, stride=pitch), :]
        weights = (w_rows[:, 0:1] + w_rows[:, 1:2] + w_rows[:, 2:3], w_rows[:, 3:4] + w_rows[:, 4:5] + w_rows[:, 5:6])
        experts = ((bucket >> EXPERT_BITS) & (e_g - 1), bucket & (e_g - 1))
        y = jnp.zeros((tm, n_slab * LANES), f32)
        for e, w in zip(experts, weights):
            a = jnp.dot(h, wg_ref[e], preferred_element_type=f32)
            u = jnp.dot(h, wu_ref[e], preferred_element_type=f32)
            y = y + w * jnp.dot((_silu(a) * u).astype(bf16), wd_ref[e], preferred_element_type=f32)
        for c in range(n_slab):
            y32_ref[pl.ds(c, tm, stride=n_slab), :] = y[:, c * LANES:(c + 1) * LANES]
        y_ref[...] = y32_ref[...].reshape(tm, n_slab, LANES).astype(bf16)

    @pl.when(i >= nt_ref[0])
    def _():
        y_ref[...] = jnp.zeros_like(y_ref)


def _moe(tb, nt, hs, wg, wu, wd):
    n_tiles_max = hs.shape[0] // MOE_TM
    e_g = EXPERTS_PER_GROUP
    d = wg.shape[2]
    pitch = hs.shape[1]
    n_slab = d // LANES
    last = lambda i, tb, nt: jnp.minimum(i, nt[0] - 1)
    group = lambda i, tb, nt: (tb[last(i, tb, nt)] >> (2 * EXPERT_BITS), 0, 0, 0)
    grid_spec = pltpu.PrefetchScalarGridSpec(
        num_scalar_prefetch=2, grid=(n_tiles_max,),
        in_specs=[pl.BlockSpec((MOE_TM, pitch, LANES), lambda i, tb, nt: (last(i, tb, nt), 0, 0)),
                  pl.BlockSpec((None, e_g, d, D_EXPERT), group),
                  pl.BlockSpec((None, e_g, d, D_EXPERT), group),
                  pl.BlockSpec((None, e_g, D_EXPERT, d), group)],
        out_specs=pl.BlockSpec((MOE_TM, n_slab, LANES), lambda i, tb, nt: (i, 0, 0)),
        scratch_shapes=[pltpu.VMEM((MOE_TM * pitch, LANES), f32), pltpu.VMEM((MOE_TM * n_slab, LANES), f32)])
    return pl.pallas_call(
        _moe_kernel,
        grid_spec=grid_spec,
        out_shape=jax.ShapeDtypeStruct((hs.shape[0], n_slab, LANES), bf16),
        compiler_params=_cparams(1),
        name="moe",
    )(tb, nt, hs, wg, wu, wd)


def _final_kernel(dest_ref, x1_ref, mod_ref, fg_ref, ys_ref, o_ref, ybuf_ref, y32_ref, sem):
    i = pl.program_id(0)
    tm = x1_ref.shape[0]

    def row_copy(tile, slot, r):
        return pltpu.make_async_copy(ys_ref.at[dest_ref[tile * tm + r]], ybuf_ref.at[slot, r], sem.at[slot])

    def gather(tile, slot):
        def issue(r, c):
            row_copy(tile, slot, r).start()
            return c
        lax.fori_loop(0, tm, issue, 0, unroll=8)

    @pl.when(i == 0)
    def _():
        gather(0, 0)

    @pl.when(i + 1 < pl.num_programs(0))
    def _():
        gather(i + 1, (i + 1) % 2)

    slot = i % 2

    def drain(r, c):
        row_copy(i, slot, r).wait()
        return c

    lax.fori_loop(0, tm, drain, 0, unroll=8)

    n_slab = ybuf_ref.shape[2]
    y32_ref[...] = ybuf_ref[slot].astype(f32).reshape(tm * n_slab, LANES)
    x2 = x1_ref[...] + mod_ref[5:6, :] * _slab_rows(y32_ref, 0, n_slab, tm, n_slab)
    o_ref[...] = _rmsnorm_rows(x2) * fg_ref[...]


def _final(dest, x1, mod3, fg, ys, seq, tm):
    t, d = x1.shape
    per_b = seq // tm
    grid_spec = pltpu.PrefetchScalarGridSpec(
        num_scalar_prefetch=1, grid=(t // tm,),
        in_specs=[pl.BlockSpec((tm, d), lambda i, dest: (i, 0)),
                  pl.BlockSpec((None, 6, d), lambda i, dest: (i // per_b, 0, 0)),
                  pl.BlockSpec((1, d), lambda i, dest: (0, 0)),
                  pl.BlockSpec(memory_space=pl.ANY)],
        out_specs=pl.BlockSpec((tm, d), lambda i, dest: (i, 0)),
        scratch_shapes=[pltpu.VMEM((2, tm) + ys.shape[1:], ys.dtype), pltpu.VMEM((tm * d // LANES, LANES), f32),
                        pltpu.SemaphoreType.DMA((2,))])
    return pl.pallas_call(
        _final_kernel,
        grid_spec=grid_spec,
        out_shape=jax.ShapeDtypeStruct((t, d), f32),
        compiler_params=_cparams(1),
        name="final",
    )(dest, x1, mod3, fg, ys)


def _prep_w_in(w_in_l):
    sizes = (GLA_QK, GLA_QK, GLA_W, GLA_W, GLA_LOWRANK, SWA_W, SWA_KV, SWA_KV)
    off = np.concatenate([[0], np.cumsum(sizes)])
    gq, gk, gv, gg, ga, sq, sk, sv = [np.arange(off[i], off[i + 1]) for i in range(len(sizes))]
    sq_perm = np.concatenate([sq[h * SWA_HD:(h + 1) * SWA_HD] for h in SWA_HEAD_ORDER])
    cols = np.concatenate([gq, gk, gv, gg, sq_perm, sk, sv, ga])
    w = jnp.take(w_in_l, jnp.asarray(cols), axis=1)
    w = jnp.pad(w, ((0, 0), (0, IN_END - w.shape[1])))
    return w.astype(bf16)


def _prep_w_out(w_out_l):
    rows = np.concatenate([np.arange(GLA_W)]
                          + [GLA_W + np.arange(h * SWA_HD, (h + 1) * SWA_HD) for h in SWA_HEAD_ORDER])
    return jnp.take(w_out_l, jnp.asarray(rows), axis=0).astype(bf16)


def _prep_router(w_grp_l, b_grp_l, w_exp_l, b_exp_l):
    d = w_grp_l.shape[0]
    n_e = N_GROUPS * EXPERTS_PER_GROUP
    wr = jnp.zeros((ROUTER_ROWS, d), f32)
    wr = wr.at[0:N_GROUPS, :].set(w_grp_l.T).at[EXPERTS_PER_GROUP:EXPERTS_PER_GROUP + n_e, :].set(w_exp_l.T)
    br = jnp.zeros((ROUTER_ROWS, 1), f32)
    br = br.at[0:N_GROUPS, 0].set(b_grp_l).at[EXPERTS_PER_GROUP:EXPERTS_PER_GROUP + n_e, 0].set(b_exp_l)
    wr_hi = wr.astype(bf16)
    wr_lo = (wr - wr_hi.astype(f32)).astype(bf16)
    return wr_hi, wr_lo, br


def _layer(x2, mod3, tab, e_mat, bsz, seq, norm_mix_g, w_in, gla_w_a2, gla_b_a2, gla_norm_g, swa_sinks, w_out,
           norm_ffn_g, w_grp, b_grp, w_exp, b_exp, w_gate, w_up, w_down, out_g, tm, tc, tm_moe):
    d = x2.shape[1]
    wa = jnp.pad(gla_w_a2, ((0, LANES - GLA_LOWRANK), (0, 0)))
    wa_hi = wa.astype(bf16)
    wa_lo = (wa - wa_hi.astype(f32)).astype(bf16)
    qk, vg, la, sq, skv = _inproj(x2, mod3, norm_mix_g.reshape(1, d), _prep_w_in(w_in), tab, e_mat,
                                  wa_hi, wa_lo, gla_b_a2.reshape(1, GLA_QK), seq, tm)
    o_gla = _gla(qk, vg, la, gla_norm_g.reshape(1, GLA_W), bsz, seq, tc)
    o_swa = _swa(swa_sinks.reshape(1, SWA_HEADS), sq, skv, bsz, seq)
    wr_hi, wr_lo, br = _prep_router(w_grp, b_grp, w_exp, b_exp)
    x1, hrow, route, cnt = _outproj(o_gla, o_swa, x2, mod3, _prep_w_out(w_out), norm_ffn_g.reshape(1, d),
                                    wr_hi, wr_lo, br, seq, tm)
    t = x2.shape[0]
    n_tiles_max = t // MOE_TM + N_PAIRS
    assert n_tiles_max <= 2 * LANES and t % MOE_TM == 0
    dest, tile_bucket, n_tiles = _plan(cnt, route, min(t, 2048))
    dest = dest.reshape(t)
    tile_bucket = tile_bucket.reshape(2 * LANES)
    n_tiles = n_tiles.reshape(LANES)[0:1]
    hs = _dispatch(dest, tile_bucket, n_tiles, hrow, n_tiles_max, min(t, 1024))
    ys = _moe(tile_bucket, n_tiles, hs, w_gate.astype(bf16), w_up.astype(bf16), w_down.astype(bf16))
    return _final(dest, x1, mod3, out_g.reshape(1, d), ys, seq, tm_moe)


def kernel(x, c, positions, ada_w, ada_b, norm_mix_g, w_in, gla_w_a2, gla_b_a2, gla_norm_g, swa_sinks, w_out,
           norm_ffn_g, w_grp, b_grp, w_exp, b_exp, w_gate, w_up, w_down, final_norm_g):
    bsz, seq, d = x.shape
    depth = ada_w.shape[0]
    assert depth == 1, "the final rmsnorm is fused into the last layer's MoE kernel"
    tab = _rope_tab(positions)
    e_mat = _rope_expand_matrix()
    x2 = x.reshape(bsz * seq, d)
    for l in range(depth):
        mod3 = _adaln(c, ada_w[l], ada_b[l]).reshape(bsz, 6, d)
        x2 = _layer(x2, mod3, tab, e_mat, bsz, seq, norm_mix_g[l], w_in[l], gla_w_a2[l], gla_b_a2[l],
                    gla_norm_g[l], swa_sinks[l], w_out[l], norm_ffn_g[l], w_grp[l], b_grp[l], w_exp[l], b_exp[l],
                    w_gate[l], w_up[l], w_down[l], final_norm_g, tm=512, tc=256, tm_moe=512)
    return x2.reshape(bsz, seq, d)
```

```python
import functools

import numpy as np
import jax
import jax.numpy as jnp
from jax import lax
from jax.experimental import pallas as pl
from jax.experimental.pallas import tpu as pltpu

f32 = jnp.float32
bf16 = jnp.bfloat16

GLA_HEADS = 4
GLA_DK = 64
GLA_DV = 128
GLA_LOWRANK = 16
GLA_GATE_NORM = 16.0
GLA_CHUNK = 64
SWA_HEADS = 8
SWA_KV_HEADS = 2
SWA_HD = 64
WINDOW = 128
ROPE_THETA = 500000.0
ROPE_DIMS = SWA_HD // 4
N_GROUPS = 4
EXPERTS_PER_GROUP = 8
D_EXPERT = 256
EPS = 1e-6

LANES = 128
VMEM_LIMIT = 52 * 1024 * 1024

GLA_QK = GLA_HEADS * GLA_DK
GLA_W = GLA_HEADS * GLA_DV
SWA_W = SWA_HEADS * SWA_HD
SWA_KV = SWA_KV_HEADS * SWA_HD
SWA_HEAD_ORDER = (0, 4, 1, 5, 2, 6, 3, 7)
ROUTE_ROWS = 8
ROUTER_ROWS = 128
EXPERT_BITS = 3
N_BUCKETS = N_GROUPS << (2 * EXPERT_BITS)
N_PAIRS = N_GROUPS * EXPERTS_PER_GROUP * (EXPERTS_PER_GROUP - 1) // 2
MOE_TM = 128
ROW_DMA_UNROLL = 8

TN = (((0,), (0,)), ((), ()))
NT = (((1,), (1,)), ((), ()))


def _cparams(n_axes):
    return pltpu.CompilerParams(dimension_semantics=("arbitrary",) * n_axes, vmem_limit_bytes=VMEM_LIMIT)


def _split_bf16(v):
    hi = v.astype(bf16)
    lo = (v - hi.astype(f32)).astype(bf16)
    return hi, lo


def _split_stack_bf16(v):
    hi = v.astype(bf16).astype(f32)
    return jnp.concatenate([hi, v - hi], axis=0).astype(bf16)


def _rmsnorm_rows(v):
    return v * lax.rsqrt(jnp.mean(v * v, axis=-1, keepdims=True) + EPS)


def _silu(v):
    return v * jax.nn.sigmoid(v)


def _adaln_kernel(c_ref, w_ref, b_ref, o_ref):
    ca = _silu(c_ref[...])
    o_ref[...] = jnp.dot(ca, w_ref[...], precision=lax.Precision.HIGHEST, preferred_element_type=f32) + b_ref[...]


def _adaln(c, w, b):
    bsz, d = c.shape
    n = w.shape[1]
    return pl.pallas_call(
        _adaln_kernel,
        grid=(n // d,),
        in_specs=[pl.BlockSpec((bsz, d), lambda j: (0, 0)),
                  pl.BlockSpec((d, d), lambda j: (0, j)),
                  pl.BlockSpec((1, d), lambda j: (0, j))],
        out_specs=pl.BlockSpec((bsz, d), lambda j: (0, j)),
        out_shape=jax.ShapeDtypeStruct((bsz, n), f32),
        compiler_params=_cparams(1),
        name="adaln",
    )(c, w, b.reshape(1, n))


def _rope_tab_kernel(pos_ref, invf_ref, o_ref):
    half = ROPE_DIMS // 2
    for b in range(pos_ref.shape[0]):
        ang = pos_ref[b:b + 1, :].astype(f32) * invf_ref[...]
        o_ref[b, 0:half, :] = jnp.cos(ang)
        o_ref[b, half:2 * half, :] = jnp.sin(ang)


def _rope_tab(positions):
    bsz, s = positions.shape
    half = ROPE_DIMS // 2
    inv_freq = (np.float32(ROPE_THETA) ** (-np.arange(0, ROPE_DIMS, 2, dtype=np.float32) / np.float32(ROPE_DIMS)))
    invf = jnp.asarray(inv_freq.astype(np.float32).reshape(half, 1))
    return pl.pallas_call(
        _rope_tab_kernel,
        out_shape=jax.ShapeDtypeStruct((bsz, 2 * half, s), f32),
        name="rope_tab",
    )(positions, invf)


def _rope_expand_matrix():
    half = ROPE_DIMS // 2
    e = np.zeros((2 * half, 3 * LANES), np.float32)
    for j in range(LANES):
        jj = j % SWA_HD
        if jj < half:
            e[jj, j] = 1.0
            e[half + jj, LANES + j] = -1.0
        elif jj < 2 * half:
            e[jj - half, j] = 1.0
            e[half + jj - half, 2 * LANES + j] = 1.0
    return jnp.asarray(np.concatenate([e, e], axis=0), dtype=bf16)


IN_QK0, IN_VG0, IN_SQ0, IN_SKV0, IN_A0, IN_END = 0, 512, 1536, 2048, 2304, 2432


def _inproj_kernel(x_ref, mod_ref, g_ref, w_ref, tab_ref, e_ref, wa_hi_ref, wa_lo_ref, ba_ref,
                   qk_ref, vg_ref, la_ref, sq_ref, skv_ref):
    x = x_ref[...]
    h = (_rmsnorm_rows(x) * g_ref[...]) * (1.0 + mod_ref[1:2, :]) + mod_ref[0:1, :]
    hb = h.astype(bf16)

    def proj(lo, hi):
        return jnp.dot(hb, w_ref[:, lo:hi], preferred_element_type=f32)

    qk = proj(IN_QK0, IN_VG0)
    lane = lax.broadcasted_iota(jnp.int32, qk.shape, 1)
    qk_ref[...] = jnp.where(lane < GLA_QK, qk * (GLA_DK ** -0.5), qk).astype(bf16)
    vg_ref[...] = proj(IN_VG0, IN_SQ0).astype(bf16)

    al_hi, al_lo = _split_bf16(proj(IN_A0, IN_END))
    z = (jnp.dot(al_hi, wa_hi_ref[...], preferred_element_type=f32)
         + jnp.dot(al_lo, wa_hi_ref[...], preferred_element_type=f32)
         + jnp.dot(al_hi, wa_lo_ref[...], preferred_element_type=f32)) + ba_ref[...]
    la_ref[...] = (jnp.minimum(z, 0.0) - jnp.log1p(jnp.exp(-jnp.abs(z)))) * (1.0 / GLA_GATE_NORM)

    tabs = lax.dot_general(_split_stack_bf16(tab_ref[...]), e_ref[...], TN, preferred_element_type=f32)
    lane1 = lax.broadcasted_iota(jnp.int32, (1, LANES), 1)
    cos_t = tabs[:, 0:LANES] + jnp.where((lane1 & (SWA_HD - 1)) < ROPE_DIMS, 0.0, 1.0)
    sa_t = tabs[:, LANES:2 * LANES]
    sb_t = tabs[:, 2 * LANES:3 * LANES]

    def rope(v):
        return (v * cos_t + pltpu.roll(v, LANES - ROPE_DIMS // 2, 1) * sa_t
                + pltpu.roll(v, ROPE_DIMS // 2, 1) * sb_t)

    sq = proj(IN_SQ0, IN_SKV0)
    for p in range(SWA_W // LANES):
        sq_ref[:, p * LANES:(p + 1) * LANES] = (rope(sq[:, p * LANES:(p + 1) * LANES]) * (SWA_HD ** -0.5)).astype(bf16)
    skv = proj(IN_SKV0, IN_A0)
    skv_ref[:, 0:LANES] = rope(skv[:, 0:LANES]).astype(bf16)
    skv_ref[:, LANES:2 * LANES] = skv[:, LANES:2 * LANES].astype(bf16)


def _inproj(x2, mod3, g, w, tab, e_mat, wa_hi, wa_lo, ba, seq, tm):
    t, d = x2.shape
    per_b = seq // tm
    row = lambda i: (i, 0)
    const = lambda i: (0, 0)
    return pl.pallas_call(
        _inproj_kernel,
        grid=(t // tm,),
        in_specs=[pl.BlockSpec((tm, d), row),
                  pl.BlockSpec((None, 6, d), lambda i: (i // per_b, 0, 0)),
                  pl.BlockSpec((1, d), const),
                  pl.BlockSpec(w.shape, const),
                  pl.BlockSpec((None, ROPE_DIMS, tm), lambda i: (i // per_b, 0, i % per_b)),
                  pl.BlockSpec(e_mat.shape, const),
                  pl.BlockSpec(wa_hi.shape, const),
                  pl.BlockSpec(wa_lo.shape, const),
                  pl.BlockSpec(ba.shape, const)],
        out_specs=[pl.BlockSpec((tm, 2 * GLA_QK), row),
                   pl.BlockSpec((tm, 2 * GLA_W), row),
                   pl.BlockSpec((tm, GLA_QK), row),
                   pl.BlockSpec((tm, SWA_W), row),
                   pl.BlockSpec((tm, 2 * SWA_KV), row)],
        out_shape=[jax.ShapeDtypeStruct((t, 2 * GLA_QK), bf16),
                   jax.ShapeDtypeStruct((t, 2 * GLA_W), bf16),
                   jax.ShapeDtypeStruct((t, GLA_QK), f32),
                   jax.ShapeDtypeStruct((t, SWA_W), bf16),
                   jax.ShapeDtypeStruct((t, 2 * SWA_KV), bf16)],
        compiler_params=_cparams(1),
        name="inproj",
    )(x2, mod3, g, w, tab, e_mat, wa_hi, wa_lo, ba)


def _gla_kernel(qk_ref, vg_ref, la_ref, gn_ref, o_ref, st_ref, *, chunks):
    c_len = GLA_CHUNK

    @pl.when(pl.program_id(1) == 0)
    def _():
        st_ref[...] = jnp.zeros_like(st_ref)

    r_i = lax.broadcasted_iota(jnp.int32, (c_len, c_len), 0)
    c_i = lax.broadcasted_iota(jnp.int32, (c_len, c_len), 1)
    causal = r_i >= c_i
    tri = jnp.where(causal, 1.0, 0.0).astype(bf16)
    ones = jnp.ones((c_len, LANES), bf16)
    lane = lax.broadcasted_iota(jnp.int32, (c_len, LANES), 1)
    first = lane < GLA_DK

    for c in range(chunks):
        rows = slice(c * c_len, (c + 1) * c_len)
        la_hi, la_lo = _split_bf16(la_ref[rows, :])
        b = jnp.dot(tri, la_hi, preferred_element_type=f32) + jnp.dot(tri, la_lo, preferred_element_type=f32)
        b_tot_t = (lax.dot_general(la_hi, ones, TN, preferred_element_type=f32)
                   + lax.dot_general(la_lo, ones, TN, preferred_element_type=f32))
        b_last = b[c_len - 1:c_len, :]
        q = qk_ref[rows, 0:GLA_QK].astype(f32)
        k = qk_ref[rows, GLA_QK:2 * GLA_QK].astype(f32)
        q_dec = q * jnp.exp(b)
        k_dec = (k * jnp.exp(-b)).astype(bf16)
        k_rem = (k * jnp.exp(b_last - b)).astype(bf16)
        decay = jnp.exp(b_tot_t)

        for p in range(GLA_HEADS // 2):
            ls = slice(p * LANES, (p + 1) * LANES)
            ss = slice(p * LANES, (p + 1) * LANES)
            s_prev = st_ref[ss, :]
            s_prev_b = s_prev.astype(bf16)
            qd = q_dec[:, ls]
            kv_halves = []
            for hh in range(2):
                h = 2 * p + hh
                vs = slice(h * GLA_DV, (h + 1) * GLA_DV)
                qm = jnp.where(first if hh == 0 else ~first, qd, 0.0).astype(bf16)
                v = vg_ref[rows, vs]
                scores = lax.dot_general(qm, k_dec[:, ls], NT, preferred_element_type=f32)
                scores = jnp.where(causal, scores, 0.0).astype(bf16)
                o = (jnp.dot(scores, v, preferred_element_type=f32)
                     + jnp.dot(qm, s_prev_b, preferred_element_type=f32))
                o = _rmsnorm_rows(o) * gn_ref[:, vs]
                gate = vg_ref[rows, GLA_W + h * GLA_DV:GLA_W + (h + 1) * GLA_DV].astype(f32)
                o_ref[rows, vs] = (o * _silu(gate)).astype(bf16)
                kv = lax.dot_general(k_rem[:, ls], v, TN, preferred_element_type=f32)
                kv_halves.append(kv[hh * GLA_DK:(hh + 1) * GLA_DK, :])
            st_ref[ss, :] = decay[ss, :] * s_prev + jnp.concatenate(kv_halves, axis=0)


def _gla(qk, vg, la, gn, bsz, seq, tc):
    t = qk.shape[0]
    nt = seq // tc
    row = lambda b, j: (b * nt + j, 0)
    return pl.pallas_call(
        functools.partial(_gla_kernel, chunks=tc // GLA_CHUNK),
        grid=(bsz, nt),
        in_specs=[pl.BlockSpec((tc, 2 * GLA_QK), row),
                  pl.BlockSpec((tc, 2 * GLA_W), row),
                  pl.BlockSpec((tc, GLA_QK), row),
                  pl.BlockSpec((1, GLA_W), lambda b, j: (0, 0))],
        out_specs=pl.BlockSpec((tc, GLA_W), row),
        out_shape=jax.ShapeDtypeStruct((t, GLA_W), bf16),
        scratch_shapes=[pltpu.VMEM((GLA_HEADS * GLA_DK, GLA_DV), f32)],
        compiler_params=_cparams(2),
        name="gla",
    )(qk, vg, la, gn)


def _swa_kernel(sink_ref, q_ref, kvp_ref, kvc_ref, o_ref):
    w = WINDOW
    j = pl.program_id(1)
    k2 = jnp.concatenate([kvp_ref[:, 0:LANES], kvc_ref[:, 0:LANES]], axis=0)
    v2 = jnp.concatenate([kvp_ref[:, LANES:2 * LANES], kvc_ref[:, LANES:2 * LANES]], axis=0)
    t_i = lax.broadcasted_iota(jnp.int32, (w, 2 * w), 0)
    s_i = lax.broadcasted_iota(jnp.int32, (w, 2 * w), 1)
    rel = t_i + w - s_i
    valid = (rel >= 0) & (rel < w) & ((j > 0) | (s_i >= w))
    lane = lax.broadcasted_iota(jnp.int32, (w, LANES), 1)
    first = lane < SWA_HD
    for p in range(SWA_W // LANES):
        qp = q_ref[:, p * LANES:(p + 1) * LANES].astype(f32)
        halves = []
        for hh in range(2):
            head = SWA_HEAD_ORDER[2 * p + hh]
            qm = jnp.where(first if hh == 0 else ~first, qp, 0.0).astype(bf16)
            s = lax.dot_general(qm, k2, NT, preferred_element_type=f32)
            s = jnp.where(valid, s, -jnp.inf)
            sink = sink_ref[0, head]
            m = jnp.maximum(jnp.max(s, axis=-1, keepdims=True), sink)
            pr = jnp.exp(s - m)
            denom = jnp.sum(pr, axis=-1, keepdims=True) + jnp.exp(sink - m)
            o = jnp.dot(pr.astype(bf16), v2, preferred_element_type=f32)
            halves.append(o / denom)
        o_ref[:, p * LANES:(p + 1) * LANES] = jnp.where(first, halves[0], halves[1]).astype(bf16)


def _swa(sinks, sq, skv, bsz, seq):
    t = sq.shape[0]
    nb = seq // WINDOW
    return pl.pallas_call(
        _swa_kernel,
        grid=(bsz, nb),
        in_specs=[pl.BlockSpec(memory_space=pltpu.SMEM),
                  pl.BlockSpec((WINDOW, SWA_W), lambda b, j: (b * nb + j, 0)),
                  pl.BlockSpec((WINDOW, 2 * SWA_KV), lambda b, j: (b * nb + jnp.maximum(j - 1, 0), 0)),
                  pl.BlockSpec((WINDOW, 2 * SWA_KV), lambda b, j: (b * nb + j, 0))],
        out_specs=pl.BlockSpec((WINDOW, SWA_W), lambda b, j: (b * nb + j, 0)),
        out_shape=jax.ShapeDtypeStruct((t, SWA_W), bf16),
        compiler_params=_cparams(2),
        name="swa",
    )(sinks, sq, skv, skv)


def _to_slabs(v):
    return v.reshape(v.shape[0], v.shape[1] // LANES, LANES).astype(bf16)


def _slab_rows(ref, first, count, tm, pitch):
    return jnp.concatenate([ref[pl.ds(first + c, tm, stride=pitch), :] for c in range(count)], axis=1)


def _bf16_pieces(v):
    p1 = v.astype(bf16).astype(f32)
    p2 = (v - p1).astype(bf16).astype(f32)
    return p1, p2, v - p1 - p2


def _outproj_kernel(og_ref, os_ref, x_ref, mod_ref, wo_ref, g2_ref, wr_hi_ref, wr_lo_ref, br_ref, eye_ref,
                    x1_ref, hrow_ref, route_ref, cnt_ref, run_ref):
    @pl.when(pl.program_id(0) == 0)
    def _():
        run_ref[...] = jnp.zeros_like(run_ref)

    mix = (jnp.dot(og_ref[...], wo_ref[0:GLA_W, :], preferred_element_type=f32)
           + jnp.dot(os_ref[...], wo_ref[GLA_W:GLA_W + SWA_W, :], preferred_element_type=f32))
    x1 = x_ref[...] + mod_ref[2:3, :] * mix
    x1_ref[...] = x1
    h = (_rmsnorm_rows(x1) * g2_ref[...]) * (1.0 + mod_ref[4:5, :]) + mod_ref[3:4, :]
    h_hi, h_lo = _split_bf16(h)
    n_slab = h.shape[1] // LANES
    hrow_ref[:, 0:n_slab, :] = _to_slabs(h)

    lt = (lax.dot_general(wr_hi_ref[...], h_hi, NT, preferred_element_type=f32)
          + lax.dot_general(wr_hi_ref[...], h_lo, NT, preferred_element_type=f32)
          + lax.dot_general(wr_lo_ref[...], h_hi, NT, preferred_element_type=f32)) + br_ref[...]
    tm = lt.shape[1]
    e_g = EXPERTS_PER_GROUP
    row = lax.broadcasted_iota(jnp.int32, (e_g, tm), 0)
    neg = -jnp.inf
    gl = jnp.where(row < N_GROUPS, lt[0:e_g, :], neg)
    g_max = jnp.max(gl, axis=0, keepdims=True)
    g_gate = 1.0 / jnp.sum(jnp.exp(gl - g_max), axis=0, keepdims=True)
    g_idx = jnp.min(jnp.where(gl == g_max, row, e_g), axis=0, keepdims=True)
    sel = lt[e_g * N_GROUPS:e_g * (N_GROUPS + 1), :]
    for g in range(N_GROUPS - 2, -1, -1):
        sel = jnp.where(g_idx == g, lt[e_g * (g + 1):e_g * (g + 2), :], sel)
    t1 = jnp.max(sel, axis=0, keepdims=True)
    i1 = jnp.min(jnp.where(sel == t1, row, e_g), axis=0, keepdims=True)
    sel2 = jnp.where(row == i1, neg, sel)
    t2 = jnp.max(sel2, axis=0, keepdims=True)
    i2 = jnp.min(jnp.where(sel2 == t2, row, e_g), axis=0, keepdims=True)
    ex = jnp.exp(t2 - t1)
    w1 = g_gate / (1.0 + ex)
    w2 = g_gate * ex / (1.0 + ex)
    first_lo = i1 < i2
    w_lo = jnp.where(first_lo, w1, w2)
    w_hi = jnp.where(first_lo, w2, w1)
    bucket = ((g_idx << (2 * EXPERT_BITS)) | (jnp.minimum(i1, i2) << EXPERT_BITS) | jnp.maximum(i1, i2))
    onehot = lax.broadcasted_iota(jnp.int32, (N_BUCKETS, tm), 0) == bucket
    oh_b = jnp.where(onehot, 1.0, 0.0).astype(bf16)
    t_r = lax.broadcasted_iota(jnp.int32, (tm, tm), 0)
    t_c = lax.broadcasted_iota(jnp.int32, (tm, tm), 1)
    earlier = jnp.where(t_r < t_c, 1.0, 0.0).astype(bf16)
    run = run_ref[...]
    prefix = jnp.dot(oh_b, earlier, preferred_element_type=f32) + run
    rank = jnp.sum(jnp.where(onehot, prefix, 0.0), axis=0, keepdims=True)
    run = run + jnp.dot(oh_b, jnp.ones((tm, tm), bf16), preferred_element_type=f32)
    run_ref[...] = run
    cnt_ref[...] = run[:, 0:LANES]
    route_ref[...] = jnp.where(row == 0, bucket.astype(f32), jnp.where(row == 1, rank, jnp.where(
        row == 2, w_lo, jnp.where(row == 3, w_hi, 0.0))))

    r_i = lax.broadcasted_iota(jnp.int32, (eye_ref.shape[0], tm), 0)
    stack = jnp.zeros(r_i.shape, f32)
    for k, piece in enumerate(_bf16_pieces(w_lo) + _bf16_pieces(w_hi)):
        stack = jnp.where(r_i == k, piece, stack)
    w_rows = lax.dot_general(stack.astype(bf16), eye_ref[...], TN, preferred_element_type=f32)
    hrow_ref[:, n_slab:2 * n_slab, :] = jnp.broadcast_to(w_rows[:, None, :], (tm, n_slab, LANES)).astype(bf16)


def _outproj(og, osw, x2, mod3, wo, g2, wr_hi, wr_lo, br, seq, tm):
    t, d = x2.shape
    eye = jnp.asarray(np.eye(2 * EXPERTS_PER_GROUP, LANES, dtype=np.float32), dtype=bf16)
    per_b = seq // tm
    row = lambda i: (i, 0)
    const = lambda i: (0, 0)
    return pl.pallas_call(
        _outproj_kernel,
        grid=(t // tm,),
        in_specs=[pl.BlockSpec((tm, GLA_W), row),
                  pl.BlockSpec((tm, SWA_W), row),
                  pl.BlockSpec((tm, d), row),
                  pl.BlockSpec((None, 6, d), lambda i: (i // per_b, 0, 0)),
                  pl.BlockSpec(wo.shape, const),
                  pl.BlockSpec((1, d), const),
                  pl.BlockSpec(wr_hi.shape, const),
                  pl.BlockSpec(wr_lo.shape, const),
                  pl.BlockSpec(br.shape, const),
                  pl.BlockSpec(eye.shape, const)],
        out_specs=[pl.BlockSpec((tm, d), row),
                   pl.BlockSpec((tm, 2 * d // LANES, LANES), lambda i: (i, 0, 0)),
                   pl.BlockSpec((ROUTE_ROWS, tm), lambda i: (0, i)),
                   pl.BlockSpec((N_BUCKETS, LANES), const)],
        out_shape=[jax.ShapeDtypeStruct((t, d), f32),
                   jax.ShapeDtypeStruct((t, 2 * d // LANES, LANES), bf16),
                   jax.ShapeDtypeStruct((ROUTE_ROWS, t), f32),
                   jax.ShapeDtypeStruct((N_BUCKETS, LANES), f32)],
        scratch_shapes=[pltpu.VMEM((N_BUCKETS, tm), f32)],
        compiler_params=_cparams(1),
        name="outproj",
    )(og, osw, x2, mod3, wo, g2, wr_hi, wr_lo, br, eye)


def _plan_kernel(cnt_ref, route_ref, dest_ref, tb_ref, nt_ref):
    nb = N_BUCKETS
    tiles = jnp.floor((cnt_ref[...] + (MOE_TM - 1)) * (1.0 / MOE_TM))
    b_r = lax.broadcasted_iota(jnp.int32, (nb, nb), 0)
    b_c = lax.broadcasted_iota(jnp.int32, (nb, nb), 1)
    before = jnp.where(b_c < b_r, 1.0, 0.0).astype(bf16)
    t_start = jnp.dot(before, tiles.astype(bf16), preferred_element_type=f32)
    t_end = t_start + tiles
    tile_i = lax.broadcasted_iota(jnp.int32, (nb, 2 * LANES), 1).astype(f32)
    ended = jnp.where(jnp.concatenate([t_end, t_end], axis=1) <= tile_i, 1.0, 0.0)
    tb_ref[...] = jnp.sum(ended, axis=0, keepdims=True).astype(jnp.int32)
    nt_ref[...] = t_end[nb - 1:nb, :].astype(jnp.int32)
    tl = route_ref.shape[1]
    onehot = lax.broadcasted_iota(jnp.int32, (nb, tl), 0) == route_ref[0:1, :].astype(jnp.int32)
    start = lax.dot_general(t_start.astype(bf16), jnp.where(onehot, 1.0, 0.0).astype(bf16), TN,
                            preferred_element_type=f32)
    dest_ref[...] = (start[0:1, :] * MOE_TM + route_ref[1:2, :]).astype(jnp.int32)


def _plan(cnt, route, tl):
    t = route.shape[1]
    return pl.pallas_call(
        _plan_kernel,
        grid=(t // tl,),
        in_specs=[pl.BlockSpec(cnt.shape, lambda i: (0, 0)),
                  pl.BlockSpec((ROUTE_ROWS, tl), lambda i: (0, i))],
        out_specs=[pl.BlockSpec((1, tl), lambda i: (0, i)),
                   pl.BlockSpec((1, 2 * LANES), lambda i: (0, 0)),
                   pl.BlockSpec((1, LANES), lambda i: (0, 0))],
        out_shape=[jax.ShapeDtypeStruct((1, t), jnp.int32),
                   jax.ShapeDtypeStruct((1, 2 * LANES), jnp.int32),
                   jax.ShapeDtypeStruct((1, LANES), jnp.int32)],
        compiler_params=_cparams(1),
        name="plan",
    )(cnt, route)


def _dispatch_kernel(dest_ref, tb_ref, nt_ref, src_ref, out_ref, zero_ref, sem_z, sem_r, *, rows, n_tiles_max):
    i = pl.program_id(0)

    def last_tile(k):
        return (k >= nt_ref[0]) | (tb_ref[k] != tb_ref[k + 1])

    def zero_copy(k):
        return pltpu.make_async_copy(zero_ref, out_ref.at[pl.ds(k * MOE_TM, MOE_TM)], sem_z)

    @pl.when(i == 0)
    def _():
        zero_ref[...] = jnp.zeros_like(zero_ref)

        def start(k, c):
            @pl.when(last_tile(k))
            def _():
                zero_copy(k).start()
            return c

        def wait(k, c):
            @pl.when(last_tile(k))
            def _():
                zero_copy(k).wait()
            return c

        lax.fori_loop(0, n_tiles_max, start, 0)
        lax.fori_loop(0, n_tiles_max, wait, 0)

    base = i * rows

    def row_copy(r):
        return pltpu.make_async_copy(src_ref.at[r], out_ref.at[dest_ref[base + r]], sem_r)

    def issue(r8, c):
        for j in range(ROW_DMA_UNROLL):
            row_copy(r8 * ROW_DMA_UNROLL + j).start(priority=j % 2)
        return c

    def drain(r8, c):
        for j in range(ROW_DMA_UNROLL):
            row_copy(r8 * ROW_DMA_UNROLL + j).wait()
        return c

    lax.fori_loop(0, rows // ROW_DMA_UNROLL, issue, 0)
    lax.fori_loop(0, rows // ROW_DMA_UNROLL, drain, 0)


def _dispatch(dest, tb, nt, hrow, n_tiles_max, rows):
    t = hrow.shape[0]
    slab = hrow.shape[1:]
    grid_spec = pltpu.PrefetchScalarGridSpec(
        num_scalar_prefetch=3, grid=(t // rows,),
        in_specs=[pl.BlockSpec((rows,) + slab, lambda i, dest, tb, nt: (i, 0, 0))],
        out_specs=pl.BlockSpec(memory_space=pl.ANY),
        scratch_shapes=[pltpu.VMEM((MOE_TM,) + slab, hrow.dtype), pltpu.SemaphoreType.DMA(()),
                        pltpu.SemaphoreType.DMA(())])
    return pl.pallas_call(
        functools.partial(_dispatch_kernel, rows=rows, n_tiles_max=n_tiles_max),
        grid_spec=grid_spec,
        out_shape=jax.ShapeDtypeStruct((n_tiles_max * MOE_TM,) + slab, hrow.dtype),
        compiler_params=_cparams(1),
        name="dispatch",
    )(dest, tb, nt, hrow)


def _moe_kernel(tb_ref, nt_ref, hs_ref, wg_ref, wu_ref, wd_ref, y_ref, h32_ref,
                hb0_ref, hb1_ref, wr0_ref, wr1_ref, yf0_ref, yf1_ref):
    s = pl.program_id(0)
    nt = nt_ref[0]
    e_g = EXPERTS_PER_GROUP
    tm, pitch = hs_ref.shape[0], hs_ref.shape[1]
    n_slab = pitch // 2

    @pl.when(s == 0)
    def _():
        for ref in (hb0_ref, hb1_ref, wr0_ref, wr1_ref, yf0_ref, yf1_ref):
            ref[...] = jnp.zeros_like(ref)

    def stages(h_new, w_new, h_cur, w_cur, y_new, y_old):
        bucket = tb_ref[jnp.clip(s - 1, 0, nt - 1)]
        h = h_cur[...]
        w_rows = w_cur[...]
        weights = (w_rows[:, 0:1] + w_rows[:, 1:2] + w_rows[:, 2:3], w_rows[:, 3:4] + w_rows[:, 4:5] + w_rows[:, 5:6])
        experts = ((bucket >> EXPERT_BITS) & (e_g - 1), bucket & (e_g - 1))
        a = jnp.dot(h, wg_ref[experts[0]], preferred_element_type=f32)
        h32_ref[...] = hs_ref[...].astype(f32).reshape(tm * pitch, LANES)
        u = jnp.dot(h, wu_ref[experts[0]], preferred_element_type=f32)
        h_new[...] = _slab_rows(h32_ref, 0, n_slab, tm, pitch).astype(bf16)
        w_new[...] = h32_ref[pl.ds(n_slab, tm, stride=pitch), :]
        y_lo = jnp.dot((_silu(a) * u).astype(bf16), wd_ref[experts[0]], preferred_element_type=f32)
        y_ref[...] = y_old[...].reshape(tm, n_slab, LANES).astype(bf16)
        a = jnp.dot(h, wg_ref[experts[1]], preferred_element_type=f32)
        u = jnp.dot(h, wu_ref[experts[1]], preferred_element_type=f32)
        y_hi = jnp.dot((_silu(a) * u).astype(bf16), wd_ref[experts[1]], preferred_element_type=f32)
        for c in range(n_slab):
            cs = slice(c * LANES, (c + 1) * LANES)
            y_new[pl.ds(c, tm, stride=n_slab), :] = weights[0] * y_lo[:, cs] + weights[1] * y_hi[:, cs]

    active = s < nt + 2

    @pl.when(active & (s % 2 == 0))
    def _():
        stages(hb0_ref, wr0_ref, hb1_ref, wr1_ref, yf1_ref, yf0_ref)

    @pl.when(active & (s % 2 == 1))
    def _():
        stages(hb1_ref, wr1_ref, hb0_ref, wr0_ref, yf0_ref, yf1_ref)

    @pl.when(jnp.logical_not(active))
    def _():
        y_ref[...] = jnp.zeros_like(y_ref)


def _moe(tb, nt, hs, wg, wu, wd):
    n_tiles_max = hs.shape[0] // MOE_TM
    e_g = EXPERTS_PER_GROUP
    d = wg.shape[2]
    pitch = hs.shape[1]
    n_slab = d // LANES
    tile_in = lambda s, tb, nt: (jnp.minimum(s, nt[0] - 1), 0, 0)
    group = lambda s, tb, nt: (tb[jnp.clip(s - 1, 0, nt[0] - 1)] >> (2 * EXPERT_BITS), 0, 0, 0)
    tile_out = lambda s, tb, nt: (jnp.maximum(s - 2, 0), 0, 0)
    grid_spec = pltpu.PrefetchScalarGridSpec(
        num_scalar_prefetch=2, grid=(n_tiles_max + 2,),
        in_specs=[pl.BlockSpec((MOE_TM, pitch, LANES), tile_in),
                  pl.BlockSpec((None, e_g, d, D_EXPERT), group),
                  pl.BlockSpec((None, e_g, d, D_EXPERT), group),
                  pl.BlockSpec((None, e_g, D_EXPERT, d), group)],
        out_specs=pl.BlockSpec((MOE_TM, n_slab, LANES), tile_out),
        scratch_shapes=[pltpu.VMEM((MOE_TM * pitch, LANES), f32),
                        pltpu.VMEM((MOE_TM, d), bf16), pltpu.VMEM((MOE_TM, d), bf16),
                        pltpu.VMEM((MOE_TM, LANES), f32), pltpu.VMEM((MOE_TM, LANES), f32),
                        pltpu.VMEM((MOE_TM * n_slab, LANES), f32), pltpu.VMEM((MOE_TM * n_slab, LANES), f32)])
    return pl.pallas_call(
        _moe_kernel,
        grid_spec=grid_spec,
        out_shape=jax.ShapeDtypeStruct((hs.shape[0], n_slab, LANES), bf16),
        compiler_params=_cparams(1),
        name="moe",
    )(tb, nt, hs, wg, wu, wd)


def _final_kernel(dest_ref, x1_ref, mod_ref, fg_ref, ys_ref, o_ref, ybuf_ref, y32_ref, sem):
    i = pl.program_id(0)
    tm = x1_ref.shape[0]

    def row_copy(tile, slot, r):
        return pltpu.make_async_copy(ys_ref.at[dest_ref[tile * tm + r]], ybuf_ref.at[slot, r], sem.at[slot])

    def gather(tile, slot):
        def issue(r8, c):
            for j in range(ROW_DMA_UNROLL):
                row_copy(tile, slot, r8 * ROW_DMA_UNROLL + j).start(priority=j % 2)
            return c
        lax.fori_loop(0, tm // ROW_DMA_UNROLL, issue, 0)

    @pl.when(i == 0)
    def _():
        gather(0, 0)

    @pl.when(i + 1 < pl.num_programs(0))
    def _():
        gather(i + 1, (i + 1) % 2)

    slot = i % 2

    def drain(r8, c):
        for j in range(ROW_DMA_UNROLL):
            row_copy(i, slot, r8 * ROW_DMA_UNROLL + j).wait()
        return c

    lax.fori_loop(0, tm // ROW_DMA_UNROLL, drain, 0)

    n_slab = ybuf_ref.shape[2]
    y32_ref[...] = ybuf_ref[slot].astype(f32).reshape(tm * n_slab, LANES)
    x2 = x1_ref[...] + mod_ref[5:6, :] * _slab_rows(y32_ref, 0, n_slab, tm, n_slab)
    o_ref[...] = _rmsnorm_rows(x2) * fg_ref[...]


def _final(dest, x1, mod3, fg, ys, seq, tm):
    t, d = x1.shape
    per_b = seq // tm
    grid_spec = pltpu.PrefetchScalarGridSpec(
        num_scalar_prefetch=1, grid=(t // tm,),
        in_specs=[pl.BlockSpec((tm, d), lambda i, dest: (i, 0)),
                  pl.BlockSpec((None, 6, d), lambda i, dest: (i // per_b, 0, 0)),
                  pl.BlockSpec((1, d), lambda i, dest: (0, 0)),
                  pl.BlockSpec(memory_space=pl.ANY)],
        out_specs=pl.BlockSpec((tm, d), lambda i, dest: (i, 0)),
        scratch_shapes=[pltpu.VMEM((2, tm) + ys.shape[1:], ys.dtype), pltpu.VMEM((tm * d // LANES, LANES), f32),
                        pltpu.SemaphoreType.DMA((2,))])
    return pl.pallas_call(
        _final_kernel,
        grid_spec=grid_spec,
        out_shape=jax.ShapeDtypeStruct((t, d), f32),
        compiler_params=_cparams(1),
        name="final",
    )(dest, x1, mod3, fg, ys)


def _prep_w_in(w_in_l):
    sizes = (GLA_QK, GLA_QK, GLA_W, GLA_W, GLA_LOWRANK, SWA_W, SWA_KV, SWA_KV)
    off = [int(o) for o in np.concatenate([[0], np.cumsum(sizes)])]
    gla = w_in_l[:, off[0]:off[4]]
    ga = w_in_l[:, off[4]:off[5]]
    sq = [w_in_l[:, off[5] + h * SWA_HD:off[5] + (h + 1) * SWA_HD] for h in SWA_HEAD_ORDER]
    skv = w_in_l[:, off[6]:off[8]]
    pad = jnp.zeros((w_in_l.shape[0], IN_END - off[8]), w_in_l.dtype)
    return jnp.concatenate([gla] + sq + [skv, ga, pad], axis=1).astype(bf16)


def _prep_w_out(w_out_l):
    swa = [w_out_l[GLA_W + h * SWA_HD:GLA_W + (h + 1) * SWA_HD] for h in SWA_HEAD_ORDER]
    return jnp.concatenate([w_out_l[0:GLA_W]] + swa, axis=0).astype(bf16)


def _prep_router(w_grp_l, b_grp_l, w_exp_l, b_exp_l):
    d = w_grp_l.shape[0]
    n_e = N_GROUPS * EXPERTS_PER_GROUP
    wr = jnp.zeros((ROUTER_ROWS, d), f32)
    wr = wr.at[0:N_GROUPS, :].set(w_grp_l.T).at[EXPERTS_PER_GROUP:EXPERTS_PER_GROUP + n_e, :].set(w_exp_l.T)
    br = jnp.zeros((ROUTER_ROWS, 1), f32)
    br = br.at[0:N_GROUPS, 0].set(b_grp_l).at[EXPERTS_PER_GROUP:EXPERTS_PER_GROUP + n_e, 0].set(b_exp_l)
    wr_hi = wr.astype(bf16)
    wr_lo = (wr - wr_hi.astype(f32)).astype(bf16)
    return wr_hi, wr_lo, br


def _layer(x2, mod3, tab, e_mat, bsz, seq, norm_mix_g, w_in, gla_w_a2, gla_b_a2, gla_norm_g, swa_sinks, w_out,
           norm_ffn_g, w_grp, b_grp, w_exp, b_exp, w_gate, w_up, w_down, out_g, tm, tc, tm_moe):
    d = x2.shape[1]
    wa = jnp.pad(gla_w_a2, ((0, LANES - GLA_LOWRANK), (0, 0)))
    wa_hi = wa.astype(bf16)
    wa_lo = (wa - wa_hi.astype(f32)).astype(bf16)
    qk, vg, la, sq, skv = _inproj(x2, mod3, norm_mix_g.reshape(1, d), _prep_w_in(w_in), tab, e_mat,
                                  wa_hi, wa_lo, gla_b_a2.reshape(1, GLA_QK), seq, tm)
    o_gla = _gla(qk, vg, la, gla_norm_g.reshape(1, GLA_W), bsz, seq, tc)
    o_swa = _swa(swa_sinks.reshape(1, SWA_HEADS), sq, skv, bsz, seq)
    wr_hi, wr_lo, br = _prep_router(w_grp, b_grp, w_exp, b_exp)
    x1, hrow, route, cnt = _outproj(o_gla, o_swa, x2, mod3, _prep_w_out(w_out), norm_ffn_g.reshape(1, d),
                                    wr_hi, wr_lo, br, seq, tm)
    t = x2.shape[0]
    n_tiles_max = t // MOE_TM + N_PAIRS
    assert n_tiles_max <= 2 * LANES and t % MOE_TM == 0
    dest, tile_bucket, n_tiles = _plan(cnt, route, min(t, 2048))
    dest = dest.reshape(t)
    tile_bucket = tile_bucket.reshape(2 * LANES)
    n_tiles = n_tiles.reshape(LANES)[0:1]
    hs = _dispatch(dest, tile_bucket, n_tiles, hrow, n_tiles_max, min(t, 1024))
    ys = _moe(tile_bucket, n_tiles, hs, w_gate.astype(bf16), w_up.astype(bf16), w_down.astype(bf16))
    return _final(dest, x1, mod3, out_g.reshape(1, d), ys, seq, tm_moe)


def kernel(x, c, positions, ada_w, ada_b, norm_mix_g, w_in, gla_w_a2, gla_b_a2, gla_norm_g, swa_sinks, w_out,
           norm_ffn_g, w_grp, b_grp, w_exp, b_exp, w_gate, w_up, w_down, final_norm_g):
    bsz, seq, d = x.shape
    depth = ada_w.shape[0]
    assert depth == 1, "the final rmsnorm is fused into the last layer's MoE kernel"
    tab = _rope_tab(positions)
    e_mat = _rope_expand_matrix()
    x2 = x.reshape(bsz * seq, d)
    for l in range(depth):
        mod3 = _adaln(c, ada_w[l], ada_b[l]).reshape(bsz, 6, d)
        x2 = _layer(x2, mod3, tab, e_mat, bsz, seq, norm_mix_g[l], w_in[l], gla_w_a2[l], gla_b_a2[l],
                    gla_norm_g[l], swa_sinks[l], w_out[l], norm_ffn_g[l], w_grp[l], b_grp[l], w_exp[l], b_exp[l],
                    w_gate[l], w_up[l], w_down[l], final_norm_g, tm=512, tc=256, tm_moe=512)
    return x2.reshape(bsz, seq, d)
```

```python
import functools

import numpy as np
import jax
import jax.numpy as jnp
from jax import lax
from jax.experimental import pallas as pl
from jax.experimental.pallas import tpu as pltpu

f32 = jnp.float32
bf16 = jnp.bfloat16

GLA_HEADS = 4
GLA_DK = 64
GLA_DV = 128
GLA_LOWRANK = 16
GLA_GATE_NORM = 16.0
GLA_CHUNK = 64
SWA_HEADS = 8
SWA_KV_HEADS = 2
SWA_HD = 64
WINDOW = 128
ROPE_THETA = 500000.0
ROPE_DIMS = SWA_HD // 4
N_GROUPS = 4
EXPERTS_PER_GROUP = 8
D_EXPERT = 256
EPS = 1e-6

LANES = 128
VMEM_LIMIT = 52 * 1024 * 1024

GLA_QK = GLA_HEADS * GLA_DK
GLA_W = GLA_HEADS * GLA_DV
SWA_W = SWA_HEADS * SWA_HD
SWA_KV = SWA_KV_HEADS * SWA_HD
SWA_HEAD_ORDER = (0, 4, 1, 5, 2, 6, 3, 7)
ROUTE_ROWS = 8
ROUTER_ROWS = 128
EXPERT_BITS = 3
N_BUCKETS = N_GROUPS << (2 * EXPERT_BITS)
N_PAIRS = N_GROUPS * EXPERTS_PER_GROUP * (EXPERTS_PER_GROUP - 1) // 2
MOE_TM = 128
ROW_DMA_UNROLL = 8

TN = (((0,), (0,)), ((), ()))
NT = (((1,), (1,)), ((), ()))


def _cparams(n_axes):
    return pltpu.CompilerParams(dimension_semantics=("arbitrary",) * n_axes, vmem_limit_bytes=VMEM_LIMIT)


def _split_bf16(v):
    hi = v.astype(bf16)
    lo = (v - hi.astype(f32)).astype(bf16)
    return hi, lo


def _split_stack_bf16(v):
    hi = v.astype(bf16).astype(f32)
    return jnp.concatenate([hi, v - hi], axis=0).astype(bf16)


def _rmsnorm_rows(v):
    return v * lax.rsqrt(jnp.mean(v * v, axis=-1, keepdims=True) + EPS)


def _silu(v):
    return v * jax.nn.sigmoid(v)


def _adaln_kernel(c_ref, w_ref, b_ref, o_ref):
    ca = _silu(c_ref[...])
    o_ref[...] = jnp.dot(ca, w_ref[...], precision=lax.Precision.HIGHEST, preferred_element_type=f32) + b_ref[...]


def _adaln(c, w, b):
    bsz, d = c.shape
    n = w.shape[1]
    return pl.pallas_call(
        _adaln_kernel,
        grid=(n // d,),
        in_specs=[pl.BlockSpec((bsz, d), lambda j: (0, 0)),
                  pl.BlockSpec((d, d), lambda j: (0, j)),
                  pl.BlockSpec((1, d), lambda j: (0, j))],
        out_specs=pl.BlockSpec((bsz, d), lambda j: (0, j)),
        out_shape=jax.ShapeDtypeStruct((bsz, n), f32),
        compiler_params=_cparams(1),
        name="adaln",
    )(c, w, b.reshape(1, n))


def _rope_tab_kernel(pos_ref, invf_ref, o_ref):
    half = ROPE_DIMS // 2
    for b in range(pos_ref.shape[0]):
        ang = pos_ref[b:b + 1, :].astype(f32) * invf_ref[...]
        o_ref[b, 0:half, :] = jnp.cos(ang)
        o_ref[b, half:2 * half, :] = jnp.sin(ang)


def _rope_tab(positions):
    bsz, s = positions.shape
    half = ROPE_DIMS // 2
    inv_freq = (np.float32(ROPE_THETA) ** (-np.arange(0, ROPE_DIMS, 2, dtype=np.float32) / np.float32(ROPE_DIMS)))
    invf = jnp.asarray(inv_freq.astype(np.float32).reshape(half, 1))
    return pl.pallas_call(
        _rope_tab_kernel,
        out_shape=jax.ShapeDtypeStruct((bsz, 2 * half, s), f32),
        name="rope_tab",
    )(positions, invf)


def _rope_expand_matrix():
    half = ROPE_DIMS // 2
    e = np.zeros((2 * half, 3 * LANES), np.float32)
    for j in range(LANES):
        jj = j % SWA_HD
        if jj < half:
            e[jj, j] = 1.0
            e[half + jj, LANES + j] = -1.0
        elif jj < 2 * half:
            e[jj - half, j] = 1.0
            e[half + jj - half, 2 * LANES + j] = 1.0
    return jnp.asarray(np.concatenate([e, e], axis=0), dtype=bf16)


IN_QK0, IN_VG0, IN_SQ0, IN_SKV0, IN_A0, IN_END = 0, 512, 1536, 2048, 2304, 2432


def _inproj_kernel(x_ref, mod_ref, g_ref, w_ref, tab_ref, e_ref, wa_hi_ref, wa_lo_ref, ba_ref,
                   qk_ref, vg_ref, la_ref, sq_ref, skv_ref):
    x = x_ref[...]
    h = (_rmsnorm_rows(x) * g_ref[...]) * (1.0 + mod_ref[1:2, :]) + mod_ref[0:1, :]
    hb = h.astype(bf16)

    def proj(lo, hi):
        return jnp.dot(hb, w_ref[:, lo:hi], preferred_element_type=f32)

    al_hi, al_lo = _split_bf16(proj(IN_A0, IN_END))
    z = (jnp.dot(al_hi, wa_hi_ref[...], preferred_element_type=f32)
         + jnp.dot(al_lo, wa_hi_ref[...], preferred_element_type=f32)
         + jnp.dot(al_hi, wa_lo_ref[...], preferred_element_type=f32)) + ba_ref[...]
    la_ref[...] = (jnp.minimum(z, 0.0) - jnp.log1p(jnp.exp(-jnp.abs(z)))) * (1.0 / GLA_GATE_NORM)

    qk = proj(IN_QK0, IN_VG0)
    lane = lax.broadcasted_iota(jnp.int32, qk.shape, 1)
    qk_ref[...] = jnp.where(lane < GLA_QK, qk * (GLA_DK ** -0.5), qk).astype(bf16)
    vg_ref[...] = proj(IN_VG0, IN_SQ0).astype(bf16)

    tabs = lax.dot_general(_split_stack_bf16(tab_ref[...]), e_ref[...], TN, preferred_element_type=f32)
    lane1 = lax.broadcasted_iota(jnp.int32, (1, LANES), 1)
    cos_t = tabs[:, 0:LANES] + jnp.where((lane1 & (SWA_HD - 1)) < ROPE_DIMS, 0.0, 1.0)
    sa_t = tabs[:, LANES:2 * LANES]
    sb_t = tabs[:, 2 * LANES:3 * LANES]

    def rope(v):
        return (v * cos_t + pltpu.roll(v, LANES - ROPE_DIMS // 2, 1) * sa_t
                + pltpu.roll(v, ROPE_DIMS // 2, 1) * sb_t)

    sq = proj(IN_SQ0, IN_SKV0)
    for p in range(SWA_W // LANES):
        sq_ref[:, p * LANES:(p + 1) * LANES] = (rope(sq[:, p * LANES:(p + 1) * LANES]) * (SWA_HD ** -0.5)).astype(bf16)
    skv = proj(IN_SKV0, IN_A0)
    skv_ref[:, 0:LANES] = rope(skv[:, 0:LANES]).astype(bf16)
    skv_ref[:, LANES:2 * LANES] = skv[:, LANES:2 * LANES].astype(bf16)


def _inproj(x2, mod3, g, w, tab, e_mat, wa_hi, wa_lo, ba, seq, tm):
    t, d = x2.shape
    per_b = seq // tm
    row = lambda i: (i, 0)
    const = lambda i: (0, 0)
    return pl.pallas_call(
        _inproj_kernel,
        grid=(t // tm,),
        in_specs=[pl.BlockSpec((tm, d), row),
                  pl.BlockSpec((None, 6, d), lambda i: (i // per_b, 0, 0)),
                  pl.BlockSpec((1, d), const),
                  pl.BlockSpec(w.shape, const),
                  pl.BlockSpec((None, ROPE_DIMS, tm), lambda i: (i // per_b, 0, i % per_b)),
                  pl.BlockSpec(e_mat.shape, const),
                  pl.BlockSpec(wa_hi.shape, const),
                  pl.BlockSpec(wa_lo.shape, const),
                  pl.BlockSpec(ba.shape, const)],
        out_specs=[pl.BlockSpec((tm, 2 * GLA_QK), row),
                   pl.BlockSpec((tm, 2 * GLA_W), row),
                   pl.BlockSpec((tm, GLA_QK), row),
                   pl.BlockSpec((tm, SWA_W), row),
                   pl.BlockSpec((tm, 2 * SWA_KV), row)],
        out_shape=[jax.ShapeDtypeStruct((t, 2 * GLA_QK), bf16),
                   jax.ShapeDtypeStruct((t, 2 * GLA_W), bf16),
                   jax.ShapeDtypeStruct((t, GLA_QK), f32),
                   jax.ShapeDtypeStruct((t, SWA_W), bf16),
                   jax.ShapeDtypeStruct((t, 2 * SWA_KV), bf16)],
        compiler_params=_cparams(1),
        name="inproj",
    )(x2, mod3, g, w, tab, e_mat, wa_hi, wa_lo, ba)


def _gla_kernel(qk_ref, vg_ref, la_ref, gn_ref, o_ref, st_ref, *, chunks):
    c_len = GLA_CHUNK

    @pl.when(pl.program_id(1) == 0)
    def _():
        st_ref[...] = jnp.zeros_like(st_ref)

    r_i = lax.broadcasted_iota(jnp.int32, (c_len, c_len), 0)
    c_i = lax.broadcasted_iota(jnp.int32, (c_len, c_len), 1)
    causal = r_i >= c_i
    tri = jnp.where(causal, 1.0, 0.0).astype(bf16)
    ones = jnp.ones((c_len, LANES), bf16)
    lane = lax.broadcasted_iota(jnp.int32, (c_len, LANES), 1)
    first = lane < GLA_DK

    for c in range(chunks):
        rows = slice(c * c_len, (c + 1) * c_len)
        la_hi, la_lo = _split_bf16(la_ref[rows, :])
        b = jnp.dot(tri, la_hi, preferred_element_type=f32) + jnp.dot(tri, la_lo, preferred_element_type=f32)
        b_tot_t = (lax.dot_general(la_hi, ones, TN, preferred_element_type=f32)
                   + lax.dot_general(la_lo, ones, TN, preferred_element_type=f32))
        b_last = b[c_len - 1:c_len, :]
        q = qk_ref[rows, 0:GLA_QK].astype(f32)
        k = qk_ref[rows, GLA_QK:2 * GLA_QK].astype(f32)
        q_dec = q * jnp.exp(b)
        k_dec = (k * jnp.exp(-b)).astype(bf16)
        k_rem = (k * jnp.exp(b_last - b)).astype(bf16)
        decay = jnp.exp(b_tot_t)

        for p in range(GLA_HEADS // 2):
            ls = slice(p * LANES, (p + 1) * LANES)
            ss = slice(p * LANES, (p + 1) * LANES)
            s_prev = st_ref[ss, :]
            s_prev_b = s_prev.astype(bf16)
            qd = q_dec[:, ls]
            kv_halves = []
            for hh in range(2):
                h = 2 * p + hh
                vs = slice(h * GLA_DV, (h + 1) * GLA_DV)
                qm = jnp.where(first if hh == 0 else ~first, qd, 0.0).astype(bf16)
                v = vg_ref[rows, vs]
                scores = lax.dot_general(qm, k_dec[:, ls], NT, preferred_element_type=f32)
                scores = jnp.where(causal, scores, 0.0).astype(bf16)
                o = (jnp.dot(scores, v, preferred_element_type=f32)
                     + jnp.dot(qm, s_prev_b, preferred_element_type=f32))
                o = _rmsnorm_rows(o) * gn_ref[:, vs]
                gate = vg_ref[rows, GLA_W + h * GLA_DV:GLA_W + (h + 1) * GLA_DV].astype(f32)
                o_ref[rows, vs] = (o * _silu(gate)).astype(bf16)
                kv = lax.dot_general(k_rem[:, ls], v, TN, preferred_element_type=f32)
                kv_halves.append(kv[hh * GLA_DK:(hh + 1) * GLA_DK, :])
            st_ref[ss, :] = decay[ss, :] * s_prev + jnp.concatenate(kv_halves, axis=0)


def _gla(qk, vg, la, gn, bsz, seq, tc):
    t = qk.shape[0]
    nt = seq // tc
    row = lambda b, j: (b * nt + j, 0)
    return pl.pallas_call(
        functools.partial(_gla_kernel, chunks=tc // GLA_CHUNK),
        grid=(bsz, nt),
        in_specs=[pl.BlockSpec((tc, 2 * GLA_QK), row),
                  pl.BlockSpec((tc, 2 * GLA_W), row),
                  pl.BlockSpec((tc, GLA_QK), row),
                  pl.BlockSpec((1, GLA_W), lambda b, j: (0, 0))],
        out_specs=pl.BlockSpec((tc, GLA_W), row),
        out_shape=jax.ShapeDtypeStruct((t, GLA_W), bf16),
        scratch_shapes=[pltpu.VMEM((GLA_HEADS * GLA_DK, GLA_DV), f32)],
        compiler_params=_cparams(2),
        name="gla",
    )(qk, vg, la, gn)


def _swa_kernel(sink_ref, q_ref, kvp_ref, kvc_ref, o_ref):
    w = WINDOW
    j = pl.program_id(1)
    k2 = jnp.concatenate([kvp_ref[:, 0:LANES], kvc_ref[:, 0:LANES]], axis=0)
    v2 = jnp.concatenate([kvp_ref[:, LANES:2 * LANES], kvc_ref[:, LANES:2 * LANES]], axis=0)
    t_i = lax.broadcasted_iota(jnp.int32, (w, 2 * w), 0)
    s_i = lax.broadcasted_iota(jnp.int32, (w, 2 * w), 1)
    rel = t_i + w - s_i
    valid = (rel >= 0) & (rel < w) & ((j > 0) | (s_i >= w))
    lane = lax.broadcasted_iota(jnp.int32, (w, LANES), 1)
    first = lane < SWA_HD
    for p in range(SWA_W // LANES):
        qp = q_ref[:, p * LANES:(p + 1) * LANES].astype(f32)
        halves = []
        for hh in range(2):
            head = SWA_HEAD_ORDER[2 * p + hh]
            qm = jnp.where(first if hh == 0 else ~first, qp, 0.0).astype(bf16)
            s = lax.dot_general(qm, k2, NT, preferred_element_type=f32)
            s = jnp.where(valid, s, -jnp.inf)
            sink = sink_ref[0, head]
            m = jnp.maximum(jnp.max(s, axis=-1, keepdims=True), sink)
            pr = jnp.exp(s - m)
            denom = jnp.sum(pr, axis=-1, keepdims=True) + jnp.exp(sink - m)
            o = jnp.dot(pr.astype(bf16), v2, preferred_element_type=f32)
            halves.append(o / denom)
        o_ref[:, p * LANES:(p + 1) * LANES] = jnp.where(first, halves[0], halves[1]).astype(bf16)


def _swa(sinks, sq, skv, bsz, seq):
    t = sq.shape[0]
    nb = seq // WINDOW
    return pl.pallas_call(
        _swa_kernel,
        grid=(bsz, nb),
        in_specs=[pl.BlockSpec(memory_space=pltpu.SMEM),
                  pl.BlockSpec((WINDOW, SWA_W), lambda b, j: (b * nb + j, 0)),
                  pl.BlockSpec((WINDOW, 2 * SWA_KV), lambda b, j: (b * nb + jnp.maximum(j - 1, 0), 0)),
                  pl.BlockSpec((WINDOW, 2 * SWA_KV), lambda b, j: (b * nb + j, 0))],
        out_specs=pl.BlockSpec((WINDOW, SWA_W), lambda b, j: (b * nb + j, 0)),
        out_shape=jax.ShapeDtypeStruct((t, SWA_W), bf16),
        compiler_params=_cparams(2),
        name="swa",
    )(sinks, sq, skv, skv)


def _bf16_pieces(v):
    p1 = v.astype(bf16).astype(f32)
    p2 = (v - p1).astype(bf16).astype(f32)
    return p1, p2, v - p1 - p2


def _outproj_kernel(og_ref, os_ref, x_ref, mod_ref, wo_ref, g2_ref, wr_hi_ref, wr_lo_ref, br_ref, sel_ref,
                    x1_ref, hrow_ref, route_ref, cnt_ref, run_ref):
    @pl.when(pl.program_id(0) == 0)
    def _():
        run_ref[...] = jnp.zeros_like(run_ref)

    mix = (jnp.dot(og_ref[...], wo_ref[0:GLA_W, :], preferred_element_type=f32)
           + jnp.dot(os_ref[...], wo_ref[GLA_W:GLA_W + SWA_W, :], preferred_element_type=f32))
    x1 = x_ref[...] + mod_ref[2:3, :] * mix
    x1_ref[...] = x1
    h = (_rmsnorm_rows(x1) * g2_ref[...]) * (1.0 + mod_ref[4:5, :]) + mod_ref[3:4, :]
    h_hi, h_lo = _split_bf16(h)
    d = h.shape[1]
    hrow_ref[:, 0:d] = h

    lt = (lax.dot_general(wr_hi_ref[...], h_hi, NT, preferred_element_type=f32)
          + lax.dot_general(wr_hi_ref[...], h_lo, NT, preferred_element_type=f32)
          + lax.dot_general(wr_lo_ref[...], h_hi, NT, preferred_element_type=f32)) + br_ref[...]
    tm = lt.shape[1]
    e_g = EXPERTS_PER_GROUP
    row = lax.broadcasted_iota(jnp.int32, (e_g, tm), 0)
    neg = -jnp.inf
    gl = jnp.where(row < N_GROUPS, lt[0:e_g, :], neg)
    g_max = jnp.max(gl, axis=0, keepdims=True)
    g_gate = 1.0 / jnp.sum(jnp.exp(gl - g_max), axis=0, keepdims=True)
    g_idx = jnp.min(jnp.where(gl == g_max, row, e_g), axis=0, keepdims=True)
    sel = lt[e_g * N_GROUPS:e_g * (N_GROUPS + 1), :]
    for g in range(N_GROUPS - 2, -1, -1):
        sel = jnp.where(g_idx == g, lt[e_g * (g + 1):e_g * (g + 2), :], sel)
    t1 = jnp.max(sel, axis=0, keepdims=True)
    i1 = jnp.min(jnp.where(sel == t1, row, e_g), axis=0, keepdims=True)
    sel2 = jnp.where(row == i1, neg, sel)
    t2 = jnp.max(sel2, axis=0, keepdims=True)
    i2 = jnp.min(jnp.where(sel2 == t2, row, e_g), axis=0, keepdims=True)
    ex = jnp.exp(t2 - t1)
    w1 = g_gate / (1.0 + ex)
    w2 = g_gate * ex / (1.0 + ex)
    first_lo = i1 < i2
    w_lo = jnp.where(first_lo, w1, w2)
    w_hi = jnp.where(first_lo, w2, w1)
    bucket = ((g_idx << (2 * EXPERT_BITS)) | (jnp.minimum(i1, i2) << EXPERT_BITS) | jnp.maximum(i1, i2))
    onehot = lax.broadcasted_iota(jnp.int32, (N_BUCKETS, tm), 0) == bucket
    oh_b = jnp.where(onehot, 1.0, 0.0).astype(bf16)
    t_r = lax.broadcasted_iota(jnp.int32, (tm, tm), 0)
    t_c = lax.broadcasted_iota(jnp.int32, (tm, tm), 1)
    earlier = jnp.where(t_r < t_c, 1.0, 0.0).astype(bf16)
    run = run_ref[...]
    prefix = jnp.dot(oh_b, earlier, preferred_element_type=f32) + run
    rank = jnp.sum(jnp.where(onehot, prefix, 0.0), axis=0, keepdims=True)
    run = run + jnp.dot(oh_b, jnp.ones((tm, tm), bf16), preferred_element_type=f32)
    run_ref[...] = run
    cnt_ref[...] = run[:, 0:LANES]
    route_ref[...] = jnp.where(row == 0, bucket.astype(f32), jnp.where(row == 1, rank, jnp.where(
        row == 2, w_lo, jnp.where(row == 3, w_hi, 0.0))))

    r_i = lax.broadcasted_iota(jnp.int32, (sel_ref.shape[0], tm), 0)
    stack = jnp.zeros(r_i.shape, f32)
    for k, piece in enumerate(_bf16_pieces(w_lo) + _bf16_pieces(w_hi)):
        stack = jnp.where(r_i == k, piece, stack)
    hrow_ref[:, d:d + LANES] = lax.dot_general(stack.astype(bf16), sel_ref[...], TN, preferred_element_type=f32)


def _outproj(og, osw, x2, mod3, wo, g2, wr_hi, wr_lo, br, seq, tm):
    t, d = x2.shape
    sel = np.zeros((2 * EXPERTS_PER_GROUP, LANES), np.float32)
    sel[0:3, 0] = 1.0
    sel[3:6, 1] = 1.0
    eye = jnp.asarray(sel, dtype=bf16)
    per_b = seq // tm
    row = lambda i: (i, 0)
    const = lambda i: (0, 0)
    return pl.pallas_call(
        _outproj_kernel,
        grid=(t // tm,),
        in_specs=[pl.BlockSpec((tm, GLA_W), row),
                  pl.BlockSpec((tm, SWA_W), row),
                  pl.BlockSpec((tm, d), row),
                  pl.BlockSpec((None, 6, d), lambda i: (i // per_b, 0, 0)),
                  pl.BlockSpec(wo.shape, const),
                  pl.BlockSpec((1, d), const),
                  pl.BlockSpec(wr_hi.shape, const),
                  pl.BlockSpec(wr_lo.shape, const),
                  pl.BlockSpec(br.shape, const),
                  pl.BlockSpec(eye.shape, const)],
        out_specs=[pl.BlockSpec((tm, d), row),
                   pl.BlockSpec((tm, d + LANES), row),
                   pl.BlockSpec((ROUTE_ROWS, tm), lambda i: (0, i)),
                   pl.BlockSpec((N_BUCKETS, LANES), const)],
        out_shape=[jax.ShapeDtypeStruct((t, d), f32),
                   jax.ShapeDtypeStruct((t, d + LANES), f32),
                   jax.ShapeDtypeStruct((ROUTE_ROWS, t), f32),
                   jax.ShapeDtypeStruct((N_BUCKETS, LANES), f32)],
        scratch_shapes=[pltpu.VMEM((N_BUCKETS, tm), f32)],
        compiler_params=_cparams(1),
        name="outproj",
    )(og, osw, x2, mod3, wo, g2, wr_hi, wr_lo, br, eye)


def _plan_kernel(cnt_ref, route_ref, dest_ref, tb_ref, nt_ref):
    nb = N_BUCKETS
    tiles = jnp.floor((cnt_ref[...] + (MOE_TM - 1)) * (1.0 / MOE_TM))
    b_r = lax.broadcasted_iota(jnp.int32, (nb, nb), 0)
    b_c = lax.broadcasted_iota(jnp.int32, (nb, nb), 1)
    before = jnp.where(b_c < b_r, 1.0, 0.0).astype(bf16)
    t_start = jnp.dot(before, tiles.astype(bf16), preferred_element_type=f32)
    t_end = t_start + tiles
    tile_i = lax.broadcasted_iota(jnp.int32, (nb, 2 * LANES), 1).astype(f32)
    ended = jnp.where(jnp.concatenate([t_end, t_end], axis=1) <= tile_i, 1.0, 0.0)
    tb_ref[...] = jnp.sum(ended, axis=0, keepdims=True).astype(jnp.int32)
    nt_ref[...] = t_end[nb - 1:nb, :].astype(jnp.int32)
    tl = route_ref.shape[1]
    onehot = lax.broadcasted_iota(jnp.int32, (nb, tl), 0) == route_ref[0:1, :].astype(jnp.int32)
    start = lax.dot_general(t_start.astype(bf16), jnp.where(onehot, 1.0, 0.0).astype(bf16), TN,
                            preferred_element_type=f32)
    dest_ref[...] = (start[0:1, :] * MOE_TM + route_ref[1:2, :]).astype(jnp.int32)


def _plan(cnt, route, tl):
    t = route.shape[1]
    return pl.pallas_call(
        _plan_kernel,
        grid=(t // tl,),
        in_specs=[pl.BlockSpec(cnt.shape, lambda i: (0, 0)),
                  pl.BlockSpec((ROUTE_ROWS, tl), lambda i: (0, i))],
        out_specs=[pl.BlockSpec((1, tl), lambda i: (0, i)),
                   pl.BlockSpec((1, 2 * LANES), lambda i: (0, 0)),
                   pl.BlockSpec((1, LANES), lambda i: (0, 0))],
        out_shape=[jax.ShapeDtypeStruct((1, t), jnp.int32),
                   jax.ShapeDtypeStruct((1, 2 * LANES), jnp.int32),
                   jax.ShapeDtypeStruct((1, LANES), jnp.int32)],
        compiler_params=_cparams(1),
        name="plan",
    )(cnt, route)


def _dispatch_kernel(dest_ref, tb_ref, nt_ref, src_ref, out_ref, zero_ref, sem_z, sem_r, *, rows, n_tiles_max):
    i = pl.program_id(0)

    def last_tile(k):
        return (k >= nt_ref[0]) | (tb_ref[k] != tb_ref[k + 1])

    def zero_copy(k):
        return pltpu.make_async_copy(zero_ref, out_ref.at[pl.ds(k * MOE_TM, MOE_TM)], sem_z)

    @pl.when(i == 0)
    def _():
        zero_ref[...] = jnp.zeros_like(zero_ref)

        def start(k, c):
            @pl.when(last_tile(k))
            def _():
                zero_copy(k).start()
            return c

        def wait(k, c):
            @pl.when(last_tile(k))
            def _():
                zero_copy(k).wait()
            return c

        lax.fori_loop(0, n_tiles_max, start, 0)
        lax.fori_loop(0, n_tiles_max, wait, 0)

    base = i * rows

    def row_copy(r):
        return pltpu.make_async_copy(src_ref.at[pl.ds(r, 1)], out_ref.at[pl.ds(dest_ref[base + r], 1)], sem_r)

    def issue(r8, c):
        for j in range(ROW_DMA_UNROLL):
            row_copy(r8 * ROW_DMA_UNROLL + j).start(priority=j % 2)
        return c

    def drain(r8, c):
        for j in range(ROW_DMA_UNROLL):
            row_copy(r8 * ROW_DMA_UNROLL + j).wait()
        return c

    lax.fori_loop(0, rows // ROW_DMA_UNROLL, issue, 0)
    lax.fori_loop(0, rows // ROW_DMA_UNROLL, drain, 0)


def _dispatch(dest, tb, nt, hrow, n_tiles_max, rows):
    t, w = hrow.shape
    grid_spec = pltpu.PrefetchScalarGridSpec(
        num_scalar_prefetch=3, grid=(t // rows,),
        in_specs=[pl.BlockSpec((rows, w), lambda i, dest, tb, nt: (i, 0))],
        out_specs=pl.BlockSpec(memory_space=pl.ANY),
        scratch_shapes=[pltpu.VMEM((MOE_TM, w), hrow.dtype), pltpu.SemaphoreType.DMA(()),
                        pltpu.SemaphoreType.DMA(())])
    return pl.pallas_call(
        functools.partial(_dispatch_kernel, rows=rows, n_tiles_max=n_tiles_max),
        grid_spec=grid_spec,
        out_shape=jax.ShapeDtypeStruct((n_tiles_max * MOE_TM, w), hrow.dtype),
        compiler_params=_cparams(1),
        name="dispatch",
    )(dest, tb, nt, hrow)


def _moe_kernel(tb_ref, nt_ref, hs_ref, wg_ref, wu_ref, wd_ref, y_ref):
    i = pl.program_id(0)

    @pl.when(i < nt_ref[0])
    def _():
        e_g = EXPERTS_PER_GROUP
        d = wg_ref.shape[1]
        bucket = tb_ref[i]
        h = hs_ref[:, 0:d].astype(bf16)
        y = None
        for k, e in enumerate(((bucket >> EXPERT_BITS) & (e_g - 1), bucket & (e_g - 1))):
            a = jnp.dot(h, wg_ref[e], preferred_element_type=f32)
            u = jnp.dot(h, wu_ref[e], preferred_element_type=f32)
            yk = hs_ref[:, d + k:d + k + 1] * jnp.dot((_silu(a) * u).astype(bf16), wd_ref[e],
                                                      preferred_element_type=f32)
            y = yk if y is None else y + yk
        y_ref[...] = y

    @pl.when(i >= nt_ref[0])
    def _():
        y_ref[...] = jnp.zeros_like(y_ref)


def _moe(tb, nt, hs, wg, wu, wd):
    n_tiles_max = hs.shape[0] // MOE_TM
    e_g = EXPERTS_PER_GROUP
    d = wg.shape[2]
    last = lambda i, tb, nt: jnp.minimum(i, nt[0] - 1)
    group = lambda i, tb, nt: (tb[last(i, tb, nt)] >> (2 * EXPERT_BITS), 0, 0, 0)
    grid_spec = pltpu.PrefetchScalarGridSpec(
        num_scalar_prefetch=2, grid=(n_tiles_max,),
        in_specs=[pl.BlockSpec((MOE_TM, hs.shape[1]), lambda i, tb, nt: (last(i, tb, nt), 0)),
                  pl.BlockSpec((None, e_g, d, D_EXPERT), group),
                  pl.BlockSpec((None, e_g, d, D_EXPERT), group),
                  pl.BlockSpec((None, e_g, D_EXPERT, d), group)],
        out_specs=pl.BlockSpec((MOE_TM, d), lambda i, tb, nt: (i, 0)))
    return pl.pallas_call(
        _moe_kernel,
        grid_spec=grid_spec,
        out_shape=jax.ShapeDtypeStruct((hs.shape[0], d), f32),
        compiler_params=_cparams(1),
        name="moe",
    )(tb, nt, hs, wg, wu, wd)


def _final_kernel(dest_ref, x1_ref, mod_ref, fg_ref, ys_ref, o_ref, ybuf_ref, sem):
    i = pl.program_id(0)
    tm = x1_ref.shape[0]

    def row_copy(tile, slot, r):
        return pltpu.make_async_copy(ys_ref.at[pl.ds(dest_ref[tile * tm + r], 1)], ybuf_ref.at[slot, pl.ds(r, 1)],
                                     sem.at[slot])

    def gather(tile, slot):
        def issue(r8, c):
            for j in range(ROW_DMA_UNROLL):
                row_copy(tile, slot, r8 * ROW_DMA_UNROLL + j).start(priority=j % 2)
            return c
        lax.fori_loop(0, tm // ROW_DMA_UNROLL, issue, 0)

    @pl.when(i == 0)
    def _():
        gather(0, 0)

    @pl.when(i + 1 < pl.num_programs(0))
    def _():
        gather(i + 1, (i + 1) % 2)

    slot = i % 2

    def drain(r8, c):
        for j in range(ROW_DMA_UNROLL):
            row_copy(i, slot, r8 * ROW_DMA_UNROLL + j).wait()
        return c

    lax.fori_loop(0, tm // ROW_DMA_UNROLL, drain, 0)

    x2 = x1_ref[...] + mod_ref[5:6, :] * ybuf_ref[slot]
    o_ref[...] = _rmsnorm_rows(x2) * fg_ref[...]


def _final(dest, x1, mod3, fg, ys, seq, tm):
    t, d = x1.shape
    per_b = seq // tm
    grid_spec = pltpu.PrefetchScalarGridSpec(
        num_scalar_prefetch=1, grid=(t // tm,),
        in_specs=[pl.BlockSpec((tm, d), lambda i, dest: (i, 0)),
                  pl.BlockSpec((None, 6, d), lambda i, dest: (i // per_b, 0, 0)),
                  pl.BlockSpec((1, d), lambda i, dest: (0, 0)),
                  pl.BlockSpec(memory_space=pl.ANY)],
        out_specs=pl.BlockSpec((tm, d), lambda i, dest: (i, 0)),
        scratch_shapes=[pltpu.VMEM((2, tm, d), ys.dtype), pltpu.SemaphoreType.DMA((2,))])
    return pl.pallas_call(
        _final_kernel,
        grid_spec=grid_spec,
        out_shape=jax.ShapeDtypeStruct((t, d), f32),
        compiler_params=_cparams(1),
        name="final",
    )(dest, x1, mod3, fg, ys)


def _prep_w_in(w_in_l):
    sizes = (GLA_QK, GLA_QK, GLA_W, GLA_W, GLA_LOWRANK, SWA_W, SWA_KV, SWA_KV)
    off = [int(o) for o in np.concatenate([[0], np.cumsum(sizes)])]
    gla = w_in_l[:, off[0]:off[4]]
    ga = w_in_l[:, off[4]:off[5]]
    sq = [w_in_l[:, off[5] + h * SWA_HD:off[5] + (h + 1) * SWA_HD] for h in SWA_HEAD_ORDER]
    skv = w_in_l[:, off[6]:off[8]]
    pad = jnp.zeros((w_in_l.shape[0], IN_END - off[8]), w_in_l.dtype)
    return jnp.concatenate([gla] + sq + [skv, ga, pad], axis=1).astype(bf16)


def _prep_w_out(w_out_l):
    swa = [w_out_l[GLA_W + h * SWA_HD:GLA_W + (h + 1) * SWA_HD] for h in SWA_HEAD_ORDER]
    return jnp.concatenate([w_out_l[0:GLA_W]] + swa, axis=0).astype(bf16)


def _prep_router(w_grp_l, b_grp_l, w_exp_l, b_exp_l):
    d = w_grp_l.shape[0]
    n_e = N_GROUPS * EXPERTS_PER_GROUP
    wr = jnp.zeros((ROUTER_ROWS, d), f32)
    wr = wr.at[0:N_GROUPS, :].set(w_grp_l.T).at[EXPERTS_PER_GROUP:EXPERTS_PER_GROUP + n_e, :].set(w_exp_l.T)
    br = jnp.zeros((ROUTER_ROWS, 1), f32)
    br = br.at[0:N_GROUPS, 0].set(b_grp_l).at[EXPERTS_PER_GROUP:EXPERTS_PER_GROUP + n_e, 0].set(b_exp_l)
    wr_hi = wr.astype(bf16)
    wr_lo = (wr - wr_hi.astype(f32)).astype(bf16)
    return wr_hi, wr_lo, br


def _layer(x2, mod3, tab, e_mat, bsz, seq, norm_mix_g, w_in, gla_w_a2, gla_b_a2, gla_norm_g, swa_sinks, w_out,
           norm_ffn_g, w_grp, b_grp, w_exp, b_exp, w_gate, w_up, w_down, out_g, tm, tc, tm_moe):
    d = x2.shape[1]
    wa = jnp.pad(gla_w_a2, ((0, LANES - GLA_LOWRANK), (0, 0)))
    wa_hi = wa.astype(bf16)
    wa_lo = (wa - wa_hi.astype(f32)).astype(bf16)
    qk, vg, la, sq, skv = _inproj(x2, mod3, norm_mix_g.reshape(1, d), _prep_w_in(w_in), tab, e_mat,
                                  wa_hi, wa_lo, gla_b_a2.reshape(1, GLA_QK), seq, tm)
    o_gla = _gla(qk, vg, la, gla_norm_g.reshape(1, GLA_W), bsz, seq, tc)
    o_swa = _swa(swa_sinks.reshape(1, SWA_HEADS), sq, skv, bsz, seq)
    wr_hi, wr_lo, br = _prep_router(w_grp, b_grp, w_exp, b_exp)
    x1, hrow, route, cnt = _outproj(o_gla, o_swa, x2, mod3, _prep_w_out(w_out), norm_ffn_g.reshape(1, d),
                                    wr_hi, wr_lo, br, seq, tm)
    t = x2.shape[0]
    n_tiles_max = t // MOE_TM + N_PAIRS
    assert n_tiles_max <= 2 * LANES and t % MOE_TM == 0
    dest, tile_bucket, n_tiles = _plan(cnt, route, min(t, 2048))
    dest = dest.reshape(t)
    tile_bucket = tile_bucket.reshape(2 * LANES)
    n_tiles = n_tiles.reshape(LANES)[0:1]
    hs = _dispatch(dest, tile_bucket, n_tiles, hrow, n_tiles_max, min(t, 1024))
    ys = _moe(tile_bucket, n_tiles, hs, w_gate.astype(bf16), w_up.astype(bf16), w_down.astype(bf16))
    return _final(dest, x1, mod3, out_g.reshape(1, d), ys, seq, tm_moe)


def kernel(x, c, positions, ada_w, ada_b, norm_mix_g, w_in, gla_w_a2, gla_b_a2, gla_norm_g, swa_sinks, w_out,
           norm_ffn_g, w_grp, b_grp, w_exp, b_exp, w_gate, w_up, w_down, final_norm_g):
    bsz, seq, d = x.shape
    depth = ada_w.shape[0]
    assert depth == 1, "the final rmsnorm is fused into the last layer's MoE kernel"
    tab = _rope_tab(positions)
    e_mat = _rope_expand_matrix()
    x2 = x.reshape(bsz * seq, d)
    for l in range(depth):
        mod3 = _adaln(c, ada_w[l], ada_b[l]).reshape(bsz, 6, d)
        x2 = _layer(x2, mod3, tab, e_mat, bsz, seq, norm_mix_g[l], w_in[l], gla_w_a2[l], gla_b_a2[l],
                    gla_norm_g[l], swa_sinks[l], w_out[l], norm_ffn_g[l], w_grp[l], b_grp[l], w_exp[l], b_exp[l],
                    w_gate[l], w_up[l], w_down[l], final_norm_g, tm=512, tc=256, tm_moe=512)
    return x2.reshape(bsz, seq, d)
```

```python
import functools

import numpy as np
import jax
import jax.numpy as jnp
from jax import lax
from jax.experimental import pallas as pl
from jax.experimental.pallas import tpu as pltpu

f32 = jnp.float32
bf16 = jnp.bfloat16

GLA_HEADS = 4
GLA_DK = 64
GLA_DV = 128
GLA_LOWRANK = 16
GLA_GATE_NORM = 16.0
GLA_CHUNK = 64
SWA_HEADS = 8
SWA_KV_HEADS = 2
SWA_HD = 64
WINDOW = 128
ROPE_THETA = 500000.0
ROPE_DIMS = SWA_HD // 4
N_GROUPS = 4
EXPERTS_PER_GROUP = 8
D_EXPERT = 256
EPS = 1e-6

LANES = 128
VMEM_LIMIT = 52 * 1024 * 1024

GLA_QK = GLA_HEADS * GLA_DK
GLA_W = GLA_HEADS * GLA_DV
SWA_W = SWA_HEADS * SWA_HD
SWA_KV = SWA_KV_HEADS * SWA_HD
SWA_HEAD_ORDER = (0, 4, 1, 5, 2, 6, 3, 7)
ROUTE_ROWS = 8
ROUTER_ROWS = 128
EXPERT_BITS = 3
N_BUCKETS = N_GROUPS << (2 * EXPERT_BITS)
N_PAIRS = N_GROUPS * EXPERTS_PER_GROUP * (EXPERTS_PER_GROUP - 1) // 2
MOE_TM = 128
ROW_DMA_UNROLL = 8

TN = (((0,), (0,)), ((), ()))
NT = (((1,), (1,)), ((), ()))


def _cparams(n_axes):
    return pltpu.CompilerParams(dimension_semantics=("arbitrary",) * n_axes, vmem_limit_bytes=VMEM_LIMIT)


def _split_bf16(v):
    hi = v.astype(bf16)
    lo = (v - hi.astype(f32)).astype(bf16)
    return hi, lo


def _split_stack_bf16(v):
    hi = v.astype(bf16).astype(f32)
    return jnp.concatenate([hi, v - hi], axis=0).astype(bf16)


def _rmsnorm_rows(v):
    return v * lax.rsqrt(jnp.mean(v * v, axis=-1, keepdims=True) + EPS)


def _silu(v):
    return v * jax.nn.sigmoid(v)


def _adaln_kernel(c_ref, w_ref, b_ref, o_ref):
    ca = _silu(c_ref[...])
    o_ref[...] = jnp.dot(ca, w_ref[...], precision=lax.Precision.HIGHEST, preferred_element_type=f32) + b_ref[...]


def _adaln(c, w, b):
    bsz, d = c.shape
    n = w.shape[1]
    return pl.pallas_call(
        _adaln_kernel,
        grid=(n // d,),
        in_specs=[pl.BlockSpec((bsz, d), lambda j: (0, 0)),
                  pl.BlockSpec((d, d), lambda j: (0, j)),
                  pl.BlockSpec((1, d), lambda j: (0, j))],
        out_specs=pl.BlockSpec((bsz, d), lambda j: (0, j)),
        out_shape=jax.ShapeDtypeStruct((bsz, n), f32),
        compiler_params=_cparams(1),
        name="adaln",
    )(c, w, b.reshape(1, n))


def _rope_tab_kernel(pos_ref, invf_ref, o_ref):
    half = ROPE_DIMS // 2
    for b in range(pos_ref.shape[0]):
        ang = pos_ref[b:b + 1, :].astype(f32) * invf_ref[...]
        o_ref[b, 0:half, :] = jnp.cos(ang)
        o_ref[b, half:2 * half, :] = jnp.sin(ang)


def _rope_tab(positions):
    bsz, s = positions.shape
    half = ROPE_DIMS // 2
    inv_freq = (np.float32(ROPE_THETA) ** (-np.arange(0, ROPE_DIMS, 2, dtype=np.float32) / np.float32(ROPE_DIMS)))
    invf = jnp.asarray(inv_freq.astype(np.float32).reshape(half, 1))
    return pl.pallas_call(
        _rope_tab_kernel,
        out_shape=jax.ShapeDtypeStruct((bsz, 2 * half, s), f32),
        name="rope_tab",
    )(positions, invf)


def _rope_expand_matrix():
    half = ROPE_DIMS // 2
    e = np.zeros((2 * half, 3 * LANES), np.float32)
    for j in range(LANES):
        jj = j % SWA_HD
        if jj < half:
            e[jj, j] = 1.0
            e[half + jj, LANES + j] = -1.0
        elif jj < 2 * half:
            e[jj - half, j] = 1.0
            e[half + jj - half, 2 * LANES + j] = 1.0
    return jnp.asarray(np.concatenate([e, e], axis=0), dtype=bf16)


IN_QK0, IN_VG0, IN_SQ0, IN_SKV0, IN_A0, IN_END = 0, 512, 1536, 2048, 2304, 2432


def _inproj_kernel(x_ref, mod_ref, g_ref, w_ref, tab_ref, e_ref, wa_hi_ref, wa_lo_ref, ba_ref,
                   qk_ref, vg_ref, la_ref, sq_ref, skv_ref):
    x = x_ref[...]
    h = (_rmsnorm_rows(x) * g_ref[...]) * (1.0 + mod_ref[1:2, :]) + mod_ref[0:1, :]
    hb = h.astype(bf16)

    def proj(lo, hi):
        return jnp.dot(hb, w_ref[:, lo:hi], preferred_element_type=f32)

    al_hi, al_lo = _split_bf16(proj(IN_A0, IN_END))
    z = (jnp.dot(al_hi, wa_hi_ref[...], preferred_element_type=f32)
         + jnp.dot(al_lo, wa_hi_ref[...], preferred_element_type=f32)
         + jnp.dot(al_hi, wa_lo_ref[...], preferred_element_type=f32)) + ba_ref[...]
    la_ref[...] = (jnp.minimum(z, 0.0) - jnp.log1p(jnp.exp(-jnp.abs(z)))) * (1.0 / GLA_GATE_NORM)

    qk = proj(IN_QK0, IN_VG0)
    lane = lax.broadcasted_iota(jnp.int32, qk.shape, 1)
    qk_ref[...] = jnp.where(lane < GLA_QK, qk * (GLA_DK ** -0.5), qk).astype(bf16)
    vg_ref[...] = proj(IN_VG0, IN_SQ0).astype(bf16)

    tabs = lax.dot_general(_split_stack_bf16(tab_ref[...]), e_ref[...], TN, preferred_element_type=f32)
    lane1 = lax.broadcasted_iota(jnp.int32, (1, LANES), 1)
    cos_t = tabs[:, 0:LANES] + jnp.where((lane1 & (SWA_HD - 1)) < ROPE_DIMS, 0.0, 1.0)
    sa_t = tabs[:, LANES:2 * LANES]
    sb_t = tabs[:, 2 * LANES:3 * LANES]

    def rope(v):
        return (v * cos_t + pltpu.roll(v, LANES - ROPE_DIMS // 2, 1) * sa_t
                + pltpu.roll(v, ROPE_DIMS // 2, 1) * sb_t)

    sq = proj(IN_SQ0, IN_SKV0)
    for p in range(SWA_W // LANES):
        sq_ref[:, p * LANES:(p + 1) * LANES] = (rope(sq[:, p * LANES:(p + 1) * LANES]) * (SWA_HD ** -0.5)).astype(bf16)
    skv = proj(IN_SKV0, IN_A0)
    skv_ref[:, 0:LANES] = rope(skv[:, 0:LANES]).astype(bf16)
    skv_ref[:, LANES:2 * LANES] = skv[:, LANES:2 * LANES].astype(bf16)


def _inproj(x2, mod3, g, w, tab, e_mat, wa_hi, wa_lo, ba, seq, tm):
    t, d = x2.shape
    per_b = seq // tm
    row = lambda i: (i, 0)
    const = lambda i: (0, 0)
    return pl.pallas_call(
        _inproj_kernel,
        grid=(t // tm,),
        in_specs=[pl.BlockSpec((tm, d), row),
                  pl.BlockSpec((None, 6, d), lambda i: (i // per_b, 0, 0)),
                  pl.BlockSpec((1, d), const),
                  pl.BlockSpec(w.shape, const),
                  pl.BlockSpec((None, ROPE_DIMS, tm), lambda i: (i // per_b, 0, i % per_b)),
                  pl.BlockSpec(e_mat.shape, const),
                  pl.BlockSpec(wa_hi.shape, const),
                  pl.BlockSpec(wa_lo.shape, const),
                  pl.BlockSpec(ba.shape, const)],
        out_specs=[pl.BlockSpec((tm, 2 * GLA_QK), row),
                   pl.BlockSpec((tm, 2 * GLA_W), row),
                   pl.BlockSpec((tm, GLA_QK), row),
                   pl.BlockSpec((tm, SWA_W), row),
                   pl.BlockSpec((tm, 2 * SWA_KV), row)],
        out_shape=[jax.ShapeDtypeStruct((t, 2 * GLA_QK), bf16),
                   jax.ShapeDtypeStruct((t, 2 * GLA_W), bf16),
                   jax.ShapeDtypeStruct((t, GLA_QK), f32),
                   jax.ShapeDtypeStruct((t, SWA_W), bf16),
                   jax.ShapeDtypeStruct((t, 2 * SWA_KV), bf16)],
        compiler_params=_cparams(1),
        name="inproj",
    )(x2, mod3, g, w, tab, e_mat, wa_hi, wa_lo, ba)


def _gla_kernel(qk_ref, vg_ref, la_ref, gn_ref, o_ref, st_ref, *, chunks):
    c_len = GLA_CHUNK

    @pl.when(pl.program_id(1) == 0)
    def _():
        st_ref[...] = jnp.zeros_like(st_ref)

    r_i = lax.broadcasted_iota(jnp.int32, (c_len, c_len), 0)
    c_i = lax.broadcasted_iota(jnp.int32, (c_len, c_len), 1)
    causal = r_i >= c_i
    tri = jnp.where(causal, 1.0, 0.0).astype(bf16)
    ones = jnp.ones((c_len, LANES), bf16)
    lane = lax.broadcasted_iota(jnp.int32, (c_len, LANES), 1)
    first = lane < GLA_DK

    for c in range(chunks):
        rows = slice(c * c_len, (c + 1) * c_len)
        la_hi, la_lo = _split_bf16(la_ref[rows, :])
        b = jnp.dot(tri, la_hi, preferred_element_type=f32) + jnp.dot(tri, la_lo, preferred_element_type=f32)
        b_tot_t = (lax.dot_general(la_hi, ones, TN, preferred_element_type=f32)
                   + lax.dot_general(la_lo, ones, TN, preferred_element_type=f32))
        b_last = b[c_len - 1:c_len, :]
        q = qk_ref[rows, 0:GLA_QK].astype(f32)
        k = qk_ref[rows, GLA_QK:2 * GLA_QK].astype(f32)
        q_dec = q * jnp.exp(b)
        k_dec = (k * jnp.exp(-b)).astype(bf16)
        k_rem = (k * jnp.exp(b_last - b)).astype(bf16)
        decay = jnp.exp(b_tot_t)

        for p in range(GLA_HEADS // 2):
            ls = slice(p * LANES, (p + 1) * LANES)
            ss = slice(p * LANES, (p + 1) * LANES)
            s_prev = st_ref[ss, :]
            s_prev_b = s_prev.astype(bf16)
            qd = q_dec[:, ls]
            kv_halves = []
            for hh in range(2):
                h = 2 * p + hh
                vs = slice(h * GLA_DV, (h + 1) * GLA_DV)
                qm = jnp.where(first if hh == 0 else ~first, qd, 0.0).astype(bf16)
                v = vg_ref[rows, vs]
                scores = lax.dot_general(qm, k_dec[:, ls], NT, preferred_element_type=f32)
                scores = jnp.where(causal, scores, 0.0).astype(bf16)
                o = (jnp.dot(scores, v, preferred_element_type=f32)
                     + jnp.dot(qm, s_prev_b, preferred_element_type=f32))
                o = _rmsnorm_rows(o) * gn_ref[:, vs]
                gate = vg_ref[rows, GLA_W + h * GLA_DV:GLA_W + (h + 1) * GLA_DV].astype(f32)
                o_ref[rows, vs] = (o * _silu(gate)).astype(bf16)
                kv = lax.dot_general(k_rem[:, ls], v, TN, preferred_element_type=f32)
                kv_halves.append(kv[hh * GLA_DK:(hh + 1) * GLA_DK, :])
            st_ref[ss, :] = decay[ss, :] * s_prev + jnp.concatenate(kv_halves, axis=0)


def _gla(qk, vg, la, gn, bsz, seq, tc):
    t = qk.shape[0]
    nt = seq // tc
    row = lambda b, j: (b * nt + j, 0)
    return pl.pallas_call(
        functools.partial(_gla_kernel, chunks=tc // GLA_CHUNK),
        grid=(bsz, nt),
        in_specs=[pl.BlockSpec((tc, 2 * GLA_QK), row),
                  pl.BlockSpec((tc, 2 * GLA_W), row),
                  pl.BlockSpec((tc, GLA_QK), row),
                  pl.BlockSpec((1, GLA_W), lambda b, j: (0, 0))],
        out_specs=pl.BlockSpec((tc, GLA_W), row),
        out_shape=jax.ShapeDtypeStruct((t, GLA_W), bf16),
        scratch_shapes=[pltpu.VMEM((GLA_HEADS * GLA_DK, GLA_DV), f32)],
        compiler_params=_cparams(2),
        name="gla",
    )(qk, vg, la, gn)


def _swa_kernel(sink_ref, q_ref, kvp_ref, kvc_ref, o_ref):
    w = WINDOW
    j = pl.program_id(1)
    k2 = jnp.concatenate([kvp_ref[:, 0:LANES], kvc_ref[:, 0:LANES]], axis=0)
    v2 = jnp.concatenate([kvp_ref[:, LANES:2 * LANES], kvc_ref[:, LANES:2 * LANES]], axis=0)
    t_i = lax.broadcasted_iota(jnp.int32, (w, 2 * w), 0)
    s_i = lax.broadcasted_iota(jnp.int32, (w, 2 * w), 1)
    rel = t_i + w - s_i
    valid = (rel >= 0) & (rel < w) & ((j > 0) | (s_i >= w))
    lane = lax.broadcasted_iota(jnp.int32, (w, LANES), 1)
    first = lane < SWA_HD
    for p in range(SWA_W // LANES):
        qp = q_ref[:, p * LANES:(p + 1) * LANES].astype(f32)
        halves = []
        for hh in range(2):
            head = SWA_HEAD_ORDER[2 * p + hh]
            qm = jnp.where(first if hh == 0 else ~first, qp, 0.0).astype(bf16)
            s = lax.dot_general(qm, k2, NT, preferred_element_type=f32)
            s = jnp.where(valid, s, -jnp.inf)
            sink = sink_ref[0, head]
            m = jnp.maximum(jnp.max(s, axis=-1, keepdims=True), sink)
            pr = jnp.exp(s - m)
            denom = jnp.sum(pr, axis=-1, keepdims=True) + jnp.exp(sink - m)
            o = jnp.dot(pr.astype(bf16), v2, preferred_element_type=f32)
            halves.append(o / denom)
        o_ref[:, p * LANES:(p + 1) * LANES] = jnp.where(first, halves[0], halves[1]).astype(bf16)


def _swa(sinks, sq, skv, bsz, seq):
    t = sq.shape[0]
    nb = seq // WINDOW
    return pl.pallas_call(
        _swa_kernel,
        grid=(bsz, nb),
        in_specs=[pl.BlockSpec(memory_space=pltpu.SMEM),
                  pl.BlockSpec((WINDOW, SWA_W), lambda b, j: (b * nb + j, 0)),
                  pl.BlockSpec((WINDOW, 2 * SWA_KV), lambda b, j: (b * nb + jnp.maximum(j - 1, 0), 0)),
                  pl.BlockSpec((WINDOW, 2 * SWA_KV), lambda b, j: (b * nb + j, 0))],
        out_specs=pl.BlockSpec((WINDOW, SWA_W), lambda b, j: (b * nb + j, 0)),
        out_shape=jax.ShapeDtypeStruct((t, SWA_W), bf16),
        compiler_params=_cparams(2),
        name="swa",
    )(sinks, sq, skv, skv)


def _bf16_pieces(v):
    p1 = v.astype(bf16).astype(f32)
    p2 = (v - p1).astype(bf16).astype(f32)
    return p1, p2, v - p1 - p2


def _store_slabs(ref, v):
    tm, n = v.shape[0], v.shape[1] // LANES
    for c in range(n):
        ref[pl.ds(c, tm, stride=n), :] = v[:, c * LANES:(c + 1) * LANES]


def _load_slabs(ref, tm):
    n = ref.shape[0] // tm
    return jnp.concatenate([ref[pl.ds(c, tm, stride=n), :] for c in range(n)], axis=1)


def _outproj_kernel(og_ref, os_ref, x_ref, mod_ref, wo_ref, g2_ref, wr_hi_ref, wr_lo_ref, br_ref,
                    x1_ref, hrow_ref, route_ref, cnt_ref, run_ref):
    @pl.when(pl.program_id(0) == 0)
    def _():
        run_ref[...] = jnp.zeros_like(run_ref)

    mix = (jnp.dot(og_ref[...], wo_ref[0:GLA_W, :], preferred_element_type=f32)
           + jnp.dot(os_ref[...], wo_ref[GLA_W:GLA_W + SWA_W, :], preferred_element_type=f32))
    x1 = x_ref[...] + mod_ref[2:3, :] * mix
    x1_ref[...] = x1
    h = (_rmsnorm_rows(x1) * g2_ref[...]) * (1.0 + mod_ref[4:5, :]) + mod_ref[3:4, :]
    h_hi, h_lo = _split_bf16(h)
    _store_slabs(hrow_ref, h)

    lt = (lax.dot_general(wr_hi_ref[...], h_hi, NT, preferred_element_type=f32)
          + lax.dot_general(wr_hi_ref[...], h_lo, NT, preferred_element_type=f32)
          + lax.dot_general(wr_lo_ref[...], h_hi, NT, preferred_element_type=f32)) + br_ref[...]
    tm = lt.shape[1]
    e_g = EXPERTS_PER_GROUP
    row = lax.broadcasted_iota(jnp.int32, (e_g, tm), 0)
    neg = -jnp.inf
    gl = jnp.where(row < N_GROUPS, lt[0:e_g, :], neg)
    g_max = jnp.max(gl, axis=0, keepdims=True)
    g_gate = 1.0 / jnp.sum(jnp.exp(gl - g_max), axis=0, keepdims=True)
    g_idx = jnp.min(jnp.where(gl == g_max, row, e_g), axis=0, keepdims=True)
    sel = lt[e_g * N_GROUPS:e_g * (N_GROUPS + 1), :]
    for g in range(N_GROUPS - 2, -1, -1):
        sel = jnp.where(g_idx == g, lt[e_g * (g + 1):e_g * (g + 2), :], sel)
    t1 = jnp.max(sel, axis=0, keepdims=True)
    i1 = jnp.min(jnp.where(sel == t1, row, e_g), axis=0, keepdims=True)
    sel2 = jnp.where(row == i1, neg, sel)
    t2 = jnp.max(sel2, axis=0, keepdims=True)
    i2 = jnp.min(jnp.where(sel2 == t2, row, e_g), axis=0, keepdims=True)
    ex = jnp.exp(t2 - t1)
    w1 = g_gate / (1.0 + ex)
    w2 = g_gate * ex / (1.0 + ex)
    first_lo = i1 < i2
    w_lo = jnp.where(first_lo, w1, w2)
    w_hi = jnp.where(first_lo, w2, w1)
    bucket = ((g_idx << (2 * EXPERT_BITS)) | (jnp.minimum(i1, i2) << EXPERT_BITS) | jnp.maximum(i1, i2))
    onehot = lax.broadcasted_iota(jnp.int32, (N_BUCKETS, tm), 0) == bucket
    oh_b = jnp.where(onehot, 1.0, 0.0).astype(bf16)
    t_r = lax.broadcasted_iota(jnp.int32, (tm, tm), 0)
    t_c = lax.broadcasted_iota(jnp.int32, (tm, tm), 1)
    earlier = jnp.where(t_r < t_c, 1.0, 0.0).astype(bf16)
    run = run_ref[...]
    prefix = jnp.dot(oh_b, earlier, preferred_element_type=f32) + run
    rank = jnp.sum(jnp.where(onehot, prefix, 0.0), axis=0, keepdims=True)
    run = run + jnp.dot(oh_b, jnp.ones((tm, tm), bf16), preferred_element_type=f32)
    run_ref[...] = run
    cnt_ref[...] = run[:, 0:LANES]
    route_ref[...] = jnp.where(row == 0, bucket.astype(f32), jnp.where(row == 1, rank, jnp.where(
        row == 2, w_lo, jnp.where(row == 3, w_hi, 0.0))))


def _outproj(og, osw, x2, mod3, wo, g2, wr_hi, wr_lo, br, seq, tm):
    t, d = x2.shape
    per_b = seq // tm
    row = lambda i: (i, 0)
    const = lambda i: (0, 0)
    return pl.pallas_call(
        _outproj_kernel,
        grid=(t // tm,),
        in_specs=[pl.BlockSpec((tm, GLA_W), row),
                  pl.BlockSpec((tm, SWA_W), row),
                  pl.BlockSpec((tm, d), row),
                  pl.BlockSpec((None, 6, d), lambda i: (i // per_b, 0, 0)),
                  pl.BlockSpec(wo.shape, const),
                  pl.BlockSpec((1, d), const),
                  pl.BlockSpec(wr_hi.shape, const),
                  pl.BlockSpec(wr_lo.shape, const),
                  pl.BlockSpec(br.shape, const)],
        out_specs=[pl.BlockSpec((tm, d), row),
                   pl.BlockSpec((tm * d // LANES, LANES), row),
                   pl.BlockSpec((ROUTE_ROWS, tm), lambda i: (0, i)),
                   pl.BlockSpec((N_BUCKETS, LANES), const)],
        out_shape=[jax.ShapeDtypeStruct((t, d), f32),
                   jax.ShapeDtypeStruct((t * d // LANES, LANES), f32),
                   jax.ShapeDtypeStruct((ROUTE_ROWS, t), f32),
                   jax.ShapeDtypeStruct((N_BUCKETS, LANES), f32)],
        scratch_shapes=[pltpu.VMEM((N_BUCKETS, tm), f32)],
        compiler_params=_cparams(1),
        name="outproj",
    )(og, osw, x2, mod3, wo, g2, wr_hi, wr_lo, br)


def _plan_kernel(cnt_ref, route_ref, dest_ref, tb_ref, nt_ref):
    nb = N_BUCKETS
    tiles = jnp.floor((cnt_ref[...] + (MOE_TM - 1)) * (1.0 / MOE_TM))
    b_r = lax.broadcasted_iota(jnp.int32, (nb, nb), 0)
    b_c = lax.broadcasted_iota(jnp.int32, (nb, nb), 1)
    before = jnp.where(b_c < b_r, 1.0, 0.0).astype(bf16)
    t_start = jnp.dot(before, tiles.astype(bf16), preferred_element_type=f32)
    t_end = t_start + tiles
    tile_i = lax.broadcasted_iota(jnp.int32, (nb, 2 * LANES), 1).astype(f32)
    ended = jnp.where(jnp.concatenate([t_end, t_end], axis=1) <= tile_i, 1.0, 0.0)
    tb_ref[...] = jnp.sum(ended, axis=0, keepdims=True).astype(jnp.int32)
    nt_ref[...] = t_end[nb - 1:nb, :].astype(jnp.int32)
    tl = route_ref.shape[1]
    onehot = lax.broadcasted_iota(jnp.int32, (nb, tl), 0) == route_ref[0:1, :].astype(jnp.int32)
    start = lax.dot_general(t_start.astype(bf16), jnp.where(onehot, 1.0, 0.0).astype(bf16), TN,
                            preferred_element_type=f32)
    dest_ref[...] = (start[0:1, :] * MOE_TM + route_ref[1:2, :]).astype(jnp.int32)


def _plan(cnt, route, tl):
    t = route.shape[1]
    return pl.pallas_call(
        _plan_kernel,
        grid=(t // tl,),
        in_specs=[pl.BlockSpec(cnt.shape, lambda i: (0, 0)),
                  pl.BlockSpec((ROUTE_ROWS, tl), lambda i: (0, i))],
        out_specs=[pl.BlockSpec((1, tl), lambda i: (0, i)),
                   pl.BlockSpec((1, 2 * LANES), lambda i: (0, 0)),
                   pl.BlockSpec((1, LANES), lambda i: (0, 0))],
        out_shape=[jax.ShapeDtypeStruct((1, t), jnp.int32),
                   jax.ShapeDtypeStruct((1, 2 * LANES), jnp.int32),
                   jax.ShapeDtypeStruct((1, LANES), jnp.int32)],
        compiler_params=_cparams(1),
        name="plan",
    )(cnt, route)


def _slab(ref, token, n):
    return ref.at[pl.ds(pl.multiple_of(token * n, n), n)]


def _dispatch_kernel(dest_ref, tb_ref, nt_ref, src_ref, out_ref, zero_ref, sem_z, sem_r, *, rows, n_tiles_max):
    i = pl.program_id(0)
    n = src_ref.shape[0] // rows
    tile_rows = zero_ref.shape[0]

    def last_tile(k):
        return (k >= nt_ref[0]) | (tb_ref[k] != tb_ref[k + 1])

    def zero_copy(k):
        return pltpu.make_async_copy(zero_ref, out_ref.at[pl.ds(pl.multiple_of(k * tile_rows, tile_rows), tile_rows)],
                                     sem_z)

    @pl.when(i == 0)
    def _():
        zero_ref[...] = jnp.zeros_like(zero_ref)

        def start(k, c):
            @pl.when(last_tile(k))
            def _():
                zero_copy(k).start()
            return c

        def wait(k, c):
            @pl.when(last_tile(k))
            def _():
                zero_copy(k).wait()
            return c

        lax.fori_loop(0, n_tiles_max, start, 0)
        lax.fori_loop(0, n_tiles_max, wait, 0)

    base = i * rows

    def row_copy(r):
        return pltpu.make_async_copy(_slab(src_ref, r, n), _slab(out_ref, dest_ref[base + r], n), sem_r)

    def issue(r8, c):
        for j in range(ROW_DMA_UNROLL):
            row_copy(r8 * ROW_DMA_UNROLL + j).start(priority=j % 2)
        return c

    def drain(r8, c):
        for j in range(ROW_DMA_UNROLL):
            row_copy(r8 * ROW_DMA_UNROLL + j).wait()
        return c

    lax.fori_loop(0, rows // ROW_DMA_UNROLL, issue, 0)
    lax.fori_loop(0, rows // ROW_DMA_UNROLL, drain, 0)


def _dispatch(dest, tb, nt, hrow, n_tiles_max, rows):
    t = dest.shape[0]
    n = hrow.shape[0] // t
    grid_spec = pltpu.PrefetchScalarGridSpec(
        num_scalar_prefetch=3, grid=(t // rows,),
        in_specs=[pl.BlockSpec((rows * n, LANES), lambda i, dest, tb, nt: (i, 0))],
        out_specs=pl.BlockSpec(memory_space=pl.ANY),
        scratch_shapes=[pltpu.VMEM((MOE_TM * n, LANES), hrow.dtype), pltpu.SemaphoreType.DMA(()),
                        pltpu.SemaphoreType.DMA(())])
    return pl.pallas_call(
        functools.partial(_dispatch_kernel, rows=rows, n_tiles_max=n_tiles_max),
        grid_spec=grid_spec,
        out_shape=jax.ShapeDtypeStruct((n_tiles_max * MOE_TM * n, LANES), hrow.dtype),
        compiler_params=_cparams(1),
        name="dispatch",
    )(dest, tb, nt, hrow)


def _moe_kernel(tb_ref, nt_ref, hs_ref, wr_ref, br_ref, wg_ref, wu_ref, wd_ref, y_ref):
    i = pl.program_id(0)

    @pl.when(i < nt_ref[0])
    def _():
        e_g = EXPERTS_PER_GROUP
        bucket = tb_ref[i]
        group = bucket >> (2 * EXPERT_BITS)
        experts = ((bucket >> EXPERT_BITS) & (e_g - 1), bucket & (e_g - 1))
        h = _load_slabs(hs_ref, MOE_TM).astype(bf16)
        gate_up = [(jnp.dot(h, wg_ref[e], preferred_element_type=f32), jnp.dot(h, wu_ref[e], preferred_element_type=f32))
                   for e in experts]
        logits = jnp.dot(h, wr_ref[...], preferred_element_type=f32) + br_ref[...]
        lane = lax.broadcasted_iota(jnp.int32, logits.shape, 1)

        def pick(col):
            return jnp.sum(jnp.where(lane == col, logits, 0.0), axis=1, keepdims=True)

        gl = jnp.where(lane < N_GROUPS, logits, -jnp.inf)
        g_max = jnp.max(gl, axis=1, keepdims=True)
        g_gate = jnp.exp(pick(group) - g_max) / jnp.sum(jnp.exp(gl - g_max), axis=1, keepdims=True)
        sel = [pick(e_g * (group + 1) + e) for e in experts]
        top = jnp.maximum(sel[0], sel[1])
        p = [jnp.exp(s - top) for s in sel]
        scale = g_gate / (p[0] + p[1])
        y = None
        for k, e in enumerate(experts):
            a, u = gate_up[k]
            yk = (scale * p[k]) * jnp.dot((_silu(a) * u).astype(bf16), wd_ref[e], preferred_element_type=f32)
            y = yk if y is None else y + yk
        _store_slabs(y_ref, y)

    @pl.when(i >= nt_ref[0])
    def _():
        y_ref[...] = jnp.zeros_like(y_ref)


def _moe(tb, nt, hs, wr, br, wg, wu, wd):
    e_g = EXPERTS_PER_GROUP
    d = wg.shape[2]
    n = d // LANES
    n_tiles_max = hs.shape[0] // (MOE_TM * n)
    last = lambda i, tb, nt: jnp.minimum(i, nt[0] - 1)
    group = lambda i, tb, nt: (tb[last(i, tb, nt)] >> (2 * EXPERT_BITS), 0, 0, 0)
    const = lambda i, tb, nt: (0, 0)
    grid_spec = pltpu.PrefetchScalarGridSpec(
        num_scalar_prefetch=2, grid=(n_tiles_max,),
        in_specs=[pl.BlockSpec((MOE_TM * n, LANES), lambda i, tb, nt: (last(i, tb, nt), 0)),
                  pl.BlockSpec(wr.shape, const),
                  pl.BlockSpec(br.shape, const),
                  pl.BlockSpec((None, e_g, d, D_EXPERT), group),
                  pl.BlockSpec((None, e_g, d, D_EXPERT), group),
                  pl.BlockSpec((None, e_g, D_EXPERT, d), group)],
        out_specs=pl.BlockSpec((MOE_TM * n, LANES), lambda i, tb, nt: (i, 0)))
    return pl.pallas_call(
        _moe_kernel,
        grid_spec=grid_spec,
        out_shape=jax.ShapeDtypeStruct(hs.shape, f32),
        compiler_params=_cparams(1),
        name="moe",
    )(tb, nt, hs, wr, br, wg, wu, wd)


def _final_kernel(dest_ref, x1_ref, mod_ref, fg_ref, ys_ref, o_ref, ybuf_ref, sem):
    i = pl.program_id(0)
    tm = x1_ref.shape[0]
    n = ybuf_ref.shape[1] // tm

    def row_copy(tile, slot, r):
        return pltpu.make_async_copy(_slab(ys_ref, dest_ref[tile * tm + r], n), _slab(ybuf_ref.at[slot], r, n),
                                     sem.at[slot])

    def gather(tile, slot):
        def issue(r8, c):
            for j in range(ROW_DMA_UNROLL):
                row_copy(tile, slot, r8 * ROW_DMA_UNROLL + j).start(priority=j % 2)
            return c
        lax.fori_loop(0, tm // ROW_DMA_UNROLL, issue, 0)

    @pl.when(i == 0)
    def _():
        gather(0, 0)

    @pl.when(i + 1 < pl.num_programs(0))
    def _():
        gather(i + 1, (i + 1) % 2)

    slot = i % 2

    def drain(r8, c):
        for j in range(ROW_DMA_UNROLL):
            row_copy(i, slot, r8 * ROW_DMA_UNROLL + j).wait()
        return c

    lax.fori_loop(0, tm // ROW_DMA_UNROLL, drain, 0)

    x2 = x1_ref[...] + mod_ref[5:6, :] * _load_slabs(ybuf_ref.at[slot], tm)
    o_ref[...] = _rmsnorm_rows(x2) * fg_ref[...]


def _final(dest, x1, mod3, fg, ys, seq, tm):
    t, d = x1.shape
    per_b = seq // tm
    grid_spec = pltpu.PrefetchScalarGridSpec(
        num_scalar_prefetch=1, grid=(t // tm,),
        in_specs=[pl.BlockSpec((tm, d), lambda i, dest: (i, 0)),
                  pl.BlockSpec((None, 6, d), lambda i, dest: (i // per_b, 0, 0)),
                  pl.BlockSpec((1, d), lambda i, dest: (0, 0)),
                  pl.BlockSpec(memory_space=pl.ANY)],
        out_specs=pl.BlockSpec((tm, d), lambda i, dest: (i, 0)),
        scratch_shapes=[pltpu.VMEM((2, tm * d // LANES, LANES), ys.dtype), pltpu.SemaphoreType.DMA((2,))])
    return pl.pallas_call(
        _final_kernel,
        grid_spec=grid_spec,
        out_shape=jax.ShapeDtypeStruct((t, d), f32),
        compiler_params=_cparams(1),
        name="final",
    )(dest, x1, mod3, fg, ys)


def _prep_w_in(w_in_l):
    sizes = (GLA_QK, GLA_QK, GLA_W, GLA_W, GLA_LOWRANK, SWA_W, SWA_KV, SWA_KV)
    off = [int(o) for o in np.concatenate([[0], np.cumsum(sizes)])]
    gla = w_in_l[:, off[0]:off[4]]
    ga = w_in_l[:, off[4]:off[5]]
    sq = [w_in_l[:, off[5] + h * SWA_HD:off[5] + (h + 1) * SWA_HD] for h in SWA_HEAD_ORDER]
    skv = w_in_l[:, off[6]:off[8]]
    pad = jnp.zeros((w_in_l.shape[0], IN_END - off[8]), w_in_l.dtype)
    return jnp.concatenate([gla] + sq + [skv, ga, pad], axis=1).astype(bf16)


def _prep_w_out(w_out_l):
    swa = [w_out_l[GLA_W + h * SWA_HD:GLA_W + (h + 1) * SWA_HD] for h in SWA_HEAD_ORDER]
    return jnp.concatenate([w_out_l[0:GLA_W]] + swa, axis=0).astype(bf16)


def _prep_router(w_grp_l, b_grp_l, w_exp_l, b_exp_l):
    d = w_grp_l.shape[0]
    n_e = N_GROUPS * EXPERTS_PER_GROUP
    wr = jnp.zeros((ROUTER_ROWS, d), f32)
    wr = wr.at[0:N_GROUPS, :].set(w_grp_l.T).at[EXPERTS_PER_GROUP:EXPERTS_PER_GROUP + n_e, :].set(w_exp_l.T)
    br = jnp.zeros((ROUTER_ROWS, 1), f32)
    br = br.at[0:N_GROUPS, 0].set(b_grp_l).at[EXPERTS_PER_GROUP:EXPERTS_PER_GROUP + n_e, 0].set(b_exp_l)
    wr_hi = wr.astype(bf16)
    wr_lo = (wr - wr_hi.astype(f32)).astype(bf16)
    return wr_hi, wr_lo, br


def _layer(x2, mod3, tab, e_mat, bsz, seq, norm_mix_g, w_in, gla_w_a2, gla_b_a2, gla_norm_g, swa_sinks, w_out,
           norm_ffn_g, w_grp, b_grp, w_exp, b_exp, w_gate, w_up, w_down, out_g, tm, tc, tm_moe):
    d = x2.shape[1]
    wa = jnp.pad(gla_w_a2, ((0, LANES - GLA_LOWRANK), (0, 0)))
    wa_hi = wa.astype(bf16)
    wa_lo = (wa - wa_hi.astype(f32)).astype(bf16)
    qk, vg, la, sq, skv = _inproj(x2, mod3, norm_mix_g.reshape(1, d), _prep_w_in(w_in), tab, e_mat,
                                  wa_hi, wa_lo, gla_b_a2.reshape(1, GLA_QK), seq, tm)
    o_gla = _gla(qk, vg, la, gla_norm_g.reshape(1, GLA_W), bsz, seq, tc)
    o_swa = _swa(swa_sinks.reshape(1, SWA_HEADS), sq, skv, bsz, seq)
    wr_hi, wr_lo, br = _prep_router(w_grp, b_grp, w_exp, b_exp)
    x1, hrow, route, cnt = _outproj(o_gla, o_swa, x2, mod3, _prep_w_out(w_out), norm_ffn_g.reshape(1, d),
                                    wr_hi, wr_lo, br, seq, tm)
    t = x2.shape[0]
    n_tiles_max = t // MOE_TM + N_PAIRS
    assert n_tiles_max <= 2 * LANES and t % MOE_TM == 0
    dest, tile_bucket, n_tiles = _plan(cnt, route, min(t, 2048))
    dest = dest.reshape(t)
    tile_bucket = tile_bucket.reshape(2 * LANES)
    n_tiles = n_tiles.reshape(LANES)[0:1]
    hs = _dispatch(dest, tile_bucket, n_tiles, hrow, n_tiles_max, min(t, 1024))
    ys = _moe(tile_bucket, n_tiles, hs, wr_hi.T, br.reshape(1, ROUTER_ROWS),
              w_gate.astype(bf16), w_up.astype(bf16), w_down.astype(bf16))
    return _final(dest, x1, mod3, out_g.reshape(1, d), ys, seq, tm_moe)


def kernel(x, c, positions, ada_w, ada_b, norm_mix_g, w_in, gla_w_a2, gla_b_a2, gla_norm_g, swa_sinks, w_out,
           norm_ffn_g, w_grp, b_grp, w_exp, b_exp, w_gate, w_up, w_down, final_norm_g):
    bsz, seq, d = x.shape
    depth = ada_w.shape[0]
    assert depth == 1, "the final rmsnorm is fused into the last layer's MoE kernel"
    tab = _rope_tab(positions)
    e_mat = _rope_expand_matrix()
    x2 = x.reshape(bsz * seq, d)
    for l in range(depth):
        mod3 = _adaln(c, ada_w[l], ada_b[l]).reshape(bsz, 6, d)
        x2 = _layer(x2, mod3, tab, e_mat, bsz, seq, norm_mix_g[l], w_in[l], gla_w_a2[l], gla_b_a2[l],
                    gla_norm_g[l], swa_sinks[l], w_out[l], norm_ffn_g[l], w_grp[l], b_grp[l], w_exp[l], b_exp[l],
                    w_gate[l], w_up[l], w_down[l], final_norm_g, tm=512, tc=256, tm_moe=512)
    return x2.reshape(bsz, seq, d)
```

```python
import functools

import numpy as np
import jax
import jax.numpy as jnp
from jax import lax
from jax.experimental import pallas as pl
from jax.experimental.pallas import tpu as pltpu

f32 = jnp.float32
bf16 = jnp.bfloat16

GLA_HEADS = 4
GLA_DK = 64
GLA_DV = 128
GLA_LOWRANK = 16
GLA_GATE_NORM = 16.0
GLA_CHUNK = 64
SWA_HEADS = 8
SWA_KV_HEADS = 2
SWA_HD = 64
WINDOW = 128
ROPE_THETA = 500000.0
ROPE_DIMS = SWA_HD // 4
N_GROUPS = 4
EXPERTS_PER_GROUP = 8
D_EXPERT = 256
EPS = 1e-6

LANES = 128
VMEM_LIMIT = 52 * 1024 * 1024

GLA_QK = GLA_HEADS * GLA_DK
GLA_W = GLA_HEADS * GLA_DV
SWA_W = SWA_HEADS * SWA_HD
SWA_KV = SWA_KV_HEADS * SWA_HD
SWA_HEAD_ORDER = (0, 4, 1, 5, 2, 6, 3, 7)
ROUTE_ROWS = 8
ROUTER_ROWS = 128
EXPERT_BITS = 3
N_BUCKETS = N_GROUPS << (2 * EXPERT_BITS)
N_PAIRS = N_GROUPS * EXPERTS_PER_GROUP * (EXPERTS_PER_GROUP - 1) // 2
MOE_TM = 128
ROW_DMA_UNROLL = 8

TN = (((0,), (0,)), ((), ()))
NT = (((1,), (1,)), ((), ()))


def _cparams(n_axes):
    return pltpu.CompilerParams(dimension_semantics=("arbitrary",) * n_axes, vmem_limit_bytes=VMEM_LIMIT)


def _split_bf16(v):
    hi = v.astype(bf16)
    lo = (v - hi.astype(f32)).astype(bf16)
    return hi, lo


def _split_stack_bf16(v):
    hi = v.astype(bf16).astype(f32)
    return jnp.concatenate([hi, v - hi], axis=0).astype(bf16)


def _rmsnorm_rows(v):
    return v * lax.rsqrt(jnp.mean(v * v, axis=-1, keepdims=True) + EPS)


def _silu(v):
    return v * jax.nn.sigmoid(v)


def _adaln_kernel(c_ref, w_ref, b_ref, o_ref):
    ca = _silu(c_ref[...])
    o_ref[...] = jnp.dot(ca, w_ref[...], precision=lax.Precision.HIGHEST, preferred_element_type=f32) + b_ref[...]


def _adaln(c, w, b):
    bsz, d = c.shape
    n = w.shape[1]
    return pl.pallas_call(
        _adaln_kernel,
        grid=(n // d,),
        in_specs=[pl.BlockSpec((bsz, d), lambda j: (0, 0)),
                  pl.BlockSpec((d, d), lambda j: (0, j)),
                  pl.BlockSpec((1, d), lambda j: (0, j))],
        out_specs=pl.BlockSpec((bsz, d), lambda j: (0, j)),
        out_shape=jax.ShapeDtypeStruct((bsz, n), f32),
        compiler_params=_cparams(1),
        name="adaln",
    )(c, w, b.reshape(1, n))


def _rope_tab_kernel(pos_ref, invf_ref, o_ref):
    half = ROPE_DIMS // 2
    for b in range(pos_ref.shape[0]):
        ang = pos_ref[b:b + 1, :].astype(f32) * invf_ref[...]
        o_ref[b, 0:half, :] = jnp.cos(ang)
        o_ref[b, half:2 * half, :] = jnp.sin(ang)


def _rope_tab(positions):
    bsz, s = positions.shape
    half = ROPE_DIMS // 2
    inv_freq = (np.float32(ROPE_THETA) ** (-np.arange(0, ROPE_DIMS, 2, dtype=np.float32) / np.float32(ROPE_DIMS)))
    invf = jnp.asarray(inv_freq.astype(np.float32).reshape(half, 1))
    return pl.pallas_call(
        _rope_tab_kernel,
        out_shape=jax.ShapeDtypeStruct((bsz, 2 * half, s), f32),
        name="rope_tab",
    )(positions, invf)


def _rope_expand_matrix():
    half = ROPE_DIMS // 2
    e = np.zeros((2 * half, 3 * LANES), np.float32)
    for j in range(LANES):
        jj = j % SWA_HD
        if jj < half:
            e[jj, j] = 1.0
            e[half + jj, LANES + j] = -1.0
        elif jj < 2 * half:
            e[jj - half, j] = 1.0
            e[half + jj - half, 2 * LANES + j] = 1.0
    return jnp.asarray(np.concatenate([e, e], axis=0), dtype=bf16)


IN_QK0, IN_VG0, IN_SQ0, IN_SKV0, IN_A0, IN_END = 0, 512, 1536, 2048, 2304, 2432


def _inproj_kernel(x_ref, mod_ref, g_ref, w_ref, tab_ref, e_ref, wa_hi_ref, wa_lo_ref, ba_ref,
                   qk_ref, vg_ref, la_ref, sq_ref, skv_ref):
    x = x_ref[...]
    h = (_rmsnorm_rows(x) * g_ref[...]) * (1.0 + mod_ref[1:2, :]) + mod_ref[0:1, :]
    hb = h.astype(bf16)

    def proj(lo, hi):
        return jnp.dot(hb, w_ref[:, lo:hi], preferred_element_type=f32)

    al_hi, al_lo = _split_bf16(proj(IN_A0, IN_END))
    z = (jnp.dot(al_hi, wa_hi_ref[...], preferred_element_type=f32)
         + jnp.dot(al_lo, wa_hi_ref[...], preferred_element_type=f32)
         + jnp.dot(al_hi, wa_lo_ref[...], preferred_element_type=f32)) + ba_ref[...]
    la_ref[...] = (jnp.minimum(z, 0.0) - jnp.log1p(jnp.exp(-jnp.abs(z)))) * (1.0 / GLA_GATE_NORM)

    qk = proj(IN_QK0, IN_VG0)
    lane = lax.broadcasted_iota(jnp.int32, qk.shape, 1)
    qk_ref[...] = jnp.where(lane < GLA_QK, qk * (GLA_DK ** -0.5), qk).astype(bf16)
    vg_ref[...] = proj(IN_VG0, IN_SQ0).astype(bf16)

    tabs = lax.dot_general(_split_stack_bf16(tab_ref[...]), e_ref[...], TN, preferred_element_type=f32)
    lane1 = lax.broadcasted_iota(jnp.int32, (1, LANES), 1)
    cos_t = tabs[:, 0:LANES] + jnp.where((lane1 & (SWA_HD - 1)) < ROPE_DIMS, 0.0, 1.0)
    sa_t = tabs[:, LANES:2 * LANES]
    sb_t = tabs[:, 2 * LANES:3 * LANES]

    def rope(v):
        return (v * cos_t + pltpu.roll(v, LANES - ROPE_DIMS // 2, 1) * sa_t
                + pltpu.roll(v, ROPE_DIMS // 2, 1) * sb_t)

    sq = proj(IN_SQ0, IN_SKV0)
    for p in range(SWA_W // LANES):
        sq_ref[:, p * LANES:(p + 1) * LANES] = (rope(sq[:, p * LANES:(p + 1) * LANES]) * (SWA_HD ** -0.5)).astype(bf16)
    skv = proj(IN_SKV0, IN_A0)
    skv_ref[:, 0:LANES] = rope(skv[:, 0:LANES]).astype(bf16)
    skv_ref[:, LANES:2 * LANES] = skv[:, LANES:2 * LANES].astype(bf16)


def _inproj(x2, mod3, g, w, tab, e_mat, wa_hi, wa_lo, ba, seq, tm):
    t, d = x2.shape
    per_b = seq // tm
    row = lambda i: (i, 0)
    const = lambda i: (0, 0)
    return pl.pallas_call(
        _inproj_kernel,
        grid=(t // tm,),
        in_specs=[pl.BlockSpec((tm, d), row),
                  pl.BlockSpec((None, 6, d), lambda i: (i // per_b, 0, 0)),
                  pl.BlockSpec((1, d), const),
                  pl.BlockSpec(w.shape, const),
                  pl.BlockSpec((None, ROPE_DIMS, tm), lambda i: (i // per_b, 0, i % per_b)),
                  pl.BlockSpec(e_mat.shape, const),
                  pl.BlockSpec(wa_hi.shape, const),
                  pl.BlockSpec(wa_lo.shape, const),
                  pl.BlockSpec(ba.shape, const)],
        out_specs=[pl.BlockSpec((tm, 2 * GLA_QK), row),
                   pl.BlockSpec((tm, 2 * GLA_W), row),
                   pl.BlockSpec((tm, GLA_QK), row),
                   pl.BlockSpec((tm, SWA_W), row),
                   pl.BlockSpec((tm, 2 * SWA_KV), row)],
        out_shape=[jax.ShapeDtypeStruct((t, 2 * GLA_QK), bf16),
                   jax.ShapeDtypeStruct((t, 2 * GLA_W), bf16),
                   jax.ShapeDtypeStruct((t, GLA_QK), f32),
                   jax.ShapeDtypeStruct((t, SWA_W), bf16),
                   jax.ShapeDtypeStruct((t, 2 * SWA_KV), bf16)],
        compiler_params=_cparams(1),
        name="inproj",
    )(x2, mod3, g, w, tab, e_mat, wa_hi, wa_lo, ba)


def _gla_kernel(qk_ref, vg_ref, la_ref, gn_ref, o_ref, st_ref, *, chunks):
    c_len = GLA_CHUNK

    @pl.when(pl.program_id(1) == 0)
    def _():
        st_ref[...] = jnp.zeros_like(st_ref)

    r_i = lax.broadcasted_iota(jnp.int32, (c_len, c_len), 0)
    c_i = lax.broadcasted_iota(jnp.int32, (c_len, c_len), 1)
    causal = r_i >= c_i
    tri = jnp.where(causal, 1.0, 0.0).astype(bf16)
    ones = jnp.ones((c_len, LANES), bf16)
    lane = lax.broadcasted_iota(jnp.int32, (c_len, LANES), 1)
    first = lane < GLA_DK

    for c in range(chunks):
        rows = slice(c * c_len, (c + 1) * c_len)
        la_hi, la_lo = _split_bf16(la_ref[rows, :])
        b = jnp.dot(tri, la_hi, preferred_element_type=f32) + jnp.dot(tri, la_lo, preferred_element_type=f32)
        b_tot_t = (lax.dot_general(la_hi, ones, TN, preferred_element_type=f32)
                   + lax.dot_general(la_lo, ones, TN, preferred_element_type=f32))
        b_last = b[c_len - 1:c_len, :]
        q = qk_ref[rows, 0:GLA_QK].astype(f32)
        k = qk_ref[rows, GLA_QK:2 * GLA_QK].astype(f32)
        q_dec = q * jnp.exp(b)
        k_dec = (k * jnp.exp(-b)).astype(bf16)
        k_rem = (k * jnp.exp(b_last - b)).astype(bf16)
        decay = jnp.exp(b_tot_t)

        for p in range(GLA_HEADS // 2):
            ls = slice(p * LANES, (p + 1) * LANES)
            ss = slice(p * LANES, (p + 1) * LANES)
            s_prev = st_ref[ss, :]
            s_prev_b = s_prev.astype(bf16)
            qd = q_dec[:, ls]
            kv_halves = []
            for hh in range(2):
                h = 2 * p + hh
                vs = slice(h * GLA_DV, (h + 1) * GLA_DV)
                qm = jnp.where(first if hh == 0 else ~first, qd, 0.0).astype(bf16)
                v = vg_ref[rows, vs]
                scores = lax.dot_general(qm, k_dec[:, ls], NT, preferred_element_type=f32)
                scores = jnp.where(causal, scores, 0.0).astype(bf16)
                o = (jnp.dot(scores, v, preferred_element_type=f32)
                     + jnp.dot(qm, s_prev_b, preferred_element_type=f32))
                o = _rmsnorm_rows(o) * gn_ref[:, vs]
                gate = vg_ref[rows, GLA_W + h * GLA_DV:GLA_W + (h + 1) * GLA_DV].astype(f32)
                o_ref[rows, vs] = (o * _silu(gate)).astype(bf16)
                kv = lax.dot_general(k_rem[:, ls], v, TN, preferred_element_type=f32)
                kv_halves.append(kv[hh * GLA_DK:(hh + 1) * GLA_DK, :])
            st_ref[ss, :] = decay[ss, :] * s_prev + jnp.concatenate(kv_halves, axis=0)


def _gla(qk, vg, la, gn, bsz, seq, tc):
    t = qk.shape[0]
    nt = seq // tc
    row = lambda b, j: (b * nt + j, 0)
    return pl.pallas_call(
        functools.partial(_gla_kernel, chunks=tc // GLA_CHUNK),
        grid=(bsz, nt),
        in_specs=[pl.BlockSpec((tc, 2 * GLA_QK), row),
                  pl.BlockSpec((tc, 2 * GLA_W), row),
                  pl.BlockSpec((tc, GLA_QK), row),
                  pl.BlockSpec((1, GLA_W), lambda b, j: (0, 0))],
        out_specs=pl.BlockSpec((tc, GLA_W), row),
        out_shape=jax.ShapeDtypeStruct((t, GLA_W), bf16),
        scratch_shapes=[pltpu.VMEM((GLA_HEADS * GLA_DK, GLA_DV), f32)],
        compiler_params=_cparams(2),
        name="gla",
    )(qk, vg, la, gn)


def _swa_kernel(sink_ref, q_ref, kvp_ref, kvc_ref, o_ref):
    w = WINDOW
    j = pl.program_id(1)
    t_i = lax.broadcasted_iota(jnp.int32, (w, 2 * w), 0)
    s_i = lax.broadcasted_iota(jnp.int32, (w, 2 * w), 1)
    rel = t_i + w - s_i
    band = (rel >= 0) & (rel < w)
    lane = lax.broadcasted_iota(jnp.int32, (w, LANES), 1)
    first = lane < SWA_HD
    for blk in range(q_ref.shape[0] // w):
        rows = slice(blk * w, (blk + 1) * w)
        prev = kvp_ref if blk == 0 else kvc_ref.at[pl.ds((blk - 1) * w, w)]
        k2 = jnp.concatenate([prev[:, 0:LANES], kvc_ref[rows, 0:LANES]], axis=0)
        v2 = jnp.concatenate([prev[:, LANES:2 * LANES], kvc_ref[rows, LANES:2 * LANES]], axis=0)
        valid = band & ((j > 0) | (s_i >= w)) if blk == 0 else band
        for p in range(SWA_W // LANES):
            qp = q_ref[rows, p * LANES:(p + 1) * LANES].astype(f32)
            halves = []
            for hh in range(2):
                head = SWA_HEAD_ORDER[2 * p + hh]
                qm = jnp.where(first if hh == 0 else ~first, qp, 0.0).astype(bf16)
                s = lax.dot_general(qm, k2, NT, preferred_element_type=f32)
                s = jnp.where(valid, s, -jnp.inf)
                sink = sink_ref[0, head]
                m = jnp.maximum(jnp.max(s, axis=-1, keepdims=True), sink)
                pr = jnp.exp(s - m)
                denom = jnp.sum(pr, axis=-1, keepdims=True) + jnp.exp(sink - m)
                o = jnp.dot(pr.astype(bf16), v2, preferred_element_type=f32)
                halves.append(o / denom)
            o_ref[rows, p * LANES:(p + 1) * LANES] = jnp.where(first, halves[0], halves[1]).astype(bf16)


def _swa(sinks, sq, skv, bsz, seq, blocks):
    t = sq.shape[0]
    ts = blocks * WINDOW
    ns = seq // ts
    tile = lambda b, j: (b * ns + j, 0)
    return pl.pallas_call(
        _swa_kernel,
        grid=(bsz, ns),
        in_specs=[pl.BlockSpec(memory_space=pltpu.SMEM),
                  pl.BlockSpec((ts, SWA_W), tile),
                  pl.BlockSpec((WINDOW, 2 * SWA_KV), lambda b, j: ((b * ns + j) * blocks - jnp.minimum(j, 1), 0)),
                  pl.BlockSpec((ts, 2 * SWA_KV), tile)],
        out_specs=pl.BlockSpec((ts, SWA_W), tile),
        out_shape=jax.ShapeDtypeStruct((t, SWA_W), bf16),
        compiler_params=_cparams(2),
        name="swa",
    )(sinks, sq, skv, skv)


def _bf16_pieces(v):
    p1 = v.astype(bf16).astype(f32)
    p2 = (v - p1).astype(bf16).astype(f32)
    return p1, p2, v - p1 - p2


def _store_slabs(ref, v):
    tm, n = v.shape[0], v.shape[1] // LANES
    for c in range(n):
        ref[pl.ds(c, tm, stride=n), :] = v[:, c * LANES:(c + 1) * LANES]


def _load_slabs(ref, tm):
    n = ref.shape[0] // tm
    return jnp.concatenate([ref[pl.ds(c, tm, stride=n), :] for c in range(n)], axis=1)


def _outproj_kernel(og_ref, os_ref, x_ref, mod_ref, wo_ref, g2_ref, wr_hi_ref, wr_lo_ref, br_ref,
                    x1_ref, hrow_ref, route_ref, cnt_ref, run_ref):
    @pl.when(pl.program_id(0) == 0)
    def _():
        run_ref[...] = jnp.zeros_like(run_ref)

    mix = (jnp.dot(og_ref[...], wo_ref[0:GLA_W, :], preferred_element_type=f32)
           + jnp.dot(os_ref[...], wo_ref[GLA_W:GLA_W + SWA_W, :], preferred_element_type=f32))
    x1 = x_ref[...] + mod_ref[2:3, :] * mix
    x1_ref[...] = x1
    h = (_rmsnorm_rows(x1) * g2_ref[...]) * (1.0 + mod_ref[4:5, :]) + mod_ref[3:4, :]
    h_hi, h_lo = _split_bf16(h)
    _store_slabs(hrow_ref, h)

    lt = (lax.dot_general(wr_hi_ref[...], h_hi, NT, preferred_element_type=f32)
          + lax.dot_general(wr_hi_ref[...], h_lo, NT, preferred_element_type=f32)
          + lax.dot_general(wr_lo_ref[...], h_hi, NT, preferred_element_type=f32)) + br_ref[...]
    tm = lt.shape[1]
    e_g = EXPERTS_PER_GROUP
    row = lax.broadcasted_iota(jnp.int32, (e_g, tm), 0)
    neg = -jnp.inf
    gl = jnp.where(row < N_GROUPS, lt[0:e_g, :], neg)
    g_max = jnp.max(gl, axis=0, keepdims=True)
    g_gate = 1.0 / jnp.sum(jnp.exp(gl - g_max), axis=0, keepdims=True)
    g_idx = jnp.min(jnp.where(gl == g_max, row, e_g), axis=0, keepdims=True)
    sel = lt[e_g * N_GROUPS:e_g * (N_GROUPS + 1), :]
    for g in range(N_GROUPS - 2, -1, -1):
        sel = jnp.where(g_idx == g, lt[e_g * (g + 1):e_g * (g + 2), :], sel)
    t1 = jnp.max(sel, axis=0, keepdims=True)
    i1 = jnp.min(jnp.where(sel == t1, row, e_g), axis=0, keepdims=True)
    sel2 = jnp.where(row == i1, neg, sel)
    t2 = jnp.max(sel2, axis=0, keepdims=True)
    i2 = jnp.min(jnp.where(sel2 == t2, row, e_g), axis=0, keepdims=True)
    ex = jnp.exp(t2 - t1)
    w1 = g_gate / (1.0 + ex)
    w2 = g_gate * ex / (1.0 + ex)
    first_lo = i1 < i2
    w_lo = jnp.where(first_lo, w1, w2)
    w_hi = jnp.where(first_lo, w2, w1)
    bucket = ((g_idx << (2 * EXPERT_BITS)) | (jnp.minimum(i1, i2) << EXPERT_BITS) | jnp.maximum(i1, i2))
    onehot = lax.broadcasted_iota(jnp.int32, (N_BUCKETS, tm), 0) == bucket
    oh_b = jnp.where(onehot, 1.0, 0.0).astype(bf16)
    t_r = lax.broadcasted_iota(jnp.int32, (tm, tm), 0)
    t_c = lax.broadcasted_iota(jnp.int32, (tm, tm), 1)
    earlier = jnp.where(t_r < t_c, 1.0, 0.0).astype(bf16)
    run = run_ref[...]
    prefix = jnp.dot(oh_b, earlier, preferred_element_type=f32) + run
    rank = jnp.sum(jnp.where(onehot, prefix, 0.0), axis=0, keepdims=True)
    run = run + jnp.dot(oh_b, jnp.ones((tm, tm), bf16), preferred_element_type=f32)
    run_ref[...] = run
    cnt_ref[...] = run[:, 0:LANES]
    route_ref[...] = jnp.where(row == 0, bucket.astype(f32), jnp.where(row == 1, rank, jnp.where(
        row == 2, w_lo, jnp.where(row == 3, w_hi, 0.0))))


def _outproj(og, osw, x2, mod3, wo, g2, wr_hi, wr_lo, br, seq, tm):
    t, d = x2.shape
    per_b = seq // tm
    row = lambda i: (i, 0)
    const = lambda i: (0, 0)
    return pl.pallas_call(
        _outproj_kernel,
        grid=(t // tm,),
        in_specs=[pl.BlockSpec((tm, GLA_W), row),
                  pl.BlockSpec((tm, SWA_W), row),
                  pl.BlockSpec((tm, d), row),
                  pl.BlockSpec((None, 6, d), lambda i: (i // per_b, 0, 0)),
                  pl.BlockSpec(wo.shape, const),
                  pl.BlockSpec((1, d), const),
                  pl.BlockSpec(wr_hi.shape, const),
                  pl.BlockSpec(wr_lo.shape, const),
                  pl.BlockSpec(br.shape, const)],
        out_specs=[pl.BlockSpec((tm, d), row),
                   pl.BlockSpec((tm * d // LANES, LANES), row),
                   pl.BlockSpec((ROUTE_ROWS, tm), lambda i: (0, i)),
                   pl.BlockSpec((N_BUCKETS, LANES), const)],
        out_shape=[jax.ShapeDtypeStruct((t, d), f32),
                   jax.ShapeDtypeStruct((t * d // LANES, LANES), f32),
                   jax.ShapeDtypeStruct((ROUTE_ROWS, t), f32),
                   jax.ShapeDtypeStruct((N_BUCKETS, LANES), f32)],
        scratch_shapes=[pltpu.VMEM((N_BUCKETS, tm), f32)],
        compiler_params=_cparams(1),
        name="outproj",
    )(og, osw, x2, mod3, wo, g2, wr_hi, wr_lo, br)


def _plan_kernel(cnt_ref, route_ref, dest_ref, tb_ref, nt_ref):
    nb = N_BUCKETS
    tiles = jnp.floor((cnt_ref[...] + (MOE_TM - 1)) * (1.0 / MOE_TM))
    b_r = lax.broadcasted_iota(jnp.int32, (nb, nb), 0)
    b_c = lax.broadcasted_iota(jnp.int32, (nb, nb), 1)
    before = jnp.where(b_c < b_r, 1.0, 0.0).astype(bf16)
    t_start = jnp.dot(before, tiles.astype(bf16), preferred_element_type=f32)
    t_end = t_start + tiles
    tile_i = lax.broadcasted_iota(jnp.int32, (nb, 2 * LANES), 1).astype(f32)
    ended = jnp.where(jnp.concatenate([t_end, t_end], axis=1) <= tile_i, 1.0, 0.0)
    tb_ref[...] = jnp.sum(ended, axis=0, keepdims=True).astype(jnp.int32)
    nt_ref[...] = t_end[nb - 1:nb, :].astype(jnp.int32)
    tl = route_ref.shape[1]
    onehot = lax.broadcasted_iota(jnp.int32, (nb, tl), 0) == route_ref[0:1, :].astype(jnp.int32)
    start = lax.dot_general(t_start.astype(bf16), jnp.where(onehot, 1.0, 0.0).astype(bf16), TN,
                            preferred_element_type=f32)
    dest_ref[...] = (start[0:1, :] * MOE_TM + route_ref[1:2, :]).astype(jnp.int32)


def _plan(cnt, route, tl):
    t = route.shape[1]
    return pl.pallas_call(
        _plan_kernel,
        grid=(t // tl,),
        in_specs=[pl.BlockSpec(cnt.shape, lambda i: (0, 0)),
                  pl.BlockSpec((ROUTE_ROWS, tl), lambda i: (0, i))],
        out_specs=[pl.BlockSpec((1, tl), lambda i: (0, i)),
                   pl.BlockSpec((1, 2 * LANES), lambda i: (0, 0)),
                   pl.BlockSpec((1, LANES), lambda i: (0, 0))],
        out_shape=[jax.ShapeDtypeStruct((1, t), jnp.int32),
                   jax.ShapeDtypeStruct((1, 2 * LANES), jnp.int32),
                   jax.ShapeDtypeStruct((1, LANES), jnp.int32)],
        compiler_params=_cparams(1),
        name="plan",
    )(cnt, route)


def _slab(ref, token, n):
    return ref.at[pl.ds(pl.multiple_of(token * n, n), n)]


def _dispatch_kernel(dest_ref, tb_ref, nt_ref, src_ref, wg_ref, wu_ref, wd_ref, out_ref, wg_out, wu_out, wd_out,
                     zero_ref, sem_z, sem_r, *, rows, n_tiles_max):
    i = pl.program_id(0)
    n = src_ref.shape[0] // rows
    tile_rows = zero_ref.shape[0]

    def last_tile(k):
        return (k >= nt_ref[0]) | (tb_ref[k] != tb_ref[k + 1])

    def zero_copy(k):
        return pltpu.make_async_copy(zero_ref, out_ref.at[pl.ds(pl.multiple_of(k * tile_rows, tile_rows), tile_rows)],
                                     sem_z)

    @pl.when(i == 0)
    def _():
        zero_ref[...] = jnp.zeros_like(zero_ref)

        def start(k, c):
            @pl.when(last_tile(k))
            def _():
                zero_copy(k).start()
            return c

        def wait(k, c):
            @pl.when(last_tile(k))
            def _():
                zero_copy(k).wait()
            return c

        lax.fori_loop(0, n_tiles_max, start, 0)
        lax.fori_loop(0, n_tiles_max, wait, 0)

    base = i * rows

    def row_copy(r):
        return pltpu.make_async_copy(_slab(src_ref, r, n), _slab(out_ref, dest_ref[base + r], n), sem_r)

    def issue(r8, c):
        for j in range(ROW_DMA_UNROLL):
            row_copy(r8 * ROW_DMA_UNROLL + j).start(priority=j % 2)
        return c

    def drain(r8, c):
        for j in range(ROW_DMA_UNROLL):
            row_copy(r8 * ROW_DMA_UNROLL + j).wait()
        return c

    lax.fori_loop(0, rows // ROW_DMA_UNROLL, issue, 0)
    wg_out[...] = wg_ref[...].astype(bf16)
    wu_out[...] = wu_ref[...].astype(bf16)
    wd_out[...] = wd_ref[...].astype(bf16)
    lax.fori_loop(0, rows // ROW_DMA_UNROLL, drain, 0)


def _dispatch(dest, tb, nt, hrow, wg, wu, wd, n_tiles_max, rows):
    t = dest.shape[0]
    n = hrow.shape[0] // t
    steps = t // rows
    n_exp = wg.shape[0] * wg.shape[1]
    assert n_exp % steps == 0
    per = n_exp // steps
    flat = [w.reshape((n_exp,) + w.shape[2:]) for w in (wg, wu, wd)]
    w_spec = lambda w: pl.BlockSpec((per,) + w.shape[1:], lambda i, dest, tb, nt: (i, 0, 0))
    grid_spec = pltpu.PrefetchScalarGridSpec(
        num_scalar_prefetch=3, grid=(steps,),
        in_specs=[pl.BlockSpec((rows * n, LANES), lambda i, dest, tb, nt: (i, 0))] + [w_spec(w) for w in flat],
        out_specs=[pl.BlockSpec(memory_space=pl.ANY)] + [w_spec(w) for w in flat],
        scratch_shapes=[pltpu.VMEM((MOE_TM * n, LANES), hrow.dtype), pltpu.SemaphoreType.DMA(()),
                        pltpu.SemaphoreType.DMA(())])
    outs = pl.pallas_call(
        functools.partial(_dispatch_kernel, rows=rows, n_tiles_max=n_tiles_max),
        grid_spec=grid_spec,
        out_shape=[jax.ShapeDtypeStruct((n_tiles_max * MOE_TM * n, LANES), hrow.dtype)]
                  + [jax.ShapeDtypeStruct(w.shape, bf16) for w in flat],
        compiler_params=_cparams(1),
        name="dispatch",
    )(dest, tb, nt, hrow, *flat)
    return (outs[0],) + tuple(o.reshape(w.shape) for o, w in zip(outs[1:], (wg, wu, wd)))


def _moe_kernel(tb_ref, nt_ref, hs_ref, wr_ref, br_ref, wg_ref, wu_ref, wd_ref, y_ref):
    i = pl.program_id(0)

    @pl.when(i < nt_ref[0])
    def _():
        e_g = EXPERTS_PER_GROUP
        bucket = tb_ref[i]
        group = bucket >> (2 * EXPERT_BITS)
        experts = ((bucket >> EXPERT_BITS) & (e_g - 1), bucket & (e_g - 1))
        h = _load_slabs(hs_ref, MOE_TM).astype(bf16)
        gate_up = [(jnp.dot(h, wg_ref[e], preferred_element_type=f32), jnp.dot(h, wu_ref[e], preferred_element_type=f32))
                   for e in experts]
        logits = jnp.dot(h, wr_ref[...], preferred_element_type=f32) + br_ref[...]
        lane = lax.broadcasted_iota(jnp.int32, logits.shape, 1)

        def pick(col):
            return jnp.sum(jnp.where(lane == col, logits, 0.0), axis=1, keepdims=True)

        gl = jnp.where(lane < N_GROUPS, logits, -jnp.inf)
        g_max = jnp.max(gl, axis=1, keepdims=True)
        g_gate = jnp.exp(pick(group) - g_max) / jnp.sum(jnp.exp(gl - g_max), axis=1, keepdims=True)
        sel = [pick(e_g * (group + 1) + e) for e in experts]
        top = jnp.maximum(sel[0], sel[1])
        p = [jnp.exp(s - top) for s in sel]
        scale = g_gate / (p[0] + p[1])
        y = None
        for k, e in enumerate(experts):
            a, u = gate_up[k]
            yk = (scale * p[k]) * jnp.dot((_silu(a) * u).astype(bf16), wd_ref[e], preferred_element_type=f32)
            y = yk if y is None else y + yk
        _store_slabs(y_ref, y)

    @pl.when(i >= nt_ref[0])
    def _():
        y_ref[...] = jnp.zeros_like(y_ref)


def _moe(tb, nt, hs, wr, br, wg, wu, wd):
    e_g = EXPERTS_PER_GROUP
    d = wg.shape[2]
    n = d // LANES
    n_tiles_max = hs.shape[0] // (MOE_TM * n)
    last = lambda i, tb, nt: jnp.minimum(i, nt[0] - 1)
    group = lambda i, tb, nt: (tb[last(i, tb, nt)] >> (2 * EXPERT_BITS), 0, 0, 0)
    const = lambda i, tb, nt: (0, 0)
    grid_spec = pltpu.PrefetchScalarGridSpec(
        num_scalar_prefetch=2, grid=(n_tiles_max,),
        in_specs=[pl.BlockSpec((MOE_TM * n, LANES), lambda i, tb, nt: (last(i, tb, nt), 0)),
                  pl.BlockSpec(wr.shape, const),
                  pl.BlockSpec(br.shape, const),
                  pl.BlockSpec((None, e_g, d, D_EXPERT), group),
                  pl.BlockSpec((None, e_g, d, D_EXPERT), group),
                  pl.BlockSpec((None, e_g, D_EXPERT, d), group)],
        out_specs=pl.BlockSpec((MOE_TM * n, LANES), lambda i, tb, nt: (i, 0)))
    return pl.pallas_call(
        _moe_kernel,
        grid_spec=grid_spec,
        out_shape=jax.ShapeDtypeStruct(hs.shape, f32),
        compiler_params=_cparams(1),
        name="moe",
    )(tb, nt, hs, wr, br, wg, wu, wd)


def _final_kernel(dest_ref, x1_ref, mod_ref, fg_ref, ys_ref, o_ref, ybuf_ref, sem):
    i = pl.program_id(0)
    tm = x1_ref.shape[0]
    n = ybuf_ref.shape[1] // tm

    def row_copy(tile, slot, r):
        return pltpu.make_async_copy(_slab(ys_ref, dest_ref[tile * tm + r], n), _slab(ybuf_ref.at[slot], r, n),
                                     sem.at[slot])

    def gather(tile, slot):
        def issue(r8, c):
            for j in range(ROW_DMA_UNROLL):
                row_copy(tile, slot, r8 * ROW_DMA_UNROLL + j).start(priority=j % 2)
            return c
        lax.fori_loop(0, tm // ROW_DMA_UNROLL, issue, 0)

    @pl.when(i == 0)
    def _():
        gather(0, 0)

    @pl.when(i + 1 < pl.num_programs(0))
    def _():
        gather(i + 1, (i + 1) % 2)

    slot = i % 2

    def drain(r8, c):
        for j in range(ROW_DMA_UNROLL):
            row_copy(i, slot, r8 * ROW_DMA_UNROLL + j).wait()
        return c

    lax.fori_loop(0, tm // ROW_DMA_UNROLL, drain, 0)

    x2 = x1_ref[...] + mod_ref[5:6, :] * _load_slabs(ybuf_ref.at[slot], tm)
    o_ref[...] = _rmsnorm_rows(x2) * fg_ref[...]


def _final(dest, x1, mod3, fg, ys, seq, tm):
    t, d = x1.shape
    per_b = seq // tm
    grid_spec = pltpu.PrefetchScalarGridSpec(
        num_scalar_prefetch=1, grid=(t // tm,),
        in_specs=[pl.BlockSpec((tm, d), lambda i, dest: (i, 0)),
                  pl.BlockSpec((None, 6, d), lambda i, dest: (i // per_b, 0, 0)),
                  pl.BlockSpec((1, d), lambda i, dest: (0, 0)),
                  pl.BlockSpec(memory_space=pl.ANY)],
        out_specs=pl.BlockSpec((tm, d), lambda i, dest: (i, 0)),
        scratch_shapes=[pltpu.VMEM((2, tm * d // LANES, LANES), ys.dtype), pltpu.SemaphoreType.DMA((2,))])
    return pl.pallas_call(
        _final_kernel,
        grid_spec=grid_spec,
        out_shape=jax.ShapeDtypeStruct((t, d), f32),
        compiler_params=_cparams(1),
        name="final",
    )(dest, x1, mod3, fg, ys)


def _prep_w_in(w_in_l):
    sizes = (GLA_QK, GLA_QK, GLA_W, GLA_W, GLA_LOWRANK, SWA_W, SWA_KV, SWA_KV)
    off = [int(o) for o in np.concatenate([[0], np.cumsum(sizes)])]
    gla = w_in_l[:, off[0]:off[4]]
    ga = w_in_l[:, off[4]:off[5]]
    sq = [w_in_l[:, off[5] + h * SWA_HD:off[5] + (h + 1) * SWA_HD] for h in SWA_HEAD_ORDER]
    skv = w_in_l[:, off[6]:off[8]]
    pad = jnp.zeros((w_in_l.shape[0], IN_END - off[8]), w_in_l.dtype)
    return jnp.concatenate([gla] + sq + [skv, ga, pad], axis=1).astype(bf16)


def _prep_w_out(w_out_l):
    swa = [w_out_l[GLA_W + h * SWA_HD:GLA_W + (h + 1) * SWA_HD] for h in SWA_HEAD_ORDER]
    return jnp.concatenate([w_out_l[0:GLA_W]] + swa, axis=0).astype(bf16)


def _prep_router(w_grp_l, b_grp_l, w_exp_l, b_exp_l):
    d = w_grp_l.shape[0]
    n_e = N_GROUPS * EXPERTS_PER_GROUP
    wr = jnp.zeros((ROUTER_ROWS, d), f32)
    wr = wr.at[0:N_GROUPS, :].set(w_grp_l.T).at[EXPERTS_PER_GROUP:EXPERTS_PER_GROUP + n_e, :].set(w_exp_l.T)
    br = jnp.zeros((ROUTER_ROWS, 1), f32)
    br = br.at[0:N_GROUPS, 0].set(b_grp_l).at[EXPERTS_PER_GROUP:EXPERTS_PER_GROUP + n_e, 0].set(b_exp_l)
    wr_hi = wr.astype(bf16)
    wr_lo = (wr - wr_hi.astype(f32)).astype(bf16)
    return wr_hi, wr_lo, br


def _layer(x2, mod3, tab, e_mat, bsz, seq, norm_mix_g, w_in, gla_w_a2, gla_b_a2, gla_norm_g, swa_sinks, w_out,
           norm_ffn_g, w_grp, b_grp, w_exp, b_exp, w_gate, w_up, w_down, out_g, tm, tc, tm_moe):
    d = x2.shape[1]
    wa = jnp.pad(gla_w_a2, ((0, LANES - GLA_LOWRANK), (0, 0)))
    wa_hi = wa.astype(bf16)
    wa_lo = (wa - wa_hi.astype(f32)).astype(bf16)
    qk, vg, la, sq, skv = _inproj(x2, mod3, norm_mix_g.reshape(1, d), _prep_w_in(w_in), tab, e_mat,
                                  wa_hi, wa_lo, gla_b_a2.reshape(1, GLA_QK), seq, tm)
    o_gla = _gla(qk, vg, la, gla_norm_g.reshape(1, GLA_W), bsz, seq, tc)
    o_swa = _swa(swa_sinks.reshape(1, SWA_HEADS), sq, skv, bsz, seq, blocks=4)
    wr_hi, wr_lo, br = _prep_router(w_grp, b_grp, w_exp, b_exp)
    x1, hrow, route, cnt = _outproj(o_gla, o_swa, x2, mod3, _prep_w_out(w_out), norm_ffn_g.reshape(1, d),
                                    wr_hi, wr_lo, br, seq, tm)
    t = x2.shape[0]
    n_tiles_max = t // MOE_TM + N_PAIRS
    assert n_tiles_max <= 2 * LANES and t % MOE_TM == 0
    dest, tile_bucket, n_tiles = _plan(cnt, route, min(t, 2048))
    dest = dest.reshape(t)
    tile_bucket = tile_bucket.reshape(2 * LANES)
    n_tiles = n_tiles.reshape(LANES)[0:1]
    hs, wg_b, wu_b, wd_b = _dispatch(dest, tile_bucket, n_tiles, hrow, w_gate, w_up, w_down, n_tiles_max,
                                     min(t, 1024))
    ys = _moe(tile_bucket, n_tiles, hs, wr_hi.T, br.reshape(1, ROUTER_ROWS), wg_b, wu_b, wd_b)
    return _final(dest, x1, mod3, out_g.reshape(1, d), ys, seq, tm_moe)


def kernel(x, c, positions, ada_w, ada_b, norm_mix_g, w_in, gla_w_a2, gla_b_a2, gla_norm_g, swa_sinks, w_out,
           norm_ffn_g, w_grp, b_grp, w_exp, b_exp, w_gate, w_up, w_down, final_norm_g):
    bsz, seq, d = x.shape
    depth = ada_w.shape[0]
    assert depth == 1, "the final rmsnorm is fused into the last layer's MoE kernel"
    tab = _rope_tab(positions)
    e_mat = _rope_expand_matrix()
    x2 = x.reshape(bsz * seq, d)
    for l in range(depth):
        mod3 = _adaln(c, ada_w[l], ada_b[l]).reshape(bsz, 6, d)
        x2 = _layer(x2, mod3, tab, e_mat, bsz, seq, norm_mix_g[l], w_in[l], gla_w_a2[l], gla_b_a2[l],
                    gla_norm_g[l], swa_sinks[l], w_out[l], norm_ffn_g[l], w_grp[l], b_grp[l], w_exp[l], b_exp[l],
                    w_gate[l], w_up[l], w_down[l], final_norm_g, tm=512, tc=512, tm_moe=512)
    return x2.reshape(bsz, seq, d)
```

```python
import functools

import numpy as np
import jax
import jax.numpy as jnp
from jax import lax
from jax.experimental import pallas as pl
from jax.experimental.pallas import tpu as pltpu

f32 = jnp.float32
bf16 = jnp.bfloat16

GLA_HEADS = 4
GLA_DK = 64
GLA_DV = 128
GLA_LOWRANK = 16
GLA_GATE_NORM = 16.0
GLA_CHUNK = 64
SWA_HEADS = 8
SWA_KV_HEADS = 2
SWA_HD = 64
WINDOW = 128
ROPE_THETA = 500000.0
ROPE_DIMS = SWA_HD // 4
N_GROUPS = 4
EXPERTS_PER_GROUP = 8
D_EXPERT = 256
EPS = 1e-6

LANES = 128
VMEM_LIMIT = 52 * 1024 * 1024

GLA_QK = GLA_HEADS * GLA_DK
GLA_W = GLA_HEADS * GLA_DV
SWA_W = SWA_HEADS * SWA_HD
SWA_KV = SWA_KV_HEADS * SWA_HD
SWA_HEAD_ORDER = (0, 4, 1, 5, 2, 6, 3, 7)
ROUTE_ROWS = 8
ROUTER_ROWS = 128
EXPERT_BITS = 3
N_BUCKETS = N_GROUPS << (2 * EXPERT_BITS)
N_PAIRS = N_GROUPS * EXPERTS_PER_GROUP * (EXPERTS_PER_GROUP - 1) // 2
MOE_TM = 128
ROW_DMA_UNROLL = 8

TN = (((0,), (0,)), ((), ()))
NT = (((1,), (1,)), ((), ()))


def _cparams(n_axes):
    return pltpu.CompilerParams(dimension_semantics=("arbitrary",) * n_axes, vmem_limit_bytes=VMEM_LIMIT)


def _split_bf16(v):
    hi = v.astype(bf16)
    lo = (v - hi.astype(f32)).astype(bf16)
    return hi, lo


def _split_stack_bf16(v):
    hi = v.astype(bf16).astype(f32)
    return jnp.concatenate([hi, v - hi], axis=0).astype(bf16)


def _rmsnorm_rows(v):
    return v * lax.rsqrt(jnp.mean(v * v, axis=-1, keepdims=True) + EPS)


def _silu(v):
    return v * jax.nn.sigmoid(v)


def _adaln_kernel(c_ref, w_ref, b_ref, o_ref):
    ca = _silu(c_ref[...])
    o_ref[...] = jnp.dot(ca, w_ref[...], precision=lax.Precision.HIGHEST, preferred_element_type=f32) + b_ref[...]


def _adaln(c, w, b):
    bsz, d = c.shape
    n = w.shape[1]
    return pl.pallas_call(
        _adaln_kernel,
        grid=(n // d,),
        in_specs=[pl.BlockSpec((bsz, d), lambda j: (0, 0)),
                  pl.BlockSpec((d, d), lambda j: (0, j)),
                  pl.BlockSpec((1, d), lambda j: (0, j))],
        out_specs=pl.BlockSpec((bsz, d), lambda j: (0, j)),
        out_shape=jax.ShapeDtypeStruct((bsz, n), f32),
        compiler_params=_cparams(1),
        name="adaln",
    )(c, w, b.reshape(1, n))


def _rope_tab_kernel(pos_ref, invf_ref, o_ref):
    half = ROPE_DIMS // 2
    for b in range(pos_ref.shape[0]):
        ang = pos_ref[b:b + 1, :].astype(f32) * invf_ref[...]
        o_ref[b, 0:half, :] = jnp.cos(ang)
        o_ref[b, half:2 * half, :] = jnp.sin(ang)


def _rope_tab(positions):
    bsz, s = positions.shape
    half = ROPE_DIMS // 2
    inv_freq = (np.float32(ROPE_THETA) ** (-np.arange(0, ROPE_DIMS, 2, dtype=np.float32) / np.float32(ROPE_DIMS)))
    invf = jnp.asarray(inv_freq.astype(np.float32).reshape(half, 1))
    return pl.pallas_call(
        _rope_tab_kernel,
        out_shape=jax.ShapeDtypeStruct((bsz, 2 * half, s), f32),
        name="rope_tab",
    )(positions, invf)


def _rope_expand_matrix():
    half = ROPE_DIMS // 2
    e = np.zeros((2 * half, 3 * LANES), np.float32)
    for j in range(LANES):
        jj = j % SWA_HD
        if jj < half:
            e[jj, j] = 1.0
            e[half + jj, LANES + j] = -1.0
        elif jj < 2 * half:
            e[jj - half, j] = 1.0
            e[half + jj - half, 2 * LANES + j] = 1.0
    return jnp.asarray(np.concatenate([e, e], axis=0), dtype=bf16)


IN_QK0, IN_VG0, IN_SQ0, IN_SKV0, IN_A0, IN_END = 0, 512, 1536, 2048, 2304, 2432


def _inproj_kernel(x_ref, mod_ref, g_ref, w_ref, tab_ref, e_ref, wa_hi_ref, wa_lo_ref, ba_ref,
                   qk_ref, vg_ref, la_ref, sq_ref, skv_ref):
    x = x_ref[...]
    h = (_rmsnorm_rows(x) * g_ref[...]) * (1.0 + mod_ref[1:2, :]) + mod_ref[0:1, :]
    hb = h.astype(bf16)

    def proj(lo, hi):
        return jnp.dot(hb, w_ref[:, lo:hi], preferred_element_type=f32)

    al_hi, al_lo = _split_bf16(proj(IN_A0, IN_END))
    z = (jnp.dot(al_hi, wa_hi_ref[...], preferred_element_type=f32)
         + jnp.dot(al_lo, wa_hi_ref[...], preferred_element_type=f32)
         + jnp.dot(al_hi, wa_lo_ref[...], preferred_element_type=f32)) + ba_ref[...]
    la_ref[...] = (jnp.minimum(z, 0.0) - jnp.log1p(jnp.exp(-jnp.abs(z)))) * (1.0 / GLA_GATE_NORM)

    qk = proj(IN_QK0, IN_VG0)
    lane = lax.broadcasted_iota(jnp.int32, qk.shape, 1)
    qk_ref[...] = jnp.where(lane < GLA_QK, qk * (GLA_DK ** -0.5), qk).astype(bf16)
    vg_ref[...] = proj(IN_VG0, IN_SQ0).astype(bf16)

    tabs = lax.dot_general(_split_stack_bf16(tab_ref[...]), e_ref[...], TN, preferred_element_type=f32)
    lane1 = lax.broadcasted_iota(jnp.int32, (1, LANES), 1)
    cos_t = tabs[:, 0:LANES] + jnp.where((lane1 & (SWA_HD - 1)) < ROPE_DIMS, 0.0, 1.0)
    sa_t = tabs[:, LANES:2 * LANES]
    sb_t = tabs[:, 2 * LANES:3 * LANES]

    def rope(v):
        return (v * cos_t + pltpu.roll(v, LANES - ROPE_DIMS // 2, 1) * sa_t
                + pltpu.roll(v, ROPE_DIMS // 2, 1) * sb_t)

    sq = proj(IN_SQ0, IN_SKV0)
    for p in range(SWA_W // LANES):
        sq_ref[:, p * LANES:(p + 1) * LANES] = (rope(sq[:, p * LANES:(p + 1) * LANES]) * (SWA_HD ** -0.5)).astype(bf16)
    skv = proj(IN_SKV0, IN_A0)
    skv_ref[:, 0:LANES] = rope(skv[:, 0:LANES]).astype(bf16)
    skv_ref[:, LANES:2 * LANES] = skv[:, LANES:2 * LANES].astype(bf16)


def _inproj(x2, mod3, g, w, tab, e_mat, wa_hi, wa_lo, ba, seq, tm):
    t, d = x2.shape
    per_b = seq // tm
    row = lambda i: (i, 0)
    const = lambda i: (0, 0)
    return pl.pallas_call(
        _inproj_kernel,
        grid=(t // tm,),
        in_specs=[pl.BlockSpec((tm, d), row),
                  pl.BlockSpec((None, 6, d), lambda i: (i // per_b, 0, 0)),
                  pl.BlockSpec((1, d), const),
                  pl.BlockSpec(w.shape, const),
                  pl.BlockSpec((None, ROPE_DIMS, tm), lambda i: (i // per_b, 0, i % per_b)),
                  pl.BlockSpec(e_mat.shape, const),
                  pl.BlockSpec(wa_hi.shape, const),
                  pl.BlockSpec(wa_lo.shape, const),
                  pl.BlockSpec(ba.shape, const)],
        out_specs=[pl.BlockSpec((tm, 2 * GLA_QK), row),
                   pl.BlockSpec((tm, 2 * GLA_W), row),
                   pl.BlockSpec((tm, GLA_QK), row),
                   pl.BlockSpec((tm, SWA_W), row),
                   pl.BlockSpec((tm, 2 * SWA_KV), row)],
        out_shape=[jax.ShapeDtypeStruct((t, 2 * GLA_QK), bf16),
                   jax.ShapeDtypeStruct((t, 2 * GLA_W), bf16),
                   jax.ShapeDtypeStruct((t, GLA_QK), f32),
                   jax.ShapeDtypeStruct((t, SWA_W), bf16),
                   jax.ShapeDtypeStruct((t, 2 * SWA_KV), bf16)],
        compiler_params=_cparams(1),
        name="inproj",
    )(x2, mod3, g, w, tab, e_mat, wa_hi, wa_lo, ba)


def _gla_chunk(c, qk_ref, vg_ref, la_ref, gn_ref, o_ref, st_ref, consts):
    causal, tri, ones, first = consts
    c_len = GLA_CHUNK
    rows = slice(c * c_len, (c + 1) * c_len)
    la_hi, la_lo = _split_bf16(la_ref[rows, :])
    b = jnp.dot(tri, la_hi, preferred_element_type=f32) + jnp.dot(tri, la_lo, preferred_element_type=f32)
    b_tot_t = (lax.dot_general(la_hi, ones, TN, preferred_element_type=f32)
               + lax.dot_general(la_lo, ones, TN, preferred_element_type=f32))
    b_last = b[c_len - 1:c_len, :]
    q = qk_ref[rows, 0:GLA_QK].astype(f32)
    k = qk_ref[rows, GLA_QK:2 * GLA_QK].astype(f32)
    q_dec = q * jnp.exp(b)
    k_dec = (k * jnp.exp(-b)).astype(bf16)
    k_rem = (k * jnp.exp(b_last - b)).astype(bf16)
    decay = jnp.exp(b_tot_t)
    for p in range(GLA_HEADS // 2):
        ls = slice(p * LANES, (p + 1) * LANES)
        s_prev = st_ref[ls, :]
        s_prev_b = s_prev.astype(bf16)
        qd = q_dec[:, ls]
        kv_halves = []
        for hh in range(2):
            h = 2 * p + hh
            vs = slice(h * GLA_DV, (h + 1) * GLA_DV)
            qm = jnp.where(first if hh == 0 else ~first, qd, 0.0).astype(bf16)
            v = vg_ref[rows, vs]
            scores = lax.dot_general(qm, k_dec[:, ls], NT, preferred_element_type=f32)
            scores = jnp.where(causal, scores, 0.0).astype(bf16)
            o = (jnp.dot(scores, v, preferred_element_type=f32)
                 + jnp.dot(qm, s_prev_b, preferred_element_type=f32))
            o = _rmsnorm_rows(o) * gn_ref[:, vs]
            gate = vg_ref[rows, GLA_W + h * GLA_DV:GLA_W + (h + 1) * GLA_DV].astype(f32)
            o_ref[rows, vs] = (o * _silu(gate)).astype(bf16)
            kv = lax.dot_general(k_rem[:, ls], v, TN, preferred_element_type=f32)
            kv_halves.append(kv[hh * GLA_DK:(hh + 1) * GLA_DK, :])
        st_ref[ls, :] = decay[ls, :] * s_prev + jnp.concatenate(kv_halves, axis=0)


def _swa_block(blk, j, sink_ref, q_ref, kvp_ref, kvc_ref, o_ref, consts):
    band, s_i, first = consts
    w = WINDOW
    rows = slice(blk * w, (blk + 1) * w)
    prev = kvp_ref if blk == 0 else kvc_ref.at[pl.ds((blk - 1) * w, w)]
    k2 = jnp.concatenate([prev[:, 0:LANES], kvc_ref[rows, 0:LANES]], axis=0)
    v2 = jnp.concatenate([prev[:, LANES:2 * LANES], kvc_ref[rows, LANES:2 * LANES]], axis=0)
    valid = band & ((j > 0) | (s_i >= w)) if blk == 0 else band
    for p in range(SWA_W // LANES):
        qp = q_ref[rows, p * LANES:(p + 1) * LANES].astype(f32)
        halves = []
        for hh in range(2):
            head = SWA_HEAD_ORDER[2 * p + hh]
            qm = jnp.where(first if hh == 0 else ~first, qp, 0.0).astype(bf16)
            s = lax.dot_general(qm, k2, NT, preferred_element_type=f32)
            s = jnp.where(valid, s, -jnp.inf)
            sink = sink_ref[0, head]
            m = jnp.maximum(jnp.max(s, axis=-1, keepdims=True), sink)
            pr = jnp.exp(s - m)
            denom = jnp.sum(pr, axis=-1, keepdims=True) + jnp.exp(sink - m)
            o = jnp.dot(pr.astype(bf16), v2, preferred_element_type=f32)
            halves.append(o / denom)
        o_ref[rows, p * LANES:(p + 1) * LANES] = jnp.where(first, halves[0], halves[1]).astype(bf16)


def _gla_kernel(qk_ref, vg_ref, la_ref, gn_ref, o_ref, st_ref):
    @pl.when(pl.program_id(1) == 0)
    def _():
        st_ref[...] = jnp.zeros_like(st_ref)

    c_len = GLA_CHUNK
    causal = lax.broadcasted_iota(jnp.int32, (c_len, c_len), 0) >= lax.broadcasted_iota(jnp.int32, (c_len, c_len), 1)
    consts = (causal, jnp.where(causal, 1.0, 0.0).astype(bf16), jnp.ones((c_len, LANES), bf16),
              lax.broadcasted_iota(jnp.int32, (c_len, LANES), 1) < GLA_DK)
    for c in range(qk_ref.shape[0] // c_len):
        _gla_chunk(c, qk_ref, vg_ref, la_ref, gn_ref, o_ref, st_ref, consts)


def _swa_kernel(sink_ref, q_ref, kvp_ref, kvc_ref, o_ref):
    w = WINDOW
    s_i = lax.broadcasted_iota(jnp.int32, (w, 2 * w), 1)
    rel = lax.broadcasted_iota(jnp.int32, (w, 2 * w), 0) + w - s_i
    consts = ((rel >= 0) & (rel < w), s_i, lax.broadcasted_iota(jnp.int32, (w, LANES), 1) < SWA_HD)
    for blk in range(q_ref.shape[0] // w):
        _swa_block(blk, pl.program_id(1), sink_ref, q_ref, kvp_ref, kvc_ref, o_ref, consts)


def _mixers(sinks, qk, vg, la, gn, sq, skv, bsz, seq, ts):
    t = qk.shape[0]
    ns = seq // ts
    blocks = ts // WINDOW
    tile = lambda b, j: (b * ns + j, 0)
    o_gla = pl.pallas_call(
        _gla_kernel,
        grid=(bsz, ns),
        in_specs=[pl.BlockSpec((ts, 2 * GLA_QK), tile),
                  pl.BlockSpec((ts, 2 * GLA_W), tile),
                  pl.BlockSpec((ts, GLA_QK), tile),
                  pl.BlockSpec((1, GLA_W), lambda b, j: (0, 0))],
        out_specs=pl.BlockSpec((ts, GLA_W), tile),
        out_shape=jax.ShapeDtypeStruct((t, GLA_W), bf16),
        scratch_shapes=[pltpu.VMEM((GLA_HEADS * GLA_DK, GLA_DV), f32)],
        compiler_params=_cparams(2),
        name="gla",
    )(qk, vg, la, gn)
    o_swa = pl.pallas_call(
        _swa_kernel,
        grid=(bsz, ns),
        in_specs=[pl.BlockSpec(memory_space=pltpu.SMEM),
                  pl.BlockSpec((ts, SWA_W), tile),
                  pl.BlockSpec((WINDOW, 2 * SWA_KV), lambda b, j: ((b * ns + j) * blocks - jnp.minimum(j, 1), 0)),
                  pl.BlockSpec((ts, 2 * SWA_KV), tile)],
        out_specs=pl.BlockSpec((ts, SWA_W), tile),
        out_shape=jax.ShapeDtypeStruct((t, SWA_W), bf16),
        compiler_params=_cparams(2),
        name="swa",
    )(sinks, sq, skv, skv)
    return o_gla, o_swa


def _store_slabs(ref, v):
    tm, n = v.shape[0], v.shape[1] // LANES
    for c in range(n):
        ref[pl.ds(c, tm, stride=n), :] = v[:, c * LANES:(c + 1) * LANES]


def _load_slabs(ref, tm):
    n = ref.shape[0] // tm
    return jnp.concatenate([ref[pl.ds(c, tm, stride=n), :] for c in range(n)], axis=1)


def _outproj_kernel(og_ref, os_ref, x_ref, mod_ref, wo_ref, g2_ref, wr_hi_ref, wr_lo_ref, br_ref,
                    x1_ref, hrow_ref, route_ref, cnt_ref, run_ref):
    @pl.when(pl.program_id(0) == 0)
    def _():
        run_ref[...] = jnp.zeros_like(run_ref)

    mix = (jnp.dot(og_ref[...], wo_ref[0:GLA_W, :], preferred_element_type=f32)
           + jnp.dot(os_ref[...], wo_ref[GLA_W:GLA_W + SWA_W, :], preferred_element_type=f32))
    x1 = x_ref[...] + mod_ref[2:3, :] * mix
    x1_ref[...] = x1
    h = (_rmsnorm_rows(x1) * g2_ref[...]) * (1.0 + mod_ref[4:5, :]) + mod_ref[3:4, :]
    h_hi, h_lo = _split_bf16(h)
    _store_slabs(hrow_ref, h)

    lt = (lax.dot_general(wr_hi_ref[...], h_hi, NT, preferred_element_type=f32)
          + lax.dot_general(wr_hi_ref[...], h_lo, NT, preferred_element_type=f32)
          + lax.dot_general(wr_lo_ref[...], h_hi, NT, preferred_element_type=f32)) + br_ref[...]
    tm = lt.shape[1]
    e_g = EXPERTS_PER_GROUP
    row = lax.broadcasted_iota(jnp.int32, (e_g, tm), 0)
    neg = -jnp.inf
    gl = jnp.where(row < N_GROUPS, lt[0:e_g, :], neg)
    g_max = jnp.max(gl, axis=0, keepdims=True)
    g_gate = 1.0 / jnp.sum(jnp.exp(gl - g_max), axis=0, keepdims=True)
    g_idx = jnp.min(jnp.where(gl == g_max, row, e_g), axis=0, keepdims=True)
    sel = lt[e_g * N_GROUPS:e_g * (N_GROUPS + 1), :]
    for g in range(N_GROUPS - 2, -1, -1):
        sel = jnp.where(g_idx == g, lt[e_g * (g + 1):e_g * (g + 2), :], sel)
    t1 = jnp.max(sel, axis=0, keepdims=True)
    i1 = jnp.min(jnp.where(sel == t1, row, e_g), axis=0, keepdims=True)
    sel2 = jnp.where(row == i1, neg, sel)
    t2 = jnp.max(sel2, axis=0, keepdims=True)
    i2 = jnp.min(jnp.where(sel2 == t2, row, e_g), axis=0, keepdims=True)
    ex = jnp.exp(t2 - t1)
    w1 = g_gate / (1.0 + ex)
    w2 = g_gate * ex / (1.0 + ex)
    first_lo = i1 < i2
    w_lo = jnp.where(first_lo, w1, w2)
    w_hi = jnp.where(first_lo, w2, w1)
    bucket = ((g_idx << (2 * EXPERT_BITS)) | (jnp.minimum(i1, i2) << EXPERT_BITS) | jnp.maximum(i1, i2))
    onehot = lax.broadcasted_iota(jnp.int32, (N_BUCKETS, tm), 0) == bucket
    oh_b = jnp.where(onehot, 1.0, 0.0).astype(bf16)
    t_r = lax.broadcasted_iota(jnp.int32, (tm, tm), 0)
    t_c = lax.broadcasted_iota(jnp.int32, (tm, tm), 1)
    earlier = jnp.where(t_r < t_c, 1.0, 0.0).astype(bf16)
    run = run_ref[...]
    prefix = jnp.dot(oh_b, earlier, preferred_element_type=f32) + run
    rank = jnp.sum(jnp.where(onehot, prefix, 0.0), axis=0, keepdims=True)
    run = run + jnp.dot(oh_b, jnp.ones((tm, tm), bf16), preferred_element_type=f32)
    run_ref[...] = run
    cnt_ref[...] = run[:, 0:LANES]
    route_ref[...] = jnp.where(row == 0, bucket.astype(f32), jnp.where(row == 1, rank, jnp.where(
        row == 2, w_lo, jnp.where(row == 3, w_hi, 0.0))))


def _outproj(og, osw, x2, mod3, wo, g2, wr_hi, wr_lo, br, seq, tm):
    t, d = x2.shape
    per_b = seq // tm
    row = lambda i: (i, 0)
    const = lambda i: (0, 0)
    return pl.pallas_call(
        _outproj_kernel,
        grid=(t // tm,),
        in_specs=[pl.BlockSpec((tm, GLA_W), row),
                  pl.BlockSpec((tm, SWA_W), row),
                  pl.BlockSpec((tm, d), row),
                  pl.BlockSpec((None, 6, d), lambda i: (i // per_b, 0, 0)),
                  pl.BlockSpec(wo.shape, const),
                  pl.BlockSpec((1, d), const),
                  pl.BlockSpec(wr_hi.shape, const),
                  pl.BlockSpec(wr_lo.shape, const),
                  pl.BlockSpec(br.shape, const)],
        out_specs=[pl.BlockSpec((tm, d), row),
                   pl.BlockSpec((tm * d // LANES, LANES), row),
                   pl.BlockSpec((ROUTE_ROWS, tm), lambda i: (0, i)),
                   pl.BlockSpec((N_BUCKETS, LANES), const)],
        out_shape=[jax.ShapeDtypeStruct((t, d), f32),
                   jax.ShapeDtypeStruct((t * d // LANES, LANES), f32),
                   jax.ShapeDtypeStruct((ROUTE_ROWS, t), f32),
                   jax.ShapeDtypeStruct((N_BUCKETS, LANES), f32)],
        scratch_shapes=[pltpu.VMEM((N_BUCKETS, tm), f32)],
        compiler_params=_cparams(1),
        name="outproj",
    )(og, osw, x2, mod3, wo, g2, wr_hi, wr_lo, br)


def _plan_kernel(cnt_ref, route_ref, dest_ref, tb_ref, own_ref, nt_ref):
    nb = N_BUCKETS
    tiles = jnp.floor((cnt_ref[...] + (MOE_TM - 1)) * (1.0 / MOE_TM))
    b_r = lax.broadcasted_iota(jnp.int32, (nb, nb), 0)
    b_c = lax.broadcasted_iota(jnp.int32, (nb, nb), 1)
    before = jnp.where(b_c < b_r, 1.0, 0.0).astype(bf16)
    tiles_b = tiles.astype(bf16)
    g_r = b_r >> (2 * EXPERT_BITS)
    g_c = b_c >> (2 * EXPERT_BITS)
    g_tiles = jnp.dot(jnp.where(g_r == g_c, 1.0, 0.0).astype(bf16), tiles_b, preferred_element_type=f32)
    fill = MOE_PAIR * jnp.floor((g_tiles + (MOE_PAIR - 1)) * (1.0 / MOE_PAIR)) - g_tiles
    pads = jnp.dot(jnp.where(g_c < g_r, 1.0, 0.0).astype(bf16), fill.astype(bf16), preferred_element_type=f32)
    t_start = (jnp.dot(before, tiles_b, preferred_element_type=f32)
               + pads * (1.0 / (1 << (2 * EXPERT_BITS))))
    t_end = t_start + tiles
    tile_i = lax.broadcasted_iota(jnp.int32, (nb, 2 * LANES), 1).astype(f32)
    ended = jnp.where(jnp.concatenate([t_end, t_end], axis=1) <= tile_i, 1.0, 0.0)
    tb_ref[...] = jnp.sum(ended, axis=0, keepdims=True).astype(jnp.int32)
    started = jnp.where(jnp.concatenate([t_start, t_start], axis=1) <= tile_i, 1.0, 0.0)
    own_ref[...] = jnp.sum(started - ended, axis=0, keepdims=True).astype(jnp.int32)
    nt_ref[...] = t_end[nb - 1:nb, :].astype(jnp.int32)
    tl = route_ref.shape[1]
    onehot = lax.broadcasted_iota(jnp.int32, (nb, tl), 0) == route_ref[0:1, :].astype(jnp.int32)
    start = lax.dot_general(t_start.astype(bf16), jnp.where(onehot, 1.0, 0.0).astype(bf16), TN,
                            preferred_element_type=f32)
    dest_ref[...] = (start[0:1, :] * MOE_TM + route_ref[1:2, :]).astype(jnp.int32)


def _plan(cnt, route, tl):
    t = route.shape[1]
    return pl.pallas_call(
        _plan_kernel,
        grid=(t // tl,),
        in_specs=[pl.BlockSpec(cnt.shape, lambda i: (0, 0)),
                  pl.BlockSpec((ROUTE_ROWS, tl), lambda i: (0, i))],
        out_specs=[pl.BlockSpec((1, tl), lambda i: (0, i)),
                   pl.BlockSpec((1, 2 * LANES), lambda i: (0, 0)),
                   pl.BlockSpec((1, 2 * LANES), lambda i: (0, 0)),
                   pl.BlockSpec((1, LANES), lambda i: (0, 0))],
        out_shape=[jax.ShapeDtypeStruct((1, t), jnp.int32),
                   jax.ShapeDtypeStruct((1, 2 * LANES), jnp.int32),
                   jax.ShapeDtypeStruct((1, 2 * LANES), jnp.int32),
                   jax.ShapeDtypeStruct((1, LANES), jnp.int32)],
        compiler_params=_cparams(1),
        name="plan",
    )(cnt, route)


def _slab(ref, token, n):
    return ref.at[pl.ds(pl.multiple_of(token * n, n), n)]


def _dispatch_kernel(dest_ref, tb_ref, own_ref, src_ref, wg_ref, wu_ref, wd_ref, out_ref, wg_out, wu_out, wd_out,
                     zero_ref, sem_z, sem_r, *, rows, n_tiles_max):
    i = pl.program_id(0)
    n = src_ref.shape[0] // rows
    tile_rows = zero_ref.shape[0]

    def last_tile(k):
        return (own_ref[k] == 0) | (tb_ref[k] != tb_ref[k + 1])

    def zero_copy(k):
        return pltpu.make_async_copy(zero_ref, out_ref.at[pl.ds(pl.multiple_of(k * tile_rows, tile_rows), tile_rows)],
                                     sem_z)

    @pl.when(i == 0)
    def _():
        zero_ref[...] = jnp.zeros_like(zero_ref)

        def start(k, c):
            @pl.when(last_tile(k))
            def _():
                zero_copy(k).start()
            return c

        def wait(k, c):
            @pl.when(last_tile(k))
            def _():
                zero_copy(k).wait()
            return c

        lax.fori_loop(0, n_tiles_max, start, 0)
        lax.fori_loop(0, n_tiles_max, wait, 0)

    base = i * rows

    def row_copy(r):
        return pltpu.make_async_copy(_slab(src_ref, r, n), _slab(out_ref, dest_ref[base + r], n), sem_r)

    def issue(r8, c):
        for j in range(ROW_DMA_UNROLL):
            row_copy(r8 * ROW_DMA_UNROLL + j).start(priority=j % 2)
        return c

    def drain(r8, c):
        for j in range(ROW_DMA_UNROLL):
            row_copy(r8 * ROW_DMA_UNROLL + j).wait()
        return c

    lax.fori_loop(0, rows // ROW_DMA_UNROLL, issue, 0)
    wg_out[...] = wg_ref[...].astype(bf16)
    wu_out[...] = wu_ref[...].astype(bf16)
    wd_out[...] = wd_ref[...].astype(bf16)
    lax.fori_loop(0, rows // ROW_DMA_UNROLL, drain, 0)


def _dispatch(dest, tb, nt, hrow, wg, wu, wd, n_tiles_max, rows):
    t = dest.shape[0]
    n = hrow.shape[0] // t
    steps = t // rows
    n_exp = wg.shape[0] * wg.shape[1]
    assert n_exp % steps == 0
    per = n_exp // steps
    flat = [w.reshape((n_exp,) + w.shape[2:]) for w in (wg, wu, wd)]
    w_spec = lambda w: pl.BlockSpec((per,) + w.shape[1:], lambda i, dest, tb, nt: (i, 0, 0))
    grid_spec = pltpu.PrefetchScalarGridSpec(
        num_scalar_prefetch=3, grid=(steps,),
        in_specs=[pl.BlockSpec((rows * n, LANES), lambda i, dest, tb, nt: (i, 0))] + [w_spec(w) for w in flat],
        out_specs=[pl.BlockSpec(memory_space=pl.ANY)] + [w_spec(w) for w in flat],
        scratch_shapes=[pltpu.VMEM((MOE_TM * n, LANES), hrow.dtype), pltpu.SemaphoreType.DMA(()),
                        pltpu.SemaphoreType.DMA(())])
    outs = pl.pallas_call(
        functools.partial(_dispatch_kernel, rows=rows, n_tiles_max=n_tiles_max),
        grid_spec=grid_spec,
        out_shape=[jax.ShapeDtypeStruct((n_tiles_max * MOE_TM * n, LANES), hrow.dtype)]
                  + [jax.ShapeDtypeStruct(w.shape, bf16) for w in flat],
        compiler_params=_cparams(1),
        name="dispatch",
    )(dest, tb, nt, hrow, *flat)
    return (outs[0],) + tuple(o.reshape(w.shape) for o, w in zip(outs[1:], (wg, wu, wd)))


PAIR_BITS = 2
MOE_PAIR = 1 << PAIR_BITS


def _moe_kernel(tb_ref, nt_ref, hs_ref, wr_ref, br_ref, wg_ref, wu_ref, wd_ref, y_ref):
    j = pl.program_id(0)
    rows = hs_ref.shape[0] // MOE_PAIR

    @pl.when(j * MOE_PAIR < nt_ref[0])
    def _():
        e_g = EXPERTS_PER_GROUP
        tiles = range(MOE_PAIR)
        bucket = [tb_ref[j * MOE_PAIR + k] for k in tiles]
        group = [b >> (2 * EXPERT_BITS) for b in bucket]
        experts = [((b >> EXPERT_BITS) & (e_g - 1), b & (e_g - 1)) for b in bucket]
        h = [_load_slabs(hs_ref.at[pl.ds(k * rows, rows)], MOE_TM).astype(bf16) for k in tiles]
        gate_up = [[(jnp.dot(h[k], wg_ref[e], preferred_element_type=f32),
                     jnp.dot(h[k], wu_ref[e], preferred_element_type=f32)) for e in experts[k]] for k in tiles]
        weights = []
        for k in tiles:
            logits = jnp.dot(h[k], wr_ref[...], preferred_element_type=f32) + br_ref[...]
            lane = lax.broadcasted_iota(jnp.int32, logits.shape, 1)

            def pick(col):
                return jnp.sum(jnp.where(lane == col, logits, 0.0), axis=1, keepdims=True)

            gl = jnp.where(lane < N_GROUPS, logits, -jnp.inf)
            g_max = jnp.max(gl, axis=1, keepdims=True)
            g_gate = jnp.exp(pick(group[k]) - g_max) / jnp.sum(jnp.exp(gl - g_max), axis=1, keepdims=True)
            sel = [pick(e_g * (group[k] + 1) + e) for e in experts[k]]
            top = jnp.maximum(sel[0], sel[1])
            p = [jnp.exp(s - top) for s in sel]
            scale = g_gate / (p[0] + p[1])
            weights.append([scale * p[0], scale * p[1]])
        ys = []
        for k in tiles:
            y = None
            for m, e in enumerate(experts[k]):
                a, u = gate_up[k][m]
                yk = weights[k][m] * jnp.dot((_silu(a) * u).astype(bf16), wd_ref[e], preferred_element_type=f32)
                y = yk if y is None else y + yk
            ys.append(y)
        for k in tiles:
            _store_slabs(y_ref.at[pl.ds(k * rows, rows)], ys[k])

    @pl.when(j * MOE_PAIR >= nt_ref[0])
    def _():
        y_ref[...] = jnp.zeros_like(y_ref)


def _moe(tb, nt, hs, wr, br, wg, wu, wd):
    e_g = EXPERTS_PER_GROUP
    d = wg.shape[2]
    n = d // LANES
    step_rows = MOE_PAIR * MOE_TM * n
    n_steps = hs.shape[0] // step_rows
    last = lambda j, tb, nt: jnp.minimum(j, lax.shift_right_logical(nt[0] - 1, PAIR_BITS))
    group = lambda j, tb, nt: (tb[last(j, tb, nt) * MOE_PAIR] >> (2 * EXPERT_BITS), 0, 0, 0)
    const = lambda j, tb, nt: (0, 0)
    grid_spec = pltpu.PrefetchScalarGridSpec(
        num_scalar_prefetch=2, grid=(n_steps,),
        in_specs=[pl.BlockSpec((step_rows, LANES), lambda j, tb, nt: (last(j, tb, nt), 0)),
                  pl.BlockSpec(wr.shape, const),
                  pl.BlockSpec(br.shape, const),
                  pl.BlockSpec((None, e_g, d, D_EXPERT), group),
                  pl.BlockSpec((None, e_g, d, D_EXPERT), group),
                  pl.BlockSpec((None, e_g, D_EXPERT, d), group)],
        out_specs=pl.BlockSpec((step_rows, LANES), lambda j, tb, nt: (j, 0)))
    return pl.pallas_call(
        _moe_kernel,
        grid_spec=grid_spec,
        out_shape=jax.ShapeDtypeStruct(hs.shape, f32),
        compiler_params=_cparams(1),
        name="moe",
    )(tb, nt, hs, wr, br, wg, wu, wd)


def _final_kernel(dest_ref, x1_ref, mod_ref, fg_ref, ys_ref, o_ref, ybuf_ref, sem):
    i = pl.program_id(0)
    tm = x1_ref.shape[0]
    n = ybuf_ref.shape[1] // tm

    def row_copy(tile, slot, r):
        return pltpu.make_async_copy(_slab(ys_ref, dest_ref[tile * tm + r], n), _slab(ybuf_ref.at[slot], r, n),
                                     sem.at[slot])

    def gather(tile, slot):
        def issue(r8, c):
            for j in range(ROW_DMA_UNROLL):
                row_copy(tile, slot, r8 * ROW_DMA_UNROLL + j).start(priority=j % 2)
            return c
        lax.fori_loop(0, tm // ROW_DMA_UNROLL, issue, 0)

    @pl.when(i == 0)
    def _():
        gather(0, 0)

    @pl.when(i + 1 < pl.num_programs(0))
    def _():
        gather(i + 1, (i + 1) % 2)

    slot = i % 2

    def drain(r8, c):
        for j in range(ROW_DMA_UNROLL):
            row_copy(i, slot, r8 * ROW_DMA_UNROLL + j).wait()
        return c

    lax.fori_loop(0, tm // ROW_DMA_UNROLL, drain, 0)

    x2 = x1_ref[...] + mod_ref[5:6, :] * _load_slabs(ybuf_ref.at[slot], tm)
    o_ref[...] = _rmsnorm_rows(x2) * fg_ref[...]


def _final(dest, x1, mod3, fg, ys, seq, tm):
    t, d = x1.shape
    per_b = seq // tm
    grid_spec = pltpu.PrefetchScalarGridSpec(
        num_scalar_prefetch=1, grid=(t // tm,),
        in_specs=[pl.BlockSpec((tm, d), lambda i, dest: (i, 0)),
                  pl.BlockSpec((None, 6, d), lambda i, dest: (i // per_b, 0, 0)),
                  pl.BlockSpec((1, d), lambda i, dest: (0, 0)),
                  pl.BlockSpec(memory_space=pl.ANY)],
        out_specs=pl.BlockSpec((tm, d), lambda i, dest: (i, 0)),
        scratch_shapes=[pltpu.VMEM((2, tm * d // LANES, LANES), ys.dtype), pltpu.SemaphoreType.DMA((2,))])
    return pl.pallas_call(
        _final_kernel,
        grid_spec=grid_spec,
        out_shape=jax.ShapeDtypeStruct((t, d), f32),
        compiler_params=_cparams(1),
        name="final",
    )(dest, x1, mod3, fg, ys)


def _prep_w_in(w_in_l):
    sizes = (GLA_QK, GLA_QK, GLA_W, GLA_W, GLA_LOWRANK, SWA_W, SWA_KV, SWA_KV)
    off = [int(o) for o in np.concatenate([[0], np.cumsum(sizes)])]
    gla = w_in_l[:, off[0]:off[4]]
    ga = w_in_l[:, off[4]:off[5]]
    sq = [w_in_l[:, off[5] + h * SWA_HD:off[5] + (h + 1) * SWA_HD] for h in SWA_HEAD_ORDER]
    skv = w_in_l[:, off[6]:off[8]]
    pad = jnp.zeros((w_in_l.shape[0], IN_END - off[8]), w_in_l.dtype)
    return jnp.concatenate([gla] + sq + [skv, ga, pad], axis=1).astype(bf16)


def _prep_w_out(w_out_l):
    swa = [w_out_l[GLA_W + h * SWA_HD:GLA_W + (h + 1) * SWA_HD] for h in SWA_HEAD_ORDER]
    return jnp.concatenate([w_out_l[0:GLA_W]] + swa, axis=0).astype(bf16)


def _prep_router(w_grp_l, b_grp_l, w_exp_l, b_exp_l):
    d = w_grp_l.shape[0]
    n_e = N_GROUPS * EXPERTS_PER_GROUP
    wr = jnp.zeros((ROUTER_ROWS, d), f32)
    wr = wr.at[0:N_GROUPS, :].set(w_grp_l.T).at[EXPERTS_PER_GROUP:EXPERTS_PER_GROUP + n_e, :].set(w_exp_l.T)
    br = jnp.zeros((ROUTER_ROWS, 1), f32)
    br = br.at[0:N_GROUPS, 0].set(b_grp_l).at[EXPERTS_PER_GROUP:EXPERTS_PER_GROUP + n_e, 0].set(b_exp_l)
    wr_hi = wr.astype(bf16)
    wr_lo = (wr - wr_hi.astype(f32)).astype(bf16)
    return wr_hi, wr_lo, br


def _layer(x2, mod3, tab, e_mat, bsz, seq, norm_mix_g, w_in, gla_w_a2, gla_b_a2, gla_norm_g, swa_sinks, w_out,
           norm_ffn_g, w_grp, b_grp, w_exp, b_exp, w_gate, w_up, w_down, out_g, tm, tc, tm_moe):
    d = x2.shape[1]
    wa = jnp.pad(gla_w_a2, ((0, LANES - GLA_LOWRANK), (0, 0)))
    wa_hi = wa.astype(bf16)
    wa_lo = (wa - wa_hi.astype(f32)).astype(bf16)
    qk, vg, la, sq, skv = _inproj(x2, mod3, norm_mix_g.reshape(1, d), _prep_w_in(w_in), tab, e_mat,
                                  wa_hi, wa_lo, gla_b_a2.reshape(1, GLA_QK), seq, tm)
    o_gla, o_swa = _mixers(swa_sinks.reshape(1, SWA_HEADS), qk, vg, la, gla_norm_g.reshape(1, GLA_W), sq, skv,
                           bsz, seq, tc)
    wr_hi, wr_lo, br = _prep_router(w_grp, b_grp, w_exp, b_exp)
    x1, hrow, route, cnt = _outproj(o_gla, o_swa, x2, mod3, _prep_w_out(w_out), norm_ffn_g.reshape(1, d),
                                    wr_hi, wr_lo, br, seq, tm)
    t = x2.shape[0]
    n_tiles_max = -(-(t // MOE_TM + N_PAIRS + (N_GROUPS - 1) * (MOE_PAIR - 1)) // MOE_PAIR) * MOE_PAIR
    assert n_tiles_max < 2 * LANES and t % MOE_TM == 0
    dest, tile_bucket, tile_owned, n_tiles = _plan(cnt, route, min(t, 2048))
    dest = dest.reshape(t)
    tile_bucket = tile_bucket.reshape(2 * LANES)
    n_tiles = n_tiles.reshape(LANES)[0:1]
    hs, wg_b, wu_b, wd_b = _dispatch(dest, tile_bucket, tile_owned.reshape(2 * LANES), hrow, w_gate, w_up, w_down, n_tiles_max,
                                     min(t, 1024))
    ys = _moe(tile_bucket, n_tiles, hs, wr_hi.T, br.reshape(1, ROUTER_ROWS), wg_b, wu_b, wd_b)
    return _final(dest, x1, mod3, out_g.reshape(1, d), ys, seq, tm_moe)


def kernel(x, c, positions, ada_w, ada_b, norm_mix_g, w_in, gla_w_a2, gla_b_a2, gla_norm_g, swa_sinks, w_out,
           norm_ffn_g, w_grp, b_grp, w_exp, b_exp, w_gate, w_up, w_down, final_norm_g):
    bsz, seq, d = x.shape
    depth = ada_w.shape[0]
    assert depth == 1, "the final rmsnorm is fused into the last layer's combine kernel"
    tab = _rope_tab(positions)
    e_mat = _rope_expand_matrix()
    x2 = x.reshape(bsz * seq, d)
    for l in range(depth):
        mod3 = _adaln(c, ada_w[l], ada_b[l]).reshape(bsz, 6, d)
        x2 = _layer(x2, mod3, tab, e_mat, bsz, seq, norm_mix_g[l], w_in[l], gla_w_a2[l], gla_b_a2[l],
                    gla_norm_g[l], swa_sinks[l], w_out[l], norm_ffn_g[l], w_grp[l], b_grp[l], w_exp[l], b_exp[l],
                    w_gate[l], w_up[l], w_down[l], final_norm_g, tm=512, tc=512, tm_moe=512)
    return x2.reshape(bsz, seq, d)
```

```python
import functools

import numpy as np
import jax
import jax.numpy as jnp
from jax import lax
from jax.experimental import pallas as pl
from jax.experimental.pallas import tpu as pltpu

f32 = jnp.float32
bf16 = jnp.bfloat16

GLA_HEADS = 4
GLA_DK = 64
GLA_DV = 128
GLA_LOWRANK = 16
GLA_GATE_NORM = 16.0
GLA_CHUNK = 64
SWA_HEADS = 8
SWA_KV_HEADS = 2
SWA_HD = 64
WINDOW = 128
ROPE_THETA = 500000.0
ROPE_DIMS = SWA_HD // 4
N_GROUPS = 4
EXPERTS_PER_GROUP = 8
D_EXPERT = 256
EPS = 1e-6

LANES = 128
VMEM_LIMIT = 52 * 1024 * 1024

GLA_QK = GLA_HEADS * GLA_DK
GLA_W = GLA_HEADS * GLA_DV
SWA_W = SWA_HEADS * SWA_HD
SWA_KV = SWA_KV_HEADS * SWA_HD
SWA_HEAD_ORDER = (0, 4, 1, 5, 2, 6, 3, 7)
ROUTE_ROWS = 8
ROUTER_ROWS = 128
EXPERT_BITS = 3
N_BUCKETS = N_GROUPS << (2 * EXPERT_BITS)
N_PAIRS = N_GROUPS * EXPERTS_PER_GROUP * (EXPERTS_PER_GROUP - 1) // 2
MOE_TM = 128
ROW_DMA_UNROLL = 8

TN = (((0,), (0,)), ((), ()))
NT = (((1,), (1,)), ((), ()))


def _cparams(n_axes):
    return pltpu.CompilerParams(dimension_semantics=("arbitrary",) * n_axes, vmem_limit_bytes=VMEM_LIMIT)


def _split_bf16(v):
    hi = v.astype(bf16)
    lo = (v - hi.astype(f32)).astype(bf16)
    return hi, lo


def _split_stack_bf16(v):
    hi = v.astype(bf16).astype(f32)
    return jnp.concatenate([hi, v - hi], axis=0).astype(bf16)


def _rmsnorm_rows(v):
    return v * lax.rsqrt(jnp.mean(v * v, axis=-1, keepdims=True) + EPS)


def _silu(v):
    return v * jax.nn.sigmoid(v)


def _adaln_kernel(c_ref, w_ref, b_ref, o_ref):
    ca = _silu(c_ref[...])
    o_ref[...] = jnp.dot(ca, w_ref[...], precision=lax.Precision.HIGHEST, preferred_element_type=f32) + b_ref[...]


def _adaln(c, w, b):
    bsz, d = c.shape
    n = w.shape[1]
    return pl.pallas_call(
        _adaln_kernel,
        grid=(n // d,),
        in_specs=[pl.BlockSpec((bsz, d), lambda j: (0, 0)),
                  pl.BlockSpec((d, d), lambda j: (0, j)),
                  pl.BlockSpec((1, d), lambda j: (0, j))],
        out_specs=pl.BlockSpec((bsz, d), lambda j: (0, j)),
        out_shape=jax.ShapeDtypeStruct((bsz, n), f32),
        compiler_params=_cparams(1),
        name="adaln",
    )(c, w, b.reshape(1, n))


def _rope_tab_kernel(pos_ref, invf_ref, o_ref):
    half = ROPE_DIMS // 2
    for b in range(pos_ref.shape[0]):
        ang = pos_ref[b:b + 1, :].astype(f32) * invf_ref[...]
        o_ref[b, 0:half, :] = jnp.cos(ang)
        o_ref[b, half:2 * half, :] = jnp.sin(ang)


def _rope_tab(positions):
    bsz, s = positions.shape
    half = ROPE_DIMS // 2
    inv_freq = (np.float32(ROPE_THETA) ** (-np.arange(0, ROPE_DIMS, 2, dtype=np.float32) / np.float32(ROPE_DIMS)))
    invf = jnp.asarray(inv_freq.astype(np.float32).reshape(half, 1))
    return pl.pallas_call(
        _rope_tab_kernel,
        out_shape=jax.ShapeDtypeStruct((bsz, 2 * half, s), f32),
        name="rope_tab",
    )(positions, invf)


def _rope_expand_matrix():
    half = ROPE_DIMS // 2
    e = np.zeros((2 * half, 3 * LANES), np.float32)
    for j in range(LANES):
        jj = j % SWA_HD
        if jj < half:
            e[jj, j] = 1.0
            e[half + jj, LANES + j] = -1.0
        elif jj < 2 * half:
            e[jj - half, j] = 1.0
            e[half + jj - half, 2 * LANES + j] = 1.0
    return jnp.asarray(np.concatenate([e, e], axis=0), dtype=bf16)


IN_QK0, IN_VG0, IN_SQ0, IN_SKV0, IN_A0, IN_END = 0, 512, 1536, 2048, 2304, 2432


def _inproj_kernel(x_ref, mod_ref, g_ref, w_ref, tab_ref, e_ref, wa_hi_ref, wa_lo_ref, ba_ref,
                   qk_ref, vg_ref, la_ref, sq_ref, skv_ref):
    x = x_ref[...]
    h = (_rmsnorm_rows(x) * g_ref[...]) * (1.0 + mod_ref[1:2, :]) + mod_ref[0:1, :]
    hb = h.astype(bf16)

    def proj(lo, hi):
        return jnp.dot(hb, w_ref[:, lo:hi], preferred_element_type=f32)

    al_hi, al_lo = _split_bf16(proj(IN_A0, IN_END))
    z = (jnp.dot(al_hi, wa_hi_ref[...], preferred_element_type=f32)
         + jnp.dot(al_lo, wa_hi_ref[...], preferred_element_type=f32)
         + jnp.dot(al_hi, wa_lo_ref[...], preferred_element_type=f32)) + ba_ref[...]
    la_ref[...] = (jnp.minimum(z, 0.0) - jnp.log1p(jnp.exp(-jnp.abs(z)))) * (1.0 / GLA_GATE_NORM)

    qk = proj(IN_QK0, IN_VG0)
    lane = lax.broadcasted_iota(jnp.int32, qk.shape, 1)
    qk_ref[...] = jnp.where(lane < GLA_QK, qk * (GLA_DK ** -0.5), qk).astype(bf16)
    vg_ref[...] = proj(IN_VG0, IN_SQ0).astype(bf16)

    tabs = lax.dot_general(_split_stack_bf16(tab_ref[...]), e_ref[...], TN, preferred_element_type=f32)
    lane1 = lax.broadcasted_iota(jnp.int32, (1, LANES), 1)
    cos_t = tabs[:, 0:LANES] + jnp.where((lane1 & (SWA_HD - 1)) < ROPE_DIMS, 0.0, 1.0)
    sa_t = tabs[:, LANES:2 * LANES]
    sb_t = tabs[:, 2 * LANES:3 * LANES]

    def rope(v):
        return (v * cos_t + pltpu.roll(v, LANES - ROPE_DIMS // 2, 1) * sa_t
                + pltpu.roll(v, ROPE_DIMS // 2, 1) * sb_t)

    sq = proj(IN_SQ0, IN_SKV0)
    for p in range(SWA_W // LANES):
        sq_ref[:, p * LANES:(p + 1) * LANES] = (rope(sq[:, p * LANES:(p + 1) * LANES]) * (SWA_HD ** -0.5)).astype(bf16)
    skv = proj(IN_SKV0, IN_A0)
    skv_ref[:, 0:LANES] = rope(skv[:, 0:LANES]).astype(bf16)
    skv_ref[:, LANES:2 * LANES] = skv[:, LANES:2 * LANES].astype(bf16)


def _inproj(x2, mod3, g, w, tab, e_mat, wa_hi, wa_lo, ba, seq, tm):
    t, d = x2.shape
    per_b = seq // tm
    row = lambda i: (i, 0)
    const = lambda i: (0, 0)
    return pl.pallas_call(
        _inproj_kernel,
        grid=(t // tm,),
        in_specs=[pl.BlockSpec((tm, d), row),
                  pl.BlockSpec((None, 6, d), lambda i: (i // per_b, 0, 0)),
                  pl.BlockSpec((1, d), const),
                  pl.BlockSpec(w.shape, const),
                  pl.BlockSpec((None, ROPE_DIMS, tm), lambda i: (i // per_b, 0, i % per_b)),
                  pl.BlockSpec(e_mat.shape, const),
                  pl.BlockSpec(wa_hi.shape, const),
                  pl.BlockSpec(wa_lo.shape, const),
                  pl.BlockSpec(ba.shape, const)],
        out_specs=[pl.BlockSpec((tm, 2 * GLA_QK), row),
                   pl.BlockSpec((tm, 2 * GLA_W), row),
                   pl.BlockSpec((tm, GLA_QK), row),
                   pl.BlockSpec((tm, SWA_W), row),
                   pl.BlockSpec((tm, 2 * SWA_KV), row)],
        out_shape=[jax.ShapeDtypeStruct((t, 2 * GLA_QK), bf16),
                   jax.ShapeDtypeStruct((t, 2 * GLA_W), bf16),
                   jax.ShapeDtypeStruct((t, GLA_QK), f32),
                   jax.ShapeDtypeStruct((t, SWA_W), bf16),
                   jax.ShapeDtypeStruct((t, 2 * SWA_KV), bf16)],
        compiler_params=_cparams(1),
        name="inproj",
    )(x2, mod3, g, w, tab, e_mat, wa_hi, wa_lo, ba)


def _gla_chunk(c, qk_ref, vg_ref, la_ref, gn_ref, o_ref, st_ref, consts):
    causal, tri, ones, first = consts
    c_len = GLA_CHUNK
    rows = slice(c * c_len, (c + 1) * c_len)
    la_hi, la_lo = _split_bf16(la_ref[rows, :])
    b = jnp.dot(tri, la_hi, preferred_element_type=f32) + jnp.dot(tri, la_lo, preferred_element_type=f32)
    b_tot_t = (lax.dot_general(la_hi, ones, TN, preferred_element_type=f32)
               + lax.dot_general(la_lo, ones, TN, preferred_element_type=f32))
    b_last = b[c_len - 1:c_len, :]
    q = qk_ref[rows, 0:GLA_QK].astype(f32)
    k = qk_ref[rows, GLA_QK:2 * GLA_QK].astype(f32)
    q_dec = q * jnp.exp(b)
    k_dec = (k * jnp.exp(-b)).astype(bf16)
    k_rem = (k * jnp.exp(b_last - b)).astype(bf16)
    decay = jnp.exp(b_tot_t)
    for p in range(GLA_HEADS // 2):
        ls = slice(p * LANES, (p + 1) * LANES)
        s_prev = st_ref[ls, :]
        s_prev_b = s_prev.astype(bf16)
        qd = q_dec[:, ls]
        kv_halves = []
        for hh in range(2):
            h = 2 * p + hh
            vs = slice(h * GLA_DV, (h + 1) * GLA_DV)
            qm = jnp.where(first if hh == 0 else ~first, qd, 0.0).astype(bf16)
            v = vg_ref[rows, vs]
            scores = lax.dot_general(qm, k_dec[:, ls], NT, preferred_element_type=f32)
            scores = jnp.where(causal, scores, 0.0).astype(bf16)
            o = (jnp.dot(scores, v, preferred_element_type=f32)
                 + jnp.dot(qm, s_prev_b, preferred_element_type=f32))
            o = _rmsnorm_rows(o) * gn_ref[:, vs]
            gate = vg_ref[rows, GLA_W + h * GLA_DV:GLA_W + (h + 1) * GLA_DV].astype(f32)
            o_ref[rows, vs] = (o * _silu(gate)).astype(bf16)
            kv = lax.dot_general(k_rem[:, ls], v, TN, preferred_element_type=f32)
            kv_halves.append(kv[hh * GLA_DK:(hh + 1) * GLA_DK, :])
        st_ref[ls, :] = decay[ls, :] * s_prev + jnp.concatenate(kv_halves, axis=0)


def _swa_block(blk, j, sink_ref, q_ref, kvp_ref, kvc_ref, o_ref, consts):
    band, s_i, first = consts
    w = WINDOW
    rows = slice(blk * w, (blk + 1) * w)
    prev = kvp_ref if blk == 0 else kvc_ref.at[pl.ds((blk - 1) * w, w)]
    k2 = jnp.concatenate([prev[:, 0:LANES], kvc_ref[rows, 0:LANES]], axis=0)
    v2 = jnp.concatenate([prev[:, LANES:2 * LANES], kvc_ref[rows, LANES:2 * LANES]], axis=0)
    valid = band & ((j > 0) | (s_i >= w)) if blk == 0 else band
    slots = range(SWA_HEADS)
    scores = []
    for sl in slots:
        p, hh = divmod(sl, 2)
        qp = q_ref[rows, p * LANES:(p + 1) * LANES].astype(f32)
        qm = jnp.where(first if hh == 0 else ~first, qp, 0.0).astype(bf16)
        s = lax.dot_general(qm, k2, NT, preferred_element_type=f32)
        scores.append(jnp.where(valid, s, -jnp.inf))
    probs, denoms = [], []
    for sl in slots:
        sink = sink_ref[0, SWA_HEAD_ORDER[sl]]
        m = jnp.maximum(jnp.max(scores[sl], axis=-1, keepdims=True), sink)
        pr = jnp.exp(scores[sl] - m)
        denoms.append(jnp.sum(pr, axis=-1, keepdims=True) + jnp.exp(sink - m))
        probs.append(pr.astype(bf16))
    outs = [jnp.dot(probs[sl], v2, preferred_element_type=f32) / denoms[sl] for sl in slots]
    for p in range(SWA_W // LANES):
        o_ref[rows, p * LANES:(p + 1) * LANES] = jnp.where(first, outs[2 * p], outs[2 * p + 1]).astype(bf16)


def _gla_kernel(qk_ref, vg_ref, la_ref, gn_ref, o_ref, st_ref):
    @pl.when(pl.program_id(1) == 0)
    def _():
        st_ref[...] = jnp.zeros_like(st_ref)

    c_len = GLA_CHUNK
    causal = lax.broadcasted_iota(jnp.int32, (c_len, c_len), 0) >= lax.broadcasted_iota(jnp.int32, (c_len, c_len), 1)
    consts = (causal, jnp.where(causal, 1.0, 0.0).astype(bf16), jnp.ones((c_len, LANES), bf16),
              lax.broadcasted_iota(jnp.int32, (c_len, LANES), 1) < GLA_DK)
    for c in range(qk_ref.shape[0] // c_len):
        _gla_chunk(c, qk_ref, vg_ref, la_ref, gn_ref, o_ref, st_ref, consts)


def _swa_kernel(sink_ref, q_ref, kvp_ref, kvc_ref, wg_ref, wu_ref, wd_ref, o_ref, wg_out, wu_out, wd_out):
    wg_out[...] = wg_ref[...].astype(bf16)
    wu_out[...] = wu_ref[...].astype(bf16)
    wd_out[...] = wd_ref[...].astype(bf16)
    w = WINDOW
    s_i = lax.broadcasted_iota(jnp.int32, (w, 2 * w), 1)
    rel = lax.broadcasted_iota(jnp.int32, (w, 2 * w), 0) + w - s_i
    consts = ((rel >= 0) & (rel < w), s_i, lax.broadcasted_iota(jnp.int32, (w, LANES), 1) < SWA_HD)
    for blk in range(q_ref.shape[0] // w):
        _swa_block(blk, pl.program_id(1), sink_ref, q_ref, kvp_ref, kvc_ref, o_ref, consts)


def _mixers(sinks, qk, vg, la, gn, sq, skv, wg, wu, wd, bsz, seq, ts):
    t = qk.shape[0]
    ns = seq // ts
    blocks = ts // WINDOW
    tile = lambda b, j: (b * ns + j, 0)
    n_exp = wg.shape[0] * wg.shape[1]
    assert n_exp % (bsz * ns) == 0
    per = n_exp // (bsz * ns)
    flat = [w.reshape((n_exp,) + w.shape[2:]) for w in (wg, wu, wd)]
    w_spec = lambda w: pl.BlockSpec((per,) + w.shape[1:], lambda b, j: (b * ns + j, 0, 0))
    o_gla = pl.pallas_call(
        _gla_kernel,
        grid=(bsz, ns),
        in_specs=[pl.BlockSpec((ts, 2 * GLA_QK), tile),
                  pl.BlockSpec((ts, 2 * GLA_W), tile),
                  pl.BlockSpec((ts, GLA_QK), tile),
                  pl.BlockSpec((1, GLA_W), lambda b, j: (0, 0))],
        out_specs=pl.BlockSpec((ts, GLA_W), tile),
        out_shape=jax.ShapeDtypeStruct((t, GLA_W), bf16),
        scratch_shapes=[pltpu.VMEM((GLA_HEADS * GLA_DK, GLA_DV), f32)],
        compiler_params=_cparams(2),
        name="gla",
    )(qk, vg, la, gn)
    o_swa, *w_bf16 = pl.pallas_call(
        _swa_kernel,
        grid=(bsz, ns),
        in_specs=[pl.BlockSpec(memory_space=pltpu.SMEM),
                  pl.BlockSpec((ts, SWA_W), tile),
                  pl.BlockSpec((WINDOW, 2 * SWA_KV), lambda b, j: ((b * ns + j) * blocks - jnp.minimum(j, 1), 0)),
                  pl.BlockSpec((ts, 2 * SWA_KV), tile)] + [w_spec(w) for w in flat],
        out_specs=[pl.BlockSpec((ts, SWA_W), tile)] + [w_spec(w) for w in flat],
        out_shape=[jax.ShapeDtypeStruct((t, SWA_W), bf16)] + [jax.ShapeDtypeStruct(w.shape, bf16) for w in flat],
        compiler_params=_cparams(2),
        name="swa",
    )(sinks, sq, skv, skv, *flat)
    return (o_gla, o_swa) + tuple(o.reshape(w.shape) for o, w in zip(w_bf16, (wg, wu, wd)))


def _store_slabs(ref, v):
    tm, n = v.shape[0], v.shape[1] // LANES
    for c in range(n):
        ref[pl.ds(c, tm, stride=n), :] = v[:, c * LANES:(c + 1) * LANES]


def _load_slabs(ref, tm):
    n = ref.shape[0] // tm
    return jnp.concatenate([ref[pl.ds(c, tm, stride=n), :] for c in range(n)], axis=1)


def _outproj_kernel(og_ref, os_ref, x_ref, mod_ref, wo_ref, g2_ref, wr_hi_ref, wr_lo_ref, br_ref,
                    x1_ref, hrow_ref, route_ref, cnt_ref, run_ref):
    @pl.when(pl.program_id(0) == 0)
    def _():
        run_ref[...] = jnp.zeros_like(run_ref)

    mix = (jnp.dot(og_ref[...], wo_ref[0:GLA_W, :], preferred_element_type=f32)
           + jnp.dot(os_ref[...], wo_ref[GLA_W:GLA_W + SWA_W, :], preferred_element_type=f32))
    x1 = x_ref[...] + mod_ref[2:3, :] * mix
    x1_ref[...] = x1
    h = (_rmsnorm_rows(x1) * g2_ref[...]) * (1.0 + mod_ref[4:5, :]) + mod_ref[3:4, :]
    h_hi, h_lo = _split_bf16(h)
    _store_slabs(hrow_ref, h)

    lt = (lax.dot_general(wr_hi_ref[...], h_hi, NT, preferred_element_type=f32)
          + lax.dot_general(wr_hi_ref[...], h_lo, NT, preferred_element_type=f32)
          + lax.dot_general(wr_lo_ref[...], h_hi, NT, preferred_element_type=f32)) + br_ref[...]
    tm = lt.shape[1]
    e_g = EXPERTS_PER_GROUP
    row = lax.broadcasted_iota(jnp.int32, (e_g, tm), 0)
    neg = -jnp.inf
    gl = jnp.where(row < N_GROUPS, lt[0:e_g, :], neg)
    g_max = jnp.max(gl, axis=0, keepdims=True)
    g_gate = 1.0 / jnp.sum(jnp.exp(gl - g_max), axis=0, keepdims=True)
    g_idx = jnp.min(jnp.where(gl == g_max, row, e_g), axis=0, keepdims=True)
    sel = lt[e_g * N_GROUPS:e_g * (N_GROUPS + 1), :]
    for g in range(N_GROUPS - 2, -1, -1):
        sel = jnp.where(g_idx == g, lt[e_g * (g + 1):e_g * (g + 2), :], sel)
    t1 = jnp.max(sel, axis=0, keepdims=True)
    i1 = jnp.min(jnp.where(sel == t1, row, e_g), axis=0, keepdims=True)
    sel2 = jnp.where(row == i1, neg, sel)
    t2 = jnp.max(sel2, axis=0, keepdims=True)
    i2 = jnp.min(jnp.where(sel2 == t2, row, e_g), axis=0, keepdims=True)
    ex = jnp.exp(t2 - t1)
    w1 = g_gate / (1.0 + ex)
    w2 = g_gate * ex / (1.0 + ex)
    first_lo = i1 < i2
    w_lo = jnp.where(first_lo, w1, w2)
    w_hi = jnp.where(first_lo, w2, w1)
    bucket = ((g_idx << (2 * EXPERT_BITS)) | (jnp.minimum(i1, i2) << EXPERT_BITS) | jnp.maximum(i1, i2))
    onehot = lax.broadcasted_iota(jnp.int32, (N_BUCKETS, tm), 0) == bucket
    oh_b = jnp.where(onehot, 1.0, 0.0).astype(bf16)
    t_r = lax.broadcasted_iota(jnp.int32, (tm, tm), 0)
    t_c = lax.broadcasted_iota(jnp.int32, (tm, tm), 1)
    earlier = jnp.where(t_r < t_c, 1.0, 0.0).astype(bf16)
    run = run_ref[...]
    prefix = jnp.dot(oh_b, earlier, preferred_element_type=f32) + run
    rank = jnp.sum(jnp.where(onehot, prefix, 0.0), axis=0, keepdims=True)
    run = run + jnp.dot(oh_b, jnp.ones((tm, tm), bf16), preferred_element_type=f32)
    run_ref[...] = run
    cnt_ref[...] = run[:, 0:LANES]
    route_ref[...] = jnp.where(row == 0, bucket.astype(f32), jnp.where(row == 1, rank, jnp.where(
        row == 2, w_lo, jnp.where(row == 3, w_hi, 0.0))))


def _outproj(og, osw, x2, mod3, wo, g2, wr_hi, wr_lo, br, seq, tm):
    t, d = x2.shape
    per_b = seq // tm
    row = lambda i: (i, 0)
    const = lambda i: (0, 0)
    return pl.pallas_call(
        _outproj_kernel,
        grid=(t // tm,),
        in_specs=[pl.BlockSpec((tm, GLA_W), row),
                  pl.BlockSpec((tm, SWA_W), row),
                  pl.BlockSpec((tm, d), row),
                  pl.BlockSpec((None, 6, d), lambda i: (i // per_b, 0, 0)),
                  pl.BlockSpec(wo.shape, const),
                  pl.BlockSpec((1, d), const),
                  pl.BlockSpec(wr_hi.shape, const),
                  pl.BlockSpec(wr_lo.shape, const),
                  pl.BlockSpec(br.shape, const)],
        out_specs=[pl.BlockSpec((tm, d), row),
                   pl.BlockSpec((tm * d // LANES, LANES), row),
                   pl.BlockSpec((ROUTE_ROWS, tm), lambda i: (0, i)),
                   pl.BlockSpec((N_BUCKETS, LANES), const)],
        out_shape=[jax.ShapeDtypeStruct((t, d), f32),
                   jax.ShapeDtypeStruct((t * d // LANES, LANES), f32),
                   jax.ShapeDtypeStruct((ROUTE_ROWS, t), f32),
                   jax.ShapeDtypeStruct((N_BUCKETS, LANES), f32)],
        scratch_shapes=[pltpu.VMEM((N_BUCKETS, tm), f32)],
        compiler_params=_cparams(1),
        name="outproj",
    )(og, osw, x2, mod3, wo, g2, wr_hi, wr_lo, br)


def _plan_kernel(cnt_ref, route_ref, dest_ref, tb_ref, own_ref, nt_ref):
    nb = N_BUCKETS
    tiles = jnp.floor((cnt_ref[...] + (MOE_TM - 1)) * (1.0 / MOE_TM))
    b_r = lax.broadcasted_iota(jnp.int32, (nb, nb), 0)
    b_c = lax.broadcasted_iota(jnp.int32, (nb, nb), 1)
    before = jnp.where(b_c < b_r, 1.0, 0.0).astype(bf16)
    tiles_b = tiles.astype(bf16)
    g_r = b_r >> (2 * EXPERT_BITS)
    g_c = b_c >> (2 * EXPERT_BITS)
    g_tiles = jnp.dot(jnp.where(g_r == g_c, 1.0, 0.0).astype(bf16), tiles_b, preferred_element_type=f32)
    fill = MOE_PAIR * jnp.floor((g_tiles + (MOE_PAIR - 1)) * (1.0 / MOE_PAIR)) - g_tiles
    pads = jnp.dot(jnp.where(g_c < g_r, 1.0, 0.0).astype(bf16), fill.astype(bf16), preferred_element_type=f32)
    t_start = (jnp.dot(before, tiles_b, preferred_element_type=f32)
               + pads * (1.0 / (1 << (2 * EXPERT_BITS))))
    t_end = t_start + tiles
    tile_i = lax.broadcasted_iota(jnp.int32, (nb, 2 * LANES), 1).astype(f32)
    ended = jnp.where(jnp.concatenate([t_end, t_end], axis=1) <= tile_i, 1.0, 0.0)
    tb_ref[...] = jnp.sum(ended, axis=0, keepdims=True).astype(jnp.int32)
    started = jnp.where(jnp.concatenate([t_start, t_start], axis=1) <= tile_i, 1.0, 0.0)
    own_ref[...] = jnp.sum(started - ended, axis=0, keepdims=True).astype(jnp.int32)
    nt_ref[...] = t_end[nb - 1:nb, :].astype(jnp.int32)
    tl = route_ref.shape[1]
    onehot = lax.broadcasted_iota(jnp.int32, (nb, tl), 0) == route_ref[0:1, :].astype(jnp.int32)
    start = lax.dot_general(t_start.astype(bf16), jnp.where(onehot, 1.0, 0.0).astype(bf16), TN,
                            preferred_element_type=f32)
    dest_ref[...] = (start[0:1, :] * MOE_TM + route_ref[1:2, :]).astype(jnp.int32)


def _plan(cnt, route, tl):
    t = route.shape[1]
    return pl.pallas_call(
        _plan_kernel,
        grid=(t // tl,),
        in_specs=[pl.BlockSpec(cnt.shape, lambda i: (0, 0)),
                  pl.BlockSpec((ROUTE_ROWS, tl), lambda i: (0, i))],
        out_specs=[pl.BlockSpec((1, tl), lambda i: (0, i)),
                   pl.BlockSpec((1, 2 * LANES), lambda i: (0, 0)),
                   pl.BlockSpec((1, 2 * LANES), lambda i: (0, 0)),
                   pl.BlockSpec((1, LANES), lambda i: (0, 0))],
        out_shape=[jax.ShapeDtypeStruct((1, t), jnp.int32),
                   jax.ShapeDtypeStruct((1, 2 * LANES), jnp.int32),
                   jax.ShapeDtypeStruct((1, 2 * LANES), jnp.int32),
                   jax.ShapeDtypeStruct((1, LANES), jnp.int32)],
        compiler_params=_cparams(1),
        name="plan",
    )(cnt, route)


def _slab(ref, token, n):
    return ref.at[pl.ds(pl.multiple_of(token * n, n), n)]


def _dispatch_kernel(dest_ref, tb_ref, own_ref, src_ref, out_ref, zero_ref, sem_z, sem_r, *, rows, n_tiles_max):
    i = pl.program_id(0)
    n = src_ref.shape[0] // rows
    tile_rows = zero_ref.shape[0]

    def last_tile(k):
        return (own_ref[k] == 0) | (tb_ref[k] != tb_ref[k + 1])

    def zero_copy(k):
        return pltpu.make_async_copy(zero_ref, out_ref.at[pl.ds(pl.multiple_of(k * tile_rows, tile_rows), tile_rows)],
                                     sem_z)

    @pl.when(i == 0)
    def _():
        zero_ref[...] = jnp.zeros_like(zero_ref)

        def start(k, c):
            @pl.when(last_tile(k))
            def _():
                zero_copy(k).start()
            return c

        def wait(k, c):
            @pl.when(last_tile(k))
            def _():
                zero_copy(k).wait()
            return c

        lax.fori_loop(0, n_tiles_max, start, 0)
        lax.fori_loop(0, n_tiles_max, wait, 0)

    base = i * rows

    def row_copy(r):
        return pltpu.make_async_copy(_slab(src_ref, r, n), _slab(out_ref, dest_ref[base + r], n), sem_r)

    def issue(r8, c):
        for j in range(ROW_DMA_UNROLL):
            row_copy(r8 * ROW_DMA_UNROLL + j).start(priority=j % 2)
        return c

    def drain(r8, c):
        for j in range(ROW_DMA_UNROLL):
            row_copy(r8 * ROW_DMA_UNROLL + j).wait()
        return c

    lax.fori_loop(0, rows // ROW_DMA_UNROLL, issue, 0)
    lax.fori_loop(0, rows // ROW_DMA_UNROLL, drain, 0)


def _dispatch(dest, tb, own, hrow, n_tiles_max, rows):
    t = dest.shape[0]
    n = hrow.shape[0] // t
    grid_spec = pltpu.PrefetchScalarGridSpec(
        num_scalar_prefetch=3, grid=(t // rows,),
        in_specs=[pl.BlockSpec((rows * n, LANES), lambda i, dest, tb, own: (i, 0))],
        out_specs=pl.BlockSpec(memory_space=pl.ANY),
        scratch_shapes=[pltpu.VMEM((MOE_TM * n, LANES), hrow.dtype), pltpu.SemaphoreType.DMA(()),
                        pltpu.SemaphoreType.DMA(())])
    return pl.pallas_call(
        functools.partial(_dispatch_kernel, rows=rows, n_tiles_max=n_tiles_max),
        grid_spec=grid_spec,
        out_shape=jax.ShapeDtypeStruct((n_tiles_max * MOE_TM * n, LANES), hrow.dtype),
        compiler_params=_cparams(1),
        name="dispatch",
    )(dest, tb, own, hrow)


PAIR_BITS = 2
MOE_PAIR = 1 << PAIR_BITS


def _moe_kernel(tb_ref, nt_ref, hs_ref, wr_ref, br_ref, wg_ref, wu_ref, wd_ref, y_ref):
    j = pl.program_id(0)
    rows = hs_ref.shape[0] // MOE_PAIR

    @pl.when(j * MOE_PAIR < nt_ref[0])
    def _():
        e_g = EXPERTS_PER_GROUP
        tiles = range(MOE_PAIR)
        bucket = [tb_ref[j * MOE_PAIR + k] for k in tiles]
        group = [b >> (2 * EXPERT_BITS) for b in bucket]
        experts = [((b >> EXPERT_BITS) & (e_g - 1), b & (e_g - 1)) for b in bucket]
        h = [_load_slabs(hs_ref.at[pl.ds(k * rows, rows)], MOE_TM).astype(bf16) for k in tiles]
        gate_up = [[(jnp.dot(h[k], wg_ref[e], preferred_element_type=f32),
                     jnp.dot(h[k], wu_ref[e], preferred_element_type=f32)) for e in experts[k]] for k in tiles]
        weights = []
        for k in tiles:
            logits = jnp.dot(h[k], wr_ref[...], preferred_element_type=f32) + br_ref[...]
            lane = lax.broadcasted_iota(jnp.int32, logits.shape, 1)

            def pick(col):
                return jnp.sum(jnp.where(lane == col, logits, 0.0), axis=1, keepdims=True)

            gl = jnp.where(lane < N_GROUPS, logits, -jnp.inf)
            g_max = jnp.max(gl, axis=1, keepdims=True)
            g_gate = jnp.exp(pick(group[k]) - g_max) / jnp.sum(jnp.exp(gl - g_max), axis=1, keepdims=True)
            sel = [pick(e_g * (group[k] + 1) + e) for e in experts[k]]
            top = jnp.maximum(sel[0], sel[1])
            p = [jnp.exp(s - top) for s in sel]
            scale = g_gate / (p[0] + p[1])
            weights.append([scale * p[0], scale * p[1]])
        ys = []
        for k in tiles:
            y = None
            for m, e in enumerate(experts[k]):
                a, u = gate_up[k][m]
                yk = weights[k][m] * jnp.dot((_silu(a) * u).astype(bf16), wd_ref[e], preferred_element_type=f32)
                y = yk if y is None else y + yk
            ys.append(y)
        for k in tiles:
            _store_slabs(y_ref.at[pl.ds(k * rows, rows)], ys[k])

    @pl.when(j * MOE_PAIR >= nt_ref[0])
    def _():
        y_ref[...] = jnp.zeros_like(y_ref)


def _moe(tb, nt, hs, wr, br, wg, wu, wd):
    e_g = EXPERTS_PER_GROUP
    d = wg.shape[2]
    n = d // LANES
    step_rows = MOE_PAIR * MOE_TM * n
    n_steps = hs.shape[0] // step_rows
    last = lambda j, tb, nt: jnp.minimum(j, lax.shift_right_logical(nt[0] - 1, PAIR_BITS))
    group = lambda j, tb, nt: (tb[last(j, tb, nt) * MOE_PAIR] >> (2 * EXPERT_BITS), 0, 0, 0)
    const = lambda j, tb, nt: (0, 0)
    grid_spec = pltpu.PrefetchScalarGridSpec(
        num_scalar_prefetch=2, grid=(n_steps,),
        in_specs=[pl.BlockSpec((step_rows, LANES), lambda j, tb, nt: (last(j, tb, nt), 0)),
                  pl.BlockSpec(wr.shape, const),
                  pl.BlockSpec(br.shape, const),
                  pl.BlockSpec((None, e_g, d, D_EXPERT), group),
                  pl.BlockSpec((None, e_g, d, D_EXPERT), group),
                  pl.BlockSpec((None, e_g, D_EXPERT, d), group)],
        out_specs=pl.BlockSpec((step_rows, LANES), lambda j, tb, nt: (j, 0)))
    return pl.pallas_call(
        _moe_kernel,
        grid_spec=grid_spec,
        out_shape=jax.ShapeDtypeStruct(hs.shape, f32),
        compiler_params=_cparams(1),
        name="moe",
    )(tb, nt, hs, wr, br, wg, wu, wd)


def _final_kernel(dest_ref, x1_ref, mod_ref, fg_ref, ys_ref, o_ref, ybuf_ref, sem):
    i = pl.program_id(0)
    tm = x1_ref.shape[0]
    n = ybuf_ref.shape[1] // tm

    def row_copy(tile, slot, r):
        return pltpu.make_async_copy(_slab(ys_ref, dest_ref[tile * tm + r], n), _slab(ybuf_ref.at[slot], r, n),
                                     sem.at[slot])

    def gather(tile, slot):
        def issue(r8, c):
            for j in range(ROW_DMA_UNROLL):
                row_copy(tile, slot, r8 * ROW_DMA_UNROLL + j).start(priority=j % 2)
            return c
        lax.fori_loop(0, tm // ROW_DMA_UNROLL, issue, 0)

    @pl.when(i == 0)
    def _():
        gather(0, 0)

    @pl.when(i + 1 < pl.num_programs(0))
    def _():
        gather(i + 1, (i + 1) % 2)

    slot = i % 2

    def drain(r8, c):
        for j in range(ROW_DMA_UNROLL):
            row_copy(i, slot, r8 * ROW_DMA_UNROLL + j).wait()
        return c

    lax.fori_loop(0, tm // ROW_DMA_UNROLL, drain, 0)

    x2 = x1_ref[...] + mod_ref[5:6, :] * _load_slabs(ybuf_ref.at[slot], tm)
    o_ref[...] = _rmsnorm_rows(x2) * fg_ref[...]


def _final(dest, x1, mod3, fg, ys, seq, tm):
    t, d = x1.shape
    per_b = seq // tm
    grid_spec = pltpu.PrefetchScalarGridSpec(
        num_scalar_prefetch=1, grid=(t // tm,),
        in_specs=[pl.BlockSpec((tm, d), lambda i, dest: (i, 0)),
                  pl.BlockSpec((None, 6, d), lambda i, dest: (i // per_b, 0, 0)),
                  pl.BlockSpec((1, d), lambda i, dest: (0, 0)),
                  pl.BlockSpec(memory_space=pl.ANY)],
        out_specs=pl.BlockSpec((tm, d), lambda i, dest: (i, 0)),
        scratch_shapes=[pltpu.VMEM((2, tm * d // LANES, LANES), ys.dtype), pltpu.SemaphoreType.DMA((2,))])
    return pl.pallas_call(
        _final_kernel,
        grid_spec=grid_spec,
        out_shape=jax.ShapeDtypeStruct((t, d), f32),
        compiler_params=_cparams(1),
        name="final",
    )(dest, x1, mod3, fg, ys)


def _prep_w_in(w_in_l):
    sizes = (GLA_QK, GLA_QK, GLA_W, GLA_W, GLA_LOWRANK, SWA_W, SWA_KV, SWA_KV)
    off = [int(o) for o in np.concatenate([[0], np.cumsum(sizes)])]
    gla = w_in_l[:, off[0]:off[4]]
    ga = w_in_l[:, off[4]:off[5]]
    sq = [w_in_l[:, off[5] + h * SWA_HD:off[5] + (h + 1) * SWA_HD] for h in SWA_HEAD_ORDER]
    skv = w_in_l[:, off[6]:off[8]]
    pad = jnp.zeros((w_in_l.shape[0], IN_END - off[8]), w_in_l.dtype)
    return jnp.concatenate([gla] + sq + [skv, ga, pad], axis=1).astype(bf16)


def _prep_w_out(w_out_l):
    swa = [w_out_l[GLA_W + h * SWA_HD:GLA_W + (h + 1) * SWA_HD] for h in SWA_HEAD_ORDER]
    return jnp.concatenate([w_out_l[0:GLA_W]] + swa, axis=0).astype(bf16)


def _prep_router(w_grp_l, b_grp_l, w_exp_l, b_exp_l):
    d = w_grp_l.shape[0]
    n_e = N_GROUPS * EXPERTS_PER_GROUP
    wr = jnp.zeros((ROUTER_ROWS, d), f32)
    wr = wr.at[0:N_GROUPS, :].set(w_grp_l.T).at[EXPERTS_PER_GROUP:EXPERTS_PER_GROUP + n_e, :].set(w_exp_l.T)
    br = jnp.zeros((ROUTER_ROWS, 1), f32)
    br = br.at[0:N_GROUPS, 0].set(b_grp_l).at[EXPERTS_PER_GROUP:EXPERTS_PER_GROUP + n_e, 0].set(b_exp_l)
    wr_hi = wr.astype(bf16)
    wr_lo = (wr - wr_hi.astype(f32)).astype(bf16)
    return wr_hi, wr_lo, br


def _layer(x2, mod3, tab, e_mat, bsz, seq, norm_mix_g, w_in, gla_w_a2, gla_b_a2, gla_norm_g, swa_sinks, w_out,
           norm_ffn_g, w_grp, b_grp, w_exp, b_exp, w_gate, w_up, w_down, out_g, tm, tc, tm_moe):
    d = x2.shape[1]
    wa = jnp.pad(gla_w_a2, ((0, LANES - GLA_LOWRANK), (0, 0)))
    wa_hi = wa.astype(bf16)
    wa_lo = (wa - wa_hi.astype(f32)).astype(bf16)
    qk, vg, la, sq, skv = _inproj(x2, mod3, norm_mix_g.reshape(1, d), _prep_w_in(w_in), tab, e_mat,
                                  wa_hi, wa_lo, gla_b_a2.reshape(1, GLA_QK), seq, tm)
    o_gla, o_swa, wg_b, wu_b, wd_b = _mixers(swa_sinks.reshape(1, SWA_HEADS), qk, vg, la,
                                             gla_norm_g.reshape(1, GLA_W), sq, skv, w_gate, w_up, w_down, bsz, seq, tc)
    wr_hi, wr_lo, br = _prep_router(w_grp, b_grp, w_exp, b_exp)
    x1, hrow, route, cnt = _outproj(o_gla, o_swa, x2, mod3, _prep_w_out(w_out), norm_ffn_g.reshape(1, d),
                                    wr_hi, wr_lo, br, seq, tm)
    t = x2.shape[0]
    n_tiles_max = -(-(t // MOE_TM + N_PAIRS + (N_GROUPS - 1) * (MOE_PAIR - 1)) // MOE_PAIR) * MOE_PAIR
    assert n_tiles_max < 2 * LANES and t % MOE_TM == 0
    dest, tile_bucket, tile_owned, n_tiles = _plan(cnt, route, min(t, 2048))
    dest = dest.reshape(t)
    tile_bucket = tile_bucket.reshape(2 * LANES)
    n_tiles = n_tiles.reshape(LANES)[0:1]
    hs = _dispatch(dest, tile_bucket, tile_owned.reshape(2 * LANES), hrow, n_tiles_max, min(t, 1024))
    ys = _moe(tile_bucket, n_tiles, hs, wr_hi.T, br.reshape(1, ROUTER_ROWS), wg_b, wu_b, wd_b)
    return _final(dest, x1, mod3, out_g.reshape(1, d), ys, seq, tm_moe)


def kernel(x, c, positions, ada_w, ada_b, norm_mix_g, w_in, gla_w_a2, gla_b_a2, gla_norm_g, swa_sinks, w_out,
           norm_ffn_g, w_grp, b_grp, w_exp, b_exp, w_gate, w_up, w_down, final_norm_g):
    bsz, seq, d = x.shape
    depth = ada_w.shape[0]
    assert depth == 1, "the final rmsnorm is fused into the last layer's combine kernel"
    tab = _rope_tab(positions)
    e_mat = _rope_expand_matrix()
    x2 = x.reshape(bsz * seq, d)
    for l in range(depth):
        mod3 = _adaln(c, ada_w[l], ada_b[l]).reshape(bsz, 6, d)
        x2 = _layer(x2, mod3, tab, e_mat, bsz, seq, norm_mix_g[l], w_in[l], gla_w_a2[l], gla_b_a2[l],
                    gla_norm_g[l], swa_sinks[l], w_out[l], norm_ffn_g[l], w_grp[l], b_grp[l], w_exp[l], b_exp[l],
                    w_gate[l], w_up[l], w_down[l], final_norm_g, tm=512, tc=512, tm_moe=512)
    return x2.reshape(bsz, seq, d)
```

```python
import functools

import numpy as np
import jax
import jax.numpy as jnp
from jax import lax
from jax.experimental import pallas as pl
from jax.experimental.pallas import tpu as pltpu

f32 = jnp.float32
bf16 = jnp.bfloat16

GLA_HEADS = 4
GLA_DK = 64
GLA_DV = 128
GLA_LOWRANK = 16
GLA_GATE_NORM = 16.0
GLA_CHUNK = 64
SWA_HEADS = 8
SWA_KV_HEADS = 2
SWA_HD = 64
WINDOW = 128
ROPE_THETA = 500000.0
ROPE_DIMS = SWA_HD // 4
N_GROUPS = 4
EXPERTS_PER_GROUP = 8
D_EXPERT = 256
EPS = 1e-6

LANES = 128
VMEM_LIMIT = 52 * 1024 * 1024

GLA_QK = GLA_HEADS * GLA_DK
GLA_W = GLA_HEADS * GLA_DV
SWA_W = SWA_HEADS * SWA_HD
SWA_KV = SWA_KV_HEADS * SWA_HD
SWA_HEAD_ORDER = (0, 4, 1, 5, 2, 6, 3, 7)
ROUTE_ROWS = 8
ROUTER_ROWS = 128
EXPERT_BITS = 3
N_BUCKETS = N_GROUPS << (2 * EXPERT_BITS)
N_PAIRS = N_GROUPS * EXPERTS_PER_GROUP * (EXPERTS_PER_GROUP - 1) // 2
MOE_TM = 128
ROW_DMA_UNROLL = 8

TN = (((0,), (0,)), ((), ()))
NT = (((1,), (1,)), ((), ()))


def _cparams(n_axes):
    return pltpu.CompilerParams(dimension_semantics=("arbitrary",) * n_axes, vmem_limit_bytes=VMEM_LIMIT)


def _split_bf16(v):
    hi = v.astype(bf16)
    lo = (v - hi.astype(f32)).astype(bf16)
    return hi, lo


def _split_stack_bf16(v):
    hi = v.astype(bf16).astype(f32)
    return jnp.concatenate([hi, v - hi], axis=0).astype(bf16)


def _rmsnorm_rows(v):
    return v * lax.rsqrt(jnp.mean(v * v, axis=-1, keepdims=True) + EPS)


def _silu(v):
    return v * jax.nn.sigmoid(v)


def _adaln_kernel(c_ref, w_ref, b_ref, o_ref):
    ca = _silu(c_ref[...])
    o_ref[...] = jnp.dot(ca, w_ref[...], precision=lax.Precision.HIGHEST, preferred_element_type=f32) + b_ref[...]


def _adaln(c, w, b):
    bsz, d = c.shape
    n = w.shape[1]
    return pl.pallas_call(
        _adaln_kernel,
        grid=(n // d,),
        in_specs=[pl.BlockSpec((bsz, d), lambda j: (0, 0)),
                  pl.BlockSpec((d, d), lambda j: (0, j)),
                  pl.BlockSpec((1, d), lambda j: (0, j))],
        out_specs=pl.BlockSpec((bsz, d), lambda j: (0, j)),
        out_shape=jax.ShapeDtypeStruct((bsz, n), f32),
        compiler_params=_cparams(1),
        name="adaln",
    )(c, w, b.reshape(1, n))


def _rope_tab_kernel(pos_ref, invf_ref, o_ref):
    half = ROPE_DIMS // 2
    for b in range(pos_ref.shape[0]):
        ang = pos_ref[b:b + 1, :].astype(f32) * invf_ref[...]
        o_ref[b, 0:half, :] = jnp.cos(ang)
        o_ref[b, half:2 * half, :] = jnp.sin(ang)


def _rope_tab(positions):
    bsz, s = positions.shape
    half = ROPE_DIMS // 2
    inv_freq = (np.float32(ROPE_THETA) ** (-np.arange(0, ROPE_DIMS, 2, dtype=np.float32) / np.float32(ROPE_DIMS)))
    invf = jnp.asarray(inv_freq.astype(np.float32).reshape(half, 1))
    return pl.pallas_call(
        _rope_tab_kernel,
        out_shape=jax.ShapeDtypeStruct((bsz, 2 * half, s), f32),
        name="rope_tab",
    )(positions, invf)


def _rope_expand_matrix():
    half = ROPE_DIMS // 2
    e = np.zeros((2 * half, 3 * LANES), np.float32)
    for j in range(LANES):
        jj = j % SWA_HD
        if jj < half:
            e[jj, j] = 1.0
            e[half + jj, LANES + j] = -1.0
        elif jj < 2 * half:
            e[jj - half, j] = 1.0
            e[half + jj - half, 2 * LANES + j] = 1.0
    return jnp.asarray(np.concatenate([e, e], axis=0), dtype=bf16)


IN_QK0, IN_VG0, IN_SQ0, IN_SKV0, IN_A0, IN_END = 0, 512, 1536, 2048, 2304, 2432
INPROJ_SUBS = 2
OUTPROJ_SUBS = 4


def _inproj_kernel(x_ref, mod_ref, g_ref, w_ref, tab_ref, e_ref, wa_hi_ref, wa_lo_ref, ba_ref,
                   qk_ref, vg_ref, la_ref, sq_ref, skv_ref):
    tm = x_ref.shape[0]
    sub = tm // INPROJ_SUBS
    groups = [slice(s * sub, (s + 1) * sub) for s in range(INPROJ_SUBS)]
    hb = []
    for rows in groups:
        h = (_rmsnorm_rows(x_ref[rows, :]) * g_ref[...]) * (1.0 + mod_ref[1:2, :]) + mod_ref[0:1, :]
        hb.append(h.astype(bf16))

    def proj(s, lo, hi):
        return jnp.dot(hb[s], w_ref[:, lo:hi], preferred_element_type=f32)

    for s, rows in enumerate(groups):
        al_hi, al_lo = _split_bf16(proj(s, IN_A0, IN_END))
        z = (jnp.dot(al_hi, wa_hi_ref[...], preferred_element_type=f32)
             + jnp.dot(al_lo, wa_hi_ref[...], preferred_element_type=f32)
             + jnp.dot(al_hi, wa_lo_ref[...], preferred_element_type=f32)) + ba_ref[...]
        la_ref[rows, :] = (jnp.minimum(z, 0.0) - jnp.log1p(jnp.exp(-jnp.abs(z)))) * (1.0 / GLA_GATE_NORM)

    for s, rows in enumerate(groups):
        qk = proj(s, IN_QK0, IN_VG0)
        lane = lax.broadcasted_iota(jnp.int32, qk.shape, 1)
        qk_ref[rows, :] = jnp.where(lane < GLA_QK, qk * (GLA_DK ** -0.5), qk).astype(bf16)
    for s, rows in enumerate(groups):
        vg_ref[rows, :] = proj(s, IN_VG0, IN_SQ0).astype(bf16)

    lane1 = lax.broadcasted_iota(jnp.int32, (1, LANES), 1)
    tables = []
    for rows in groups:
        tabs = lax.dot_general(_split_stack_bf16(tab_ref[:, rows]), e_ref[...], TN, preferred_element_type=f32)
        tables.append((tabs[:, 0:LANES] + jnp.where((lane1 & (SWA_HD - 1)) < ROPE_DIMS, 0.0, 1.0),
                       tabs[:, LANES:2 * LANES], tabs[:, 2 * LANES:3 * LANES]))

    def rope(s, v):
        cos_t, sa_t, sb_t = tables[s]
        return (v * cos_t + pltpu.roll(v, LANES - ROPE_DIMS // 2, 1) * sa_t
                + pltpu.roll(v, ROPE_DIMS // 2, 1) * sb_t)

    for s, rows in enumerate(groups):
        sq = proj(s, IN_SQ0, IN_SKV0)
        for p in range(SWA_W // LANES):
            cols = slice(p * LANES, (p + 1) * LANES)
            sq_ref[rows, cols] = (rope(s, sq[:, cols]) * (SWA_HD ** -0.5)).astype(bf16)
    for s, rows in enumerate(groups):
        skv = proj(s, IN_SKV0, IN_A0)
        skv_ref[rows, 0:LANES] = rope(s, skv[:, 0:LANES]).astype(bf16)
        skv_ref[rows, LANES:2 * LANES] = skv[:, LANES:2 * LANES].astype(bf16)


def _inproj(x2, mod3, g, w, tab, e_mat, wa_hi, wa_lo, ba, seq, tm):
    t, d = x2.shape
    per_b = seq // tm
    row = lambda i: (i, 0)
    const = lambda i: (0, 0)
    return pl.pallas_call(
        _inproj_kernel,
        grid=(t // tm,),
        in_specs=[pl.BlockSpec((tm, d), row),
                  pl.BlockSpec((None, 6, d), lambda i: (i // per_b, 0, 0)),
                  pl.BlockSpec((1, d), const),
                  pl.BlockSpec(w.shape, const),
                  pl.BlockSpec((None, ROPE_DIMS, tm), lambda i: (i // per_b, 0, i % per_b)),
                  pl.BlockSpec(e_mat.shape, const),
                  pl.BlockSpec(wa_hi.shape, const),
                  pl.BlockSpec(wa_lo.shape, const),
                  pl.BlockSpec(ba.shape, const)],
        out_specs=[pl.BlockSpec((tm, 2 * GLA_QK), row),
                   pl.BlockSpec((tm, 2 * GLA_W), row),
                   pl.BlockSpec((tm, GLA_QK), row),
                   pl.BlockSpec((tm, SWA_W), row),
                   pl.BlockSpec((tm, 2 * SWA_KV), row)],
        out_shape=[jax.ShapeDtypeStruct((t, 2 * GLA_QK), bf16),
                   jax.ShapeDtypeStruct((t, 2 * GLA_W), bf16),
                   jax.ShapeDtypeStruct((t, GLA_QK), f32),
                   jax.ShapeDtypeStruct((t, SWA_W), bf16),
                   jax.ShapeDtypeStruct((t, 2 * SWA_KV), bf16)],
        compiler_params=_cparams(1),
        name="inproj",
    )(x2, mod3, g, w, tab, e_mat, wa_hi, wa_lo, ba)


def _gla_chunk(c, qk_ref, vg_ref, la_ref, gn_ref, o_ref, st_ref, consts):
    causal, tri, ones, first = consts
    c_len = GLA_CHUNK
    rows = slice(c * c_len, (c + 1) * c_len)
    la_hi, la_lo = _split_bf16(la_ref[rows, :])
    b = jnp.dot(tri, la_hi, preferred_element_type=f32) + jnp.dot(tri, la_lo, preferred_element_type=f32)
    b_tot_t = (lax.dot_general(la_hi, ones, TN, preferred_element_type=f32)
               + lax.dot_general(la_lo, ones, TN, preferred_element_type=f32))
    b_last = b[c_len - 1:c_len, :]
    q = qk_ref[rows, 0:GLA_QK].astype(f32)
    k = qk_ref[rows, GLA_QK:2 * GLA_QK].astype(f32)
    q_dec = q * jnp.exp(b)
    k_dec = (k * jnp.exp(-b)).astype(bf16)
    k_rem = (k * jnp.exp(b_last - b)).astype(bf16)
    decay = jnp.exp(b_tot_t)
    for p in range(GLA_HEADS // 2):
        ls = slice(p * LANES, (p + 1) * LANES)
        s_prev = st_ref[ls, :]
        s_prev_b = s_prev.astype(bf16)
        qd = q_dec[:, ls]
        kv_halves = []
        for hh in range(2):
            h = 2 * p + hh
            vs = slice(h * GLA_DV, (h + 1) * GLA_DV)
            qm = jnp.where(first if hh == 0 else ~first, qd, 0.0).astype(bf16)
            v = vg_ref[rows, vs]
            scores = lax.dot_general(qm, k_dec[:, ls], NT, preferred_element_type=f32)
            scores = jnp.where(causal, scores, 0.0).astype(bf16)
            o = (jnp.dot(scores, v, preferred_element_type=f32)
                 + jnp.dot(qm, s_prev_b, preferred_element_type=f32))
            o = _rmsnorm_rows(o) * gn_ref[:, vs]
            gate = vg_ref[rows, GLA_W + h * GLA_DV:GLA_W + (h + 1) * GLA_DV].astype(f32)
            o_ref[rows, vs] = (o * _silu(gate)).astype(bf16)
            kv = lax.dot_general(k_rem[:, ls], v, TN, preferred_element_type=f32)
            kv_halves.append(kv[hh * GLA_DK:(hh + 1) * GLA_DK, :])
        st_ref[ls, :] = decay[ls, :] * s_prev + jnp.concatenate(kv_halves, axis=0)


def _swa_block(blk, j, sink_ref, q_ref, kvp_ref, kvc_ref, o_ref, consts):
    band, s_i, first = consts
    w = WINDOW
    rows = slice(blk * w, (blk + 1) * w)
    prev = kvp_ref if blk == 0 else kvc_ref.at[pl.ds((blk - 1) * w, w)]
    k2 = jnp.concatenate([prev[:, 0:LANES], kvc_ref[rows, 0:LANES]], axis=0)
    v2 = jnp.concatenate([prev[:, LANES:2 * LANES], kvc_ref[rows, LANES:2 * LANES]], axis=0)
    valid = band & ((j > 0) | (s_i >= w)) if blk == 0 else band
    slots = range(SWA_HEADS)
    scores = []
    for sl in slots:
        p, hh = divmod(sl, 2)
        qp = q_ref[rows, p * LANES:(p + 1) * LANES].astype(f32)
        qm = jnp.where(first if hh == 0 else ~first, qp, 0.0).astype(bf16)
        s = lax.dot_general(qm, k2, NT, preferred_element_type=f32)
        scores.append(jnp.where(valid, s, -jnp.inf))
    probs, denoms = [], []
    for sl in slots:
        sink = sink_ref[0, SWA_HEAD_ORDER[sl]]
        m = jnp.maximum(jnp.max(scores[sl], axis=-1, keepdims=True), sink)
        pr = jnp.exp(scores[sl] - m)
        denoms.append(jnp.sum(pr, axis=-1, keepdims=True) + jnp.exp(sink - m))
        probs.append(pr.astype(bf16))
    outs = [jnp.dot(probs[sl], v2, preferred_element_type=f32) / denoms[sl] for sl in slots]
    for p in range(SWA_W // LANES):
        o_ref[rows, p * LANES:(p + 1) * LANES] = jnp.where(first, outs[2 * p], outs[2 * p + 1]).astype(bf16)


def _gla_kernel(qk_ref, vg_ref, la_ref, gn_ref, o_ref, st_ref):
    @pl.when(pl.program_id(1) == 0)
    def _():
        st_ref[...] = jnp.zeros_like(st_ref)

    c_len = GLA_CHUNK
    causal = lax.broadcasted_iota(jnp.int32, (c_len, c_len), 0) >= lax.broadcasted_iota(jnp.int32, (c_len, c_len), 1)
    consts = (causal, jnp.where(causal, 1.0, 0.0).astype(bf16), jnp.ones((c_len, LANES), bf16),
              lax.broadcasted_iota(jnp.int32, (c_len, LANES), 1) < GLA_DK)
    for c in range(qk_ref.shape[0] // c_len):
        _gla_chunk(c, qk_ref, vg_ref, la_ref, gn_ref, o_ref, st_ref, consts)


def _swa_kernel(sink_ref, q_ref, kvp_ref, kvc_ref, wg_ref, wu_ref, wd_ref, o_ref, wg_out, wu_out, wd_out):
    wg_out[...] = wg_ref[...].astype(bf16)
    wu_out[...] = wu_ref[...].astype(bf16)
    wd_out[...] = wd_ref[...].astype(bf16)
    w = WINDOW
    s_i = lax.broadcasted_iota(jnp.int32, (w, 2 * w), 1)
    rel = lax.broadcasted_iota(jnp.int32, (w, 2 * w), 0) + w - s_i
    consts = ((rel >= 0) & (rel < w), s_i, lax.broadcasted_iota(jnp.int32, (w, LANES), 1) < SWA_HD)
    for blk in range(q_ref.shape[0] // w):
        _swa_block(blk, pl.program_id(1), sink_ref, q_ref, kvp_ref, kvc_ref, o_ref, consts)


def _mixers(sinks, qk, vg, la, gn, sq, skv, wg, wu, wd, bsz, seq, ts):
    t = qk.shape[0]
    ns = seq // ts
    blocks = ts // WINDOW
    tile = lambda b, j: (b * ns + j, 0)
    n_exp = wg.shape[0] * wg.shape[1]
    assert n_exp % (bsz * ns) == 0
    per = n_exp // (bsz * ns)
    flat = [w.reshape((n_exp,) + w.shape[2:]) for w in (wg, wu, wd)]
    w_spec = lambda w: pl.BlockSpec((per,) + w.shape[1:], lambda b, j: (b * ns + j, 0, 0))
    o_gla = pl.pallas_call(
        _gla_kernel,
        grid=(bsz, ns),
        in_specs=[pl.BlockSpec((ts, 2 * GLA_QK), tile),
                  pl.BlockSpec((ts, 2 * GLA_W), tile),
                  pl.BlockSpec((ts, GLA_QK), tile),
                  pl.BlockSpec((1, GLA_W), lambda b, j: (0, 0))],
        out_specs=pl.BlockSpec((ts, GLA_W), tile),
        out_shape=jax.ShapeDtypeStruct((t, GLA_W), bf16),
        scratch_shapes=[pltpu.VMEM((GLA_HEADS * GLA_DK, GLA_DV), f32)],
        compiler_params=_cparams(2),
        name="gla",
    )(qk, vg, la, gn)
    o_swa, *w_bf16 = pl.pallas_call(
        _swa_kernel,
        grid=(bsz, ns),
        in_specs=[pl.BlockSpec(memory_space=pltpu.SMEM),
                  pl.BlockSpec((ts, SWA_W), tile),
                  pl.BlockSpec((WINDOW, 2 * SWA_KV), lambda b, j: ((b * ns + j) * blocks - jnp.minimum(j, 1), 0)),
                  pl.BlockSpec((ts, 2 * SWA_KV), tile)] + [w_spec(w) for w in flat],
        out_specs=[pl.BlockSpec((ts, SWA_W), tile)] + [w_spec(w) for w in flat],
        out_shape=[jax.ShapeDtypeStruct((t, SWA_W), bf16)] + [jax.ShapeDtypeStruct(w.shape, bf16) for w in flat],
        compiler_params=_cparams(2),
        name="swa",
    )(sinks, sq, skv, skv, *flat)
    return (o_gla, o_swa) + tuple(o.reshape(w.shape) for o, w in zip(w_bf16, (wg, wu, wd)))


def _store_slabs(ref, v):
    tm, n = v.shape[0], v.shape[1] // LANES
    for c in range(n):
        ref[pl.ds(c, tm, stride=n), :] = v[:, c * LANES:(c + 1) * LANES]


def _load_slabs(ref, tm):
    n = ref.shape[0] // tm
    return jnp.concatenate([ref[pl.ds(c, tm, stride=n), :] for c in range(n)], axis=1)


def _outproj_kernel(og_ref, os_ref, x_ref, mod_ref, wo_ref, g2_ref, wr_hi_ref, wr_lo_ref, br_ref,
                    x1_ref, hrow_ref, route_ref, cnt_ref, run_ref):
    @pl.when(pl.program_id(0) == 0)
    def _():
        run_ref[...] = jnp.zeros_like(run_ref)

    tm = x_ref.shape[0]
    sub = tm // OUTPROJ_SUBS
    n = hrow_ref.shape[0] // tm
    groups = [slice(s * sub, (s + 1) * sub) for s in range(OUTPROJ_SUBS)]
    hs = []
    for rows in groups:
        mix = (jnp.dot(og_ref[rows, :], wo_ref[0:GLA_W, :], preferred_element_type=f32)
               + jnp.dot(os_ref[rows, :], wo_ref[GLA_W:GLA_W + SWA_W, :], preferred_element_type=f32))
        x1 = x_ref[rows, :] + mod_ref[2:3, :] * mix
        x1_ref[rows, :] = x1
        hs.append((_rmsnorm_rows(x1) * g2_ref[...]) * (1.0 + mod_ref[4:5, :]) + mod_ref[3:4, :])

    lts = []
    for s, h in enumerate(hs):
        _store_slabs(hrow_ref.at[pl.ds(s * sub * n, sub * n)], h)
        h_hi, h_lo = _split_bf16(h)
        lts.append((lax.dot_general(wr_hi_ref[...], h_hi, NT, preferred_element_type=f32)
                    + lax.dot_general(wr_hi_ref[...], h_lo, NT, preferred_element_type=f32)
                    + lax.dot_general(wr_lo_ref[...], h_hi, NT, preferred_element_type=f32)) + br_ref[...])

    e_g = EXPERTS_PER_GROUP
    row = lax.broadcasted_iota(jnp.int32, (e_g, sub), 0)
    neg = -jnp.inf
    routed = []
    for lt in lts:
        gl = jnp.where(row < N_GROUPS, lt[0:e_g, :], neg)
        g_max = jnp.max(gl, axis=0, keepdims=True)
        g_gate = 1.0 / jnp.sum(jnp.exp(gl - g_max), axis=0, keepdims=True)
        g_idx = jnp.min(jnp.where(gl == g_max, row, e_g), axis=0, keepdims=True)
        sel = lt[e_g * N_GROUPS:e_g * (N_GROUPS + 1), :]
        for g in range(N_GROUPS - 2, -1, -1):
            sel = jnp.where(g_idx == g, lt[e_g * (g + 1):e_g * (g + 2), :], sel)
        t1 = jnp.max(sel, axis=0, keepdims=True)
        i1 = jnp.min(jnp.where(sel == t1, row, e_g), axis=0, keepdims=True)
        sel2 = jnp.where(row == i1, neg, sel)
        t2 = jnp.max(sel2, axis=0, keepdims=True)
        i2 = jnp.min(jnp.where(sel2 == t2, row, e_g), axis=0, keepdims=True)
        ex = jnp.exp(t2 - t1)
        w1 = g_gate / (1.0 + ex)
        w2 = g_gate * ex / (1.0 + ex)
        first_lo = i1 < i2
        bucket = ((g_idx << (2 * EXPERT_BITS)) | (jnp.minimum(i1, i2) << EXPERT_BITS) | jnp.maximum(i1, i2))
        routed.append((bucket, jnp.where(first_lo, w1, w2), jnp.where(first_lo, w2, w1)))

    t_r = lax.broadcasted_iota(jnp.int32, (sub, sub), 0)
    t_c = lax.broadcasted_iota(jnp.int32, (sub, sub), 1)
    earlier = jnp.where(t_r < t_c, 1.0, 0.0).astype(bf16)
    ones = jnp.ones((sub, sub), bf16)
    run = run_ref[...]
    for rows, (bucket, w_lo, w_hi) in zip(groups, routed):
        onehot = lax.broadcasted_iota(jnp.int32, (N_BUCKETS, sub), 0) == bucket
        oh_b = jnp.where(onehot, 1.0, 0.0).astype(bf16)
        prefix = jnp.dot(oh_b, earlier, preferred_element_type=f32) + run
        rank = jnp.sum(jnp.where(onehot, prefix, 0.0), axis=0, keepdims=True)
        run = run + jnp.dot(oh_b, ones, preferred_element_type=f32)
        route_ref[:, rows] = jnp.where(row == 0, bucket.astype(f32), jnp.where(row == 1, rank, jnp.where(
            row == 2, w_lo, jnp.where(row == 3, w_hi, 0.0))))
    run_ref[...] = run
    cnt_ref[...] = run[:, 0:LANES]


def _outproj(og, osw, x2, mod3, wo, g2, wr_hi, wr_lo, br, seq, tm):
    t, d = x2.shape
    per_b = seq // tm
    row = lambda i: (i, 0)
    const = lambda i: (0, 0)
    return pl.pallas_call(
        _outproj_kernel,
        grid=(t // tm,),
        in_specs=[pl.BlockSpec((tm, GLA_W), row),
                  pl.BlockSpec((tm, SWA_W), row),
                  pl.BlockSpec((tm, d), row),
                  pl.BlockSpec((None, 6, d), lambda i: (i // per_b, 0, 0)),
                  pl.BlockSpec(wo.shape, const),
                  pl.BlockSpec((1, d), const),
                  pl.BlockSpec(wr_hi.shape, const),
                  pl.BlockSpec(wr_lo.shape, const),
                  pl.BlockSpec(br.shape, const)],
        out_specs=[pl.BlockSpec((tm, d), row),
                   pl.BlockSpec((tm * d // LANES, LANES), row),
                   pl.BlockSpec((ROUTE_ROWS, tm), lambda i: (0, i)),
                   pl.BlockSpec((N_BUCKETS, LANES), const)],
        out_shape=[jax.ShapeDtypeStruct((t, d), f32),
                   jax.ShapeDtypeStruct((t * d // LANES, LANES), f32),
                   jax.ShapeDtypeStruct((ROUTE_ROWS, t), f32),
                   jax.ShapeDtypeStruct((N_BUCKETS, LANES), f32)],
        scratch_shapes=[pltpu.VMEM((N_BUCKETS, tm // OUTPROJ_SUBS), f32)],
        compiler_params=_cparams(1),
        name="outproj",
    )(og, osw, x2, mod3, wo, g2, wr_hi, wr_lo, br)


def _plan_kernel(cnt_ref, route_ref, dest_ref, tb_ref, own_ref, nt_ref):
    nb = N_BUCKETS
    tiles = jnp.floor((cnt_ref[...] + (MOE_TM - 1)) * (1.0 / MOE_TM))
    b_r = lax.broadcasted_iota(jnp.int32, (nb, nb), 0)
    b_c = lax.broadcasted_iota(jnp.int32, (nb, nb), 1)
    before = jnp.where(b_c < b_r, 1.0, 0.0).astype(bf16)
    tiles_b = tiles.astype(bf16)
    g_r = b_r >> (2 * EXPERT_BITS)
    g_c = b_c >> (2 * EXPERT_BITS)
    g_tiles = jnp.dot(jnp.where(g_r == g_c, 1.0, 0.0).astype(bf16), tiles_b, preferred_element_type=f32)
    fill = MOE_PAIR * jnp.floor((g_tiles + (MOE_PAIR - 1)) * (1.0 / MOE_PAIR)) - g_tiles
    pads = jnp.dot(jnp.where(g_c < g_r, 1.0, 0.0).astype(bf16), fill.astype(bf16), preferred_element_type=f32)
    t_start = (jnp.dot(before, tiles_b, preferred_element_type=f32)
               + pads * (1.0 / (1 << (2 * EXPERT_BITS))))
    t_end = t_start + tiles
    tile_i = lax.broadcasted_iota(jnp.int32, (nb, 2 * LANES), 1).astype(f32)
    ended = jnp.where(jnp.concatenate([t_end, t_end], axis=1) <= tile_i, 1.0, 0.0)
    tb_ref[...] = jnp.sum(ended, axis=0, keepdims=True).astype(jnp.int32)
    started = jnp.where(jnp.concatenate([t_start, t_start], axis=1) <= tile_i, 1.0, 0.0)
    own_ref[...] = jnp.sum(started - ended, axis=0, keepdims=True).astype(jnp.int32)
    nt_ref[...] = t_end[nb - 1:nb, :].astype(jnp.int32)
    tl = route_ref.shape[1]
    onehot = lax.broadcasted_iota(jnp.int32, (nb, tl), 0) == route_ref[0:1, :].astype(jnp.int32)
    start = lax.dot_general(t_start.astype(bf16), jnp.where(onehot, 1.0, 0.0).astype(bf16), TN,
                            preferred_element_type=f32)
    dest_ref[...] = (start[0:1, :] * MOE_TM + route_ref[1:2, :]).astype(jnp.int32)


def _plan(cnt, route, tl):
    t = route.shape[1]
    return pl.pallas_call(
        _plan_kernel,
        grid=(t // tl,),
        in_specs=[pl.BlockSpec(cnt.shape, lambda i: (0, 0)),
                  pl.BlockSpec((ROUTE_ROWS, tl), lambda i: (0, i))],
        out_specs=[pl.BlockSpec((1, tl), lambda i: (0, i)),
                   pl.BlockSpec((1, 2 * LANES), lambda i: (0, 0)),
                   pl.BlockSpec((1, 2 * LANES), lambda i: (0, 0)),
                   pl.BlockSpec((1, LANES), lambda i: (0, 0))],
        out_shape=[jax.ShapeDtypeStruct((1, t), jnp.int32),
                   jax.ShapeDtypeStruct((1, 2 * LANES), jnp.int32),
                   jax.ShapeDtypeStruct((1, 2 * LANES), jnp.int32),
                   jax.ShapeDtypeStruct((1, LANES), jnp.int32)],
        compiler_params=_cparams(1),
        name="plan",
    )(cnt, route)


def _slab(ref, token, n):
    return ref.at[pl.ds(pl.multiple_of(token * n, n), n)]


def _dispatch_kernel(dest_ref, tb_ref, own_ref, src_ref, out_ref, zero_ref, sem_z, sem_r, *, rows, n_tiles_max):
    i = pl.program_id(0)
    n = src_ref.shape[0] // rows
    tile_rows = zero_ref.shape[0]

    def last_tile(k):
        return (own_ref[k] == 0) | (tb_ref[k] != tb_ref[k + 1])

    def zero_copy(k):
        return pltpu.make_async_copy(zero_ref, out_ref.at[pl.ds(pl.multiple_of(k * tile_rows, tile_rows), tile_rows)],
                                     sem_z)

    @pl.when(i == 0)
    def _():
        zero_ref[...] = jnp.zeros_like(zero_ref)

        def start(k, c):
            @pl.when(last_tile(k))
            def _():
                zero_copy(k).start()
            return c

        def wait(k, c):
            @pl.when(last_tile(k))
            def _():
                zero_copy(k).wait()
            return c

        lax.fori_loop(0, n_tiles_max, start, 0)
        lax.fori_loop(0, n_tiles_max, wait, 0)

    base = i * rows

    def row_copy(r):
        return pltpu.make_async_copy(_slab(src_ref, r, n), _slab(out_ref, dest_ref[base + r], n), sem_r)

    def issue(r8, c):
        for j in range(ROW_DMA_UNROLL):
            row_copy(r8 * ROW_DMA_UNROLL + j).start(priority=j % 2)
        return c

    def drain(r8, c):
        for j in range(ROW_DMA_UNROLL):
            row_copy(r8 * ROW_DMA_UNROLL + j).wait()
        return c

    lax.fori_loop(0, rows // ROW_DMA_UNROLL, issue, 0)
    lax.fori_loop(0, rows // ROW_DMA_UNROLL, drain, 0)


def _dispatch(dest, tb, own, hrow, n_tiles_max, rows):
    t = dest.shape[0]
    n = hrow.shape[0] // t
    grid_spec = pltpu.PrefetchScalarGridSpec(
        num_scalar_prefetch=3, grid=(t // rows,),
        in_specs=[pl.BlockSpec((rows * n, LANES), lambda i, dest, tb, own: (i, 0))],
        out_specs=pl.BlockSpec(memory_space=pl.ANY),
        scratch_shapes=[pltpu.VMEM((MOE_TM * n, LANES), hrow.dtype), pltpu.SemaphoreType.DMA(()),
                        pltpu.SemaphoreType.DMA(())])
    return pl.pallas_call(
        functools.partial(_dispatch_kernel, rows=rows, n_tiles_max=n_tiles_max),
        grid_spec=grid_spec,
        out_shape=jax.ShapeDtypeStruct((n_tiles_max * MOE_TM * n, LANES), hrow.dtype),
        compiler_params=_cparams(1),
        name="dispatch",
    )(dest, tb, own, hrow)


PAIR_BITS = 2
MOE_PAIR = 1 << PAIR_BITS


def _moe_kernel(tb_ref, nt_ref, hs_ref, wr_ref, br_ref, wg_ref, wu_ref, wd_ref, y_ref):
    j = pl.program_id(0)
    rows = hs_ref.shape[0] // MOE_PAIR

    @pl.when(j * MOE_PAIR < nt_ref[0])
    def _():
        e_g = EXPERTS_PER_GROUP
        tiles = range(MOE_PAIR)
        bucket = [tb_ref[j * MOE_PAIR + k] for k in tiles]
        group = [b >> (2 * EXPERT_BITS) for b in bucket]
        experts = [((b >> EXPERT_BITS) & (e_g - 1), b & (e_g - 1)) for b in bucket]
        h = [_load_slabs(hs_ref.at[pl.ds(k * rows, rows)], MOE_TM).astype(bf16) for k in tiles]
        gate_up = [[(jnp.dot(h[k], wg_ref[e], preferred_element_type=f32),
                     jnp.dot(h[k], wu_ref[e], preferred_element_type=f32)) for e in experts[k]] for k in tiles]
        weights = []
        for k in tiles:
            logits = jnp.dot(h[k], wr_ref[...], preferred_element_type=f32) + br_ref[...]
            lane = lax.broadcasted_iota(jnp.int32, logits.shape, 1)

            def pick(col):
                return jnp.sum(jnp.where(lane == col, logits, 0.0), axis=1, keepdims=True)

            gl = jnp.where(lane < N_GROUPS, logits, -jnp.inf)
            g_max = jnp.max(gl, axis=1, keepdims=True)
            g_gate = jnp.exp(pick(group[k]) - g_max) / jnp.sum(jnp.exp(gl - g_max), axis=1, keepdims=True)
            sel = [pick(e_g * (group[k] + 1) + e) for e in experts[k]]
            top = jnp.maximum(sel[0], sel[1])
            p = [jnp.exp(s - top) for s in sel]
            scale = g_gate / (p[0] + p[1])
            weights.append([scale * p[0], scale * p[1]])
        ys = []
        for k in tiles:
            y = None
            for m, e in enumerate(experts[k]):
                a, u = gate_up[k][m]
                yk = weights[k][m] * jnp.dot((_silu(a) * u).astype(bf16), wd_ref[e], preferred_element_type=f32)
                y = yk if y is None else y + yk
            ys.append(y)
        for k in tiles:
            _store_slabs(y_ref.at[pl.ds(k * rows, rows)], ys[k])

    @pl.when(j * MOE_PAIR >= nt_ref[0])
    def _():
        y_ref[...] = jnp.zeros_like(y_ref)


def _moe(tb, nt, hs, wr, br, wg, wu, wd):
    e_g = EXPERTS_PER_GROUP
    d = wg.shape[2]
    n = d // LANES
    step_rows = MOE_PAIR * MOE_TM * n
    n_steps = hs.shape[0] // step_rows
    last = lambda j, tb, nt: jnp.minimum(j, lax.shift_right_logical(nt[0] - 1, PAIR_BITS))
    group = lambda j, tb, nt: (tb[last(j, tb, nt) * MOE_PAIR] >> (2 * EXPERT_BITS), 0, 0, 0)
    const = lambda j, tb, nt: (0, 0)
    grid_spec = pltpu.PrefetchScalarGridSpec(
        num_scalar_prefetch=2, grid=(n_steps,),
        in_specs=[pl.BlockSpec((step_rows, LANES), lambda j, tb, nt: (last(j, tb, nt), 0)),
                  pl.BlockSpec(wr.shape, const),
                  pl.BlockSpec(br.shape, const),
                  pl.BlockSpec((None, e_g, d, D_EXPERT), group),
                  pl.BlockSpec((None, e_g, d, D_EXPERT), group),
                  pl.BlockSpec((None, e_g, D_EXPERT, d), group)],
        out_specs=pl.BlockSpec((step_rows, LANES), lambda j, tb, nt: (j, 0)))
    return pl.pallas_call(
        _moe_kernel,
        grid_spec=grid_spec,
        out_shape=jax.ShapeDtypeStruct(hs.shape, f32),
        compiler_params=_cparams(1),
        name="moe",
    )(tb, nt, hs, wr, br, wg, wu, wd)


def _final_kernel(dest_ref, x1_ref, mod_ref, fg_ref, ys_ref, o_ref, ybuf_ref, sem):
    i = pl.program_id(0)
    tm = x1_ref.shape[0]
    n = ybuf_ref.shape[1] // tm

    def row_copy(tile, slot, r):
        return pltpu.make_async_copy(_slab(ys_ref, dest_ref[tile * tm + r], n), _slab(ybuf_ref.at[slot], r, n),
                                     sem.at[slot])

    def gather(tile, slot):
        def issue(r8, c):
            for j in range(ROW_DMA_UNROLL):
                row_copy(tile, slot, r8 * ROW_DMA_UNROLL + j).start(priority=j % 2)
            return c
        lax.fori_loop(0, tm // ROW_DMA_UNROLL, issue, 0)

    @pl.when(i == 0)
    def _():
        gather(0, 0)

    @pl.when(i + 1 < pl.num_programs(0))
    def _():
        gather(i + 1, (i + 1) % 2)

    slot = i % 2

    def drain(r8, c):
        for j in range(ROW_DMA_UNROLL):
            row_copy(i, slot, r8 * ROW_DMA_UNROLL + j).wait()
        return c

    lax.fori_loop(0, tm // ROW_DMA_UNROLL, drain, 0)

    x2 = x1_ref[...] + mod_ref[5:6, :] * _load_slabs(ybuf_ref.at[slot], tm)
    o_ref[...] = _rmsnorm_rows(x2) * fg_ref[...]


def _final(dest, x1, mod3, fg, ys, seq, tm):
    t, d = x1.shape
    per_b = seq // tm
    grid_spec = pltpu.PrefetchScalarGridSpec(
        num_scalar_prefetch=1, grid=(t // tm,),
        in_specs=[pl.BlockSpec((tm, d), lambda i, dest: (i, 0)),
                  pl.BlockSpec((None, 6, d), lambda i, dest: (i // per_b, 0, 0)),
                  pl.BlockSpec((1, d), lambda i, dest: (0, 0)),
                  pl.BlockSpec(memory_space=pl.ANY)],
        out_specs=pl.BlockSpec((tm, d), lambda i, dest: (i, 0)),
        scratch_shapes=[pltpu.VMEM((2, tm * d // LANES, LANES), ys.dtype), pltpu.SemaphoreType.DMA((2,))])
    return pl.pallas_call(
        _final_kernel,
        grid_spec=grid_spec,
        out_shape=jax.ShapeDtypeStruct((t, d), f32),
        compiler_params=_cparams(1),
        name="final",
    )(dest, x1, mod3, fg, ys)


def _prep_w_in(w_in_l):
    sizes = (GLA_QK, GLA_QK, GLA_W, GLA_W, GLA_LOWRANK, SWA_W, SWA_KV, SWA_KV)
    off = [int(o) for o in np.concatenate([[0], np.cumsum(sizes)])]
    gla = w_in_l[:, off[0]:off[4]]
    ga = w_in_l[:, off[4]:off[5]]
    sq = [w_in_l[:, off[5] + h * SWA_HD:off[5] + (h + 1) * SWA_HD] for h in SWA_HEAD_ORDER]
    skv = w_in_l[:, off[6]:off[8]]
    pad = jnp.zeros((w_in_l.shape[0], IN_END - off[8]), w_in_l.dtype)
    return jnp.concatenate([gla] + sq + [skv, ga, pad], axis=1).astype(bf16)


def _prep_w_out(w_out_l):
    swa = [w_out_l[GLA_W + h * SWA_HD:GLA_W + (h + 1) * SWA_HD] for h in SWA_HEAD_ORDER]
    return jnp.concatenate([w_out_l[0:GLA_W]] + swa, axis=0).astype(bf16)


def _prep_router(w_grp_l, b_grp_l, w_exp_l, b_exp_l):
    d = w_grp_l.shape[0]
    n_e = N_GROUPS * EXPERTS_PER_GROUP
    wr = jnp.zeros((ROUTER_ROWS, d), f32)
    wr = wr.at[0:N_GROUPS, :].set(w_grp_l.T).at[EXPERTS_PER_GROUP:EXPERTS_PER_GROUP + n_e, :].set(w_exp_l.T)
    br = jnp.zeros((ROUTER_ROWS, 1), f32)
    br = br.at[0:N_GROUPS, 0].set(b_grp_l).at[EXPERTS_PER_GROUP:EXPERTS_PER_GROUP + n_e, 0].set(b_exp_l)
    wr_hi = wr.astype(bf16)
    wr_lo = (wr - wr_hi.astype(f32)).astype(bf16)
    return wr_hi, wr_lo, br


def _layer(x2, mod3, tab, e_mat, bsz, seq, norm_mix_g, w_in, gla_w_a2, gla_b_a2, gla_norm_g, swa_sinks, w_out,
           norm_ffn_g, w_grp, b_grp, w_exp, b_exp, w_gate, w_up, w_down, out_g, tm, tc, tm_moe):
    d = x2.shape[1]
    wa = jnp.pad(gla_w_a2, ((0, LANES - GLA_LOWRANK), (0, 0)))
    wa_hi = wa.astype(bf16)
    wa_lo = (wa - wa_hi.astype(f32)).astype(bf16)
    qk, vg, la, sq, skv = _inproj(x2, mod3, norm_mix_g.reshape(1, d), _prep_w_in(w_in), tab, e_mat,
                                  wa_hi, wa_lo, gla_b_a2.reshape(1, GLA_QK), seq, tm)
    o_gla, o_swa, wg_b, wu_b, wd_b = _mixers(swa_sinks.reshape(1, SWA_HEADS), qk, vg, la,
                                             gla_norm_g.reshape(1, GLA_W), sq, skv, w_gate, w_up, w_down, bsz, seq, tc)
    wr_hi, wr_lo, br = _prep_router(w_grp, b_grp, w_exp, b_exp)
    x1, hrow, route, cnt = _outproj(o_gla, o_swa, x2, mod3, _prep_w_out(w_out), norm_ffn_g.reshape(1, d),
                                    wr_hi, wr_lo, br, seq, tm)
    t = x2.shape[0]
    n_tiles_max = -(-(t // MOE_TM + N_PAIRS + (N_GROUPS - 1) * (MOE_PAIR - 1)) // MOE_PAIR) * MOE_PAIR
    assert n_tiles_max < 2 * LANES and t % MOE_TM == 0
    dest, tile_bucket, tile_owned, n_tiles = _plan(cnt, route, min(t, 2048))
    dest = dest.reshape(t)
    tile_bucket = tile_bucket.reshape(2 * LANES)
    n_tiles = n_tiles.reshape(LANES)[0:1]
    hs = _dispatch(dest, tile_bucket, tile_owned.reshape(2 * LANES), hrow, n_tiles_max, min(t, 1024))
    ys = _moe(tile_bucket, n_tiles, hs, wr_hi.T, br.reshape(1, ROUTER_ROWS), wg_b, wu_b, wd_b)
    return _final(dest, x1, mod3, out_g.reshape(1, d), ys, seq, tm_moe)


def kernel(x, c, positions, ada_w, ada_b, norm_mix_g, w_in, gla_w_a2, gla_b_a2, gla_norm_g, swa_sinks, w_out,
           norm_ffn_g, w_grp, b_grp, w_exp, b_exp, w_gate, w_up, w_down, final_norm_g):
    bsz, seq, d = x.shape
    depth = ada_w.shape[0]
    assert depth == 1, "the final rmsnorm is fused into the last layer's combine kernel"
    tab = _rope_tab(positions)
    e_mat = _rope_expand_matrix()
    x2 = x.reshape(bsz * seq, d)
    for l in range(depth):
        mod3 = _adaln(c, ada_w[l], ada_b[l]).reshape(bsz, 6, d)
        x2 = _layer(x2, mod3, tab, e_mat, bsz, seq, norm_mix_g[l], w_in[l], gla_w_a2[l], gla_b_a2[l],
                    gla_norm_g[l], swa_sinks[l], w_out[l], norm_ffn_g[l], w_grp[l], b_grp[l], w_exp[l], b_exp[l],
                    w_gate[l], w_up[l], w_down[l], final_norm_g, tm=512, tc=512, tm_moe=512)
    return x2.reshape(bsz, seq, d)
```

```python
import functools

import numpy as np
import jax
import jax.numpy as jnp
from jax import lax
from jax.experimental import pallas as pl
from jax.experimental.pallas import tpu as pltpu

f32 = jnp.float32
bf16 = jnp.bfloat16

GLA_HEADS = 4
GLA_DK = 64
GLA_DV = 128
GLA_LOWRANK = 16
GLA_GATE_NORM = 16.0
GLA_CHUNK = 64
SWA_HEADS = 8
SWA_KV_HEADS = 2
SWA_HD = 64
WINDOW = 128
ROPE_THETA = 500000.0
ROPE_DIMS = SWA_HD // 4
N_GROUPS = 4
EXPERTS_PER_GROUP = 8
D_EXPERT = 256
EPS = 1e-6

LANES = 128
VMEM_LIMIT = 52 * 1024 * 1024

GLA_QK = GLA_HEADS * GLA_DK
GLA_W = GLA_HEADS * GLA_DV
SWA_W = SWA_HEADS * SWA_HD
SWA_KV = SWA_KV_HEADS * SWA_HD
ROUTE_ROWS = 8
ROUTER_ROWS = 128
EXPERT_BITS = 3
N_BUCKETS = N_GROUPS << (2 * EXPERT_BITS)
N_PAIRS = N_GROUPS * EXPERTS_PER_GROUP * (EXPERTS_PER_GROUP - 1) // 2
MOE_TM = 128
ROW_DMA_UNROLL = 8

TN = (((0,), (0,)), ((), ()))
NT = (((1,), (1,)), ((), ()))


def _cparams(n_axes):
    return pltpu.CompilerParams(dimension_semantics=("arbitrary",) * n_axes, vmem_limit_bytes=VMEM_LIMIT)


def _split_bf16(v):
    hi = v.astype(bf16)
    lo = (v - hi.astype(f32)).astype(bf16)
    return hi, lo


def _split_stack_bf16(v):
    hi = v.astype(bf16).astype(f32)
    return jnp.concatenate([hi, v - hi], axis=0).astype(bf16)


def _rmsnorm_rows(v):
    return v * lax.rsqrt(jnp.mean(v * v, axis=-1, keepdims=True) + EPS)


def _silu(v):
    return v * jax.nn.sigmoid(v)


def _adaln_kernel(c_ref, w_ref, b_ref, o_ref):
    c_hi, c_lo = _split_bf16(_silu(c_ref[...]))
    w_hi, w_lo = _split_bf16(w_ref[...])
    o_ref[...] = (jnp.dot(c_hi, w_hi, preferred_element_type=f32) + jnp.dot(c_lo, w_hi, preferred_element_type=f32)
                  + jnp.dot(c_hi, w_lo, preferred_element_type=f32)) + b_ref[...]


def _adaln(c, w, b):
    bsz, d = c.shape
    n = w.shape[1]
    return pl.pallas_call(
        _adaln_kernel,
        grid=(n // d,),
        in_specs=[pl.BlockSpec((bsz, d), lambda j: (0, 0)),
                  pl.BlockSpec((d, d), lambda j: (0, j)),
                  pl.BlockSpec((1, d), lambda j: (0, j))],
        out_specs=pl.BlockSpec((bsz, d), lambda j: (0, j)),
        out_shape=jax.ShapeDtypeStruct((bsz, n), f32),
        compiler_params=_cparams(1),
        name="adaln",
    )(c, w, b.reshape(1, n))


def _rope_tab_kernel(pos_ref, invf_ref, o_ref):
    half = ROPE_DIMS // 2
    for b in range(pos_ref.shape[0]):
        ang = pos_ref[b:b + 1, :].astype(f32) * invf_ref[...]
        o_ref[b, 0:half, :] = jnp.cos(ang)
        o_ref[b, half:2 * half, :] = jnp.sin(ang)


def _rope_tab(positions):
    bsz, s = positions.shape
    half = ROPE_DIMS // 2
    inv_freq = (np.float32(ROPE_THETA) ** (-np.arange(0, ROPE_DIMS, 2, dtype=np.float32) / np.float32(ROPE_DIMS)))
    invf = jnp.asarray(inv_freq.astype(np.float32).reshape(half, 1))
    return pl.pallas_call(
        _rope_tab_kernel,
        out_shape=jax.ShapeDtypeStruct((bsz, 2 * half, s), f32),
        name="rope_tab",
    )(positions, invf)


def _rope_expand_matrix():
    half = ROPE_DIMS // 2
    e = np.zeros((2 * half, 3 * LANES), np.float32)
    for j in range(LANES):
        jj = j % SWA_HD
        if jj < half:
            e[jj, j] = 1.0
            e[half + jj, LANES + j] = -1.0
        elif jj < 2 * half:
            e[jj - half, j] = 1.0
            e[half + jj - half, 2 * LANES + j] = 1.0
    return jnp.asarray(np.concatenate([e, e], axis=0), dtype=bf16)


IN_QK0, IN_VG0, IN_SQ0, IN_SKV0, IN_A0, IN_END = 0, 512, 1536, 2048, 2304, 2432
INPROJ_SUBS = 2
OUTPROJ_SUBS = 4


def _inproj_kernel(x_ref, mod_ref, g_ref, w_ref, tab_ref, e_ref, wa_hi_ref, wa_lo_ref, ba_ref,
                   qk_ref, vg_ref, la_ref, sq_ref, skv_ref):
    tm = x_ref.shape[0]
    sub = tm // INPROJ_SUBS
    groups = [slice(s * sub, (s + 1) * sub) for s in range(INPROJ_SUBS)]
    hb = []
    for rows in groups:
        h = (_rmsnorm_rows(x_ref[rows, :]) * g_ref[...]) * (1.0 + mod_ref[1:2, :]) + mod_ref[0:1, :]
        hb.append(h.astype(bf16))

    def proj(s, lo, hi):
        return jnp.dot(hb[s], w_ref[:, lo:hi], preferred_element_type=f32)

    for s, rows in enumerate(groups):
        al_hi, al_lo = _split_bf16(proj(s, IN_A0, IN_END))
        z = (jnp.dot(al_hi, wa_hi_ref[...], preferred_element_type=f32)
             + jnp.dot(al_lo, wa_hi_ref[...], preferred_element_type=f32)
             + jnp.dot(al_hi, wa_lo_ref[...], preferred_element_type=f32)) + ba_ref[...]
        la_ref[rows, :] = (jnp.minimum(z, 0.0) - jnp.log1p(jnp.exp(-jnp.abs(z)))) * (1.0 / GLA_GATE_NORM)

    for s, rows in enumerate(groups):
        qk = proj(s, IN_QK0, IN_VG0)
        lane = lax.broadcasted_iota(jnp.int32, qk.shape, 1)
        qk_ref[rows, :] = jnp.where(lane < GLA_QK, qk * (GLA_DK ** -0.5), qk).astype(bf16)
    for s, rows in enumerate(groups):
        vg_ref[rows, :] = proj(s, IN_VG0, IN_SQ0).astype(bf16)

    lane1 = lax.broadcasted_iota(jnp.int32, (1, LANES), 1)
    tables = []
    for rows in groups:
        tabs = lax.dot_general(_split_stack_bf16(tab_ref[:, rows]), e_ref[...], TN, preferred_element_type=f32)
        tables.append((tabs[:, 0:LANES] + jnp.where((lane1 & (SWA_HD - 1)) < ROPE_DIMS, 0.0, 1.0),
                       tabs[:, LANES:2 * LANES], tabs[:, 2 * LANES:3 * LANES]))

    def rope(s, v):
        cos_t, sa_t, sb_t = tables[s]
        return (v * cos_t + pltpu.roll(v, LANES - ROPE_DIMS // 2, 1) * sa_t
                + pltpu.roll(v, ROPE_DIMS // 2, 1) * sb_t)

    for s, rows in enumerate(groups):
        sq = proj(s, IN_SQ0, IN_SKV0)
        for p in range(SWA_W // LANES):
            cols = slice(p * LANES, (p + 1) * LANES)
            sq_ref[rows, cols] = (rope(s, sq[:, cols]) * (SWA_HD ** -0.5)).astype(bf16)
    for s, rows in enumerate(groups):
        skv = proj(s, IN_SKV0, IN_A0)
        first = lax.broadcasted_iota(jnp.int32, (sub, LANES), 1) < SWA_HD
        for part, val in enumerate((rope(s, skv[:, 0:LANES]), skv[:, LANES:2 * LANES])):
            swapped = pltpu.roll(val, SWA_HD, 1)
            skv_ref[rows, (2 * part) * LANES:(2 * part + 1) * LANES] = jnp.where(first, val, swapped).astype(bf16)
            skv_ref[rows, (2 * part + 1) * LANES:(2 * part + 2) * LANES] = jnp.where(first, swapped, val).astype(bf16)


def _inproj(x2, mod3, g, w, tab, e_mat, wa_hi, wa_lo, ba, seq, tm):
    t, d = x2.shape
    per_b = seq // tm
    row = lambda i: (i, 0)
    const = lambda i: (0, 0)
    return pl.pallas_call(
        _inproj_kernel,
        grid=(t // tm,),
        in_specs=[pl.BlockSpec((tm, d), row),
                  pl.BlockSpec((None, 6, d), lambda i: (i // per_b, 0, 0)),
                  pl.BlockSpec((1, d), const),
                  pl.BlockSpec(w.shape, const),
                  pl.BlockSpec((None, ROPE_DIMS, tm), lambda i: (i // per_b, 0, i % per_b)),
                  pl.BlockSpec(e_mat.shape, const),
                  pl.BlockSpec(wa_hi.shape, const),
                  pl.BlockSpec(wa_lo.shape, const),
                  pl.BlockSpec(ba.shape, const)],
        out_specs=[pl.BlockSpec((tm, 2 * GLA_QK), row),
                   pl.BlockSpec((tm, 2 * GLA_W), row),
                   pl.BlockSpec((tm, GLA_QK), row),
                   pl.BlockSpec((tm, SWA_W), row),
                   pl.BlockSpec((tm, 4 * SWA_KV), row)],
        out_shape=[jax.ShapeDtypeStruct((t, 2 * GLA_QK), bf16),
                   jax.ShapeDtypeStruct((t, 2 * GLA_W), bf16),
                   jax.ShapeDtypeStruct((t, GLA_QK), f32),
                   jax.ShapeDtypeStruct((t, SWA_W), bf16),
                   jax.ShapeDtypeStruct((t, 4 * SWA_KV), bf16)],
        compiler_params=_cparams(1),
        name="inproj",
    )(x2, mod3, g, w, tab, e_mat, wa_hi, wa_lo, ba)


def _gla_chunk(c, qk_ref, vg_ref, la_ref, gn_ref, o_ref, st_ref, consts):
    causal, tri, ones, first = consts
    c_len = GLA_CHUNK
    rows = slice(c * c_len, (c + 1) * c_len)
    la_hi, la_lo = _split_bf16(la_ref[rows, :])
    b = jnp.dot(tri, la_hi, preferred_element_type=f32) + jnp.dot(tri, la_lo, preferred_element_type=f32)
    b_tot_t = (lax.dot_general(la_hi, ones, TN, preferred_element_type=f32)
               + lax.dot_general(la_lo, ones, TN, preferred_element_type=f32))
    b_last = b[c_len - 1:c_len, :]
    q = qk_ref[rows, 0:GLA_QK].astype(f32)
    k = qk_ref[rows, GLA_QK:2 * GLA_QK].astype(f32)
    q_dec = q * jnp.exp(b)
    k_dec = (k * jnp.exp(-b)).astype(bf16)
    k_rem = (k * jnp.exp(b_last - b)).astype(bf16)
    decay = jnp.exp(b_tot_t)
    for p in range(GLA_HEADS // 2):
        ls = slice(p * LANES, (p + 1) * LANES)
        s_prev = st_ref[ls, :]
        s_prev_b = s_prev.astype(bf16)
        qd = q_dec[:, ls]
        kv_halves = []
        for hh in range(2):
            h = 2 * p + hh
            vs = slice(h * GLA_DV, (h + 1) * GLA_DV)
            qm = jnp.where(first if hh == 0 else ~first, qd, 0.0).astype(bf16)
            v = vg_ref[rows, vs]
            scores = lax.dot_general(qm, k_dec[:, ls], NT, preferred_element_type=f32)
            scores = jnp.where(causal, scores, 0.0).astype(bf16)
            o = (jnp.dot(scores, v, preferred_element_type=f32)
                 + jnp.dot(qm, s_prev_b, preferred_element_type=f32))
            o = _rmsnorm_rows(o) * gn_ref[:, vs]
            gate = vg_ref[rows, GLA_W + h * GLA_DV:GLA_W + (h + 1) * GLA_DV].astype(f32)
            o_ref[rows, vs] = (o * _silu(gate)).astype(bf16)
            kv = lax.dot_general(k_rem[:, ls], v, TN, preferred_element_type=f32)
            kv_halves.append(kv[hh * GLA_DK:(hh + 1) * GLA_DK, :])
        st_ref[ls, :] = decay[ls, :] * s_prev + jnp.concatenate(kv_halves, axis=0)


def _swa_block(blk, j, sink_ref, q_ref, kvp_ref, kvc_ref, o_ref, consts):
    band, s_i, first = consts
    w = WINDOW
    rows = slice(blk * w, (blk + 1) * w)
    prev = kvp_ref if blk == 0 else kvc_ref.at[pl.ds((blk - 1) * w, w)]
    k2 = [jnp.concatenate([prev[:, g * LANES:(g + 1) * LANES], kvc_ref[rows, g * LANES:(g + 1) * LANES]], axis=0)
          for g in range(SWA_KV_HEADS)]
    v2 = [jnp.concatenate([prev[:, (SWA_KV_HEADS + g) * LANES:(SWA_KV_HEADS + g + 1) * LANES],
                           kvc_ref[rows, (SWA_KV_HEADS + g) * LANES:(SWA_KV_HEADS + g + 1) * LANES]], axis=0)
          for g in range(SWA_KV_HEADS)]
    valid = band & ((j > 0) | (s_i >= w)) if blk == 0 else band
    heads = range(SWA_HEADS)
    kv_of = [h // (SWA_HEADS // SWA_KV_HEADS) for h in heads]
    scores = []
    for h in heads:
        p, hh = divmod(h, 2)
        qp = q_ref[rows, p * LANES:(p + 1) * LANES].astype(f32)
        qm = jnp.where(first if hh == 0 else ~first, qp, 0.0).astype(bf16)
        s = lax.dot_general(qm, k2[kv_of[h]], NT, preferred_element_type=f32)
        scores.append(jnp.where(valid, s, -jnp.inf))
    probs, denoms = [], []
    for h in heads:
        sink = sink_ref[0, h]
        m = jnp.maximum(jnp.max(scores[h], axis=-1, keepdims=True), sink)
        pr = jnp.exp(scores[h] - m)
        denoms.append(jnp.sum(pr, axis=-1, keepdims=True) + jnp.exp(sink - m))
        probs.append(pr.astype(bf16))
    outs = [jnp.dot(probs[h], v2[kv_of[h]], preferred_element_type=f32) / denoms[h] for h in heads]
    for p in range(SWA_W // LANES):
        o_ref[rows, p * LANES:(p + 1) * LANES] = jnp.where(first, outs[2 * p], outs[2 * p + 1]).astype(bf16)


def _gla_kernel(qk_ref, vg_ref, la_ref, gn_ref, o_ref, st_ref):
    @pl.when(pl.program_id(1) == 0)
    def _():
        st_ref[...] = jnp.zeros_like(st_ref)

    c_len = GLA_CHUNK
    causal = lax.broadcasted_iota(jnp.int32, (c_len, c_len), 0) >= lax.broadcasted_iota(jnp.int32, (c_len, c_len), 1)
    consts = (causal, jnp.where(causal, 1.0, 0.0).astype(bf16), jnp.ones((c_len, LANES), bf16),
              lax.broadcasted_iota(jnp.int32, (c_len, LANES), 1) < GLA_DK)
    for c in range(qk_ref.shape[0] // c_len):
        _gla_chunk(c, qk_ref, vg_ref, la_ref, gn_ref, o_ref, st_ref, consts)


def _swa_kernel(sink_ref, q_ref, kvp_ref, kvc_ref, wg_ref, wu_ref, wd_ref, o_ref, wg_out, wu_out, wd_out):
    wg_out[...] = wg_ref[...].astype(bf16)
    wu_out[...] = wu_ref[...].astype(bf16)
    wd_out[...] = wd_ref[...].astype(bf16)
    w = WINDOW
    s_i = lax.broadcasted_iota(jnp.int32, (w, 2 * w), 1)
    rel = lax.broadcasted_iota(jnp.int32, (w, 2 * w), 0) + w - s_i
    consts = ((rel >= 0) & (rel < w), s_i, lax.broadcasted_iota(jnp.int32, (w, LANES), 1) < SWA_HD)
    for blk in range(q_ref.shape[0] // w):
        _swa_block(blk, pl.program_id(1), sink_ref, q_ref, kvp_ref, kvc_ref, o_ref, consts)


def _mixers(sinks, qk, vg, la, gn, sq, skv, wg, wu, wd, bsz, seq, ts):
    t = qk.shape[0]
    ns = seq // ts
    blocks = ts // WINDOW
    tile = lambda b, j: (b * ns + j, 0)
    n_exp = wg.shape[0] * wg.shape[1]
    assert n_exp % (bsz * ns) == 0
    per = n_exp // (bsz * ns)
    flat = [w.reshape((n_exp,) + w.shape[2:]) for w in (wg, wu, wd)]
    w_spec = lambda w: pl.BlockSpec((per,) + w.shape[1:], lambda b, j: (b * ns + j, 0, 0))
    o_gla = pl.pallas_call(
        _gla_kernel,
        grid=(bsz, ns),
        in_specs=[pl.BlockSpec((ts, 2 * GLA_QK), tile),
                  pl.BlockSpec((ts, 2 * GLA_W), tile),
                  pl.BlockSpec((ts, GLA_QK), tile),
                  pl.BlockSpec((1, GLA_W), lambda b, j: (0, 0))],
        out_specs=pl.BlockSpec((ts, GLA_W), tile),
        out_shape=jax.ShapeDtypeStruct((t, GLA_W), bf16),
        scratch_shapes=[pltpu.VMEM((GLA_HEADS * GLA_DK, GLA_DV), f32)],
        compiler_params=_cparams(2),
        name="gla",
    )(qk, vg, la, gn)
    o_swa, *w_bf16 = pl.pallas_call(
        _swa_kernel,
        grid=(bsz, ns),
        in_specs=[pl.BlockSpec(memory_space=pltpu.SMEM),
                  pl.BlockSpec((ts, SWA_W), tile),
                  pl.BlockSpec((WINDOW, 4 * SWA_KV), lambda b, j: ((b * ns + j) * blocks - jnp.minimum(j, 1), 0)),
                  pl.BlockSpec((ts, 4 * SWA_KV), tile)] + [w_spec(w) for w in flat],
        out_specs=[pl.BlockSpec((ts, SWA_W), tile)] + [w_spec(w) for w in flat],
        out_shape=[jax.ShapeDtypeStruct((t, SWA_W), bf16)] + [jax.ShapeDtypeStruct(w.shape, bf16) for w in flat],
        compiler_params=_cparams(2),
        name="swa",
    )(sinks, sq, skv, skv, *flat)
    return (o_gla, o_swa) + tuple(o.reshape(w.shape) for o, w in zip(w_bf16, (wg, wu, wd)))


def _store_slabs(ref, v):
    tm, n = v.shape[0], v.shape[1] // LANES
    for c in range(n):
        ref[pl.ds(c, tm, stride=n), :] = v[:, c * LANES:(c + 1) * LANES]


def _load_slabs(ref, tm):
    n = ref.shape[0] // tm
    return jnp.concatenate([ref[pl.ds(c, tm, stride=n), :] for c in range(n)], axis=1)


def _outproj_kernel(og_ref, os_ref, x_ref, mod_ref, wo_ref, g2_ref, wr_hi_ref, wr_lo_ref, br_ref,
                    x1_ref, hrow_ref, route_ref, cnt_ref, run_ref):
    @pl.when(pl.program_id(0) == 0)
    def _():
        run_ref[...] = jnp.zeros_like(run_ref)

    tm = x_ref.shape[0]
    sub = tm // OUTPROJ_SUBS
    n = hrow_ref.shape[0] // tm
    groups = [slice(s * sub, (s + 1) * sub) for s in range(OUTPROJ_SUBS)]
    hs = []
    for rows in groups:
        mix = (jnp.dot(og_ref[rows, :], wo_ref[0:GLA_W, :], preferred_element_type=f32)
               + jnp.dot(os_ref[rows, :], wo_ref[GLA_W:GLA_W + SWA_W, :], preferred_element_type=f32))
        x1 = x_ref[rows, :] + mod_ref[2:3, :] * mix
        x1_ref[rows, :] = x1
        hs.append((_rmsnorm_rows(x1) * g2_ref[...]) * (1.0 + mod_ref[4:5, :]) + mod_ref[3:4, :])

    lts = []
    for s, h in enumerate(hs):
        _store_slabs(hrow_ref.at[pl.ds(s * sub * n, sub * n)], h)
        h_hi, h_lo = _split_bf16(h)
        lts.append((lax.dot_general(wr_hi_ref[...], h_hi, NT, preferred_element_type=f32)
                    + lax.dot_general(wr_hi_ref[...], h_lo, NT, preferred_element_type=f32)
                    + lax.dot_general(wr_lo_ref[...], h_hi, NT, preferred_element_type=f32)) + br_ref[...])

    e_g = EXPERTS_PER_GROUP
    row = lax.broadcasted_iota(jnp.int32, (e_g, sub), 0)
    neg = -jnp.inf
    routed = []
    for lt in lts:
        gl = jnp.where(row < N_GROUPS, lt[0:e_g, :], neg)
        g_max = jnp.max(gl, axis=0, keepdims=True)
        g_gate = 1.0 / jnp.sum(jnp.exp(gl - g_max), axis=0, keepdims=True)
        g_idx = jnp.min(jnp.where(gl == g_max, row, e_g), axis=0, keepdims=True)
        sel = lt[e_g * N_GROUPS:e_g * (N_GROUPS + 1), :]
        for g in range(N_GROUPS - 2, -1, -1):
            sel = jnp.where(g_idx == g, lt[e_g * (g + 1):e_g * (g + 2), :], sel)
        t1 = jnp.max(sel, axis=0, keepdims=True)
        i1 = jnp.min(jnp.where(sel == t1, row, e_g), axis=0, keepdims=True)
        sel2 = jnp.where(row == i1, neg, sel)
        t2 = jnp.max(sel2, axis=0, keepdims=True)
        i2 = jnp.min(jnp.where(sel2 == t2, row, e_g), axis=0, keepdims=True)
        ex = jnp.exp(t2 - t1)
        w1 = g_gate / (1.0 + ex)
        w2 = g_gate * ex / (1.0 + ex)
        first_lo = i1 < i2
        bucket = ((g_idx << (2 * EXPERT_BITS)) | (jnp.minimum(i1, i2) << EXPERT_BITS) | jnp.maximum(i1, i2))
        routed.append((bucket, jnp.where(first_lo, w1, w2), jnp.where(first_lo, w2, w1)))

    t_r = lax.broadcasted_iota(jnp.int32, (sub, sub), 0)
    t_c = lax.broadcasted_iota(jnp.int32, (sub, sub), 1)
    earlier = jnp.where(t_r < t_c, 1.0, 0.0).astype(bf16)
    ones = jnp.ones((sub, sub), bf16)
    run = run_ref[...]
    for rows, (bucket, w_lo, w_hi) in zip(groups, routed):
        onehot = lax.broadcasted_iota(jnp.int32, (N_BUCKETS, sub), 0) == bucket
        oh_b = jnp.where(onehot, 1.0, 0.0).astype(bf16)
        prefix = jnp.dot(oh_b, earlier, preferred_element_type=f32) + run
        rank = jnp.sum(jnp.where(onehot, prefix, 0.0), axis=0, keepdims=True)
        run = run + jnp.dot(oh_b, ones, preferred_element_type=f32)
        route_ref[:, rows] = jnp.where(row == 0, bucket.astype(f32), jnp.where(row == 1, rank, jnp.where(
            row == 2, w_lo, jnp.where(row == 3, w_hi, 0.0))))
    run_ref[...] = run
    cnt_ref[...] = run[:, 0:LANES]


def _outproj(og, osw, x2, mod3, wo, g2, wr_hi, wr_lo, br, seq, tm):
    t, d = x2.shape
    per_b = seq // tm
    row = lambda i: (i, 0)
    const = lambda i: (0, 0)
    return pl.pallas_call(
        _outproj_kernel,
        grid=(t // tm,),
        in_specs=[pl.BlockSpec((tm, GLA_W), row),
                  pl.BlockSpec((tm, SWA_W), row),
                  pl.BlockSpec((tm, d), row),
                  pl.BlockSpec((None, 6, d), lambda i: (i // per_b, 0, 0)),
                  pl.BlockSpec(wo.shape, const),
                  pl.BlockSpec((1, d), const),
                  pl.BlockSpec(wr_hi.shape, const),
                  pl.BlockSpec(wr_lo.shape, const),
                  pl.BlockSpec(br.shape, const)],
        out_specs=[pl.BlockSpec((tm, d), row),
                   pl.BlockSpec((tm * d // LANES, LANES), row),
                   pl.BlockSpec((ROUTE_ROWS, tm), lambda i: (0, i)),
                   pl.BlockSpec((N_BUCKETS, LANES), const)],
        out_shape=[jax.ShapeDtypeStruct((t, d), f32),
                   jax.ShapeDtypeStruct((t * d // LANES, LANES), f32),
                   jax.ShapeDtypeStruct((ROUTE_ROWS, t), f32),
                   jax.ShapeDtypeStruct((N_BUCKETS, LANES), f32)],
        scratch_shapes=[pltpu.VMEM((N_BUCKETS, tm // OUTPROJ_SUBS), f32)],
        compiler_params=_cparams(1),
        name="outproj",
    )(og, osw, x2, mod3, wo, g2, wr_hi, wr_lo, br)


def _plan_kernel(cnt_ref, route_ref, dest_ref, tb_ref, own_ref, nt_ref):
    nb = N_BUCKETS
    tiles = jnp.floor((cnt_ref[...] + (MOE_TM - 1)) * (1.0 / MOE_TM))
    b_r = lax.broadcasted_iota(jnp.int32, (nb, nb), 0)
    b_c = lax.broadcasted_iota(jnp.int32, (nb, nb), 1)
    before = jnp.where(b_c < b_r, 1.0, 0.0).astype(bf16)
    tiles_b = tiles.astype(bf16)
    g_r = b_r >> (2 * EXPERT_BITS)
    g_c = b_c >> (2 * EXPERT_BITS)
    g_tiles = jnp.dot(jnp.where(g_r == g_c, 1.0, 0.0).astype(bf16), tiles_b, preferred_element_type=f32)
    fill = MOE_PAIR * jnp.floor((g_tiles + (MOE_PAIR - 1)) * (1.0 / MOE_PAIR)) - g_tiles
    pads = jnp.dot(jnp.where(g_c < g_r, 1.0, 0.0).astype(bf16), fill.astype(bf16), preferred_element_type=f32)
    t_start = (jnp.dot(before, tiles_b, preferred_element_type=f32)
               + pads * (1.0 / (1 << (2 * EXPERT_BITS))))
    t_end = t_start + tiles
    tile_i = lax.broadcasted_iota(jnp.int32, (nb, 2 * LANES), 1).astype(f32)
    ended = jnp.where(jnp.concatenate([t_end, t_end], axis=1) <= tile_i, 1.0, 0.0)
    tb_ref[...] = jnp.sum(ended, axis=0, keepdims=True).astype(jnp.int32)
    started = jnp.where(jnp.concatenate([t_start, t_start], axis=1) <= tile_i, 1.0, 0.0)
    own_ref[...] = jnp.sum(started - ended, axis=0, keepdims=True).astype(jnp.int32)
    nt_ref[...] = t_end[nb - 1:nb, :].astype(jnp.int32)
    tl = route_ref.shape[1]
    onehot = lax.broadcasted_iota(jnp.int32, (nb, tl), 0) == route_ref[0:1, :].astype(jnp.int32)
    start = lax.dot_general(t_start.astype(bf16), jnp.where(onehot, 1.0, 0.0).astype(bf16), TN,
                            preferred_element_type=f32)
    dest_ref[...] = (start[0:1, :] * MOE_TM + route_ref[1:2, :]).astype(jnp.int32)


def _plan(cnt, route, tl):
    t = route.shape[1]
    return pl.pallas_call(
        _plan_kernel,
        grid=(t // tl,),
        in_specs=[pl.BlockSpec(cnt.shape, lambda i: (0, 0)),
                  pl.BlockSpec((ROUTE_ROWS, tl), lambda i: (0, i))],
        out_specs=[pl.BlockSpec((1, tl), lambda i: (0, i)),
                   pl.BlockSpec((1, 2 * LANES), lambda i: (0, 0)),
                   pl.BlockSpec((1, 2 * LANES), lambda i: (0, 0)),
                   pl.BlockSpec((1, LANES), lambda i: (0, 0))],
        out_shape=[jax.ShapeDtypeStruct((1, t), jnp.int32),
                   jax.ShapeDtypeStruct((1, 2 * LANES), jnp.int32),
                   jax.ShapeDtypeStruct((1, 2 * LANES), jnp.int32),
                   jax.ShapeDtypeStruct((1, LANES), jnp.int32)],
        compiler_params=_cparams(1),
        name="plan",
    )(cnt, route)


def _slab(ref, token, n):
    return ref.at[pl.ds(pl.multiple_of(token * n, n), n)]


def _dispatch_kernel(dest_ref, tb_ref, own_ref, src_ref, out_ref, zero_ref, sem_z, sem_r, *, rows, n_tiles_max):
    i = pl.program_id(0)
    n = src_ref.shape[0] // rows
    tile_rows = zero_ref.shape[0]

    def last_tile(k):
        return (own_ref[k] == 0) | (tb_ref[k] != tb_ref[k + 1])

    def zero_copy(k):
        return pltpu.make_async_copy(zero_ref, out_ref.at[pl.ds(pl.multiple_of(k * tile_rows, tile_rows), tile_rows)],
                                     sem_z)

    @pl.when(i == 0)
    def _():
        zero_ref[...] = jnp.zeros_like(zero_ref)

        def start(k, c):
            @pl.when(last_tile(k))
            def _():
                zero_copy(k).start()
            return c

        def wait(k, c):
            @pl.when(last_tile(k))
            def _():
                zero_copy(k).wait()
            return c

        lax.fori_loop(0, n_tiles_max, start, 0)
        lax.fori_loop(0, n_tiles_max, wait, 0)

    base = i * rows

    def row_copy(r):
        return pltpu.make_async_copy(_slab(src_ref, r, n), _slab(out_ref, dest_ref[base + r], n), sem_r)

    def issue(r8, c):
        for j in range(ROW_DMA_UNROLL):
            row_copy(r8 * ROW_DMA_UNROLL + j).start(priority=j % 2)
        return c

    def drain(r8, c):
        for j in range(ROW_DMA_UNROLL):
            row_copy(r8 * ROW_DMA_UNROLL + j).wait()
        return c

    lax.fori_loop(0, rows // ROW_DMA_UNROLL, issue, 0)
    lax.fori_loop(0, rows // ROW_DMA_UNROLL, drain, 0)


def _dispatch(dest, tb, own, hrow, n_tiles_max, rows):
    t = dest.shape[0]
    n = hrow.shape[0] // t
    grid_spec = pltpu.PrefetchScalarGridSpec(
        num_scalar_prefetch=3, grid=(t // rows,),
        in_specs=[pl.BlockSpec((rows * n, LANES), lambda i, dest, tb, own: (i, 0))],
        out_specs=pl.BlockSpec(memory_space=pl.ANY),
        scratch_shapes=[pltpu.VMEM((MOE_TM * n, LANES), hrow.dtype), pltpu.SemaphoreType.DMA(()),
                        pltpu.SemaphoreType.DMA(())])
    return pl.pallas_call(
        functools.partial(_dispatch_kernel, rows=rows, n_tiles_max=n_tiles_max),
        grid_spec=grid_spec,
        out_shape=jax.ShapeDtypeStruct((n_tiles_max * MOE_TM * n, LANES), hrow.dtype),
        compiler_params=_cparams(1),
        name="dispatch",
    )(dest, tb, own, hrow)


PAIR_BITS = 2
MOE_PAIR = 1 << PAIR_BITS


def _moe_kernel(tb_ref, nt_ref, hs_ref, wr_ref, br_ref, wg_ref, wu_ref, wd_ref, y_ref):
    j = pl.program_id(0)
    rows = hs_ref.shape[0] // MOE_PAIR

    @pl.when(j * MOE_PAIR < nt_ref[0])
    def _():
        e_g = EXPERTS_PER_GROUP
        tiles = range(MOE_PAIR)
        bucket = [tb_ref[j * MOE_PAIR + k] for k in tiles]
        group = [b >> (2 * EXPERT_BITS) for b in bucket]
        experts = [((b >> EXPERT_BITS) & (e_g - 1), b & (e_g - 1)) for b in bucket]
        h = [_load_slabs(hs_ref.at[pl.ds(k * rows, rows)], MOE_TM).astype(bf16) for k in tiles]
        gate_up = [[(jnp.dot(h[k], wg_ref[e], preferred_element_type=f32),
                     jnp.dot(h[k], wu_ref[e], preferred_element_type=f32)) for e in experts[k]] for k in tiles]
        weights = []
        for k in tiles:
            logits = jnp.dot(h[k], wr_ref[...], preferred_element_type=f32) + br_ref[...]
            lane = lax.broadcasted_iota(jnp.int32, logits.shape, 1)

            def pick(col):
                return jnp.sum(jnp.where(lane == col, logits, 0.0), axis=1, keepdims=True)

            gl = jnp.where(lane < N_GROUPS, logits, -jnp.inf)
            g_max = jnp.max(gl, axis=1, keepdims=True)
            g_gate = jnp.exp(pick(group[k]) - g_max) / jnp.sum(jnp.exp(gl - g_max), axis=1, keepdims=True)
            sel = [pick(e_g * (group[k] + 1) + e) for e in experts[k]]
            top = jnp.maximum(sel[0], sel[1])
            p = [jnp.exp(s - top) for s in sel]
            scale = g_gate / (p[0] + p[1])
            weights.append([scale * p[0], scale * p[1]])
        ys = []
        for k in tiles:
            y = None
            for m, e in enumerate(experts[k]):
                a, u = gate_up[k][m]
                yk = weights[k][m] * jnp.dot((_silu(a) * u).astype(bf16), wd_ref[e], preferred_element_type=f32)
                y = yk if y is None else y + yk
            ys.append(y)
        for k in tiles:
            _store_slabs(y_ref.at[pl.ds(k * rows, rows)], ys[k])

    @pl.when(j * MOE_PAIR >= nt_ref[0])
    def _():
        y_ref[...] = jnp.zeros_like(y_ref)


def _moe(tb, nt, hs, wr, br, wg, wu, wd):
    e_g = EXPERTS_PER_GROUP
    d = wg.shape[2]
    n = d // LANES
    step_rows = MOE_PAIR * MOE_TM * n
    n_steps = hs.shape[0] // step_rows
    last = lambda j, tb, nt: jnp.minimum(j, lax.shift_right_logical(nt[0] - 1, PAIR_BITS))
    group = lambda j, tb, nt: (tb[last(j, tb, nt) * MOE_PAIR] >> (2 * EXPERT_BITS), 0, 0, 0)
    const = lambda j, tb, nt: (0, 0)
    grid_spec = pltpu.PrefetchScalarGridSpec(
        num_scalar_prefetch=2, grid=(n_steps,),
        in_specs=[pl.BlockSpec((step_rows, LANES), lambda j, tb, nt: (last(j, tb, nt), 0)),
                  pl.BlockSpec(wr.shape, const),
                  pl.BlockSpec(br.shape, const),
                  pl.BlockSpec((None, e_g, d, D_EXPERT), group),
                  pl.BlockSpec((None, e_g, d, D_EXPERT), group),
                  pl.BlockSpec((None, e_g, D_EXPERT, d), group)],
        out_specs=pl.BlockSpec((step_rows, LANES), lambda j, tb, nt: (j, 0)))
    return pl.pallas_call(
        _moe_kernel,
        grid_spec=grid_spec,
        out_shape=jax.ShapeDtypeStruct(hs.shape, f32),
        compiler_params=_cparams(1),
        name="moe",
    )(tb, nt, hs, wr, br, wg, wu, wd)


def _final_kernel(dest_ref, x1_ref, mod_ref, fg_ref, ys_ref, o_ref, ybuf_ref, sem):
    i = pl.program_id(0)
    tm = x1_ref.shape[0]
    n = ybuf_ref.shape[1] // tm

    def row_copy(tile, slot, r):
        return pltpu.make_async_copy(_slab(ys_ref, dest_ref[tile * tm + r], n), _slab(ybuf_ref.at[slot], r, n),
                                     sem.at[slot])

    def gather(tile, slot):
        def issue(r8, c):
            for j in range(ROW_DMA_UNROLL):
                row_copy(tile, slot, r8 * ROW_DMA_UNROLL + j).start(priority=j % 2)
            return c
        lax.fori_loop(0, tm // ROW_DMA_UNROLL, issue, 0)

    @pl.when(i == 0)
    def _():
        gather(0, 0)

    @pl.when(i + 1 < pl.num_programs(0))
    def _():
        gather(i + 1, (i + 1) % 2)

    slot = i % 2

    def drain(r8, c):
        for j in range(ROW_DMA_UNROLL):
            row_copy(i, slot, r8 * ROW_DMA_UNROLL + j).wait()
        return c

    lax.fori_loop(0, tm // ROW_DMA_UNROLL, drain, 0)

    x2 = x1_ref[...] + mod_ref[5:6, :] * _load_slabs(ybuf_ref.at[slot], tm)
    o_ref[...] = _rmsnorm_rows(x2) * fg_ref[...]


def _final(dest, x1, mod3, fg, ys, seq, tm):
    t, d = x1.shape
    per_b = seq // tm
    grid_spec = pltpu.PrefetchScalarGridSpec(
        num_scalar_prefetch=1, grid=(t // tm,),
        in_specs=[pl.BlockSpec((tm, d), lambda i, dest: (i, 0)),
                  pl.BlockSpec((None, 6, d), lambda i, dest: (i // per_b, 0, 0)),
                  pl.BlockSpec((1, d), lambda i, dest: (0, 0)),
                  pl.BlockSpec(memory_space=pl.ANY)],
        out_specs=pl.BlockSpec((tm, d), lambda i, dest: (i, 0)),
        scratch_shapes=[pltpu.VMEM((2, tm * d // LANES, LANES), ys.dtype), pltpu.SemaphoreType.DMA((2,))])
    return pl.pallas_call(
        _final_kernel,
        grid_spec=grid_spec,
        out_shape=jax.ShapeDtypeStruct((t, d), f32),
        compiler_params=_cparams(1),
        name="final",
    )(dest, x1, mod3, fg, ys)


def _prep_w_in(w_in_l):
    sizes = (GLA_QK, GLA_QK, GLA_W, GLA_W, GLA_LOWRANK, SWA_W, SWA_KV, SWA_KV)
    off = [int(o) for o in np.concatenate([[0], np.cumsum(sizes)])]
    pad = jnp.zeros((w_in_l.shape[0], IN_END - off[8]), w_in_l.dtype)
    return jnp.concatenate([w_in_l[:, off[0]:off[4]], w_in_l[:, off[5]:off[8]], w_in_l[:, off[4]:off[5]], pad],
                           axis=1).astype(bf16)


def _prep_w_out(w_out_l):
    return w_out_l.astype(bf16)


def _prep_router(w_grp_l, b_grp_l, w_exp_l, b_exp_l):
    d = w_grp_l.shape[0]
    n_e = N_GROUPS * EXPERTS_PER_GROUP
    wr = jnp.zeros((ROUTER_ROWS, d), f32)
    wr = wr.at[0:N_GROUPS, :].set(w_grp_l.T).at[EXPERTS_PER_GROUP:EXPERTS_PER_GROUP + n_e, :].set(w_exp_l.T)
    br = jnp.zeros((ROUTER_ROWS, 1), f32)
    br = br.at[0:N_GROUPS, 0].set(b_grp_l).at[EXPERTS_PER_GROUP:EXPERTS_PER_GROUP + n_e, 0].set(b_exp_l)
    wr_hi = wr.astype(bf16)
    wr_lo = (wr - wr_hi.astype(f32)).astype(bf16)
    return wr_hi, wr_lo, br


def _layer(x2, mod3, tab, e_mat, bsz, seq, norm_mix_g, w_in, gla_w_a2, gla_b_a2, gla_norm_g, swa_sinks, w_out,
           norm_ffn_g, w_grp, b_grp, w_exp, b_exp, w_gate, w_up, w_down, out_g, tm, tc, tm_moe):
    d = x2.shape[1]
    wa = jnp.pad(gla_w_a2, ((0, LANES - GLA_LOWRANK), (0, 0)))
    wa_hi = wa.astype(bf16)
    wa_lo = (wa - wa_hi.astype(f32)).astype(bf16)
    qk, vg, la, sq, skv = _inproj(x2, mod3, norm_mix_g.reshape(1, d), _prep_w_in(w_in), tab, e_mat,
                                  wa_hi, wa_lo, gla_b_a2.reshape(1, GLA_QK), seq, tm)
    o_gla, o_swa, wg_b, wu_b, wd_b = _mixers(swa_sinks.reshape(1, SWA_HEADS), qk, vg, la,
                                             gla_norm_g.reshape(1, GLA_W), sq, skv, w_gate, w_up, w_down, bsz, seq, tc)
    wr_hi, wr_lo, br = _prep_router(w_grp, b_grp, w_exp, b_exp)
    x1, hrow, route, cnt = _outproj(o_gla, o_swa, x2, mod3, _prep_w_out(w_out), norm_ffn_g.reshape(1, d),
                                    wr_hi, wr_lo, br, seq, tm)
    t = x2.shape[0]
    n_tiles_max = -(-(t // MOE_TM + N_PAIRS + (N_GROUPS - 1) * (MOE_PAIR - 1)) // MOE_PAIR) * MOE_PAIR
    assert n_tiles_max < 2 * LANES and t % MOE_TM == 0
    dest, tile_bucket, tile_owned, n_tiles = _plan(cnt, route, min(t, 2048))
    dest = dest.reshape(t)
    tile_bucket = tile_bucket.reshape(2 * LANES)
    n_tiles = n_tiles.reshape(LANES)[0:1]
    hs = _dispatch(dest, tile_bucket, tile_owned.reshape(2 * LANES), hrow, n_tiles_max, min(t, 1024))
    ys = _moe(tile_bucket, n_tiles, hs, wr_hi.T, br.reshape(1, ROUTER_ROWS), wg_b, wu_b, wd_b)
    return _final(dest, x1, mod3, out_g.reshape(1, d), ys, seq, tm_moe)


def kernel(x, c, positions, ada_w, ada_b, norm_mix_g, w_in, gla_w_a2, gla_b_a2, gla_norm_g, swa_sinks, w_out,
           norm_ffn_g, w_grp, b_grp, w_exp, b_exp, w_gate, w_up, w_down, final_norm_g):
    bsz, seq, d = x.shape
    depth = ada_w.shape[0]
    assert depth == 1, "the final rmsnorm is fused into the last layer's combine kernel"
    tab = _rope_tab(positions)
    e_mat = _rope_expand_matrix()
    x2 = x.reshape(bsz * seq, d)
    for l in range(depth):
        mod3 = _adaln(c, ada_w[l], ada_b[l]).reshape(bsz, 6, d)
        x2 = _layer(x2, mod3, tab, e_mat, bsz, seq, norm_mix_g[l], w_in[l], gla_w_a2[l], gla_b_a2[l],
                    gla_norm_g[l], swa_sinks[l], w_out[l], norm_ffn_g[l], w_grp[l], b_grp[l], w_exp[l], b_exp[l],
                    w_gate[l], w_up[l], w_down[l], final_norm_g, tm=512, tc=512, tm_moe=512)
    return x2.reshape(bsz, seq, d)
```

```python
import functools

import numpy as np
import jax
import jax.numpy as jnp
from jax import lax
from jax.experimental import pallas as pl
from jax.experimental.pallas import tpu as pltpu

f32 = jnp.float32
bf16 = jnp.bfloat16

GLA_HEADS = 4
GLA_DK = 64
GLA_DV = 128
GLA_LOWRANK = 16
GLA_GATE_NORM = 16.0
GLA_CHUNK = 64
SWA_HEADS = 8
SWA_KV_HEADS = 2
SWA_HD = 64
WINDOW = 128
ROPE_THETA = 500000.0
ROPE_DIMS = SWA_HD // 4
N_GROUPS = 4
EXPERTS_PER_GROUP = 8
D_EXPERT = 256
EPS = 1e-6

LANES = 128
VMEM_LIMIT = 52 * 1024 * 1024

GLA_QK = GLA_HEADS * GLA_DK
GLA_W = GLA_HEADS * GLA_DV
SWA_W = SWA_HEADS * SWA_HD
SWA_KV = SWA_KV_HEADS * SWA_HD
ROUTE_ROWS = 8
ROUTER_ROWS = 128
EXPERT_BITS = 3
N_BUCKETS = N_GROUPS << (2 * EXPERT_BITS)
N_PAIRS = N_GROUPS * EXPERTS_PER_GROUP * (EXPERTS_PER_GROUP - 1) // 2
MOE_TM = 128
ROW_DMA_UNROLL = 8

TN = (((0,), (0,)), ((), ()))
NT = (((1,), (1,)), ((), ()))


def _cparams(n_axes):
    return pltpu.CompilerParams(dimension_semantics=("arbitrary",) * n_axes, vmem_limit_bytes=VMEM_LIMIT)


def _split_bf16(v):
    hi = v.astype(bf16)
    lo = (v - hi.astype(f32)).astype(bf16)
    return hi, lo


def _split_stack_bf16(v):
    hi = v.astype(bf16).astype(f32)
    return jnp.concatenate([hi, v - hi], axis=0).astype(bf16)


def _rmsnorm_rows(v):
    return v * lax.rsqrt(jnp.mean(v * v, axis=-1, keepdims=True) + EPS)


def _silu(v):
    return v * jax.nn.sigmoid(v)


def _adaln_kernel(c_ref, w_ref, b_ref, o_ref):
    c_hi, c_lo = _split_bf16(_silu(c_ref[...]))
    w_hi, w_lo = _split_bf16(w_ref[...])
    o_ref[...] = (jnp.dot(c_hi, w_hi, preferred_element_type=f32) + jnp.dot(c_lo, w_hi, preferred_element_type=f32)
                  + jnp.dot(c_hi, w_lo, preferred_element_type=f32)) + b_ref[...]


def _adaln(c, w, b):
    bsz, d = c.shape
    n = w.shape[1]
    return pl.pallas_call(
        _adaln_kernel,
        grid=(n // d,),
        in_specs=[pl.BlockSpec((bsz, d), lambda j: (0, 0)),
                  pl.BlockSpec((d, d), lambda j: (0, j)),
                  pl.BlockSpec((1, d), lambda j: (0, j))],
        out_specs=pl.BlockSpec((bsz, d), lambda j: (0, j)),
        out_shape=jax.ShapeDtypeStruct((bsz, n), f32),
        compiler_params=_cparams(1),
        name="adaln",
    )(c, w, b.reshape(1, n))


def _rope_tab_kernel(pos_ref, invf_ref, o_ref):
    half = ROPE_DIMS // 2
    for b in range(pos_ref.shape[0]):
        ang = pos_ref[b:b + 1, :].astype(f32) * invf_ref[...]
        o_ref[b, 0:half, :] = jnp.cos(ang)
        o_ref[b, half:2 * half, :] = jnp.sin(ang)


def _rope_tab(positions):
    bsz, s = positions.shape
    half = ROPE_DIMS // 2
    inv_freq = (np.float32(ROPE_THETA) ** (-np.arange(0, ROPE_DIMS, 2, dtype=np.float32) / np.float32(ROPE_DIMS)))
    invf = jnp.asarray(inv_freq.astype(np.float32).reshape(half, 1))
    return pl.pallas_call(
        _rope_tab_kernel,
        out_shape=jax.ShapeDtypeStruct((bsz, 2 * half, s), f32),
        name="rope_tab",
    )(positions, invf)


def _rope_expand_matrix():
    half = ROPE_DIMS // 2
    e = np.zeros((2 * half, 3 * LANES), np.float32)
    for j in range(LANES):
        jj = j % SWA_HD
        if jj < half:
            e[jj, j] = 1.0
            e[half + jj, LANES + j] = -1.0
        elif jj < 2 * half:
            e[jj - half, j] = 1.0
            e[half + jj - half, 2 * LANES + j] = 1.0
    return jnp.asarray(np.concatenate([e, e], axis=0), dtype=bf16)


IN_QK0, IN_VG0, IN_SQ0, IN_SKV0, IN_A0, IN_END = 0, 512, 1536, 2048, 2304, 2432
INPROJ_SUBS = 2
OUTPROJ_SUBS = 4


def _inproj_kernel(x_ref, mod_ref, g_ref, w_ref, tab_ref, e_ref, wa_hi_ref, wa_lo_ref, ba_ref,
                   qk_ref, vg_ref, la_ref, sq_ref, skv_ref):
    tm = x_ref.shape[0]
    sub = tm // INPROJ_SUBS
    groups = [slice(s * sub, (s + 1) * sub) for s in range(INPROJ_SUBS)]
    hb = []
    for rows in groups:
        h = (_rmsnorm_rows(x_ref[rows, :]) * g_ref[...]) * (1.0 + mod_ref[1:2, :]) + mod_ref[0:1, :]
        hb.append(h.astype(bf16))

    def proj(s, lo, hi):
        return jnp.dot(hb[s], w_ref[:, lo:hi], preferred_element_type=f32)

    a_low = [_split_bf16(proj(s, IN_A0, IN_END)) for s in range(INPROJ_SUBS)]

    for s, rows in enumerate(groups):
        qk = proj(s, IN_QK0, IN_VG0)
        lane = lax.broadcasted_iota(jnp.int32, qk.shape, 1)
        qk_ref[rows, :] = jnp.where(lane < GLA_QK, qk * (GLA_DK ** -0.5), qk).astype(bf16)

    log_a = []
    for al_hi, al_lo in a_low:
        z = (jnp.dot(al_hi, wa_hi_ref[...], preferred_element_type=f32)
             + jnp.dot(al_lo, wa_hi_ref[...], preferred_element_type=f32)
             + jnp.dot(al_hi, wa_lo_ref[...], preferred_element_type=f32)) + ba_ref[...]
        log_a.append(_split_bf16((jnp.minimum(z, 0.0) - jnp.log1p(jnp.exp(-jnp.abs(z)))) * (1.0 / GLA_GATE_NORM)))

    for s, rows in enumerate(groups):
        vg_ref[rows, :] = proj(s, IN_VG0, IN_SQ0).astype(bf16)

    c_r = lax.broadcasted_iota(jnp.int32, (sub, sub), 0)
    c_c = lax.broadcasted_iota(jnp.int32, (sub, sub), 1)
    tri = jnp.where((c_r >= c_c) & ((c_r ^ c_c) < GLA_CHUNK), 1.0, 0.0).astype(bf16)
    for rows, (la_hi, la_lo) in zip(groups, log_a):
        la_ref[rows, :] = (jnp.dot(tri, la_hi, preferred_element_type=f32)
                           + jnp.dot(tri, la_lo, preferred_element_type=f32))

    lane1 = lax.broadcasted_iota(jnp.int32, (1, LANES), 1)
    tables = []
    for rows in groups:
        tabs = lax.dot_general(_split_stack_bf16(tab_ref[:, rows]), e_ref[...], TN, preferred_element_type=f32)
        tables.append((tabs[:, 0:LANES] + jnp.where((lane1 & (SWA_HD - 1)) < ROPE_DIMS, 0.0, 1.0),
                       tabs[:, LANES:2 * LANES], tabs[:, 2 * LANES:3 * LANES]))

    def rope(s, v):
        cos_t, sa_t, sb_t = tables[s]
        return (v * cos_t + pltpu.roll(v, LANES - ROPE_DIMS // 2, 1) * sa_t
                + pltpu.roll(v, ROPE_DIMS // 2, 1) * sb_t)

    for s, rows in enumerate(groups):
        sq = proj(s, IN_SQ0, IN_SKV0)
        for p in range(SWA_W // LANES):
            cols = slice(p * LANES, (p + 1) * LANES)
            sq_ref[rows, cols] = (rope(s, sq[:, cols]) * (SWA_HD ** -0.5)).astype(bf16)
    for s, rows in enumerate(groups):
        skv = proj(s, IN_SKV0, IN_A0)
        first = lax.broadcasted_iota(jnp.int32, (sub, LANES), 1) < SWA_HD
        for part, val in enumerate((rope(s, skv[:, 0:LANES]), skv[:, LANES:2 * LANES])):
            swapped = pltpu.roll(val, SWA_HD, 1)
            skv_ref[rows, (2 * part) * LANES:(2 * part + 1) * LANES] = jnp.where(first, val, swapped).astype(bf16)
            skv_ref[rows, (2 * part + 1) * LANES:(2 * part + 2) * LANES] = jnp.where(first, swapped, val).astype(bf16)


def _inproj(x2, mod3, g, w, tab, e_mat, wa_hi, wa_lo, ba, seq, tm):
    t, d = x2.shape
    per_b = seq // tm
    row = lambda i: (i, 0)
    const = lambda i: (0, 0)
    return pl.pallas_call(
        _inproj_kernel,
        grid=(t // tm,),
        in_specs=[pl.BlockSpec((tm, d), row),
                  pl.BlockSpec((None, 6, d), lambda i: (i // per_b, 0, 0)),
                  pl.BlockSpec((1, d), const),
                  pl.BlockSpec(w.shape, const),
                  pl.BlockSpec((None, ROPE_DIMS, tm), lambda i: (i // per_b, 0, i % per_b)),
                  pl.BlockSpec(e_mat.shape, const),
                  pl.BlockSpec(wa_hi.shape, const),
                  pl.BlockSpec(wa_lo.shape, const),
                  pl.BlockSpec(ba.shape, const)],
        out_specs=[pl.BlockSpec((tm, 2 * GLA_QK), row),
                   pl.BlockSpec((tm, 2 * GLA_W), row),
                   pl.BlockSpec((tm, GLA_QK), row),
                   pl.BlockSpec((tm, SWA_W), row),
                   pl.BlockSpec((tm, 4 * SWA_KV), row)],
        out_shape=[jax.ShapeDtypeStruct((t, 2 * GLA_QK), bf16),
                   jax.ShapeDtypeStruct((t, 2 * GLA_W), bf16),
                   jax.ShapeDtypeStruct((t, GLA_QK), f32),
                   jax.ShapeDtypeStruct((t, SWA_W), bf16),
                   jax.ShapeDtypeStruct((t, 4 * SWA_KV), bf16)],
        compiler_params=_cparams(1),
        name="inproj",
    )(x2, mod3, g, w, tab, e_mat, wa_hi, wa_lo, ba)


def _gla_chunk(c, qk_ref, vg_ref, la_ref, gn_ref, o_ref, st_ref, consts):
    causal, tri, ones, first = consts
    c_len = GLA_CHUNK
    rows = slice(c * c_len, (c + 1) * c_len)
    b = la_ref[rows, :]
    b_last = b[c_len - 1:c_len, :]
    p1 = b_last.astype(bf16).astype(f32)
    p2 = (b_last - p1).astype(bf16).astype(f32)
    r_i = lax.broadcasted_iota(jnp.int32, (ones.shape[0], GLA_QK), 0)
    stack = jnp.where(r_i == 0, p1, jnp.where(r_i == 1, p2, jnp.where(r_i == 2, b_last - p1 - p2, 0.0)))
    b_tot_t = lax.dot_general(stack.astype(bf16), ones, TN, preferred_element_type=f32)
    q = qk_ref[rows, 0:GLA_QK].astype(f32)
    k = qk_ref[rows, GLA_QK:2 * GLA_QK].astype(f32)
    q_dec = q * jnp.exp(b)
    k_dec = (k * jnp.exp(-b)).astype(bf16)
    k_rem = (k * jnp.exp(b_last - b)).astype(bf16)
    decay = jnp.exp(b_tot_t)
    for p in range(GLA_HEADS // 2):
        ls = slice(p * LANES, (p + 1) * LANES)
        s_prev = st_ref[ls, :]
        s_prev_b = s_prev.astype(bf16)
        qd = q_dec[:, ls]
        kv_halves = []
        for hh in range(2):
            h = 2 * p + hh
            vs = slice(h * GLA_DV, (h + 1) * GLA_DV)
            qm = jnp.where(first if hh == 0 else ~first, qd, 0.0).astype(bf16)
            v = vg_ref[rows, vs]
            scores = lax.dot_general(qm, k_dec[:, ls], NT, preferred_element_type=f32)
            scores = jnp.where(causal, scores, 0.0).astype(bf16)
            o = (jnp.dot(scores, v, preferred_element_type=f32)
                 + jnp.dot(qm, s_prev_b, preferred_element_type=f32))
            o = _rmsnorm_rows(o) * gn_ref[:, vs]
            gate = vg_ref[rows, GLA_W + h * GLA_DV:GLA_W + (h + 1) * GLA_DV].astype(f32)
            o_ref[rows, vs] = (o * _silu(gate)).astype(bf16)
            kv = lax.dot_general(k_rem[:, ls], v, TN, preferred_element_type=f32)
            kv_halves.append(kv[hh * GLA_DK:(hh + 1) * GLA_DK, :])
        st_ref[ls, :] = decay[ls, :] * s_prev + jnp.concatenate(kv_halves, axis=0)


def _swa_block(blk, j, sink_ref, q_ref, kvp_ref, kvc_ref, o_ref, consts):
    band, s_i, first = consts
    w = WINDOW
    rows = slice(blk * w, (blk + 1) * w)
    prev = kvp_ref if blk == 0 else kvc_ref.at[pl.ds((blk - 1) * w, w)]
    k2 = [jnp.concatenate([prev[:, g * LANES:(g + 1) * LANES], kvc_ref[rows, g * LANES:(g + 1) * LANES]], axis=0)
          for g in range(SWA_KV_HEADS)]
    v2 = [jnp.concatenate([prev[:, (SWA_KV_HEADS + g) * LANES:(SWA_KV_HEADS + g + 1) * LANES],
                           kvc_ref[rows, (SWA_KV_HEADS + g) * LANES:(SWA_KV_HEADS + g + 1) * LANES]], axis=0)
          for g in range(SWA_KV_HEADS)]
    valid = band & ((j > 0) | (s_i >= w)) if blk == 0 else band
    heads = range(SWA_HEADS)
    kv_of = [h // (SWA_HEADS // SWA_KV_HEADS) for h in heads]
    scores = []
    for h in heads:
        p, hh = divmod(h, 2)
        qp = q_ref[rows, p * LANES:(p + 1) * LANES].astype(f32)
        qm = jnp.where(first if hh == 0 else ~first, qp, 0.0).astype(bf16)
        s = lax.dot_general(qm, k2[kv_of[h]], NT, preferred_element_type=f32)
        scores.append(jnp.where(valid, s, -jnp.inf))
    probs, denoms = [], []
    for h in heads:
        sink = sink_ref[0, h]
        m = jnp.maximum(jnp.max(scores[h], axis=-1, keepdims=True), sink)
        pr = jnp.exp(scores[h] - m)
        denoms.append(jnp.sum(pr, axis=-1, keepdims=True) + jnp.exp(sink - m))
        probs.append(pr.astype(bf16))
    outs = [jnp.dot(probs[h], v2[kv_of[h]], preferred_element_type=f32) / denoms[h] for h in heads]
    for p in range(SWA_W // LANES):
        o_ref[rows, p * LANES:(p + 1) * LANES] = jnp.where(first, outs[2 * p], outs[2 * p + 1]).astype(bf16)


def _gla_kernel(qk_ref, vg_ref, la_ref, gn_ref, o_ref, st_ref):
    @pl.when(pl.program_id(1) == 0)
    def _():
        st_ref[...] = jnp.zeros_like(st_ref)

    c_len = GLA_CHUNK
    causal = lax.broadcasted_iota(jnp.int32, (c_len, c_len), 0) >= lax.broadcasted_iota(jnp.int32, (c_len, c_len), 1)
    consts = (causal, None, jnp.ones((2 * ROUTE_ROWS, LANES), bf16),
              lax.broadcasted_iota(jnp.int32, (c_len, LANES), 1) < GLA_DK)
    for c in range(qk_ref.shape[0] // c_len):
        _gla_chunk(c, qk_ref, vg_ref, la_ref, gn_ref, o_ref, st_ref, consts)


def _swa_kernel(sink_ref, q_ref, kvp_ref, kvc_ref, wg_ref, wu_ref, wd_ref, o_ref, wg_out, wu_out, wd_out):
    wg_out[...] = wg_ref[...].astype(bf16)
    wu_out[...] = wu_ref[...].astype(bf16)
    wd_out[...] = wd_ref[...].astype(bf16)
    w = WINDOW
    s_i = lax.broadcasted_iota(jnp.int32, (w, 2 * w), 1)
    rel = lax.broadcasted_iota(jnp.int32, (w, 2 * w), 0) + w - s_i
    consts = ((rel >= 0) & (rel < w), s_i, lax.broadcasted_iota(jnp.int32, (w, LANES), 1) < SWA_HD)
    for blk in range(q_ref.shape[0] // w):
        _swa_block(blk, pl.program_id(1), sink_ref, q_ref, kvp_ref, kvc_ref, o_ref, consts)


def _mixers(sinks, qk, vg, la, gn, sq, skv, wg, wu, wd, bsz, seq, ts):
    t = qk.shape[0]
    ns = seq // ts
    blocks = ts // WINDOW
    tile = lambda b, j: (b * ns + j, 0)
    n_exp = wg.shape[0] * wg.shape[1]
    assert n_exp % (bsz * ns) == 0
    per = n_exp // (bsz * ns)
    flat = [w.reshape((n_exp,) + w.shape[2:]) for w in (wg, wu, wd)]
    w_spec = lambda w: pl.BlockSpec((per,) + w.shape[1:], lambda b, j: (b * ns + j, 0, 0))
    o_gla = pl.pallas_call(
        _gla_kernel,
        grid=(bsz, ns),
        in_specs=[pl.BlockSpec((ts, 2 * GLA_QK), tile),
                  pl.BlockSpec((ts, 2 * GLA_W), tile),
                  pl.BlockSpec((ts, GLA_QK), tile),
                  pl.BlockSpec((1, GLA_W), lambda b, j: (0, 0))],
        out_specs=pl.BlockSpec((ts, GLA_W), tile),
        out_shape=jax.ShapeDtypeStruct((t, GLA_W), bf16),
        scratch_shapes=[pltpu.VMEM((GLA_HEADS * GLA_DK, GLA_DV), f32)],
        compiler_params=_cparams(2),
        name="gla",
    )(qk, vg, la, gn)
    o_swa, *w_bf16 = pl.pallas_call(
        _swa_kernel,
        grid=(bsz, ns),
        in_specs=[pl.BlockSpec(memory_space=pltpu.SMEM),
                  pl.BlockSpec((ts, SWA_W), tile),
                  pl.BlockSpec((WINDOW, 4 * SWA_KV), lambda b, j: ((b * ns + j) * blocks - jnp.minimum(j, 1), 0)),
                  pl.BlockSpec((ts, 4 * SWA_KV), tile)] + [w_spec(w) for w in flat],
        out_specs=[pl.BlockSpec((ts, SWA_W), tile)] + [w_spec(w) for w in flat],
        out_shape=[jax.ShapeDtypeStruct((t, SWA_W), bf16)] + [jax.ShapeDtypeStruct(w.shape, bf16) for w in flat],
        compiler_params=_cparams(2),
        name="swa",
    )(sinks, sq, skv, skv, *flat)
    return (o_gla, o_swa) + tuple(o.reshape(w.shape) for o, w in zip(w_bf16, (wg, wu, wd)))


def _store_slabs(ref, v):
    tm, n = v.shape[0], v.shape[1] // LANES
    for c in range(n):
        ref[pl.ds(c, tm, stride=n), :] = v[:, c * LANES:(c + 1) * LANES]


def _load_slabs(ref, tm):
    n = ref.shape[0] // tm
    return jnp.concatenate([ref[pl.ds(c, tm, stride=n), :] for c in range(n)], axis=1)


def _outproj_kernel(og_ref, os_ref, x_ref, mod_ref, wo_ref, g2_ref, wr_hi_ref, wr_lo_ref, br_ref,
                    x1_ref, hrow_ref, route_ref, cnt_ref, run_ref):
    @pl.when(pl.program_id(0) == 0)
    def _():
        run_ref[...] = jnp.zeros_like(run_ref)

    tm = x_ref.shape[0]
    sub = tm // OUTPROJ_SUBS
    n = hrow_ref.shape[0] // tm
    groups = [slice(s * sub, (s + 1) * sub) for s in range(OUTPROJ_SUBS)]
    hs = []
    for rows in groups:
        mix = (jnp.dot(og_ref[rows, :], wo_ref[0:GLA_W, :], preferred_element_type=f32)
               + jnp.dot(os_ref[rows, :], wo_ref[GLA_W:GLA_W + SWA_W, :], preferred_element_type=f32))
        x1 = x_ref[rows, :] + mod_ref[2:3, :] * mix
        x1_ref[rows, :] = x1
        hs.append((_rmsnorm_rows(x1) * g2_ref[...]) * (1.0 + mod_ref[4:5, :]) + mod_ref[3:4, :])

    lts = []
    for s, h in enumerate(hs):
        _store_slabs(hrow_ref.at[pl.ds(s * sub * n, sub * n)], h)
        h_hi, h_lo = _split_bf16(h)
        lts.append((lax.dot_general(wr_hi_ref[...], h_hi, NT, preferred_element_type=f32)
                    + lax.dot_general(wr_hi_ref[...], h_lo, NT, preferred_element_type=f32)
                    + lax.dot_general(wr_lo_ref[...], h_hi, NT, preferred_element_type=f32)) + br_ref[...])

    e_g = EXPERTS_PER_GROUP
    row = lax.broadcasted_iota(jnp.int32, (e_g, sub), 0)
    neg = -jnp.inf
    routed = []
    for lt in lts:
        gl = jnp.where(row < N_GROUPS, lt[0:e_g, :], neg)
        g_max = jnp.max(gl, axis=0, keepdims=True)
        g_gate = 1.0 / jnp.sum(jnp.exp(gl - g_max), axis=0, keepdims=True)
        g_idx = jnp.min(jnp.where(gl == g_max, row, e_g), axis=0, keepdims=True)
        sel = lt[e_g * N_GROUPS:e_g * (N_GROUPS + 1), :]
        for g in range(N_GROUPS - 2, -1, -1):
            sel = jnp.where(g_idx == g, lt[e_g * (g + 1):e_g * (g + 2), :], sel)
        t1 = jnp.max(sel, axis=0, keepdims=True)
        i1 = jnp.min(jnp.where(sel == t1, row, e_g), axis=0, keepdims=True)
        sel2 = jnp.where(row == i1, neg, sel)
        t2 = jnp.max(sel2, axis=0, keepdims=True)
        i2 = jnp.min(jnp.where(sel2 == t2, row, e_g), axis=0, keepdims=True)
        ex = jnp.exp(t2 - t1)
        w1 = g_gate / (1.0 + ex)
        w2 = g_gate * ex / (1.0 + ex)
        first_lo = i1 < i2
        bucket = ((g_idx << (2 * EXPERT_BITS)) | (jnp.minimum(i1, i2) << EXPERT_BITS) | jnp.maximum(i1, i2))
        routed.append((bucket, jnp.where(first_lo, w1, w2), jnp.where(first_lo, w2, w1)))

    t_r = lax.broadcasted_iota(jnp.int32, (sub, sub), 0)
    t_c = lax.broadcasted_iota(jnp.int32, (sub, sub), 1)
    earlier = jnp.where(t_r < t_c, 1.0, 0.0).astype(bf16)
    ones = jnp.ones((sub, sub), bf16)
    run = run_ref[...]
    for rows, (bucket, w_lo, w_hi) in zip(groups, routed):
        onehot = lax.broadcasted_iota(jnp.int32, (N_BUCKETS, sub), 0) == bucket
        oh_b = jnp.where(onehot, 1.0, 0.0).astype(bf16)
        prefix = jnp.dot(oh_b, earlier, preferred_element_type=f32) + run
        rank = jnp.sum(jnp.where(onehot, prefix, 0.0), axis=0, keepdims=True)
        run = run + jnp.dot(oh_b, ones, preferred_element_type=f32)
        route_ref[:, rows] = jnp.where(row == 0, bucket.astype(f32), jnp.where(row == 1, rank, jnp.where(
            row == 2, w_lo, jnp.where(row == 3, w_hi, 0.0))))
    run_ref[...] = run
    cnt_ref[...] = run[:, 0:LANES]


def _outproj(og, osw, x2, mod3, wo, g2, wr_hi, wr_lo, br, seq, tm):
    t, d = x2.shape
    per_b = seq // tm
    row = lambda i: (i, 0)
    const = lambda i: (0, 0)
    return pl.pallas_call(
        _outproj_kernel,
        grid=(t // tm,),
        in_specs=[pl.BlockSpec((tm, GLA_W), row),
                  pl.BlockSpec((tm, SWA_W), row),
                  pl.BlockSpec((tm, d), row),
                  pl.BlockSpec((None, 6, d), lambda i: (i // per_b, 0, 0)),
                  pl.BlockSpec(wo.shape, const),
                  pl.BlockSpec((1, d), const),
                  pl.BlockSpec(wr_hi.shape, const),
                  pl.BlockSpec(wr_lo.shape, const),
                  pl.BlockSpec(br.shape, const)],
        out_specs=[pl.BlockSpec((tm, d), row),
                   pl.BlockSpec((tm * d // LANES, LANES), row),
                   pl.BlockSpec((ROUTE_ROWS, tm), lambda i: (0, i)),
                   pl.BlockSpec((N_BUCKETS, LANES), const)],
        out_shape=[jax.ShapeDtypeStruct((t, d), f32),
                   jax.ShapeDtypeStruct((t * d // LANES, LANES), f32),
                   jax.ShapeDtypeStruct((ROUTE_ROWS, t), f32),
                   jax.ShapeDtypeStruct((N_BUCKETS, LANES), f32)],
        scratch_shapes=[pltpu.VMEM((N_BUCKETS, tm // OUTPROJ_SUBS), f32)],
        compiler_params=_cparams(1),
        name="outproj",
    )(og, osw, x2, mod3, wo, g2, wr_hi, wr_lo, br)


def _plan_kernel(cnt_ref, route_ref, dest_ref, tb_ref, own_ref, nt_ref):
    nb = N_BUCKETS
    tiles = jnp.floor((cnt_ref[...] + (MOE_TM - 1)) * (1.0 / MOE_TM))
    b_r = lax.broadcasted_iota(jnp.int32, (nb, nb), 0)
    b_c = lax.broadcasted_iota(jnp.int32, (nb, nb), 1)
    before = jnp.where(b_c < b_r, 1.0, 0.0).astype(bf16)
    tiles_b = tiles.astype(bf16)
    g_r = b_r >> (2 * EXPERT_BITS)
    g_c = b_c >> (2 * EXPERT_BITS)
    g_tiles = jnp.dot(jnp.where(g_r == g_c, 1.0, 0.0).astype(bf16), tiles_b, preferred_element_type=f32)
    fill = MOE_PAIR * jnp.floor((g_tiles + (MOE_PAIR - 1)) * (1.0 / MOE_PAIR)) - g_tiles
    pads = jnp.dot(jnp.where(g_c < g_r, 1.0, 0.0).astype(bf16), fill.astype(bf16), preferred_element_type=f32)
    t_start = (jnp.dot(before, tiles_b, preferred_element_type=f32)
               + pads * (1.0 / (1 << (2 * EXPERT_BITS))))
    t_end = t_start + tiles
    tile_i = lax.broadcasted_iota(jnp.int32, (nb, 2 * LANES), 1).astype(f32)
    ended = jnp.where(jnp.concatenate([t_end, t_end], axis=1) <= tile_i, 1.0, 0.0)
    tb_ref[...] = jnp.sum(ended, axis=0, keepdims=True).astype(jnp.int32)
    started = jnp.where(jnp.concatenate([t_start, t_start], axis=1) <= tile_i, 1.0, 0.0)
    own_ref[...] = jnp.sum(started - ended, axis=0, keepdims=True).astype(jnp.int32)
    nt_ref[...] = t_end[nb - 1:nb, :].astype(jnp.int32)
    tl = route_ref.shape[1]
    onehot = lax.broadcasted_iota(jnp.int32, (nb, tl), 0) == route_ref[0:1, :].astype(jnp.int32)
    start = lax.dot_general(t_start.astype(bf16), jnp.where(onehot, 1.0, 0.0).astype(bf16), TN,
                            preferred_element_type=f32)
    dest_ref[...] = (start[0:1, :] * MOE_TM + route_ref[1:2, :]).astype(jnp.int32)


def _plan(cnt, route, tl):
    t = route.shape[1]
    return pl.pallas_call(
        _plan_kernel,
        grid=(t // tl,),
        in_specs=[pl.BlockSpec(cnt.shape, lambda i: (0, 0)),
                  pl.BlockSpec((ROUTE_ROWS, tl), lambda i: (0, i))],
        out_specs=[pl.BlockSpec((1, tl), lambda i: (0, i)),
                   pl.BlockSpec((1, 2 * LANES), lambda i: (0, 0)),
                   pl.BlockSpec((1, 2 * LANES), lambda i: (0, 0)),
                   pl.BlockSpec((1, LANES), lambda i: (0, 0))],
        out_shape=[jax.ShapeDtypeStruct((1, t), jnp.int32),
                   jax.ShapeDtypeStruct((1, 2 * LANES), jnp.int32),
                   jax.ShapeDtypeStruct((1, 2 * LANES), jnp.int32),
                   jax.ShapeDtypeStruct((1, LANES), jnp.int32)],
        compiler_params=_cparams(1),
        name="plan",
    )(cnt, route)


def _slab(ref, token, n):
    return ref.at[pl.ds(pl.multiple_of(token * n, n), n)]


def _dispatch_kernel(dest_ref, tb_ref, own_ref, src_ref, out_ref, zero_ref, sem_z, sem_r, *, rows, n_tiles_max):
    i = pl.program_id(0)
    n = src_ref.shape[0] // rows
    tile_rows = zero_ref.shape[0]

    def last_tile(k):
        return (own_ref[k] == 0) | (tb_ref[k] != tb_ref[k + 1])

    def zero_copy(k):
        return pltpu.make_async_copy(zero_ref, out_ref.at[pl.ds(pl.multiple_of(k * tile_rows, tile_rows), tile_rows)],
                                     sem_z)

    @pl.when(i == 0)
    def _():
        zero_ref[...] = jnp.zeros_like(zero_ref)

        def start(k, c):
            @pl.when(last_tile(k))
            def _():
                zero_copy(k).start()
            return c

        def wait(k, c):
            @pl.when(last_tile(k))
            def _():
                zero_copy(k).wait()
            return c

        lax.fori_loop(0, n_tiles_max, start, 0)
        lax.fori_loop(0, n_tiles_max, wait, 0)

    base = i * rows

    def row_copy(r):
        return pltpu.make_async_copy(_slab(src_ref, r, n), _slab(out_ref, dest_ref[base + r], n), sem_r)

    def issue(r8, c):
        for j in range(ROW_DMA_UNROLL):
            row_copy(r8 * ROW_DMA_UNROLL + j).start(priority=j % 2)
        return c

    def drain(r8, c):
        for j in range(ROW_DMA_UNROLL):
            row_copy(r8 * ROW_DMA_UNROLL + j).wait()
        return c

    lax.fori_loop(0, rows // ROW_DMA_UNROLL, issue, 0)
    lax.fori_loop(0, rows // ROW_DMA_UNROLL, drain, 0)


def _dispatch(dest, tb, own, hrow, n_tiles_max, rows):
    t = dest.shape[0]
    n = hrow.shape[0] // t
    grid_spec = pltpu.PrefetchScalarGridSpec(
        num_scalar_prefetch=3, grid=(t // rows,),
        in_specs=[pl.BlockSpec((rows * n, LANES), lambda i, dest, tb, own: (i, 0))],
        out_specs=pl.BlockSpec(memory_space=pl.ANY),
        scratch_shapes=[pltpu.VMEM((MOE_TM * n, LANES), hrow.dtype), pltpu.SemaphoreType.DMA(()),
                        pltpu.SemaphoreType.DMA(())])
    return pl.pallas_call(
        functools.partial(_dispatch_kernel, rows=rows, n_tiles_max=n_tiles_max),
        grid_spec=grid_spec,
        out_shape=jax.ShapeDtypeStruct((n_tiles_max * MOE_TM * n, LANES), hrow.dtype),
        compiler_params=_cparams(1),
        name="dispatch",
    )(dest, tb, own, hrow)


PAIR_BITS = 2
MOE_PAIR = 1 << PAIR_BITS


def _moe_kernel(tb_ref, nt_ref, hs_ref, wr_ref, br_ref, wg_ref, wu_ref, wd_ref, y_ref):
    j = pl.program_id(0)
    rows = hs_ref.shape[0] // MOE_PAIR

    @pl.when(j * MOE_PAIR < nt_ref[0])
    def _():
        e_g = EXPERTS_PER_GROUP
        tiles = range(MOE_PAIR)
        bucket = [tb_ref[j * MOE_PAIR + k] for k in tiles]
        group = [b >> (2 * EXPERT_BITS) for b in bucket]
        experts = [((b >> EXPERT_BITS) & (e_g - 1), b & (e_g - 1)) for b in bucket]
        h = [_load_slabs(hs_ref.at[pl.ds(k * rows, rows)], MOE_TM).astype(bf16) for k in tiles]
        gate_up = [[(jnp.dot(h[k], wg_ref[e], preferred_element_type=f32),
                     jnp.dot(h[k], wu_ref[e], preferred_element_type=f32)) for e in experts[k]] for k in tiles]
        weights = []
        for k in tiles:
            logits = jnp.dot(h[k], wr_ref[...], preferred_element_type=f32) + br_ref[...]
            lane = lax.broadcasted_iota(jnp.int32, logits.shape, 1)

            def pick(col):
                return jnp.sum(jnp.where(lane == col, logits, 0.0), axis=1, keepdims=True)

            gl = jnp.where(lane < N_GROUPS, logits, -jnp.inf)
            g_max = jnp.max(gl, axis=1, keepdims=True)
            g_gate = jnp.exp(pick(group[k]) - g_max) / jnp.sum(jnp.exp(gl - g_max), axis=1, keepdims=True)
            sel = [pick(e_g * (group[k] + 1) + e) for e in experts[k]]
            top = jnp.maximum(sel[0], sel[1])
            p = [jnp.exp(s - top) for s in sel]
            scale = g_gate / (p[0] + p[1])
            weights.append([scale * p[0], scale * p[1]])
        ys = []
        for k in tiles:
            y = None
            for m, e in enumerate(experts[k]):
                a, u = gate_up[k][m]
                yk = weights[k][m] * jnp.dot((_silu(a) * u).astype(bf16), wd_ref[e], preferred_element_type=f32)
                y = yk if y is None else y + yk
            ys.append(y)
        for k in tiles:
            _store_slabs(y_ref.at[pl.ds(k * rows, rows)], ys[k])

    @pl.when(j * MOE_PAIR >= nt_ref[0])
    def _():
        y_ref[...] = jnp.zeros_like(y_ref)


def _moe(tb, nt, hs, wr, br, wg, wu, wd):
    e_g = EXPERTS_PER_GROUP
    d = wg.shape[2]
    n = d // LANES
    step_rows = MOE_PAIR * MOE_TM * n
    n_steps = hs.shape[0] // step_rows
    last = lambda j, tb, nt: jnp.minimum(j, lax.shift_right_logical(nt[0] - 1, PAIR_BITS))
    group = lambda j, tb, nt: (tb[last(j, tb, nt) * MOE_PAIR] >> (2 * EXPERT_BITS), 0, 0, 0)
    const = lambda j, tb, nt: (0, 0)
    grid_spec = pltpu.PrefetchScalarGridSpec(
        num_scalar_prefetch=2, grid=(n_steps,),
        in_specs=[pl.BlockSpec((step_rows, LANES), lambda j, tb, nt: (last(j, tb, nt), 0)),
                  pl.BlockSpec(wr.shape, const),
                  pl.BlockSpec(br.shape, const),
                  pl.BlockSpec((None, e_g, d, D_EXPERT), group),
                  pl.BlockSpec((None, e_g, d, D_EXPERT), group),
                  pl.BlockSpec((None, e_g, D_EXPERT, d), group)],
        out_specs=pl.BlockSpec((step_rows, LANES), lambda j, tb, nt: (j, 0)))
    return pl.pallas_call(
        _moe_kernel,
        grid_spec=grid_spec,
        out_shape=jax.ShapeDtypeStruct(hs.shape, f32),
        compiler_params=_cparams(1),
        name="moe",
    )(tb, nt, hs, wr, br, wg, wu, wd)


def _final_kernel(dest_ref, x1_ref, mod_ref, fg_ref, ys_ref, o_ref, ybuf_ref, sem):
    i = pl.program_id(0)
    tm = x1_ref.shape[0]
    n = ybuf_ref.shape[1] // tm

    def row_copy(tile, slot, r):
        return pltpu.make_async_copy(_slab(ys_ref, dest_ref[tile * tm + r], n), _slab(ybuf_ref.at[slot], r, n),
                                     sem.at[slot])

    def gather(tile, slot):
        def issue(r8, c):
            for j in range(ROW_DMA_UNROLL):
                row_copy(tile, slot, r8 * ROW_DMA_UNROLL + j).start(priority=j % 2)
            return c
        lax.fori_loop(0, tm // ROW_DMA_UNROLL, issue, 0)

    @pl.when(i == 0)
    def _():
        gather(0, 0)

    @pl.when(i + 1 < pl.num_programs(0))
    def _():
        gather(i + 1, (i + 1) % 2)

    slot = i % 2

    def drain(r8, c):
        for j in range(ROW_DMA_UNROLL):
            row_copy(i, slot, r8 * ROW_DMA_UNROLL + j).wait()
        return c

    lax.fori_loop(0, tm // ROW_DMA_UNROLL, drain, 0)

    x2 = x1_ref[...] + mod_ref[5:6, :] * _load_slabs(ybuf_ref.at[slot], tm)
    o_ref[...] = _rmsnorm_rows(x2) * fg_ref[...]


def _final(dest, x1, mod3, fg, ys, seq, tm):
    t, d = x1.shape
    per_b = seq // tm
    grid_spec = pltpu.PrefetchScalarGridSpec(
        num_scalar_prefetch=1, grid=(t // tm,),
        in_specs=[pl.BlockSpec((tm, d), lambda i, dest: (i, 0)),
                  pl.BlockSpec((None, 6, d), lambda i, dest: (i // per_b, 0, 0)),
                  pl.BlockSpec((1, d), lambda i, dest: (0, 0)),
                  pl.BlockSpec(memory_space=pl.ANY)],
        out_specs=pl.BlockSpec((tm, d), lambda i, dest: (i, 0)),
        scratch_shapes=[pltpu.VMEM((2, tm * d // LANES, LANES), ys.dtype), pltpu.SemaphoreType.DMA((2,))])
    return pl.pallas_call(
        _final_kernel,
        grid_spec=grid_spec,
        out_shape=jax.ShapeDtypeStruct((t, d), f32),
        compiler_params=_cparams(1),
        name="final",
    )(dest, x1, mod3, fg, ys)


def _prep_w_in(w_in_l):
    sizes = (GLA_QK, GLA_QK, GLA_W, GLA_W, GLA_LOWRANK, SWA_W, SWA_KV, SWA_KV)
    off = [int(o) for o in np.concatenate([[0], np.cumsum(sizes)])]
    pad = jnp.zeros((w_in_l.shape[0], IN_END - off[8]), w_in_l.dtype)
    return jnp.concatenate([w_in_l[:, off[0]:off[4]], w_in_l[:, off[5]:off[8]], w_in_l[:, off[4]:off[5]], pad],
                           axis=1).astype(bf16)


def _prep_w_out(w_out_l):
    return w_out_l.astype(bf16)


def _prep_router(w_grp_l, b_grp_l, w_exp_l, b_exp_l):
    d = w_grp_l.shape[0]
    n_e = N_GROUPS * EXPERTS_PER_GROUP
    wr = jnp.zeros((ROUTER_ROWS, d), f32)
    wr = wr.at[0:N_GROUPS, :].set(w_grp_l.T).at[EXPERTS_PER_GROUP:EXPERTS_PER_GROUP + n_e, :].set(w_exp_l.T)
    br = jnp.zeros((ROUTER_ROWS, 1), f32)
    br = br.at[0:N_GROUPS, 0].set(b_grp_l).at[EXPERTS_PER_GROUP:EXPERTS_PER_GROUP + n_e, 0].set(b_exp_l)
    wr_hi = wr.astype(bf16)
    wr_lo = (wr - wr_hi.astype(f32)).astype(bf16)
    return wr_hi, wr_lo, br


def _layer(x2, mod3, tab, e_mat, bsz, seq, norm_mix_g, w_in, gla_w_a2, gla_b_a2, gla_norm_g, swa_sinks, w_out,
           norm_ffn_g, w_grp, b_grp, w_exp, b_exp, w_gate, w_up, w_down, out_g, tm, tc, tm_moe):
    d = x2.shape[1]
    wa = jnp.pad(gla_w_a2, ((0, LANES - GLA_LOWRANK), (0, 0)))
    wa_hi = wa.astype(bf16)
    wa_lo = (wa - wa_hi.astype(f32)).astype(bf16)
    qk, vg, la, sq, skv = _inproj(x2, mod3, norm_mix_g.reshape(1, d), _prep_w_in(w_in), tab, e_mat,
                                  wa_hi, wa_lo, gla_b_a2.reshape(1, GLA_QK), seq, tm)
    o_gla, o_swa, wg_b, wu_b, wd_b = _mixers(swa_sinks.reshape(1, SWA_HEADS), qk, vg, la,
                                             gla_norm_g.reshape(1, GLA_W), sq, skv, w_gate, w_up, w_down, bsz, seq, tc)
    wr_hi, wr_lo, br = _prep_router(w_grp, b_grp, w_exp, b_exp)
    x1, hrow, route, cnt = _outproj(o_gla, o_swa, x2, mod3, _prep_w_out(w_out), norm_ffn_g.reshape(1, d),
                                    wr_hi, wr_lo, br, seq, tm)
    t = x2.shape[0]
    n_tiles_max = -(-(t // MOE_TM + N_PAIRS + (N_GROUPS - 1) * (MOE_PAIR - 1)) // MOE_PAIR) * MOE_PAIR
    assert n_tiles_max < 2 * LANES and t % MOE_TM == 0
    dest, tile_bucket, tile_owned, n_tiles = _plan(cnt, route, min(t, 2048))
    dest = dest.reshape(t)
    tile_bucket = tile_bucket.reshape(2 * LANES)
    n_tiles = n_tiles.reshape(LANES)[0:1]
    hs = _dispatch(dest, tile_bucket, tile_owned.reshape(2 * LANES), hrow, n_tiles_max, min(t, 1024))
    ys = _moe(tile_bucket, n_tiles, hs, wr_hi.T, br.reshape(1, ROUTER_ROWS), wg_b, wu_b, wd_b)
    return _final(dest, x1, mod3, out_g.reshape(1, d), ys, seq, tm_moe)


def kernel(x, c, positions, ada_w, ada_b, norm_mix_g, w_in, gla_w_a2, gla_b_a2, gla_norm_g, swa_sinks, w_out,
           norm_ffn_g, w_grp, b_grp, w_exp, b_exp, w_gate, w_up, w_down, final_norm_g):
    bsz, seq, d = x.shape
    depth = ada_w.shape[0]
    assert depth == 1, "the final rmsnorm is fused into the last layer's combine kernel"
    tab = _rope_tab(positions)
    e_mat = _rope_expand_matrix()
    x2 = x.reshape(bsz * seq, d)
    for l in range(depth):
        mod3 = _adaln(c, ada_w[l], ada_b[l]).reshape(bsz, 6, d)
        x2 = _layer(x2, mod3, tab, e_mat, bsz, seq, norm_mix_g[l], w_in[l], gla_w_a2[l], gla_b_a2[l],
                    gla_norm_g[l], swa_sinks[l], w_out[l], norm_ffn_g[l], w_grp[l], b_grp[l], w_exp[l], b_exp[l],
                    w_gate[l], w_up[l], w_down[l], final_norm_g, tm=512, tc=512, tm_moe=512)
    return x2.reshape(bsz, seq, d)
```

```python
import functools

import numpy as np
import jax
import jax.numpy as jnp
from jax import lax
from jax.experimental import pallas as pl
from jax.experimental.pallas import tpu as pltpu

f32 = jnp.float32
bf16 = jnp.bfloat16

GLA_HEADS = 4
GLA_DK = 64
GLA_DV = 128
GLA_LOWRANK = 16
GLA_GATE_NORM = 16.0
GLA_CHUNK = 64
SWA_HEADS = 8
SWA_KV_HEADS = 2
SWA_HD = 64
WINDOW = 128
ROPE_THETA = 500000.0
ROPE_DIMS = SWA_HD // 4
N_GROUPS = 4
EXPERTS_PER_GROUP = 8
D_EXPERT = 256
EPS = 1e-6

LANES = 128
BF16_SUBLANES = 16
VMEM_LIMIT = 52 * 1024 * 1024

GLA_QK = GLA_HEADS * GLA_DK
GLA_W = GLA_HEADS * GLA_DV
SWA_W = SWA_HEADS * SWA_HD
SWA_KV = SWA_KV_HEADS * SWA_HD
ROUTE_ROWS = 8
ROUTER_ROWS = 128
EXPERT_BITS = 3
N_BUCKETS = N_GROUPS << (2 * EXPERT_BITS)
N_PAIRS = N_GROUPS * EXPERTS_PER_GROUP * (EXPERTS_PER_GROUP - 1) // 2
MOE_TM = 128
ROW_DMA_UNROLL = 8

TN = (((0,), (0,)), ((), ()))
NT = (((1,), (1,)), ((), ()))


def _cparams(n_axes):
    return pltpu.CompilerParams(dimension_semantics=("arbitrary",) * n_axes, vmem_limit_bytes=VMEM_LIMIT)


def _split_bf16(v):
    hi = v.astype(bf16)
    lo = (v - hi.astype(f32)).astype(bf16)
    return hi, lo


def _split_stack_bf16(v):
    hi = v.astype(bf16).astype(f32)
    return jnp.concatenate([hi, v - hi], axis=0).astype(bf16)


def _rmsnorm_rows(v):
    return v * lax.rsqrt(jnp.mean(v * v, axis=-1, keepdims=True) + EPS)


def _silu(v):
    return v * jax.nn.sigmoid(v)


def _adaln_kernel(c_ref, w_ref, b_ref, o_ref):
    c_hi, c_lo = _split_bf16(_silu(c_ref[...]))
    w_hi, w_lo = _split_bf16(w_ref[...])
    o_ref[...] = (jnp.dot(c_hi, w_hi, preferred_element_type=f32) + jnp.dot(c_lo, w_hi, preferred_element_type=f32)
                  + jnp.dot(c_hi, w_lo, preferred_element_type=f32)) + b_ref[...]


def _adaln(c, w, b):
    bsz, d = c.shape
    n = w.shape[1]
    return pl.pallas_call(
        _adaln_kernel,
        grid=(n // d,),
        in_specs=[pl.BlockSpec((bsz, d), lambda j: (0, 0)),
                  pl.BlockSpec((d, d), lambda j: (0, j)),
                  pl.BlockSpec((1, d), lambda j: (0, j))],
        out_specs=pl.BlockSpec((bsz, d), lambda j: (0, j)),
        out_shape=jax.ShapeDtypeStruct((bsz, n), f32),
        compiler_params=_cparams(1),
        name="adaln",
    )(c, w, b.reshape(1, n))


def _rope_tab_kernel(pos_ref, invf_ref, o_ref):
    half = ROPE_DIMS // 2
    for b in range(pos_ref.shape[0]):
        ang = pos_ref[b:b + 1, :].astype(f32) * invf_ref[...]
        o_ref[b, 0:half, :] = jnp.cos(ang)
        o_ref[b, half:2 * half, :] = jnp.sin(ang)


def _rope_tab(positions):
    bsz, s = positions.shape
    half = ROPE_DIMS // 2
    inv_freq = (np.float32(ROPE_THETA) ** (-np.arange(0, ROPE_DIMS, 2, dtype=np.float32) / np.float32(ROPE_DIMS)))
    invf = jnp.asarray(inv_freq.astype(np.float32).reshape(half, 1))
    return pl.pallas_call(
        _rope_tab_kernel,
        out_shape=jax.ShapeDtypeStruct((bsz, 2 * half, s), f32),
        name="rope_tab",
    )(positions, invf)


def _rope_expand_matrix():
    half = ROPE_DIMS // 2
    e = np.zeros((2 * half, 3 * LANES), np.float32)
    for j in range(LANES):
        jj = j % SWA_HD
        if jj < half:
            e[jj, j] = 1.0
            e[half + jj, LANES + j] = -1.0
        elif jj < 2 * half:
            e[jj - half, j] = 1.0
            e[half + jj - half, 2 * LANES + j] = 1.0
    return jnp.asarray(np.concatenate([e, e], axis=0), dtype=bf16)


IN_QK0, IN_VG0, IN_SQ0, IN_SKV0, IN_A0, IN_END = 0, 512, 1536, 2048, 2304, 2432
INPROJ_SUBS = 2
OUTPROJ_SUBS = 4


def _inproj_kernel(x_ref, mod_ref, g_ref, w_ref, tab_ref, e_ref, wa_hi_ref, wa_lo_ref, ba_ref,
                   qk_ref, vg_ref, la_ref, sq_ref, skv_ref):
    tm = x_ref.shape[0]
    sub = tm // INPROJ_SUBS
    groups = [slice(s * sub, (s + 1) * sub) for s in range(INPROJ_SUBS)]
    hb = []
    for rows in groups:
        h = (_rmsnorm_rows(x_ref[rows, :]) * g_ref[...]) * (1.0 + mod_ref[1:2, :]) + mod_ref[0:1, :]
        hb.append(h.astype(bf16))

    def proj(s, lo, hi):
        return jnp.dot(hb[s], w_ref[:, lo:hi], preferred_element_type=f32)

    a_low = [_split_bf16(proj(s, IN_A0, IN_END)) for s in range(INPROJ_SUBS)]

    for s, rows in enumerate(groups):
        qk = proj(s, IN_QK0, IN_VG0)
        lane = lax.broadcasted_iota(jnp.int32, qk.shape, 1)
        qk_ref[rows, :] = jnp.where(lane < GLA_QK, qk * (GLA_DK ** -0.5), qk).astype(bf16)

    log_a = []
    for al_hi, al_lo in a_low:
        z = (jnp.dot(al_hi, wa_hi_ref[...], preferred_element_type=f32)
             + jnp.dot(al_lo, wa_hi_ref[...], preferred_element_type=f32)
             + jnp.dot(al_hi, wa_lo_ref[...], preferred_element_type=f32)) + ba_ref[...]
        log_a.append(_split_bf16((jnp.minimum(z, 0.0) - jnp.log1p(jnp.exp(-jnp.abs(z)))) * (1.0 / GLA_GATE_NORM)))

    for s, rows in enumerate(groups):
        vg_ref[rows, :] = proj(s, IN_VG0, IN_SQ0).astype(bf16)

    c_r = lax.broadcasted_iota(jnp.int32, (sub, sub), 0)
    c_c = lax.broadcasted_iota(jnp.int32, (sub, sub), 1)
    tri = jnp.where((c_r >= c_c) & ((c_r ^ c_c) < GLA_CHUNK), 1.0, 0.0).astype(bf16)
    for rows, (la_hi, la_lo) in zip(groups, log_a):
        la_ref[rows, :] = (jnp.dot(tri, la_hi, preferred_element_type=f32)
                           + jnp.dot(tri, la_lo, preferred_element_type=f32))

    lane1 = lax.broadcasted_iota(jnp.int32, (1, LANES), 1)
    tables = []
    for rows in groups:
        tabs = lax.dot_general(_split_stack_bf16(tab_ref[:, rows]), e_ref[...], TN, preferred_element_type=f32)
        tables.append((tabs[:, 0:LANES] + jnp.where((lane1 & (SWA_HD - 1)) < ROPE_DIMS, 0.0, 1.0),
                       tabs[:, LANES:2 * LANES], tabs[:, 2 * LANES:3 * LANES]))

    def rope(s, v):
        cos_t, sa_t, sb_t = tables[s]
        return (v * cos_t + pltpu.roll(v, LANES - ROPE_DIMS // 2, 1) * sa_t
                + pltpu.roll(v, ROPE_DIMS // 2, 1) * sb_t)

    for s, rows in enumerate(groups):
        sq = proj(s, IN_SQ0, IN_SKV0)
        for p in range(SWA_W // LANES):
            cols = slice(p * LANES, (p + 1) * LANES)
            sq_ref[rows, cols] = (rope(s, sq[:, cols]) * (SWA_HD ** -0.5)).astype(bf16)
    for s, rows in enumerate(groups):
        skv = proj(s, IN_SKV0, IN_A0)
        first = lax.broadcasted_iota(jnp.int32, (sub, LANES), 1) < SWA_HD
        for part, val in enumerate((rope(s, skv[:, 0:LANES]), skv[:, LANES:2 * LANES])):
            swapped = pltpu.roll(val, SWA_HD, 1)
            skv_ref[rows, (2 * part) * LANES:(2 * part + 1) * LANES] = jnp.where(first, val, swapped).astype(bf16)
            skv_ref[rows, (2 * part + 1) * LANES:(2 * part + 2) * LANES] = jnp.where(first, swapped, val).astype(bf16)


def _inproj(x2, mod3, g, w, tab, e_mat, wa_hi, wa_lo, ba, seq, tm):
    t, d = x2.shape
    per_b = seq // tm
    row = lambda i: (i, 0)
    const = lambda i: (0, 0)
    return pl.pallas_call(
        _inproj_kernel,
        grid=(t // tm,),
        in_specs=[pl.BlockSpec((tm, d), row),
                  pl.BlockSpec((None, 6, d), lambda i: (i // per_b, 0, 0)),
                  pl.BlockSpec((1, d), const),
                  pl.BlockSpec(w.shape, const),
                  pl.BlockSpec((None, ROPE_DIMS, tm), lambda i: (i // per_b, 0, i % per_b)),
                  pl.BlockSpec(e_mat.shape, const),
                  pl.BlockSpec(wa_hi.shape, const),
                  pl.BlockSpec(wa_lo.shape, const),
                  pl.BlockSpec(ba.shape, const)],
        out_specs=[pl.BlockSpec((tm, 2 * GLA_QK), row),
                   pl.BlockSpec((tm, 2 * GLA_W), row),
                   pl.BlockSpec((tm, GLA_QK), row),
                   pl.BlockSpec((tm, SWA_W), row),
                   pl.BlockSpec((tm, 4 * SWA_KV), row)],
        out_shape=[jax.ShapeDtypeStruct((t, 2 * GLA_QK), bf16),
                   jax.ShapeDtypeStruct((t, 2 * GLA_W), bf16),
                   jax.ShapeDtypeStruct((t, GLA_QK), f32),
                   jax.ShapeDtypeStruct((t, SWA_W), bf16),
                   jax.ShapeDtypeStruct((t, 4 * SWA_KV), bf16)],
        compiler_params=_cparams(1),
        name="inproj",
    )(x2, mod3, g, w, tab, e_mat, wa_hi, wa_lo, ba)


def _gla_chunk(c, qk_ref, vg_ref, la_ref, gn_ref, o_ref, st_ref, consts):
    causal, ones, first = consts
    c_len = GLA_CHUNK
    rows = slice(c * c_len, (c + 1) * c_len)
    b = la_ref[rows, :]
    b_last = b[c_len - 1:c_len, :]
    p1 = b_last.astype(bf16).astype(f32)
    p2 = (b_last - p1).astype(bf16).astype(f32)
    r_i = lax.broadcasted_iota(jnp.int32, (ones.shape[0], GLA_QK), 0)
    stack = jnp.where(r_i == 0, p1, jnp.where(r_i == 1, p2, jnp.where(r_i == 2, b_last - p1 - p2, 0.0)))
    b_tot_t = lax.dot_general(stack.astype(bf16), ones, TN, preferred_element_type=f32)
    q = qk_ref[rows, 0:GLA_QK].astype(f32)
    k = qk_ref[rows, GLA_QK:2 * GLA_QK].astype(f32)
    q_dec = q * jnp.exp(b)
    k_dec = (k * jnp.exp(-b)).astype(bf16)
    k_rem = (k * jnp.exp(b_last - b)).astype(bf16)
    decay = jnp.exp(b_tot_t)
    for p in range(GLA_HEADS // 2):
        ls = slice(p * LANES, (p + 1) * LANES)
        s_prev = st_ref[ls, :]
        s_prev_b = s_prev.astype(bf16)
        qd = q_dec[:, ls]
        kv_halves = []
        for hh in range(2):
            h = 2 * p + hh
            vs = slice(h * GLA_DV, (h + 1) * GLA_DV)
            qm = jnp.where(first if hh == 0 else ~first, qd, 0.0).astype(bf16)
            v = vg_ref[rows, vs]
            scores = lax.dot_general(qm, k_dec[:, ls], NT, preferred_element_type=f32)
            scores = jnp.where(causal, scores, 0.0).astype(bf16)
            o = (jnp.dot(scores, v, preferred_element_type=f32)
                 + jnp.dot(qm, s_prev_b, preferred_element_type=f32))
            o = _rmsnorm_rows(o) * gn_ref[:, vs]
            gate = vg_ref[rows, GLA_W + h * GLA_DV:GLA_W + (h + 1) * GLA_DV].astype(f32)
            o_ref[rows, vs] = (o * _silu(gate)).astype(bf16)
            kv = lax.dot_general(k_rem[:, ls], v, TN, preferred_element_type=f32)
            kv_halves.append(kv[hh * GLA_DK:(hh + 1) * GLA_DK, :])
        st_ref[ls, :] = decay[ls, :] * s_prev + jnp.concatenate(kv_halves, axis=0)


def _swa_block(blk, j, sink_ref, q_ref, kvp_ref, kvc_ref, o_ref, consts):
    band, s_i, first = consts
    w = WINDOW
    rows = slice(blk * w, (blk + 1) * w)
    prev = kvp_ref if blk == 0 else kvc_ref.at[pl.ds((blk - 1) * w, w)]
    k2 = [jnp.concatenate([prev[:, g * LANES:(g + 1) * LANES], kvc_ref[rows, g * LANES:(g + 1) * LANES]], axis=0)
          for g in range(SWA_KV_HEADS)]
    v2 = [jnp.concatenate([prev[:, (SWA_KV_HEADS + g) * LANES:(SWA_KV_HEADS + g + 1) * LANES],
                           kvc_ref[rows, (SWA_KV_HEADS + g) * LANES:(SWA_KV_HEADS + g + 1) * LANES]], axis=0)
          for g in range(SWA_KV_HEADS)]
    valid = band & ((j > 0) | (s_i >= w)) if blk == 0 else band
    heads = range(SWA_HEADS)
    kv_of = [h // (SWA_HEADS // SWA_KV_HEADS) for h in heads]
    scores = []
    for h in heads:
        p, hh = divmod(h, 2)
        qp = q_ref[rows, p * LANES:(p + 1) * LANES].astype(f32)
        qm = jnp.where(first if hh == 0 else ~first, qp, 0.0).astype(bf16)
        s = lax.dot_general(qm, k2[kv_of[h]], NT, preferred_element_type=f32)
        scores.append(jnp.where(valid, s, -jnp.inf))
    probs, denoms = [], []
    for h in heads:
        sink = sink_ref[0, h]
        m = jnp.maximum(jnp.max(scores[h], axis=-1, keepdims=True), sink)
        pr = jnp.exp(scores[h] - m)
        denoms.append(jnp.sum(pr, axis=-1, keepdims=True) + jnp.exp(sink - m))
        probs.append(pr.astype(bf16))
    outs = [jnp.dot(probs[h], v2[kv_of[h]], preferred_element_type=f32) / denoms[h] for h in heads]
    for p in range(SWA_W // LANES):
        o_ref[rows, p * LANES:(p + 1) * LANES] = jnp.where(first, outs[2 * p], outs[2 * p + 1]).astype(bf16)


def _gla_kernel(qk_ref, vg_ref, la_ref, gn_ref, o_ref, st_ref):
    @pl.when(pl.program_id(1) == 0)
    def _():
        st_ref[...] = jnp.zeros_like(st_ref)

    c_len = GLA_CHUNK
    causal = lax.broadcasted_iota(jnp.int32, (c_len, c_len), 0) >= lax.broadcasted_iota(jnp.int32, (c_len, c_len), 1)
    consts = (causal, jnp.ones((BF16_SUBLANES, LANES), bf16),
              lax.broadcasted_iota(jnp.int32, (c_len, LANES), 1) < GLA_DK)
    for c in range(qk_ref.shape[0] // c_len):
        _gla_chunk(c, qk_ref, vg_ref, la_ref, gn_ref, o_ref, st_ref, consts)


def _swa_kernel(sink_ref, q_ref, kvp_ref, kvc_ref, wg_ref, wu_ref, wd_ref, o_ref, wg_out, wu_out, wd_out):
    wg_out[...] = wg_ref[...].astype(bf16)
    wu_out[...] = wu_ref[...].astype(bf16)
    wd_out[...] = wd_ref[...].astype(bf16)
    w = WINDOW
    s_i = lax.broadcasted_iota(jnp.int32, (w, 2 * w), 1)
    rel = lax.broadcasted_iota(jnp.int32, (w, 2 * w), 0) + w - s_i
    consts = ((rel >= 0) & (rel < w), s_i, lax.broadcasted_iota(jnp.int32, (w, LANES), 1) < SWA_HD)
    for blk in range(q_ref.shape[0] // w):
        _swa_block(blk, pl.program_id(1), sink_ref, q_ref, kvp_ref, kvc_ref, o_ref, consts)


def _mixers(sinks, qk, vg, la, gn, sq, skv, wg, wu, wd, bsz, seq, ts_gla, ts):
    t = qk.shape[0]
    ng = seq // ts_gla
    tile_g = lambda b, j: (b * ng + j, 0)
    o_gla = pl.pallas_call(
        _gla_kernel,
        grid=(bsz, ng),
        in_specs=[pl.BlockSpec((ts_gla, 2 * GLA_QK), tile_g),
                  pl.BlockSpec((ts_gla, 2 * GLA_W), tile_g),
                  pl.BlockSpec((ts_gla, GLA_QK), tile_g),
                  pl.BlockSpec((1, GLA_W), lambda b, j: (0, 0))],
        out_specs=pl.BlockSpec((ts_gla, GLA_W), tile_g),
        out_shape=jax.ShapeDtypeStruct((t, GLA_W), bf16),
        scratch_shapes=[pltpu.VMEM((GLA_HEADS * GLA_DK, GLA_DV), f32)],
        compiler_params=_cparams(2),
        name="gla",
    )(qk, vg, la, gn)
    ns = seq // ts
    blocks = ts // WINDOW
    tile = lambda b, j: (b * ns + j, 0)
    n_exp = wg.shape[0] * wg.shape[1]
    assert n_exp % (bsz * ns) == 0
    per = n_exp // (bsz * ns)
    flat = [w.reshape((n_exp,) + w.shape[2:]) for w in (wg, wu, wd)]
    w_spec = lambda w: pl.BlockSpec((per,) + w.shape[1:], lambda b, j: (b * ns + j, 0, 0))
    o_swa, *w_bf16 = pl.pallas_call(
        _swa_kernel,
        grid=(bsz, ns),
        in_specs=[pl.BlockSpec(memory_space=pltpu.SMEM),
                  pl.BlockSpec((ts, SWA_W), tile),
                  pl.BlockSpec((WINDOW, 4 * SWA_KV), lambda b, j: ((b * ns + j) * blocks - jnp.minimum(j, 1), 0)),
                  pl.BlockSpec((ts, 4 * SWA_KV), tile)] + [w_spec(w) for w in flat],
        out_specs=[pl.BlockSpec((ts, SWA_W), tile)] + [w_spec(w) for w in flat],
        out_shape=[jax.ShapeDtypeStruct((t, SWA_W), bf16)] + [jax.ShapeDtypeStruct(w.shape, bf16) for w in flat],
        compiler_params=_cparams(2),
        name="swa",
    )(sinks, sq, skv, skv, *flat)
    return (o_gla, o_swa) + tuple(o.reshape(w.shape) for o, w in zip(w_bf16, (wg, wu, wd)))


def _store_slabs(ref, v):
    tm, n = v.shape[0], v.shape[1] // LANES
    for c in range(n):
        ref[pl.ds(c, tm, stride=n), :] = v[:, c * LANES:(c + 1) * LANES]


def _load_slabs(ref, tm):
    n = ref.shape[0] // tm
    return jnp.concatenate([ref[pl.ds(c, tm, stride=n), :] for c in range(n)], axis=1)


def _outproj_kernel(og_ref, os_ref, x_ref, mod_ref, wo_ref, g2_ref, wr_hi_ref, wr_lo_ref, br_ref,
                    x1_ref, hrow_ref, route_ref, cnt_ref, run_ref):
    @pl.when(pl.program_id(0) == 0)
    def _():
        run_ref[...] = jnp.zeros_like(run_ref)

    tm = x_ref.shape[0]
    sub = tm // OUTPROJ_SUBS
    n = hrow_ref.shape[0] // tm
    groups = [slice(s * sub, (s + 1) * sub) for s in range(OUTPROJ_SUBS)]
    hs = []
    for rows in groups:
        mix = (jnp.dot(og_ref[rows, :], wo_ref[0:GLA_W, :], preferred_element_type=f32)
               + jnp.dot(os_ref[rows, :], wo_ref[GLA_W:GLA_W + SWA_W, :], preferred_element_type=f32))
        x1 = x_ref[rows, :] + mod_ref[2:3, :] * mix
        x1_ref[rows, :] = x1
        hs.append((_rmsnorm_rows(x1) * g2_ref[...]) * (1.0 + mod_ref[4:5, :]) + mod_ref[3:4, :])

    lts = []
    for s, h in enumerate(hs):
        _store_slabs(hrow_ref.at[pl.ds(s * sub * n, sub * n)], h)
        h_hi, h_lo = _split_bf16(h)
        lts.append((lax.dot_general(wr_hi_ref[...], h_hi, NT, preferred_element_type=f32)
                    + lax.dot_general(wr_hi_ref[...], h_lo, NT, preferred_element_type=f32)
                    + lax.dot_general(wr_lo_ref[...], h_hi, NT, preferred_element_type=f32)) + br_ref[...])

    e_g = EXPERTS_PER_GROUP
    row = lax.broadcasted_iota(jnp.int32, (e_g, sub), 0)
    neg = -jnp.inf
    routed = []
    for lt in lts:
        gl = jnp.where(row < N_GROUPS, lt[0:e_g, :], neg)
        g_max = jnp.max(gl, axis=0, keepdims=True)
        g_gate = 1.0 / jnp.sum(jnp.exp(gl - g_max), axis=0, keepdims=True)
        g_idx = jnp.min(jnp.where(gl == g_max, row, e_g), axis=0, keepdims=True)
        sel = lt[e_g * N_GROUPS:e_g * (N_GROUPS + 1), :]
        for g in range(N_GROUPS - 2, -1, -1):
            sel = jnp.where(g_idx == g, lt[e_g * (g + 1):e_g * (g + 2), :], sel)
        t1 = jnp.max(sel, axis=0, keepdims=True)
        i1 = jnp.min(jnp.where(sel == t1, row, e_g), axis=0, keepdims=True)
        sel2 = jnp.where(row == i1, neg, sel)
        t2 = jnp.max(sel2, axis=0, keepdims=True)
        i2 = jnp.min(jnp.where(sel2 == t2, row, e_g), axis=0, keepdims=True)
        ex = jnp.exp(t2 - t1)
        w1 = g_gate / (1.0 + ex)
        w2 = g_gate * ex / (1.0 + ex)
        first_lo = i1 < i2
        bucket = ((g_idx << (2 * EXPERT_BITS)) | (jnp.minimum(i1, i2) << EXPERT_BITS) | jnp.maximum(i1, i2))
        routed.append((bucket, jnp.where(first_lo, w1, w2), jnp.where(first_lo, w2, w1)))

    t_r = lax.broadcasted_iota(jnp.int32, (sub, sub), 0)
    t_c = lax.broadcasted_iota(jnp.int32, (sub, sub), 1)
    earlier = jnp.where(t_r < t_c, 1.0, 0.0).astype(bf16)
    ones = jnp.ones((sub, sub), bf16)
    run = run_ref[...]
    for rows, (bucket, w_lo, w_hi) in zip(groups, routed):
        onehot = lax.broadcasted_iota(jnp.int32, (N_BUCKETS, sub), 0) == bucket
        oh_b = jnp.where(onehot, 1.0, 0.0).astype(bf16)
        prefix = jnp.dot(oh_b, earlier, preferred_element_type=f32) + run
        rank = jnp.sum(jnp.where(onehot, prefix, 0.0), axis=0, keepdims=True)
        run = run + jnp.dot(oh_b, ones, preferred_element_type=f32)
        route_ref[:, rows] = jnp.where(row == 0, bucket.astype(f32), jnp.where(row == 1, rank, jnp.where(
            row == 2, w_lo, jnp.where(row == 3, w_hi, 0.0))))
    run_ref[...] = run
    cnt_ref[...] = run[:, 0:LANES]


def _outproj(og, osw, x2, mod3, wo, g2, wr_hi, wr_lo, br, seq, tm):
    t, d = x2.shape
    per_b = seq // tm
    row = lambda i: (i, 0)
    const = lambda i: (0, 0)
    return pl.pallas_call(
        _outproj_kernel,
        grid=(t // tm,),
        in_specs=[pl.BlockSpec((tm, GLA_W), row),
                  pl.BlockSpec((tm, SWA_W), row),
                  pl.BlockSpec((tm, d), row),
                  pl.BlockSpec((None, 6, d), lambda i: (i // per_b, 0, 0)),
                  pl.BlockSpec(wo.shape, const),
                  pl.BlockSpec((1, d), const),
                  pl.BlockSpec(wr_hi.shape, const),
                  pl.BlockSpec(wr_lo.shape, const),
                  pl.BlockSpec(br.shape, const)],
        out_specs=[pl.BlockSpec((tm, d), row),
                   pl.BlockSpec((tm * d // LANES, LANES), row),
                   pl.BlockSpec((ROUTE_ROWS, tm), lambda i: (0, i)),
                   pl.BlockSpec((N_BUCKETS, LANES), const)],
        out_shape=[jax.ShapeDtypeStruct((t, d), f32),
                   jax.ShapeDtypeStruct((t * d // LANES, LANES), f32),
                   jax.ShapeDtypeStruct((ROUTE_ROWS, t), f32),
                   jax.ShapeDtypeStruct((N_BUCKETS, LANES), f32)],
        scratch_shapes=[pltpu.VMEM((N_BUCKETS, tm // OUTPROJ_SUBS), f32)],
        compiler_params=_cparams(1),
        name="outproj",
    )(og, osw, x2, mod3, wo, g2, wr_hi, wr_lo, br)


def _plan_kernel(cnt_ref, route_ref, dest_ref, tb_ref, own_ref, nt_ref):
    nb = N_BUCKETS
    tiles = jnp.floor((cnt_ref[...] + (MOE_TM - 1)) * (1.0 / MOE_TM))
    b_r = lax.broadcasted_iota(jnp.int32, (nb, nb), 0)
    b_c = lax.broadcasted_iota(jnp.int32, (nb, nb), 1)
    before = jnp.where(b_c < b_r, 1.0, 0.0).astype(bf16)
    tiles_b = tiles.astype(bf16)
    g_r = b_r >> (2 * EXPERT_BITS)
    g_c = b_c >> (2 * EXPERT_BITS)
    g_tiles = jnp.dot(jnp.where(g_r == g_c, 1.0, 0.0).astype(bf16), tiles_b, preferred_element_type=f32)
    fill = MOE_PAIR * jnp.floor((g_tiles + (MOE_PAIR - 1)) * (1.0 / MOE_PAIR)) - g_tiles
    pads = jnp.dot(jnp.where(g_c < g_r, 1.0, 0.0).astype(bf16), fill.astype(bf16), preferred_element_type=f32)
    t_start = (jnp.dot(before, tiles_b, preferred_element_type=f32)
               + pads * (1.0 / (1 << (2 * EXPERT_BITS))))
    t_end = t_start + tiles
    tile_i = lax.broadcasted_iota(jnp.int32, (nb, 2 * LANES), 1).astype(f32)
    ended = jnp.where(jnp.concatenate([t_end, t_end], axis=1) <= tile_i, 1.0, 0.0)
    tb_ref[...] = jnp.sum(ended, axis=0, keepdims=True).astype(jnp.int32)
    started = jnp.where(jnp.concatenate([t_start, t_start], axis=1) <= tile_i, 1.0, 0.0)
    own_ref[...] = jnp.sum(started - ended, axis=0, keepdims=True).astype(jnp.int32)
    nt_ref[...] = t_end[nb - 1:nb, :].astype(jnp.int32)
    tl = route_ref.shape[1]
    onehot = lax.broadcasted_iota(jnp.int32, (nb, tl), 0) == route_ref[0:1, :].astype(jnp.int32)
    start = lax.dot_general(t_start.astype(bf16), jnp.where(onehot, 1.0, 0.0).astype(bf16), TN,
                            preferred_element_type=f32)
    dest_ref[...] = (start[0:1, :] * MOE_TM + route_ref[1:2, :]).astype(jnp.int32)


def _plan(cnt, route, tl):
    t = route.shape[1]
    return pl.pallas_call(
        _plan_kernel,
        grid=(t // tl,),
        in_specs=[pl.BlockSpec(cnt.shape, lambda i: (0, 0)),
                  pl.BlockSpec((ROUTE_ROWS, tl), lambda i: (0, i))],
        out_specs=[pl.BlockSpec((1, tl), lambda i: (0, i)),
                   pl.BlockSpec((1, 2 * LANES), lambda i: (0, 0)),
                   pl.BlockSpec((1, 2 * LANES), lambda i: (0, 0)),
                   pl.BlockSpec((1, LANES), lambda i: (0, 0))],
        out_shape=[jax.ShapeDtypeStruct((1, t), jnp.int32),
                   jax.ShapeDtypeStruct((1, 2 * LANES), jnp.int32),
                   jax.ShapeDtypeStruct((1, 2 * LANES), jnp.int32),
                   jax.ShapeDtypeStruct((1, LANES), jnp.int32)],
        compiler_params=_cparams(1),
        name="plan",
    )(cnt, route)


def _slab(ref, token, n):
    return ref.at[pl.ds(pl.multiple_of(token * n, n), n)]


def _dispatch_kernel(dest_ref, tb_ref, own_ref, src_ref, out_ref, zero_ref, sem_z, sem_r, *, rows, n_tiles_max):
    i = pl.program_id(0)
    n = src_ref.shape[0] // rows
    tile_rows = zero_ref.shape[0]

    def last_tile(k):
        return (own_ref[k] == 0) | (tb_ref[k] != tb_ref[k + 1])

    def zero_copy(k):
        return pltpu.make_async_copy(zero_ref, out_ref.at[pl.ds(pl.multiple_of(k * tile_rows, tile_rows), tile_rows)],
                                     sem_z)

    @pl.when(i == 0)
    def _():
        zero_ref[...] = jnp.zeros_like(zero_ref)

        def start(k, c):
            @pl.when(last_tile(k))
            def _():
                zero_copy(k).start()
            return c

        def wait(k, c):
            @pl.when(last_tile(k))
            def _():
                zero_copy(k).wait()
            return c

        lax.fori_loop(0, n_tiles_max, start, 0)
        lax.fori_loop(0, n_tiles_max, wait, 0)

    base = i * rows

    def row_copy(r):
        return pltpu.make_async_copy(_slab(src_ref, r, n), _slab(out_ref, dest_ref[base + r], n), sem_r)

    def issue(r8, c):
        for j in range(ROW_DMA_UNROLL):
            row_copy(r8 * ROW_DMA_UNROLL + j).start(priority=j % 2)
        return c

    def drain(r8, c):
        for j in range(ROW_DMA_UNROLL):
            row_copy(r8 * ROW_DMA_UNROLL + j).wait()
        return c

    lax.fori_loop(0, rows // ROW_DMA_UNROLL, issue, 0)
    lax.fori_loop(0, rows // ROW_DMA_UNROLL, drain, 0)


def _dispatch(dest, tb, own, hrow, n_tiles_max, rows):
    t = dest.shape[0]
    n = hrow.shape[0] // t
    grid_spec = pltpu.PrefetchScalarGridSpec(
        num_scalar_prefetch=3, grid=(t // rows,),
        in_specs=[pl.BlockSpec((rows * n, LANES), lambda i, dest, tb, own: (i, 0))],
        out_specs=pl.BlockSpec(memory_space=pl.ANY),
        scratch_shapes=[pltpu.VMEM((MOE_TM * n, LANES), hrow.dtype), pltpu.SemaphoreType.DMA(()),
                        pltpu.SemaphoreType.DMA(())])
    return pl.pallas_call(
        functools.partial(_dispatch_kernel, rows=rows, n_tiles_max=n_tiles_max),
        grid_spec=grid_spec,
        out_shape=jax.ShapeDtypeStruct((n_tiles_max * MOE_TM * n, LANES), hrow.dtype),
        compiler_params=_cparams(1),
        name="dispatch",
    )(dest, tb, own, hrow)


PAIR_BITS = 2
MOE_PAIR = 1 << PAIR_BITS


def _moe_kernel(tb_ref, nt_ref, hs_ref, wr_ref, br_ref, wg_ref, wu_ref, wd_ref, y_ref):
    j = pl.program_id(0)
    rows = hs_ref.shape[0] // MOE_PAIR

    @pl.when(j * MOE_PAIR < nt_ref[0])
    def _():
        e_g = EXPERTS_PER_GROUP
        tiles = range(MOE_PAIR)
        bucket = [tb_ref[j * MOE_PAIR + k] for k in tiles]
        group = [b >> (2 * EXPERT_BITS) for b in bucket]
        experts = [((b >> EXPERT_BITS) & (e_g - 1), b & (e_g - 1)) for b in bucket]
        h = [_load_slabs(hs_ref.at[pl.ds(k * rows, rows)], MOE_TM).astype(bf16) for k in tiles]
        gate_up = [[(jnp.dot(h[k], wg_ref[e], preferred_element_type=f32),
                     jnp.dot(h[k], wu_ref[e], preferred_element_type=f32)) for e in experts[k]] for k in tiles]
        weights = []
        for k in tiles:
            logits = jnp.dot(h[k], wr_ref[...], preferred_element_type=f32) + br_ref[...]
            lane = lax.broadcasted_iota(jnp.int32, logits.shape, 1)

            def pick(col):
                return jnp.sum(jnp.where(lane == col, logits, 0.0), axis=1, keepdims=True)

            gl = jnp.where(lane < N_GROUPS, logits, -jnp.inf)
            g_max = jnp.max(gl, axis=1, keepdims=True)
            g_gate = jnp.exp(pick(group[k]) - g_max) / jnp.sum(jnp.exp(gl - g_max), axis=1, keepdims=True)
            sel = [pick(e_g * (group[k] + 1) + e) for e in experts[k]]
            top = jnp.maximum(sel[0], sel[1])
            p = [jnp.exp(s - top) for s in sel]
            scale = g_gate / (p[0] + p[1])
            weights.append([scale * p[0], scale * p[1]])
        ys = []
        for k in tiles:
            y = None
            for m, e in enumerate(experts[k]):
                a, u = gate_up[k][m]
                yk = weights[k][m] * jnp.dot((_silu(a) * u).astype(bf16), wd_ref[e], preferred_element_type=f32)
                y = yk if y is None else y + yk
            ys.append(y)
        for k in tiles:
            _store_slabs(y_ref.at[pl.ds(k * rows, rows)], ys[k])

    @pl.when(j * MOE_PAIR >= nt_ref[0])
    def _():
        y_ref[...] = jnp.zeros_like(y_ref)


def _moe(tb, nt, hs, wr, br, wg, wu, wd):
    e_g = EXPERTS_PER_GROUP
    d = wg.shape[2]
    n = d // LANES
    step_rows = MOE_PAIR * MOE_TM * n
    n_steps = hs.shape[0] // step_rows
    last = lambda j, tb, nt: jnp.minimum(j, lax.shift_right_logical(nt[0] - 1, PAIR_BITS))
    group = lambda j, tb, nt: (tb[last(j, tb, nt) * MOE_PAIR] >> (2 * EXPERT_BITS), 0, 0, 0)
    const = lambda j, tb, nt: (0, 0)
    grid_spec = pltpu.PrefetchScalarGridSpec(
        num_scalar_prefetch=2, grid=(n_steps,),
        in_specs=[pl.BlockSpec((step_rows, LANES), lambda j, tb, nt: (last(j, tb, nt), 0)),
                  pl.BlockSpec(wr.shape, const),
                  pl.BlockSpec(br.shape, const),
                  pl.BlockSpec((None, e_g, d, D_EXPERT), group),
                  pl.BlockSpec((None, e_g, d, D_EXPERT), group),
                  pl.BlockSpec((None, e_g, D_EXPERT, d), group)],
        out_specs=pl.BlockSpec((step_rows, LANES), lambda j, tb, nt: (j, 0)))
    return pl.pallas_call(
        _moe_kernel,
        grid_spec=grid_spec,
        out_shape=jax.ShapeDtypeStruct(hs.shape, f32),
        compiler_params=_cparams(1),
        name="moe",
    )(tb, nt, hs, wr, br, wg, wu, wd)


def _final_kernel(dest_ref, x1_ref, mod_ref, fg_ref, ys_ref, o_ref, ybuf_ref, sem):
    i = pl.program_id(0)
    tm = x1_ref.shape[0]
    n = ybuf_ref.shape[1] // tm

    def row_copy(tile, slot, r):
        return pltpu.make_async_copy(_slab(ys_ref, dest_ref[tile * tm + r], n), _slab(ybuf_ref.at[slot], r, n),
                                     sem.at[slot])

    def gather(tile, slot):
        def issue(r8, c):
            for j in range(ROW_DMA_UNROLL):
                row_copy(tile, slot, r8 * ROW_DMA_UNROLL + j).start(priority=j % 2)
            return c
        lax.fori_loop(0, tm // ROW_DMA_UNROLL, issue, 0)

    @pl.when(i == 0)
    def _():
        gather(0, 0)

    @pl.when(i + 1 < pl.num_programs(0))
    def _():
        gather(i + 1, (i + 1) % 2)

    slot = i % 2

    def drain(r8, c):
        for j in range(ROW_DMA_UNROLL):
            row_copy(i, slot, r8 * ROW_DMA_UNROLL + j).wait()
        return c

    lax.fori_loop(0, tm // ROW_DMA_UNROLL, drain, 0)

    x2 = x1_ref[...] + mod_ref[5:6, :] * _load_slabs(ybuf_ref.at[slot], tm)
    o_ref[...] = _rmsnorm_rows(x2) * fg_ref[...]


def _final(dest, x1, mod3, fg, ys, seq, tm):
    t, d = x1.shape
    per_b = seq // tm
    grid_spec = pltpu.PrefetchScalarGridSpec(
        num_scalar_prefetch=1, grid=(t // tm,),
        in_specs=[pl.BlockSpec((tm, d), lambda i, dest: (i, 0)),
                  pl.BlockSpec((None, 6, d), lambda i, dest: (i // per_b, 0, 0)),
                  pl.BlockSpec((1, d), lambda i, dest: (0, 0)),
                  pl.BlockSpec(memory_space=pl.ANY)],
        out_specs=pl.BlockSpec((tm, d), lambda i, dest: (i, 0)),
        scratch_shapes=[pltpu.VMEM((2, tm * d // LANES, LANES), ys.dtype), pltpu.SemaphoreType.DMA((2,))])
    return pl.pallas_call(
        _final_kernel,
        grid_spec=grid_spec,
        out_shape=jax.ShapeDtypeStruct((t, d), f32),
        compiler_params=_cparams(1),
        name="final",
    )(dest, x1, mod3, fg, ys)


def _prep_w_in(w_in_l):
    sizes = (GLA_QK, GLA_QK, GLA_W, GLA_W, GLA_LOWRANK, SWA_W, SWA_KV, SWA_KV)
    off = [int(o) for o in np.concatenate([[0], np.cumsum(sizes)])]
    pad = jnp.zeros((w_in_l.shape[0], IN_END - off[8]), w_in_l.dtype)
    return jnp.concatenate([w_in_l[:, off[0]:off[4]], w_in_l[:, off[5]:off[8]], w_in_l[:, off[4]:off[5]], pad],
                           axis=1).astype(bf16)


def _prep_w_out(w_out_l):
    return w_out_l.astype(bf16)


def _prep_router(w_grp_l, b_grp_l, w_exp_l, b_exp_l):
    d = w_grp_l.shape[0]
    n_e = N_GROUPS * EXPERTS_PER_GROUP
    wr = jnp.zeros((ROUTER_ROWS, d), f32)
    wr = wr.at[0:N_GROUPS, :].set(w_grp_l.T).at[EXPERTS_PER_GROUP:EXPERTS_PER_GROUP + n_e, :].set(w_exp_l.T)
    br = jnp.zeros((ROUTER_ROWS, 1), f32)
    br = br.at[0:N_GROUPS, 0].set(b_grp_l).at[EXPERTS_PER_GROUP:EXPERTS_PER_GROUP + n_e, 0].set(b_exp_l)
    wr_hi = wr.astype(bf16)
    wr_lo = (wr - wr_hi.astype(f32)).astype(bf16)
    return wr_hi, wr_lo, br


def _tiles(seq):
    return dict(tm=min(seq, 512), ts_gla=min(seq, 512), ts_swa=min(seq, 1024), tl=min(seq, 2048), rows=min(seq, 1024))


def _layer(x2, mod3, tab, e_mat, bsz, seq, norm_mix_g, w_in, gla_w_a2, gla_b_a2, gla_norm_g, swa_sinks, w_out,
           norm_ffn_g, w_grp, b_grp, w_exp, b_exp, w_gate, w_up, w_down, out_g):
    d = x2.shape[1]
    tiles = _tiles(seq)
    tm = tiles["tm"]
    wa = jnp.pad(gla_w_a2, ((0, LANES - GLA_LOWRANK), (0, 0)))
    wa_hi = wa.astype(bf16)
    wa_lo = (wa - wa_hi.astype(f32)).astype(bf16)
    qk, vg, la, sq, skv = _inproj(x2, mod3, norm_mix_g.reshape(1, d), _prep_w_in(w_in), tab, e_mat,
                                  wa_hi, wa_lo, gla_b_a2.reshape(1, GLA_QK), seq, tm)
    o_gla, o_swa, wg_b, wu_b, wd_b = _mixers(swa_sinks.reshape(1, SWA_HEADS), qk, vg, la,
                                             gla_norm_g.reshape(1, GLA_W), sq, skv, w_gate, w_up, w_down, bsz, seq,
                                             tiles["ts_gla"], tiles["ts_swa"])
    wr_hi, wr_lo, br = _prep_router(w_grp, b_grp, w_exp, b_exp)
    x1, hrow, route, cnt = _outproj(o_gla, o_swa, x2, mod3, _prep_w_out(w_out), norm_ffn_g.reshape(1, d),
                                    wr_hi, wr_lo, br, seq, tm)
    t = x2.shape[0]
    n_tiles_max = -(-(t // MOE_TM + N_PAIRS + (N_GROUPS - 1) * (MOE_PAIR - 1)) // MOE_PAIR) * MOE_PAIR
    assert n_tiles_max < 2 * LANES and t % MOE_TM == 0
    dest, tile_bucket, tile_owned, n_tiles = _plan(cnt, route, tiles["tl"])
    dest = dest.reshape(t)
    tile_bucket = tile_bucket.reshape(2 * LANES)
    n_tiles = n_tiles.reshape(LANES)[0:1]
    hs = _dispatch(dest, tile_bucket, tile_owned.reshape(2 * LANES), hrow, n_tiles_max, tiles["rows"])
    ys = _moe(tile_bucket, n_tiles, hs, wr_hi.T, br.reshape(1, ROUTER_ROWS), wg_b, wu_b, wd_b)
    return _final(dest, x1, mod3, out_g.reshape(1, d), ys, seq, tm)


def kernel(x, c, positions, ada_w, ada_b, norm_mix_g, w_in, gla_w_a2, gla_b_a2, gla_norm_g, swa_sinks, w_out,
           norm_ffn_g, w_grp, b_grp, w_exp, b_exp, w_gate, w_up, w_down, final_norm_g):
    bsz, seq, d = x.shape
    depth = ada_w.shape[0]
    assert depth == 1, "the final rmsnorm is fused into the last layer's combine kernel"
    tab = _rope_tab(positions)
    e_mat = _rope_expand_matrix()
    x2 = x.reshape(bsz * seq, d)
    for l in range(depth):
        mod3 = _adaln(c, ada_w[l], ada_b[l]).reshape(bsz, 6, d)
        x2 = _layer(x2, mod3, tab, e_mat, bsz, seq, norm_mix_g[l], w_in[l], gla_w_a2[l], gla_b_a2[l],
                    gla_norm_g[l], swa_sinks[l], w_out[l], norm_ffn_g[l], w_grp[l], b_grp[l], w_exp[l], b_exp[l],
                    w_gate[l], w_up[l], w_down[l], final_norm_g)
    return x2.reshape(bsz, seq, d)
```

```python
import functools

import numpy as np
import jax
import jax.numpy as jnp
from jax import lax
from jax.experimental import pallas as pl
from jax.experimental.pallas import tpu as pltpu

f32 = jnp.float32
bf16 = jnp.bfloat16

GLA_HEADS = 4
GLA_DK = 64
GLA_DV = 128
GLA_LOWRANK = 16
GLA_GATE_NORM = 16.0
GLA_CHUNK = 64
SWA_HEADS = 8
SWA_KV_HEADS = 2
SWA_HD = 64
WINDOW = 128
ROPE_THETA = 500000.0
ROPE_DIMS = SWA_HD // 4
N_GROUPS = 4
EXPERTS_PER_GROUP = 8
D_EXPERT = 256
EPS = 1e-6

LANES = 128
BF16_SUBLANES = 16
VMEM_LIMIT = 52 * 1024 * 1024

GLA_QK = GLA_HEADS * GLA_DK
GLA_W = GLA_HEADS * GLA_DV
SWA_W = SWA_HEADS * SWA_HD
SWA_KV = SWA_KV_HEADS * SWA_HD
ROUTE_ROWS = 8
ROUTER_ROWS = 128
EXPERT_BITS = 3
N_BUCKETS = N_GROUPS << (2 * EXPERT_BITS)
N_PAIRS = N_GROUPS * EXPERTS_PER_GROUP * (EXPERTS_PER_GROUP - 1) // 2
MOE_TM = 128
ROW_DMA_UNROLL = 8

TN = (((0,), (0,)), ((), ()))
NT = (((1,), (1,)), ((), ()))


def _cparams(n_axes):
    return pltpu.CompilerParams(dimension_semantics=("arbitrary",) * n_axes, vmem_limit_bytes=VMEM_LIMIT)


def _split_bf16(v):
    hi = v.astype(bf16)
    lo = (v - hi.astype(f32)).astype(bf16)
    return hi, lo


def _split_stack_bf16(v):
    hi = v.astype(bf16).astype(f32)
    return jnp.concatenate([hi, v - hi], axis=0).astype(bf16)


def _rmsnorm_rows(v):
    return v * lax.rsqrt(jnp.mean(v * v, axis=-1, keepdims=True) + EPS)


def _silu(v):
    return v * jax.nn.sigmoid(v)


def _adaln_kernel(c_ref, w_ref, b_ref, o_ref):
    c_hi, c_lo = _split_bf16(_silu(c_ref[...]))
    w_hi, w_lo = _split_bf16(w_ref[...])
    o_ref[...] = (jnp.dot(c_hi, w_hi, preferred_element_type=f32) + jnp.dot(c_lo, w_hi, preferred_element_type=f32)
                  + jnp.dot(c_hi, w_lo, preferred_element_type=f32)) + b_ref[...]


def _adaln(c, w, b):
    bsz, d = c.shape
    n = w.shape[1]
    return pl.pallas_call(
        _adaln_kernel,
        grid=(n // d,),
        in_specs=[pl.BlockSpec((bsz, d), lambda j: (0, 0)),
                  pl.BlockSpec((d, d), lambda j: (0, j)),
                  pl.BlockSpec((1, d), lambda j: (0, j))],
        out_specs=pl.BlockSpec((bsz, d), lambda j: (0, j)),
        out_shape=jax.ShapeDtypeStruct((bsz, n), f32),
        compiler_params=_cparams(1),
        name="adaln",
    )(c, w, b.reshape(1, n))


def _rope_tab_kernel(pos_ref, invf_ref, o_ref):
    half = ROPE_DIMS // 2
    for b in range(pos_ref.shape[0]):
        ang = pos_ref[b:b + 1, :].astype(f32) * invf_ref[...]
        o_ref[b, 0:half, :] = jnp.cos(ang)
        o_ref[b, half:2 * half, :] = jnp.sin(ang)


def _rope_tab(positions):
    bsz, s = positions.shape
    half = ROPE_DIMS // 2
    inv_freq = (np.float32(ROPE_THETA) ** (-np.arange(0, ROPE_DIMS, 2, dtype=np.float32) / np.float32(ROPE_DIMS)))
    invf = jnp.asarray(inv_freq.astype(np.float32).reshape(half, 1))
    return pl.pallas_call(
        _rope_tab_kernel,
        out_shape=jax.ShapeDtypeStruct((bsz, 2 * half, s), f32),
        name="rope_tab",
    )(positions, invf)


def _rope_expand_matrix():
    half = ROPE_DIMS // 2
    e = np.zeros((2 * half, 3 * LANES), np.float32)
    for j in range(LANES):
        jj = j % SWA_HD
        if jj < half:
            e[jj, j] = 1.0
            e[half + jj, LANES + j] = -1.0
        elif jj < 2 * half:
            e[jj - half, j] = 1.0
            e[half + jj - half, 2 * LANES + j] = 1.0
    return jnp.asarray(np.concatenate([e, e], axis=0), dtype=bf16)


IN_QK0, IN_VG0, IN_SQ0, IN_SKV0, IN_A0, IN_END = 0, 512, 1536, 2048, 2304, 2432
INPROJ_SUBS = 4
OUTPROJ_SUBS = 8


def _inproj_kernel(x_ref, mod_ref, g_ref, w_ref, tab_ref, e_ref, wa_hi_ref, wa_lo_ref, ba_ref,
                   qk_ref, vg_ref, la_ref, sq_ref, skv_ref):
    tm = x_ref.shape[0]
    sub = tm // INPROJ_SUBS
    groups = [slice(s * sub, (s + 1) * sub) for s in range(INPROJ_SUBS)]
    hb = []
    for rows in groups:
        h = (_rmsnorm_rows(x_ref[rows, :]) * g_ref[...]) * (1.0 + mod_ref[1:2, :]) + mod_ref[0:1, :]
        hb.append(h.astype(bf16))

    def proj(s, lo, hi):
        return jnp.dot(hb[s], w_ref[:, lo:hi], preferred_element_type=f32)

    a_low = [_split_bf16(proj(s, IN_A0, IN_END)) for s in range(INPROJ_SUBS)]

    for s, rows in enumerate(groups):
        qk = proj(s, IN_QK0, IN_VG0)
        lane = lax.broadcasted_iota(jnp.int32, qk.shape, 1)
        qk_ref[rows, :] = jnp.where(lane < GLA_QK, qk * (GLA_DK ** -0.5), qk).astype(bf16)

    log_a = []
    for al_hi, al_lo in a_low:
        z = (jnp.dot(al_hi, wa_hi_ref[...], preferred_element_type=f32)
             + jnp.dot(al_lo, wa_hi_ref[...], preferred_element_type=f32)
             + jnp.dot(al_hi, wa_lo_ref[...], preferred_element_type=f32)) + ba_ref[...]
        log_a.append(_split_bf16((jnp.minimum(z, 0.0) - jnp.log1p(jnp.exp(-jnp.abs(z)))) * (1.0 / GLA_GATE_NORM)))

    for s, rows in enumerate(groups):
        vg_ref[rows, :] = proj(s, IN_VG0, IN_SQ0).astype(bf16)

    c_r = lax.broadcasted_iota(jnp.int32, (sub, sub), 0)
    c_c = lax.broadcasted_iota(jnp.int32, (sub, sub), 1)
    tri = jnp.where((c_r >= c_c) & ((c_r ^ c_c) < GLA_CHUNK), 1.0, 0.0).astype(bf16)
    for rows, (la_hi, la_lo) in zip(groups, log_a):
        la_ref[rows, :] = (jnp.dot(tri, la_hi, preferred_element_type=f32)
                           + jnp.dot(tri, la_lo, preferred_element_type=f32))

    lane1 = lax.broadcasted_iota(jnp.int32, (1, LANES), 1)
    tables = []
    for rows in groups:
        tabs = lax.dot_general(_split_stack_bf16(tab_ref[:, rows]), e_ref[...], TN, preferred_element_type=f32)
        tables.append((tabs[:, 0:LANES] + jnp.where((lane1 & (SWA_HD - 1)) < ROPE_DIMS, 0.0, 1.0),
                       tabs[:, LANES:2 * LANES], tabs[:, 2 * LANES:3 * LANES]))

    def rope(s, v):
        cos_t, sa_t, sb_t = tables[s]
        return (v * cos_t + pltpu.roll(v, LANES - ROPE_DIMS // 2, 1) * sa_t
                + pltpu.roll(v, ROPE_DIMS // 2, 1) * sb_t)

    for s, rows in enumerate(groups):
        sq = proj(s, IN_SQ0, IN_SKV0)
        for p in range(SWA_W // LANES):
            cols = slice(p * LANES, (p + 1) * LANES)
            sq_ref[rows, cols] = (rope(s, sq[:, cols]) * (SWA_HD ** -0.5)).astype(bf16)
    for s, rows in enumerate(groups):
        skv = proj(s, IN_SKV0, IN_A0)
        first = lax.broadcasted_iota(jnp.int32, (sub, LANES), 1) < SWA_HD
        for part, val in enumerate((rope(s, skv[:, 0:LANES]), skv[:, LANES:2 * LANES])):
            swapped = pltpu.roll(val, SWA_HD, 1)
            skv_ref[rows, (2 * part) * LANES:(2 * part + 1) * LANES] = jnp.where(first, val, swapped).astype(bf16)
            skv_ref[rows, (2 * part + 1) * LANES:(2 * part + 2) * LANES] = jnp.where(first, swapped, val).astype(bf16)


def _inproj(x2, mod3, g, w, tab, e_mat, wa_hi, wa_lo, ba, seq, tm):
    t, d = x2.shape
    per_b = seq // tm
    row = lambda i: (i, 0)
    const = lambda i: (0, 0)
    return pl.pallas_call(
        _inproj_kernel,
        grid=(t // tm,),
        in_specs=[pl.BlockSpec((tm, d), row),
                  pl.BlockSpec((None, 6, d), lambda i: (i // per_b, 0, 0)),
                  pl.BlockSpec((1, d), const),
                  pl.BlockSpec(w.shape, const),
                  pl.BlockSpec((None, ROPE_DIMS, tm), lambda i: (i // per_b, 0, i % per_b)),
                  pl.BlockSpec(e_mat.shape, const),
                  pl.BlockSpec(wa_hi.shape, const),
                  pl.BlockSpec(wa_lo.shape, const),
                  pl.BlockSpec(ba.shape, const)],
        out_specs=[pl.BlockSpec((tm, 2 * GLA_QK), row),
                   pl.BlockSpec((tm, 2 * GLA_W), row),
                   pl.BlockSpec((tm, GLA_QK), row),
                   pl.BlockSpec((tm, SWA_W), row),
                   pl.BlockSpec((tm, 4 * SWA_KV), row)],
        out_shape=[jax.ShapeDtypeStruct((t, 2 * GLA_QK), bf16),
                   jax.ShapeDtypeStruct((t, 2 * GLA_W), bf16),
                   jax.ShapeDtypeStruct((t, GLA_QK), f32),
                   jax.ShapeDtypeStruct((t, SWA_W), bf16),
                   jax.ShapeDtypeStruct((t, 4 * SWA_KV), bf16)],
        compiler_params=_cparams(1),
        name="inproj",
    )(x2, mod3, g, w, tab, e_mat, wa_hi, wa_lo, ba)


def _gla_chunk(c, qk_ref, vg_ref, la_ref, gn_ref, o_ref, st_ref, consts):
    causal, ones, first = consts
    c_len = GLA_CHUNK
    rows = slice(c * c_len, (c + 1) * c_len)
    b = la_ref[rows, :]
    b_last = b[c_len - 1:c_len, :]
    p1 = b_last.astype(bf16).astype(f32)
    p2 = (b_last - p1).astype(bf16).astype(f32)
    r_i = lax.broadcasted_iota(jnp.int32, (ones.shape[0], GLA_QK), 0)
    stack = jnp.where(r_i == 0, p1, jnp.where(r_i == 1, p2, jnp.where(r_i == 2, b_last - p1 - p2, 0.0)))
    b_tot_t = lax.dot_general(stack.astype(bf16), ones, TN, preferred_element_type=f32)
    q = qk_ref[rows, 0:GLA_QK].astype(f32)
    k = qk_ref[rows, GLA_QK:2 * GLA_QK].astype(f32)
    q_dec = q * jnp.exp(b)
    k_dec = (k * jnp.exp(-b)).astype(bf16)
    k_rem = (k * jnp.exp(b_last - b)).astype(bf16)
    decay = jnp.exp(b_tot_t)
    for p in range(GLA_HEADS // 2):
        ls = slice(p * LANES, (p + 1) * LANES)
        s_prev = st_ref[ls, :]
        s_prev_b = s_prev.astype(bf16)
        qd = q_dec[:, ls]
        kv_halves = []
        for hh in range(2):
            h = 2 * p + hh
            vs = slice(h * GLA_DV, (h + 1) * GLA_DV)
            qm = jnp.where(first if hh == 0 else ~first, qd, 0.0).astype(bf16)
            v = vg_ref[rows, vs]
            scores = lax.dot_general(qm, k_dec[:, ls], NT, preferred_element_type=f32)
            scores = jnp.where(causal, scores, 0.0).astype(bf16)
            o = (jnp.dot(scores, v, preferred_element_type=f32)
                 + jnp.dot(qm, s_prev_b, preferred_element_type=f32))
            o = _rmsnorm_rows(o) * gn_ref[:, vs]
            gate = vg_ref[rows, GLA_W + h * GLA_DV:GLA_W + (h + 1) * GLA_DV].astype(f32)
            o_ref[rows, vs] = (o * _silu(gate)).astype(bf16)
            kv = lax.dot_general(k_rem[:, ls], v, TN, preferred_element_type=f32)
            kv_halves.append(kv[hh * GLA_DK:(hh + 1) * GLA_DK, :])
        st_ref[ls, :] = decay[ls, :] * s_prev + jnp.concatenate(kv_halves, axis=0)


def _swa_block(blk, j, sink_ref, q_ref, kvp_ref, kvc_ref, o_ref, consts):
    band, s_i, first = consts
    w = WINDOW
    rows = slice(blk * w, (blk + 1) * w)
    prev = kvp_ref if blk == 0 else kvc_ref.at[pl.ds((blk - 1) * w, w)]
    k2 = [jnp.concatenate([prev[:, g * LANES:(g + 1) * LANES], kvc_ref[rows, g * LANES:(g + 1) * LANES]], axis=0)
          for g in range(SWA_KV_HEADS)]
    v2 = [jnp.concatenate([prev[:, (SWA_KV_HEADS + g) * LANES:(SWA_KV_HEADS + g + 1) * LANES],
                           kvc_ref[rows, (SWA_KV_HEADS + g) * LANES:(SWA_KV_HEADS + g + 1) * LANES]], axis=0)
          for g in range(SWA_KV_HEADS)]
    valid = band & ((j > 0) | (s_i >= w)) if blk == 0 else band
    heads = range(SWA_HEADS)
    kv_of = [h // (SWA_HEADS // SWA_KV_HEADS) for h in heads]
    scores = []
    for h in heads:
        p, hh = divmod(h, 2)
        qp = q_ref[rows, p * LANES:(p + 1) * LANES].astype(f32)
        qm = jnp.where(first if hh == 0 else ~first, qp, 0.0).astype(bf16)
        s = lax.dot_general(qm, k2[kv_of[h]], NT, preferred_element_type=f32)
        scores.append(jnp.where(valid, s, -jnp.inf))
    probs, denoms = [], []
    for h in heads:
        sink = sink_ref[0, h]
        m = jnp.maximum(jnp.max(scores[h], axis=-1, keepdims=True), sink)
        pr = jnp.exp(scores[h] - m)
        denoms.append(jnp.sum(pr, axis=-1, keepdims=True) + jnp.exp(sink - m))
        probs.append(pr.astype(bf16))
    outs = [jnp.dot(probs[h], v2[kv_of[h]], preferred_element_type=f32) / denoms[h] for h in heads]
    for p in range(SWA_W // LANES):
        o_ref[rows, p * LANES:(p + 1) * LANES] = jnp.where(first, outs[2 * p], outs[2 * p + 1]).astype(bf16)


def _gla_kernel(qk_ref, vg_ref, la_ref, gn_ref, o_ref, st_ref):
    @pl.when(pl.program_id(1) == 0)
    def _():
        st_ref[...] = jnp.zeros_like(st_ref)

    c_len = GLA_CHUNK
    causal = lax.broadcasted_iota(jnp.int32, (c_len, c_len), 0) >= lax.broadcasted_iota(jnp.int32, (c_len, c_len), 1)
    consts = (causal, jnp.ones((BF16_SUBLANES, LANES), bf16),
              lax.broadcasted_iota(jnp.int32, (c_len, LANES), 1) < GLA_DK)
    for c in range(qk_ref.shape[0] // c_len):
        _gla_chunk(c, qk_ref, vg_ref, la_ref, gn_ref, o_ref, st_ref, consts)


def _swa_kernel(sink_ref, q_ref, kvp_ref, kvc_ref, wg_ref, wu_ref, wd_ref, o_ref, wg_out, wu_out, wd_out):
    wg_out[...] = wg_ref[...].astype(bf16)
    wu_out[...] = wu_ref[...].astype(bf16)
    wd_out[...] = wd_ref[...].astype(bf16)
    w = WINDOW
    s_i = lax.broadcasted_iota(jnp.int32, (w, 2 * w), 1)
    rel = lax.broadcasted_iota(jnp.int32, (w, 2 * w), 0) + w - s_i
    consts = ((rel >= 0) & (rel < w), s_i, lax.broadcasted_iota(jnp.int32, (w, LANES), 1) < SWA_HD)
    for blk in range(q_ref.shape[0] // w):
        _swa_block(blk, pl.program_id(1), sink_ref, q_ref, kvp_ref, kvc_ref, o_ref, consts)


def _mixers(sinks, qk, vg, la, gn, sq, skv, wg, wu, wd, bsz, seq, ts_gla, ts):
    t = qk.shape[0]
    ng = seq // ts_gla
    tile_g = lambda b, j: (b * ng + j, 0)
    o_gla = pl.pallas_call(
        _gla_kernel,
        grid=(bsz, ng),
        in_specs=[pl.BlockSpec((ts_gla, 2 * GLA_QK), tile_g),
                  pl.BlockSpec((ts_gla, 2 * GLA_W), tile_g),
                  pl.BlockSpec((ts_gla, GLA_QK), tile_g),
                  pl.BlockSpec((1, GLA_W), lambda b, j: (0, 0))],
        out_specs=pl.BlockSpec((ts_gla, GLA_W), tile_g),
        out_shape=jax.ShapeDtypeStruct((t, GLA_W), bf16),
        scratch_shapes=[pltpu.VMEM((GLA_HEADS * GLA_DK, GLA_DV), f32)],
        compiler_params=_cparams(2),
        name="gla",
    )(qk, vg, la, gn)
    ns = seq // ts
    blocks = ts // WINDOW
    tile = lambda b, j: (b * ns + j, 0)
    n_exp = wg.shape[0] * wg.shape[1]
    assert n_exp % (bsz * ns) == 0
    per = n_exp // (bsz * ns)
    flat = [w.reshape((n_exp,) + w.shape[2:]) for w in (wg, wu, wd)]
    w_spec = lambda w: pl.BlockSpec((per,) + w.shape[1:], lambda b, j: (b * ns + j, 0, 0))
    o_swa, *w_bf16 = pl.pallas_call(
        _swa_kernel,
        grid=(bsz, ns),
        in_specs=[pl.BlockSpec(memory_space=pltpu.SMEM),
                  pl.BlockSpec((ts, SWA_W), tile),
                  pl.BlockSpec((WINDOW, 4 * SWA_KV), lambda b, j: ((b * ns + j) * blocks - jnp.minimum(j, 1), 0)),
                  pl.BlockSpec((ts, 4 * SWA_KV), tile)] + [w_spec(w) for w in flat],
        out_specs=[pl.BlockSpec((ts, SWA_W), tile)] + [w_spec(w) for w in flat],
        out_shape=[jax.ShapeDtypeStruct((t, SWA_W), bf16)] + [jax.ShapeDtypeStruct(w.shape, bf16) for w in flat],
        compiler_params=_cparams(2),
        name="swa",
    )(sinks, sq, skv, skv, *flat)
    return (o_gla, o_swa) + tuple(o.reshape(w.shape) for o, w in zip(w_bf16, (wg, wu, wd)))


def _store_slabs(ref, v):
    tm, n = v.shape[0], v.shape[1] // LANES
    for c in range(n):
        ref[pl.ds(c, tm, stride=n), :] = v[:, c * LANES:(c + 1) * LANES]


def _load_slabs(ref, tm):
    n = ref.shape[0] // tm
    return jnp.concatenate([ref[pl.ds(c, tm, stride=n), :] for c in range(n)], axis=1)


def _outproj_kernel(og_ref, os_ref, x_ref, mod_ref, wo_ref, g2_ref, wr_hi_ref, wr_lo_ref, br_ref,
                    x1_ref, hrow_ref, route_ref, cnt_ref, run_ref):
    @pl.when(pl.program_id(0) == 0)
    def _():
        run_ref[...] = jnp.zeros_like(run_ref)

    tm = x_ref.shape[0]
    sub = tm // OUTPROJ_SUBS
    n = hrow_ref.shape[0] // tm
    groups = [slice(s * sub, (s + 1) * sub) for s in range(OUTPROJ_SUBS)]
    hs = []
    for rows in groups:
        mix = (jnp.dot(og_ref[rows, :], wo_ref[0:GLA_W, :], preferred_element_type=f32)
               + jnp.dot(os_ref[rows, :], wo_ref[GLA_W:GLA_W + SWA_W, :], preferred_element_type=f32))
        x1 = x_ref[rows, :] + mod_ref[2:3, :] * mix
        x1_ref[rows, :] = x1
        hs.append((_rmsnorm_rows(x1) * g2_ref[...]) * (1.0 + mod_ref[4:5, :]) + mod_ref[3:4, :])

    lts = []
    for s, h in enumerate(hs):
        _store_slabs(hrow_ref.at[pl.ds(s * sub * n, sub * n)], h)
        h_hi, h_lo = _split_bf16(h)
        lts.append((lax.dot_general(wr_hi_ref[...], h_hi, NT, preferred_element_type=f32)
                    + lax.dot_general(wr_hi_ref[...], h_lo, NT, preferred_element_type=f32)
                    + lax.dot_general(wr_lo_ref[...], h_hi, NT, preferred_element_type=f32)) + br_ref[...])

    e_g = EXPERTS_PER_GROUP
    row = lax.broadcasted_iota(jnp.int32, (e_g, sub), 0)
    neg = -jnp.inf
    routed = []
    for lt in lts:
        gl = jnp.where(row < N_GROUPS, lt[0:e_g, :], neg)
        g_max = jnp.max(gl, axis=0, keepdims=True)
        g_gate = 1.0 / jnp.sum(jnp.exp(gl - g_max), axis=0, keepdims=True)
        g_idx = jnp.min(jnp.where(gl == g_max, row, e_g), axis=0, keepdims=True)
        sel = lt[e_g * N_GROUPS:e_g * (N_GROUPS + 1), :]
        for g in range(N_GROUPS - 2, -1, -1):
            sel = jnp.where(g_idx == g, lt[e_g * (g + 1):e_g * (g + 2), :], sel)
        t1 = jnp.max(sel, axis=0, keepdims=True)
        i1 = jnp.min(jnp.where(sel == t1, row, e_g), axis=0, keepdims=True)
        sel2 = jnp.where(row == i1, neg, sel)
        t2 = jnp.max(sel2, axis=0, keepdims=True)
        i2 = jnp.min(jnp.where(sel2 == t2, row, e_g), axis=0, keepdims=True)
        ex = jnp.exp(t2 - t1)
        w1 = g_gate / (1.0 + ex)
        w2 = g_gate * ex / (1.0 + ex)
        first_lo = i1 < i2
        bucket = ((g_idx << (2 * EXPERT_BITS)) | (jnp.minimum(i1, i2) << EXPERT_BITS) | jnp.maximum(i1, i2))
        routed.append((bucket, jnp.where(first_lo, w1, w2), jnp.where(first_lo, w2, w1)))

    t_r = lax.broadcasted_iota(jnp.int32, (sub, sub), 0)
    t_c = lax.broadcasted_iota(jnp.int32, (sub, sub), 1)
    earlier = jnp.where(t_r < t_c, 1.0, 0.0).astype(bf16)
    ones = jnp.ones((sub, sub), bf16)
    run = run_ref[...]
    for rows, (bucket, w_lo, w_hi) in zip(groups, routed):
        onehot = lax.broadcasted_iota(jnp.int32, (N_BUCKETS, sub), 0) == bucket
        oh_b = jnp.where(onehot, 1.0, 0.0).astype(bf16)
        prefix = jnp.dot(oh_b, earlier, preferred_element_type=f32) + run
        rank = jnp.sum(jnp.where(onehot, prefix, 0.0), axis=0, keepdims=True)
        run = run + jnp.dot(oh_b, ones, preferred_element_type=f32)
        route_ref[:, rows] = jnp.where(row == 0, bucket.astype(f32), jnp.where(row == 1, rank, jnp.where(
            row == 2, w_lo, jnp.where(row == 3, w_hi, 0.0))))
    run_ref[...] = run
    cnt_ref[...] = run[:, 0:LANES]


def _outproj(og, osw, x2, mod3, wo, g2, wr_hi, wr_lo, br, seq, tm):
    t, d = x2.shape
    per_b = seq // tm
    row = lambda i: (i, 0)
    const = lambda i: (0, 0)
    return pl.pallas_call(
        _outproj_kernel,
        grid=(t // tm,),
        in_specs=[pl.BlockSpec((tm, GLA_W), row),
                  pl.BlockSpec((tm, SWA_W), row),
                  pl.BlockSpec((tm, d), row),
                  pl.BlockSpec((None, 6, d), lambda i: (i // per_b, 0, 0)),
                  pl.BlockSpec(wo.shape, const),
                  pl.BlockSpec((1, d), const),
                  pl.BlockSpec(wr_hi.shape, const),
                  pl.BlockSpec(wr_lo.shape, const),
                  pl.BlockSpec(br.shape, const)],
        out_specs=[pl.BlockSpec((tm, d), row),
                   pl.BlockSpec((tm * d // LANES, LANES), row),
                   pl.BlockSpec((ROUTE_ROWS, tm), lambda i: (0, i)),
                   pl.BlockSpec((N_BUCKETS, LANES), const)],
        out_shape=[jax.ShapeDtypeStruct((t, d), f32),
                   jax.ShapeDtypeStruct((t * d // LANES, LANES), f32),
                   jax.ShapeDtypeStruct((ROUTE_ROWS, t), f32),
                   jax.ShapeDtypeStruct((N_BUCKETS, LANES), f32)],
        scratch_shapes=[pltpu.VMEM((N_BUCKETS, tm // OUTPROJ_SUBS), f32)],
        compiler_params=_cparams(1),
        name="outproj",
    )(og, osw, x2, mod3, wo, g2, wr_hi, wr_lo, br)


def _plan_kernel(cnt_ref, route_ref, dest_ref, tb_ref, own_ref, nt_ref):
    nb = N_BUCKETS
    tiles = jnp.floor((cnt_ref[...] + (MOE_TM - 1)) * (1.0 / MOE_TM))
    b_r = lax.broadcasted_iota(jnp.int32, (nb, nb), 0)
    b_c = lax.broadcasted_iota(jnp.int32, (nb, nb), 1)
    before = jnp.where(b_c < b_r, 1.0, 0.0).astype(bf16)
    tiles_b = tiles.astype(bf16)
    g_r = b_r >> (2 * EXPERT_BITS)
    g_c = b_c >> (2 * EXPERT_BITS)
    g_tiles = jnp.dot(jnp.where(g_r == g_c, 1.0, 0.0).astype(bf16), tiles_b, preferred_element_type=f32)
    fill = MOE_PAIR * jnp.floor((g_tiles + (MOE_PAIR - 1)) * (1.0 / MOE_PAIR)) - g_tiles
    pads = jnp.dot(jnp.where(g_c < g_r, 1.0, 0.0).astype(bf16), fill.astype(bf16), preferred_element_type=f32)
    t_start = (jnp.dot(before, tiles_b, preferred_element_type=f32)
               + pads * (1.0 / (1 << (2 * EXPERT_BITS))))
    t_end = t_start + tiles
    tile_i = lax.broadcasted_iota(jnp.int32, (nb, 2 * LANES), 1).astype(f32)
    ended = jnp.where(jnp.concatenate([t_end, t_end], axis=1) <= tile_i, 1.0, 0.0)
    tb_ref[...] = jnp.sum(ended, axis=0, keepdims=True).astype(jnp.int32)
    started = jnp.where(jnp.concatenate([t_start, t_start], axis=1) <= tile_i, 1.0, 0.0)
    own_ref[...] = jnp.sum(started - ended, axis=0, keepdims=True).astype(jnp.int32)
    nt_ref[...] = t_end[nb - 1:nb, :].astype(jnp.int32)
    tl = route_ref.shape[1]
    onehot = lax.broadcasted_iota(jnp.int32, (nb, tl), 0) == route_ref[0:1, :].astype(jnp.int32)
    start = lax.dot_general(t_start.astype(bf16), jnp.where(onehot, 1.0, 0.0).astype(bf16), TN,
                            preferred_element_type=f32)
    dest_ref[...] = (start[0:1, :] * MOE_TM + route_ref[1:2, :]).astype(jnp.int32)


def _plan(cnt, route, tl):
    t = route.shape[1]
    return pl.pallas_call(
        _plan_kernel,
        grid=(t // tl,),
        in_specs=[pl.BlockSpec(cnt.shape, lambda i: (0, 0)),
                  pl.BlockSpec((ROUTE_ROWS, tl), lambda i: (0, i))],
        out_specs=[pl.BlockSpec((1, tl), lambda i: (0, i)),
                   pl.BlockSpec((1, 2 * LANES), lambda i: (0, 0)),
                   pl.BlockSpec((1, 2 * LANES), lambda i: (0, 0)),
                   pl.BlockSpec((1, LANES), lambda i: (0, 0))],
        out_shape=[jax.ShapeDtypeStruct((1, t), jnp.int32),
                   jax.ShapeDtypeStruct((1, 2 * LANES), jnp.int32),
                   jax.ShapeDtypeStruct((1, 2 * LANES), jnp.int32),
                   jax.ShapeDtypeStruct((1, LANES), jnp.int32)],
        compiler_params=_cparams(1),
        name="plan",
    )(cnt, route)


def _slab(ref, token, n):
    return ref.at[pl.ds(pl.multiple_of(token * n, n), n)]


def _dispatch_kernel(dest_ref, tb_ref, own_ref, src_ref, out_ref, zero_ref, sem_z, sem_r, *, rows, n_tiles_max):
    i = pl.program_id(0)
    n = src_ref.shape[0] // rows
    tile_rows = zero_ref.shape[0]

    def last_tile(k):
        return (own_ref[k] == 0) | (tb_ref[k] != tb_ref[k + 1])

    def zero_copy(k):
        return pltpu.make_async_copy(zero_ref, out_ref.at[pl.ds(pl.multiple_of(k * tile_rows, tile_rows), tile_rows)],
                                     sem_z)

    @pl.when(i == 0)
    def _():
        zero_ref[...] = jnp.zeros_like(zero_ref)

        def start(k, c):
            @pl.when(last_tile(k))
            def _():
                zero_copy(k).start()
            return c

        def wait(k, c):
            @pl.when(last_tile(k))
            def _():
                zero_copy(k).wait()
            return c

        lax.fori_loop(0, n_tiles_max, start, 0)
        lax.fori_loop(0, n_tiles_max, wait, 0)

    base = i * rows

    def row_copy(r):
        return pltpu.make_async_copy(_slab(src_ref, r, n), _slab(out_ref, dest_ref[base + r], n), sem_r)

    def issue(r8, c):
        for j in range(ROW_DMA_UNROLL):
            row_copy(r8 * ROW_DMA_UNROLL + j).start(priority=j % 2)
        return c

    def drain(r8, c):
        for j in range(ROW_DMA_UNROLL):
            row_copy(r8 * ROW_DMA_UNROLL + j).wait()
        return c

    lax.fori_loop(0, rows // ROW_DMA_UNROLL, issue, 0)
    lax.fori_loop(0, rows // ROW_DMA_UNROLL, drain, 0)


def _dispatch(dest, tb, own, hrow, n_tiles_max, rows):
    t = dest.shape[0]
    n = hrow.shape[0] // t
    grid_spec = pltpu.PrefetchScalarGridSpec(
        num_scalar_prefetch=3, grid=(t // rows,),
        in_specs=[pl.BlockSpec((rows * n, LANES), lambda i, dest, tb, own: (i, 0))],
        out_specs=pl.BlockSpec(memory_space=pl.ANY),
        scratch_shapes=[pltpu.VMEM((MOE_TM * n, LANES), hrow.dtype), pltpu.SemaphoreType.DMA(()),
                        pltpu.SemaphoreType.DMA(())])
    return pl.pallas_call(
        functools.partial(_dispatch_kernel, rows=rows, n_tiles_max=n_tiles_max),
        grid_spec=grid_spec,
        out_shape=jax.ShapeDtypeStruct((n_tiles_max * MOE_TM * n, LANES), hrow.dtype),
        compiler_params=_cparams(1),
        name="dispatch",
    )(dest, tb, own, hrow)


PAIR_BITS = 2
MOE_PAIR = 1 << PAIR_BITS


def _moe_kernel(tb_ref, nt_ref, hs_ref, wr_ref, br_ref, wg_ref, wu_ref, wd_ref, y_ref):
    j = pl.program_id(0)
    rows = hs_ref.shape[0] // MOE_PAIR

    @pl.when(j * MOE_PAIR < nt_ref[0])
    def _():
        e_g = EXPERTS_PER_GROUP
        tiles = range(MOE_PAIR)
        bucket = [tb_ref[j * MOE_PAIR + k] for k in tiles]
        group = [b >> (2 * EXPERT_BITS) for b in bucket]
        experts = [((b >> EXPERT_BITS) & (e_g - 1), b & (e_g - 1)) for b in bucket]
        h = [_load_slabs(hs_ref.at[pl.ds(k * rows, rows)], MOE_TM).astype(bf16) for k in tiles]
        gate_up = [[(jnp.dot(h[k], wg_ref[e], preferred_element_type=f32),
                     jnp.dot(h[k], wu_ref[e], preferred_element_type=f32)) for e in experts[k]] for k in tiles]
        weights = []
        for k in tiles:
            logits = jnp.dot(h[k], wr_ref[...], preferred_element_type=f32) + br_ref[...]
            lane = lax.broadcasted_iota(jnp.int32, logits.shape, 1)

            def pick(col):
                return jnp.sum(jnp.where(lane == col, logits, 0.0), axis=1, keepdims=True)

            gl = jnp.where(lane < N_GROUPS, logits, -jnp.inf)
            g_max = jnp.max(gl, axis=1, keepdims=True)
            g_gate = jnp.exp(pick(group[k]) - g_max) / jnp.sum(jnp.exp(gl - g_max), axis=1, keepdims=True)
            sel = [pick(e_g * (group[k] + 1) + e) for e in experts[k]]
            top = jnp.maximum(sel[0], sel[1])
            p = [jnp.exp(s - top) for s in sel]
            scale = g_gate / (p[0] + p[1])
            weights.append([scale * p[0], scale * p[1]])
        ys = []
        for k in tiles:
            y = None
            for m, e in enumerate(experts[k]):
                a, u = gate_up[k][m]
                yk = weights[k][m] * jnp.dot((_silu(a) * u).astype(bf16), wd_ref[e], preferred_element_type=f32)
                y = yk if y is None else y + yk
            ys.append(y)
        for k in tiles:
            _store_slabs(y_ref.at[pl.ds(k * rows, rows)], ys[k])

    @pl.when(j * MOE_PAIR >= nt_ref[0])
    def _():
        y_ref[...] = jnp.zeros_like(y_ref)


def _moe(tb, nt, hs, wr, br, wg, wu, wd):
    e_g = EXPERTS_PER_GROUP
    d = wg.shape[2]
    n = d // LANES
    step_rows = MOE_PAIR * MOE_TM * n
    n_steps = hs.shape[0] // step_rows
    last = lambda j, tb, nt: jnp.minimum(j, lax.shift_right_logical(nt[0] - 1, PAIR_BITS))
    group = lambda j, tb, nt: (tb[last(j, tb, nt) * MOE_PAIR] >> (2 * EXPERT_BITS), 0, 0, 0)
    const = lambda j, tb, nt: (0, 0)
    grid_spec = pltpu.PrefetchScalarGridSpec(
        num_scalar_prefetch=2, grid=(n_steps,),
        in_specs=[pl.BlockSpec((step_rows, LANES), lambda j, tb, nt: (last(j, tb, nt), 0)),
                  pl.BlockSpec(wr.shape, const),
                  pl.BlockSpec(br.shape, const),
                  pl.BlockSpec((None, e_g, d, D_EXPERT), group),
                  pl.BlockSpec((None, e_g, d, D_EXPERT), group),
                  pl.BlockSpec((None, e_g, D_EXPERT, d), group)],
        out_specs=pl.BlockSpec((step_rows, LANES), lambda j, tb, nt: (j, 0)))
    return pl.pallas_call(
        _moe_kernel,
        grid_spec=grid_spec,
        out_shape=jax.ShapeDtypeStruct(hs.shape, f32),
        compiler_params=_cparams(1),
        name="moe",
    )(tb, nt, hs, wr, br, wg, wu, wd)


def _final_kernel(dest_ref, x1_ref, mod_ref, fg_ref, ys_ref, o_ref, ybuf_ref, sem):
    i = pl.program_id(0)
    tm = x1_ref.shape[0]
    n = ybuf_ref.shape[1] // tm

    def row_copy(tile, slot, r):
        return pltpu.make_async_copy(_slab(ys_ref, dest_ref[tile * tm + r], n), _slab(ybuf_ref.at[slot], r, n),
                                     sem.at[slot])

    def gather(tile, slot):
        def issue(r8, c):
            for j in range(ROW_DMA_UNROLL):
                row_copy(tile, slot, r8 * ROW_DMA_UNROLL + j).start(priority=j % 2)
            return c
        lax.fori_loop(0, tm // ROW_DMA_UNROLL, issue, 0)

    @pl.when(i == 0)
    def _():
        gather(0, 0)

    @pl.when(i + 1 < pl.num_programs(0))
    def _():
        gather(i + 1, (i + 1) % 2)

    slot = i % 2

    def drain(r8, c):
        for j in range(ROW_DMA_UNROLL):
            row_copy(i, slot, r8 * ROW_DMA_UNROLL + j).wait()
        return c

    lax.fori_loop(0, tm // ROW_DMA_UNROLL, drain, 0)

    x2 = x1_ref[...] + mod_ref[5:6, :] * _load_slabs(ybuf_ref.at[slot], tm)
    o_ref[...] = _rmsnorm_rows(x2) * fg_ref[...]


def _final(dest, x1, mod3, fg, ys, seq, tm):
    t, d = x1.shape
    per_b = seq // tm
    grid_spec = pltpu.PrefetchScalarGridSpec(
        num_scalar_prefetch=1, grid=(t // tm,),
        in_specs=[pl.BlockSpec((tm, d), lambda i, dest: (i, 0)),
                  pl.BlockSpec((None, 6, d), lambda i, dest: (i // per_b, 0, 0)),
                  pl.BlockSpec((1, d), lambda i, dest: (0, 0)),
                  pl.BlockSpec(memory_space=pl.ANY)],
        out_specs=pl.BlockSpec((tm, d), lambda i, dest: (i, 0)),
        scratch_shapes=[pltpu.VMEM((2, tm * d // LANES, LANES), ys.dtype), pltpu.SemaphoreType.DMA((2,))])
    return pl.pallas_call(
        _final_kernel,
        grid_spec=grid_spec,
        out_shape=jax.ShapeDtypeStruct((t, d), f32),
        compiler_params=_cparams(1),
        name="final",
    )(dest, x1, mod3, fg, ys)


def _prep_w_in(w_in_l):
    sizes = (GLA_QK, GLA_QK, GLA_W, GLA_W, GLA_LOWRANK, SWA_W, SWA_KV, SWA_KV)
    off = [int(o) for o in np.concatenate([[0], np.cumsum(sizes)])]
    pad = jnp.zeros((w_in_l.shape[0], IN_END - off[8]), w_in_l.dtype)
    return jnp.concatenate([w_in_l[:, off[0]:off[4]], w_in_l[:, off[5]:off[8]], w_in_l[:, off[4]:off[5]], pad],
                           axis=1).astype(bf16)


def _prep_w_out(w_out_l):
    return w_out_l.astype(bf16)


def _prep_router(w_grp_l, b_grp_l, w_exp_l, b_exp_l):
    d = w_grp_l.shape[0]
    n_e = N_GROUPS * EXPERTS_PER_GROUP
    wr = jnp.zeros((ROUTER_ROWS, d), f32)
    wr = wr.at[0:N_GROUPS, :].set(w_grp_l.T).at[EXPERTS_PER_GROUP:EXPERTS_PER_GROUP + n_e, :].set(w_exp_l.T)
    br = jnp.zeros((ROUTER_ROWS, 1), f32)
    br = br.at[0:N_GROUPS, 0].set(b_grp_l).at[EXPERTS_PER_GROUP:EXPERTS_PER_GROUP + n_e, 0].set(b_exp_l)
    wr_hi = wr.astype(bf16)
    wr_lo = (wr - wr_hi.astype(f32)).astype(bf16)
    return wr_hi, wr_lo, br


def _tiles(seq):
    return dict(tm=min(seq, 512), tm_proj=min(seq, 1024), ts_gla=min(seq, 512), ts_swa=min(seq, 1024),
                tl=min(seq, 2048), rows=min(seq, 1024))


def _layer(x2, mod3, tab, e_mat, bsz, seq, norm_mix_g, w_in, gla_w_a2, gla_b_a2, gla_norm_g, swa_sinks, w_out,
           norm_ffn_g, w_grp, b_grp, w_exp, b_exp, w_gate, w_up, w_down, out_g):
    d = x2.shape[1]
    tiles = _tiles(seq)
    tm = tiles["tm"]
    wa = jnp.pad(gla_w_a2, ((0, LANES - GLA_LOWRANK), (0, 0)))
    wa_hi = wa.astype(bf16)
    wa_lo = (wa - wa_hi.astype(f32)).astype(bf16)
    qk, vg, la, sq, skv = _inproj(x2, mod3, norm_mix_g.reshape(1, d), _prep_w_in(w_in), tab, e_mat,
                                  wa_hi, wa_lo, gla_b_a2.reshape(1, GLA_QK), seq, tiles["tm_proj"])
    o_gla, o_swa, wg_b, wu_b, wd_b = _mixers(swa_sinks.reshape(1, SWA_HEADS), qk, vg, la,
                                             gla_norm_g.reshape(1, GLA_W), sq, skv, w_gate, w_up, w_down, bsz, seq,
                                             tiles["ts_gla"], tiles["ts_swa"])
    wr_hi, wr_lo, br = _prep_router(w_grp, b_grp, w_exp, b_exp)
    x1, hrow, route, cnt = _outproj(o_gla, o_swa, x2, mod3, _prep_w_out(w_out), norm_ffn_g.reshape(1, d),
                                    wr_hi, wr_lo, br, seq, tiles["tm_proj"])
    t = x2.shape[0]
    n_tiles_max = -(-(t // MOE_TM + N_PAIRS + (N_GROUPS - 1) * (MOE_PAIR - 1)) // MOE_PAIR) * MOE_PAIR
    assert n_tiles_max < 2 * LANES and t % MOE_TM == 0
    dest, tile_bucket, tile_owned, n_tiles = _plan(cnt, route, tiles["tl"])
    dest = dest.reshape(t)
    tile_bucket = tile_bucket.reshape(2 * LANES)
    n_tiles = n_tiles.reshape(LANES)[0:1]
    hs = _dispatch(dest, tile_bucket, tile_owned.reshape(2 * LANES), hrow, n_tiles_max, tiles["rows"])
    ys = _moe(tile_bucket, n_tiles, hs, wr_hi.T, br.reshape(1, ROUTER_ROWS), wg_b, wu_b, wd_b)
    return _final(dest, x1, mod3, out_g.reshape(1, d), ys, seq, tm)


def kernel(x, c, positions, ada_w, ada_b, norm_mix_g, w_in, gla_w_a2, gla_b_a2, gla_norm_g, swa_sinks, w_out,
           norm_ffn_g, w_grp, b_grp, w_exp, b_exp, w_gate, w_up, w_down, final_norm_g):
    bsz, seq, d = x.shape
    depth = ada_w.shape[0]
    assert depth == 1, "the final rmsnorm is fused into the last layer's combine kernel"
    tab = _rope_tab(positions)
    e_mat = _rope_expand_matrix()
    x2 = x.reshape(bsz * seq, d)
    for l in range(depth):
        mod3 = _adaln(c, ada_w[l], ada_b[l]).reshape(bsz, 6, d)
        x2 = _layer(x2, mod3, tab, e_mat, bsz, seq, norm_mix_g[l], w_in[l], gla_w_a2[l], gla_b_a2[l],
                    gla_norm_g[l], swa_sinks[l], w_out[l], norm_ffn_g[l], w_grp[l], b_grp[l], w_exp[l], b_exp[l],
                    w_gate[l], w_up[l], w_down[l], final_norm_g)
    return x2.reshape(bsz, seq, d)
```

```python
import functools

import numpy as np
import jax
import jax.numpy as jnp
from jax import lax
from jax.experimental import pallas as pl
from jax.experimental.pallas import tpu as pltpu

f32 = jnp.float32
bf16 = jnp.bfloat16

GLA_HEADS = 4
GLA_DK = 64
GLA_DV = 128
GLA_LOWRANK = 16
GLA_GATE_NORM = 16.0
GLA_CHUNK = 64
SWA_HEADS = 8
SWA_KV_HEADS = 2
SWA_HD = 64
WINDOW = 128
ROPE_THETA = 500000.0
ROPE_DIMS = SWA_HD // 4
N_GROUPS = 4
EXPERTS_PER_GROUP = 8
D_EXPERT = 256
EPS = 1e-6

LANES = 128
BF16_SUBLANES = 16
VMEM_LIMIT = 52 * 1024 * 1024

GLA_QK = GLA_HEADS * GLA_DK
GLA_W = GLA_HEADS * GLA_DV
SWA_W = SWA_HEADS * SWA_HD
SWA_KV = SWA_KV_HEADS * SWA_HD
ROUTE_ROWS = 8
ROUTER_ROWS = 128
EXPERT_BITS = 3
N_BUCKETS = N_GROUPS << (2 * EXPERT_BITS)
N_PAIRS = N_GROUPS * EXPERTS_PER_GROUP * (EXPERTS_PER_GROUP - 1) // 2
MOE_TM = 128
ROW_DMA_UNROLL = 8

TN = (((0,), (0,)), ((), ()))
NT = (((1,), (1,)), ((), ()))


def _cparams(n_axes):
    return pltpu.CompilerParams(dimension_semantics=("arbitrary",) * n_axes, vmem_limit_bytes=VMEM_LIMIT)


def _split_bf16(v):
    hi = v.astype(bf16)
    lo = (v - hi.astype(f32)).astype(bf16)
    return hi, lo


def _split_stack_bf16(v):
    hi = v.astype(bf16).astype(f32)
    return jnp.concatenate([hi, v - hi], axis=0).astype(bf16)


def _rmsnorm_rows(v):
    return v * lax.rsqrt(jnp.mean(v * v, axis=-1, keepdims=True) + EPS)


def _silu(v):
    return v * jax.nn.sigmoid(v)


def _adaln_kernel(c_ref, w_ref, b_ref, o_ref):
    c_hi, c_lo = _split_bf16(_silu(c_ref[...]))
    w_hi, w_lo = _split_bf16(w_ref[...])
    o_ref[...] = (jnp.dot(c_hi, w_hi, preferred_element_type=f32) + jnp.dot(c_lo, w_hi, preferred_element_type=f32)
                  + jnp.dot(c_hi, w_lo, preferred_element_type=f32)) + b_ref[...]


def _adaln(c, w, b):
    bsz, d = c.shape
    n = w.shape[1]
    return pl.pallas_call(
        _adaln_kernel,
        grid=(n // d,),
        in_specs=[pl.BlockSpec((bsz, d), lambda j: (0, 0)),
                  pl.BlockSpec((d, d), lambda j: (0, j)),
                  pl.BlockSpec((1, d), lambda j: (0, j))],
        out_specs=pl.BlockSpec((bsz, d), lambda j: (0, j)),
        out_shape=jax.ShapeDtypeStruct((bsz, n), f32),
        compiler_params=_cparams(1),
        name="adaln",
    )(c, w, b.reshape(1, n))


def _rope_tab_kernel(pos_ref, invf_ref, o_ref):
    half = ROPE_DIMS // 2
    for b in range(pos_ref.shape[0]):
        ang = pos_ref[b:b + 1, :].astype(f32) * invf_ref[...]
        o_ref[b, 0:half, :] = jnp.cos(ang)
        o_ref[b, half:2 * half, :] = jnp.sin(ang)


def _rope_tab(positions):
    bsz, s = positions.shape
    half = ROPE_DIMS // 2
    inv_freq = (np.float32(ROPE_THETA) ** (-np.arange(0, ROPE_DIMS, 2, dtype=np.float32) / np.float32(ROPE_DIMS)))
    invf = jnp.asarray(inv_freq.astype(np.float32).reshape(half, 1))
    return pl.pallas_call(
        _rope_tab_kernel,
        out_shape=jax.ShapeDtypeStruct((bsz, 2 * half, s), f32),
        name="rope_tab",
    )(positions, invf)


def _rope_expand_matrix():
    half = ROPE_DIMS // 2
    e = np.zeros((2 * half, 3 * LANES), np.float32)
    for j in range(LANES):
        jj = j % SWA_HD
        if jj < half:
            e[jj, j] = 1.0
            e[half + jj, LANES + j] = -1.0
        elif jj < 2 * half:
            e[jj - half, j] = 1.0
            e[half + jj - half, 2 * LANES + j] = 1.0
    return jnp.asarray(np.concatenate([e, e], axis=0), dtype=bf16)


IN_QK0, IN_VG0, IN_SQ0, IN_SKV0, IN_A0, IN_END = 0, 512, 1536, 2048, 2304, 2432
INPROJ_SUBS = 4
OUTPROJ_SUBS = 8


def _inproj_kernel(x_ref, mod_ref, g_ref, w_ref, tab_ref, e_ref, wa_hi_ref, wa_lo_ref, ba_ref,
                   qk_ref, vg_ref, la_ref, sq_ref, skv_ref):
    tm = x_ref.shape[0]
    sub = tm // INPROJ_SUBS
    groups = [slice(s * sub, (s + 1) * sub) for s in range(INPROJ_SUBS)]
    hb = []
    for rows in groups:
        h = (_rmsnorm_rows(x_ref[rows, :]) * g_ref[...]) * (1.0 + mod_ref[1:2, :]) + mod_ref[0:1, :]
        hb.append(h.astype(bf16))

    def proj(s, lo, hi):
        return jnp.dot(hb[s], w_ref[:, lo:hi], preferred_element_type=f32)

    a_low = [_split_bf16(proj(s, IN_A0, IN_END)) for s in range(INPROJ_SUBS)]

    for s, rows in enumerate(groups):
        qk = proj(s, IN_QK0, IN_VG0)
        lane = lax.broadcasted_iota(jnp.int32, qk.shape, 1)
        qk_ref[rows, :] = jnp.where(lane < GLA_QK, qk * (GLA_DK ** -0.5), qk).astype(bf16)

    log_a = []
    for al_hi, al_lo in a_low:
        z = (jnp.dot(al_hi, wa_hi_ref[...], preferred_element_type=f32)
             + jnp.dot(al_lo, wa_hi_ref[...], preferred_element_type=f32)
             + jnp.dot(al_hi, wa_lo_ref[...], preferred_element_type=f32)) + ba_ref[...]
        log_a.append(_split_bf16((jnp.minimum(z, 0.0) - jnp.log1p(jnp.exp(-jnp.abs(z)))) * (1.0 / GLA_GATE_NORM)))

    for s, rows in enumerate(groups):
        vg_ref[rows, :] = proj(s, IN_VG0, IN_SQ0).astype(bf16)

    c_r = lax.broadcasted_iota(jnp.int32, (sub, sub), 0)
    c_c = lax.broadcasted_iota(jnp.int32, (sub, sub), 1)
    tri = jnp.where((c_r >= c_c) & ((c_r ^ c_c) < GLA_CHUNK), 1.0, 0.0).astype(bf16)
    for rows, (la_hi, la_lo) in zip(groups, log_a):
        la_ref[rows, :] = (jnp.dot(tri, la_hi, preferred_element_type=f32)
                           + jnp.dot(tri, la_lo, preferred_element_type=f32))

    lane1 = lax.broadcasted_iota(jnp.int32, (1, LANES), 1)
    tables = []
    for rows in groups:
        tabs = lax.dot_general(_split_stack_bf16(tab_ref[:, rows]), e_ref[...], TN, preferred_element_type=f32)
        tables.append((tabs[:, 0:LANES] + jnp.where((lane1 & (SWA_HD - 1)) < ROPE_DIMS, 0.0, 1.0),
                       tabs[:, LANES:2 * LANES], tabs[:, 2 * LANES:3 * LANES]))

    def rope(s, v):
        cos_t, sa_t, sb_t = tables[s]
        return (v * cos_t + pltpu.roll(v, LANES - ROPE_DIMS // 2, 1) * sa_t
                + pltpu.roll(v, ROPE_DIMS // 2, 1) * sb_t)

    for s, rows in enumerate(groups):
        sq = proj(s, IN_SQ0, IN_SKV0)
        for p in range(SWA_W // LANES):
            cols = slice(p * LANES, (p + 1) * LANES)
            sq_ref[rows, cols] = (rope(s, sq[:, cols]) * (SWA_HD ** -0.5)).astype(bf16)
    for s, rows in enumerate(groups):
        skv = proj(s, IN_SKV0, IN_A0)
        first = lax.broadcasted_iota(jnp.int32, (sub, LANES), 1) < SWA_HD
        for part, val in enumerate((rope(s, skv[:, 0:LANES]), skv[:, LANES:2 * LANES])):
            swapped = pltpu.roll(val, SWA_HD, 1)
            skv_ref[rows, (2 * part) * LANES:(2 * part + 1) * LANES] = jnp.where(first, val, swapped).astype(bf16)
            skv_ref[rows, (2 * part + 1) * LANES:(2 * part + 2) * LANES] = jnp.where(first, swapped, val).astype(bf16)


def _inproj(x2, mod3, g, w, tab, e_mat, wa_hi, wa_lo, ba, seq, tm):
    t, d = x2.shape
    assert (tm // INPROJ_SUBS) % LANES == 0, "row groups are also lane groups of the rotary table block"
    per_b = seq // tm
    row = lambda i: (i, 0)
    const = lambda i: (0, 0)
    return pl.pallas_call(
        _inproj_kernel,
        grid=(t // tm,),
        in_specs=[pl.BlockSpec((tm, d), row),
                  pl.BlockSpec((None, 6, d), lambda i: (i // per_b, 0, 0)),
                  pl.BlockSpec((1, d), const),
                  pl.BlockSpec(w.shape, const),
                  pl.BlockSpec((None, ROPE_DIMS, tm), lambda i: (i // per_b, 0, i % per_b)),
                  pl.BlockSpec(e_mat.shape, const),
                  pl.BlockSpec(wa_hi.shape, const),
                  pl.BlockSpec(wa_lo.shape, const),
                  pl.BlockSpec(ba.shape, const)],
        out_specs=[pl.BlockSpec((tm, 2 * GLA_QK), row),
                   pl.BlockSpec((tm, 2 * GLA_W), row),
                   pl.BlockSpec((tm, GLA_QK), row),
                   pl.BlockSpec((tm, SWA_W), row),
                   pl.BlockSpec((tm, 4 * SWA_KV), row)],
        out_shape=[jax.ShapeDtypeStruct((t, 2 * GLA_QK), bf16),
                   jax.ShapeDtypeStruct((t, 2 * GLA_W), bf16),
                   jax.ShapeDtypeStruct((t, GLA_QK), f32),
                   jax.ShapeDtypeStruct((t, SWA_W), bf16),
                   jax.ShapeDtypeStruct((t, 4 * SWA_KV), bf16)],
        compiler_params=_cparams(1),
        name="inproj",
    )(x2, mod3, g, w, tab, e_mat, wa_hi, wa_lo, ba)


def _gla_chunk(c, qk_ref, vg_ref, la_ref, gn_ref, o_ref, st_ref, consts):
    causal, ones, first = consts
    c_len = GLA_CHUNK
    rows = slice(c * c_len, (c + 1) * c_len)
    b = la_ref[rows, :]
    b_last = b[c_len - 1:c_len, :]
    p1 = b_last.astype(bf16).astype(f32)
    p2 = (b_last - p1).astype(bf16).astype(f32)
    r_i = lax.broadcasted_iota(jnp.int32, (ones.shape[0], GLA_QK), 0)
    stack = jnp.where(r_i == 0, p1, jnp.where(r_i == 1, p2, jnp.where(r_i == 2, b_last - p1 - p2, 0.0)))
    b_tot_t = lax.dot_general(stack.astype(bf16), ones, TN, preferred_element_type=f32)
    q = qk_ref[rows, 0:GLA_QK].astype(f32)
    k = qk_ref[rows, GLA_QK:2 * GLA_QK].astype(f32)
    q_dec = q * jnp.exp(b)
    k_dec = (k * jnp.exp(-b)).astype(bf16)
    k_rem = (k * jnp.exp(b_last - b)).astype(bf16)
    decay = jnp.exp(b_tot_t)
    for p in range(GLA_HEADS // 2):
        ls = slice(p * LANES, (p + 1) * LANES)
        s_prev = st_ref[ls, :]
        s_prev_b = s_prev.astype(bf16)
        qd = q_dec[:, ls]
        kv_halves = []
        for hh in range(2):
            h = 2 * p + hh
            vs = slice(h * GLA_DV, (h + 1) * GLA_DV)
            qm = jnp.where(first if hh == 0 else ~first, qd, 0.0).astype(bf16)
            v = vg_ref[rows, vs]
            scores = lax.dot_general(qm, k_dec[:, ls], NT, preferred_element_type=f32)
            scores = jnp.where(causal, scores, 0.0).astype(bf16)
            o = (jnp.dot(scores, v, preferred_element_type=f32)
                 + jnp.dot(qm, s_prev_b, preferred_element_type=f32))
            o = _rmsnorm_rows(o) * gn_ref[:, vs]
            gate = vg_ref[rows, GLA_W + h * GLA_DV:GLA_W + (h + 1) * GLA_DV].astype(f32)
            o_ref[rows, vs] = (o * _silu(gate)).astype(bf16)
            kv = lax.dot_general(k_rem[:, ls], v, TN, preferred_element_type=f32)
            kv_halves.append(kv[hh * GLA_DK:(hh + 1) * GLA_DK, :])
        st_ref[ls, :] = decay[ls, :] * s_prev + jnp.concatenate(kv_halves, axis=0)


def _swa_block(blk, j, sink_ref, q_ref, kvp_ref, kvc_ref, o_ref, consts):
    band, s_i, first = consts
    w = WINDOW
    rows = slice(blk * w, (blk + 1) * w)
    prev = kvp_ref if blk == 0 else kvc_ref.at[pl.ds((blk - 1) * w, w)]
    k2 = [jnp.concatenate([prev[:, g * LANES:(g + 1) * LANES], kvc_ref[rows, g * LANES:(g + 1) * LANES]], axis=0)
          for g in range(SWA_KV_HEADS)]
    v2 = [jnp.concatenate([prev[:, (SWA_KV_HEADS + g) * LANES:(SWA_KV_HEADS + g + 1) * LANES],
                           kvc_ref[rows, (SWA_KV_HEADS + g) * LANES:(SWA_KV_HEADS + g + 1) * LANES]], axis=0)
          for g in range(SWA_KV_HEADS)]
    valid = band & ((j > 0) | (s_i >= w)) if blk == 0 else band
    heads = range(SWA_HEADS)
    kv_of = [h // (SWA_HEADS // SWA_KV_HEADS) for h in heads]
    scores = []
    for h in heads:
        p, hh = divmod(h, 2)
        qp = q_ref[rows, p * LANES:(p + 1) * LANES].astype(f32)
        qm = jnp.where(first if hh == 0 else ~first, qp, 0.0).astype(bf16)
        s = lax.dot_general(qm, k2[kv_of[h]], NT, preferred_element_type=f32)
        scores.append(jnp.where(valid, s, -jnp.inf))
    probs, denoms = [], []
    for h in heads:
        sink = sink_ref[0, h]
        m = jnp.maximum(jnp.max(scores[h], axis=-1, keepdims=True), sink)
        pr = jnp.exp(scores[h] - m)
        denoms.append(jnp.sum(pr, axis=-1, keepdims=True) + jnp.exp(sink - m))
        probs.append(pr.astype(bf16))
    outs = [jnp.dot(probs[h], v2[kv_of[h]], preferred_element_type=f32) / denoms[h] for h in heads]
    for p in range(SWA_W // LANES):
        o_ref[rows, p * LANES:(p + 1) * LANES] = jnp.where(first, outs[2 * p], outs[2 * p + 1]).astype(bf16)


def _gla_kernel(qk_ref, vg_ref, la_ref, gn_ref, o_ref, st_ref):
    @pl.when(pl.program_id(1) == 0)
    def _():
        st_ref[...] = jnp.zeros_like(st_ref)

    c_len = GLA_CHUNK
    causal = lax.broadcasted_iota(jnp.int32, (c_len, c_len), 0) >= lax.broadcasted_iota(jnp.int32, (c_len, c_len), 1)
    consts = (causal, jnp.ones((BF16_SUBLANES, LANES), bf16),
              lax.broadcasted_iota(jnp.int32, (c_len, LANES), 1) < GLA_DK)
    for c in range(qk_ref.shape[0] // c_len):
        _gla_chunk(c, qk_ref, vg_ref, la_ref, gn_ref, o_ref, st_ref, consts)


def _swa_kernel(sink_ref, q_ref, kvp_ref, kvc_ref, wg_ref, wu_ref, wd_ref, o_ref, wg_out, wu_out, wd_out):
    wg_out[...] = wg_ref[...].astype(bf16)
    wu_out[...] = wu_ref[...].astype(bf16)
    wd_out[...] = wd_ref[...].astype(bf16)
    w = WINDOW
    s_i = lax.broadcasted_iota(jnp.int32, (w, 2 * w), 1)
    rel = lax.broadcasted_iota(jnp.int32, (w, 2 * w), 0) + w - s_i
    consts = ((rel >= 0) & (rel < w), s_i, lax.broadcasted_iota(jnp.int32, (w, LANES), 1) < SWA_HD)
    for blk in range(q_ref.shape[0] // w):
        _swa_block(blk, pl.program_id(1), sink_ref, q_ref, kvp_ref, kvc_ref, o_ref, consts)


def _mixers(sinks, qk, vg, la, gn, sq, skv, wg, wu, wd, bsz, seq, ts_gla, ts):
    t = qk.shape[0]
    ng = seq // ts_gla
    tile_g = lambda b, j: (b * ng + j, 0)
    o_gla = pl.pallas_call(
        _gla_kernel,
        grid=(bsz, ng),
        in_specs=[pl.BlockSpec((ts_gla, 2 * GLA_QK), tile_g),
                  pl.BlockSpec((ts_gla, 2 * GLA_W), tile_g),
                  pl.BlockSpec((ts_gla, GLA_QK), tile_g),
                  pl.BlockSpec((1, GLA_W), lambda b, j: (0, 0))],
        out_specs=pl.BlockSpec((ts_gla, GLA_W), tile_g),
        out_shape=jax.ShapeDtypeStruct((t, GLA_W), bf16),
        scratch_shapes=[pltpu.VMEM((GLA_HEADS * GLA_DK, GLA_DV), f32)],
        compiler_params=_cparams(2),
        name="gla",
    )(qk, vg, la, gn)
    ns = seq // ts
    blocks = ts // WINDOW
    tile = lambda b, j: (b * ns + j, 0)
    n_exp = wg.shape[0] * wg.shape[1]
    assert n_exp % (bsz * ns) == 0
    per = n_exp // (bsz * ns)
    flat = [w.reshape((n_exp,) + w.shape[2:]) for w in (wg, wu, wd)]
    w_spec = lambda w: pl.BlockSpec((per,) + w.shape[1:], lambda b, j: (b * ns + j, 0, 0))
    o_swa, *w_bf16 = pl.pallas_call(
        _swa_kernel,
        grid=(bsz, ns),
        in_specs=[pl.BlockSpec(memory_space=pltpu.SMEM),
                  pl.BlockSpec((ts, SWA_W), tile),
                  pl.BlockSpec((WINDOW, 4 * SWA_KV), lambda b, j: ((b * ns + j) * blocks - jnp.minimum(j, 1), 0)),
                  pl.BlockSpec((ts, 4 * SWA_KV), tile)] + [w_spec(w) for w in flat],
        out_specs=[pl.BlockSpec((ts, SWA_W), tile)] + [w_spec(w) for w in flat],
        out_shape=[jax.ShapeDtypeStruct((t, SWA_W), bf16)] + [jax.ShapeDtypeStruct(w.shape, bf16) for w in flat],
        compiler_params=_cparams(2),
        name="swa",
    )(sinks, sq, skv, skv, *flat)
    return (o_gla, o_swa) + tuple(o.reshape(w.shape) for o, w in zip(w_bf16, (wg, wu, wd)))


def _store_slabs(ref, v):
    tm, n = v.shape[0], v.shape[1] // LANES
    for c in range(n):
        ref[pl.ds(c, tm, stride=n), :] = v[:, c * LANES:(c + 1) * LANES]


def _load_slabs(ref, tm):
    n = ref.shape[0] // tm
    return jnp.concatenate([ref[pl.ds(c, tm, stride=n), :] for c in range(n)], axis=1)


def _outproj_kernel(og_ref, os_ref, x_ref, mod_ref, wo_ref, g2_ref, wr_hi_ref, wr_lo_ref, br_ref,
                    x1_ref, hrow_ref, route_ref, cnt_ref, run_ref):
    @pl.when(pl.program_id(0) == 0)
    def _():
        run_ref[...] = jnp.zeros_like(run_ref)

    tm = x_ref.shape[0]
    sub = tm // OUTPROJ_SUBS
    n = hrow_ref.shape[0] // tm
    groups = [slice(s * sub, (s + 1) * sub) for s in range(OUTPROJ_SUBS)]
    hs = []
    for rows in groups:
        mix = (jnp.dot(og_ref[rows, :], wo_ref[0:GLA_W, :], preferred_element_type=f32)
               + jnp.dot(os_ref[rows, :], wo_ref[GLA_W:GLA_W + SWA_W, :], preferred_element_type=f32))
        x1 = x_ref[rows, :] + mod_ref[2:3, :] * mix
        x1_ref[rows, :] = x1
        hs.append((_rmsnorm_rows(x1) * g2_ref[...]) * (1.0 + mod_ref[4:5, :]) + mod_ref[3:4, :])

    lts = []
    for s, h in enumerate(hs):
        _store_slabs(hrow_ref.at[pl.ds(s * sub * n, sub * n)], h)
        h_hi, h_lo = _split_bf16(h)
        lts.append((lax.dot_general(wr_hi_ref[...], h_hi, NT, preferred_element_type=f32)
                    + lax.dot_general(wr_hi_ref[...], h_lo, NT, preferred_element_type=f32)
                    + lax.dot_general(wr_lo_ref[...], h_hi, NT, preferred_element_type=f32)) + br_ref[...])

    e_g = EXPERTS_PER_GROUP
    row = lax.broadcasted_iota(jnp.int32, (e_g, sub), 0)
    neg = -jnp.inf
    routed = []
    for lt in lts:
        gl = jnp.where(row < N_GROUPS, lt[0:e_g, :], neg)
        g_max = jnp.max(gl, axis=0, keepdims=True)
        g_gate = 1.0 / jnp.sum(jnp.exp(gl - g_max), axis=0, keepdims=True)
        g_idx = jnp.min(jnp.where(gl == g_max, row, e_g), axis=0, keepdims=True)
        sel = lt[e_g * N_GROUPS:e_g * (N_GROUPS + 1), :]
        for g in range(N_GROUPS - 2, -1, -1):
            sel = jnp.where(g_idx == g, lt[e_g * (g + 1):e_g * (g + 2), :], sel)
        t1 = jnp.max(sel, axis=0, keepdims=True)
        i1 = jnp.min(jnp.where(sel == t1, row, e_g), axis=0, keepdims=True)
        sel2 = jnp.where(row == i1, neg, sel)
        t2 = jnp.max(sel2, axis=0, keepdims=True)
        i2 = jnp.min(jnp.where(sel2 == t2, row, e_g), axis=0, keepdims=True)
        ex = jnp.exp(t2 - t1)
        w1 = g_gate / (1.0 + ex)
        w2 = g_gate * ex / (1.0 + ex)
        first_lo = i1 < i2
        bucket = ((g_idx << (2 * EXPERT_BITS)) | (jnp.minimum(i1, i2) << EXPERT_BITS) | jnp.maximum(i1, i2))
        routed.append((bucket, jnp.where(first_lo, w1, w2), jnp.where(first_lo, w2, w1)))

    t_r = lax.broadcasted_iota(jnp.int32, (sub, sub), 0)
    t_c = lax.broadcasted_iota(jnp.int32, (sub, sub), 1)
    earlier = jnp.where(t_r < t_c, 1.0, 0.0).astype(bf16)
    ones = jnp.ones((sub, sub), bf16)
    run = run_ref[...]
    for rows, (bucket, w_lo, w_hi) in zip(groups, routed):
        onehot = lax.broadcasted_iota(jnp.int32, (N_BUCKETS, sub), 0) == bucket
        oh_b = jnp.where(onehot, 1.0, 0.0).astype(bf16)
        prefix = jnp.dot(oh_b, earlier, preferred_element_type=f32) + run
        rank = jnp.sum(jnp.where(onehot, prefix, 0.0), axis=0, keepdims=True)
        run = run + jnp.dot(oh_b, ones, preferred_element_type=f32)
        route_ref[:, rows] = jnp.where(row == 0, bucket.astype(f32), jnp.where(row == 1, rank, jnp.where(
            row == 2, w_lo, jnp.where(row == 3, w_hi, 0.0))))
    run_ref[...] = run
    cnt_ref[...] = run[:, 0:LANES]


def _outproj(og, osw, x2, mod3, wo, g2, wr_hi, wr_lo, br, seq, tm):
    t, d = x2.shape
    assert (tm // OUTPROJ_SUBS) % LANES == 0, "row groups are also lane groups of the routing table"
    per_b = seq // tm
    row = lambda i: (i, 0)
    const = lambda i: (0, 0)
    return pl.pallas_call(
        _outproj_kernel,
        grid=(t // tm,),
        in_specs=[pl.BlockSpec((tm, GLA_W), row),
                  pl.BlockSpec((tm, SWA_W), row),
                  pl.BlockSpec((tm, d), row),
                  pl.BlockSpec((None, 6, d), lambda i: (i // per_b, 0, 0)),
                  pl.BlockSpec(wo.shape, const),
                  pl.BlockSpec((1, d), const),
                  pl.BlockSpec(wr_hi.shape, const),
                  pl.BlockSpec(wr_lo.shape, const),
                  pl.BlockSpec(br.shape, const)],
        out_specs=[pl.BlockSpec((tm, d), row),
                   pl.BlockSpec((tm * d // LANES, LANES), row),
                   pl.BlockSpec((ROUTE_ROWS, tm), lambda i: (0, i)),
                   pl.BlockSpec((N_BUCKETS, LANES), const)],
        out_shape=[jax.ShapeDtypeStruct((t, d), f32),
                   jax.ShapeDtypeStruct((t * d // LANES, LANES), f32),
                   jax.ShapeDtypeStruct((ROUTE_ROWS, t), f32),
                   jax.ShapeDtypeStruct((N_BUCKETS, LANES), f32)],
        scratch_shapes=[pltpu.VMEM((N_BUCKETS, tm // OUTPROJ_SUBS), f32)],
        compiler_params=_cparams(1),
        name="outproj",
    )(og, osw, x2, mod3, wo, g2, wr_hi, wr_lo, br)


def _plan_kernel(cnt_ref, route_ref, dest_ref, tb_ref, own_ref, nt_ref):
    nb = N_BUCKETS
    tiles = jnp.floor((cnt_ref[...] + (MOE_TM - 1)) * (1.0 / MOE_TM))
    b_r = lax.broadcasted_iota(jnp.int32, (nb, nb), 0)
    b_c = lax.broadcasted_iota(jnp.int32, (nb, nb), 1)
    before = jnp.where(b_c < b_r, 1.0, 0.0).astype(bf16)
    tiles_b = tiles.astype(bf16)
    g_r = b_r >> (2 * EXPERT_BITS)
    g_c = b_c >> (2 * EXPERT_BITS)
    g_tiles = jnp.dot(jnp.where(g_r == g_c, 1.0, 0.0).astype(bf16), tiles_b, preferred_element_type=f32)
    fill = MOE_PAIR * jnp.floor((g_tiles + (MOE_PAIR - 1)) * (1.0 / MOE_PAIR)) - g_tiles
    pads = jnp.dot(jnp.where(g_c < g_r, 1.0, 0.0).astype(bf16), fill.astype(bf16), preferred_element_type=f32)
    t_start = (jnp.dot(before, tiles_b, preferred_element_type=f32)
               + pads * (1.0 / (1 << (2 * EXPERT_BITS))))
    t_end = t_start + tiles
    tile_i = lax.broadcasted_iota(jnp.int32, (nb, 2 * LANES), 1).astype(f32)
    ended = jnp.where(jnp.concatenate([t_end, t_end], axis=1) <= tile_i, 1.0, 0.0)
    tb_ref[...] = jnp.sum(ended, axis=0, keepdims=True).astype(jnp.int32)
    started = jnp.where(jnp.concatenate([t_start, t_start], axis=1) <= tile_i, 1.0, 0.0)
    own_ref[...] = jnp.sum(started - ended, axis=0, keepdims=True).astype(jnp.int32)
    nt_ref[...] = t_end[nb - 1:nb, :].astype(jnp.int32)
    tl = route_ref.shape[1]
    onehot = lax.broadcasted_iota(jnp.int32, (nb, tl), 0) == route_ref[0:1, :].astype(jnp.int32)
    start = lax.dot_general(t_start.astype(bf16), jnp.where(onehot, 1.0, 0.0).astype(bf16), TN,
                            preferred_element_type=f32)
    dest_ref[...] = (start[0:1, :] * MOE_TM + route_ref[1:2, :]).astype(jnp.int32)


def _plan(cnt, route, tl):
    t = route.shape[1]
    return pl.pallas_call(
        _plan_kernel,
        grid=(t // tl,),
        in_specs=[pl.BlockSpec(cnt.shape, lambda i: (0, 0)),
                  pl.BlockSpec((ROUTE_ROWS, tl), lambda i: (0, i))],
        out_specs=[pl.BlockSpec((1, tl), lambda i: (0, i)),
                   pl.BlockSpec((1, 2 * LANES), lambda i: (0, 0)),
                   pl.BlockSpec((1, 2 * LANES), lambda i: (0, 0)),
                   pl.BlockSpec((1, LANES), lambda i: (0, 0))],
        out_shape=[jax.ShapeDtypeStruct((1, t), jnp.int32),
                   jax.ShapeDtypeStruct((1, 2 * LANES), jnp.int32),
                   jax.ShapeDtypeStruct((1, 2 * LANES), jnp.int32),
                   jax.ShapeDtypeStruct((1, LANES), jnp.int32)],
        compiler_params=_cparams(1),
        name="plan",
    )(cnt, route)


def _slab(ref, token, n):
    return ref.at[pl.ds(pl.multiple_of(token * n, n), n)]


def _dispatch_kernel(dest_ref, tb_ref, own_ref, src_ref, out_ref, zero_ref, sem_z, sem_r, *, rows, n_tiles_max):
    i = pl.program_id(0)
    n = src_ref.shape[0] // rows
    tile_rows = zero_ref.shape[0]

    def last_tile(k):
        return (own_ref[k] == 0) | (tb_ref[k] != tb_ref[k + 1])

    def zero_copy(k):
        return pltpu.make_async_copy(zero_ref, out_ref.at[pl.ds(pl.multiple_of(k * tile_rows, tile_rows), tile_rows)],
                                     sem_z)

    @pl.when(i == 0)
    def _():
        zero_ref[...] = jnp.zeros_like(zero_ref)

        def start(k, c):
            @pl.when(last_tile(k))
            def _():
                zero_copy(k).start()
            return c

        def wait(k, c):
            @pl.when(last_tile(k))
            def _():
                zero_copy(k).wait()
            return c

        lax.fori_loop(0, n_tiles_max, start, 0)
        lax.fori_loop(0, n_tiles_max, wait, 0)

    base = i * rows

    def row_copy(r):
        return pltpu.make_async_copy(_slab(src_ref, r, n), _slab(out_ref, dest_ref[base + r], n), sem_r)

    def issue(r8, c):
        for j in range(ROW_DMA_UNROLL):
            row_copy(r8 * ROW_DMA_UNROLL + j).start(priority=j % 2)
        return c

    def drain(r8, c):
        for j in range(ROW_DMA_UNROLL):
            row_copy(r8 * ROW_DMA_UNROLL + j).wait()
        return c

    lax.fori_loop(0, rows // ROW_DMA_UNROLL, issue, 0)
    lax.fori_loop(0, rows // ROW_DMA_UNROLL, drain, 0)


def _dispatch(dest, tb, own, hrow, n_tiles_max, rows):
    t = dest.shape[0]
    n = hrow.shape[0] // t
    grid_spec = pltpu.PrefetchScalarGridSpec(
        num_scalar_prefetch=3, grid=(t // rows,),
        in_specs=[pl.BlockSpec((rows * n, LANES), lambda i, dest, tb, own: (i, 0))],
        out_specs=pl.BlockSpec(memory_space=pl.ANY),
        scratch_shapes=[pltpu.VMEM((MOE_TM * n, LANES), hrow.dtype), pltpu.SemaphoreType.DMA(()),
                        pltpu.SemaphoreType.DMA(())])
    return pl.pallas_call(
        functools.partial(_dispatch_kernel, rows=rows, n_tiles_max=n_tiles_max),
        grid_spec=grid_spec,
        out_shape=jax.ShapeDtypeStruct((n_tiles_max * MOE_TM * n, LANES), hrow.dtype),
        compiler_params=_cparams(1),
        name="dispatch",
    )(dest, tb, own, hrow)


PAIR_BITS = 2
MOE_PAIR = 1 << PAIR_BITS


def _moe_kernel(tb_ref, nt_ref, hs_ref, wr_ref, br_ref, wg_ref, wu_ref, wd_ref, y_ref):
    j = pl.program_id(0)
    rows = hs_ref.shape[0] // MOE_PAIR

    @pl.when(j * MOE_PAIR < nt_ref[0])
    def _():
        e_g = EXPERTS_PER_GROUP
        tiles = range(MOE_PAIR)
        bucket = [tb_ref[j * MOE_PAIR + k] for k in tiles]
        group = [b >> (2 * EXPERT_BITS) for b in bucket]
        experts = [((b >> EXPERT_BITS) & (e_g - 1), b & (e_g - 1)) for b in bucket]
        h = [_load_slabs(hs_ref.at[pl.ds(k * rows, rows)], MOE_TM).astype(bf16) for k in tiles]
        gate_up = [[(jnp.dot(h[k], wg_ref[e], preferred_element_type=f32),
                     jnp.dot(h[k], wu_ref[e], preferred_element_type=f32)) for e in experts[k]] for k in tiles]
        weights = []
        for k in tiles:
            logits = jnp.dot(h[k], wr_ref[...], preferred_element_type=f32) + br_ref[...]
            lane = lax.broadcasted_iota(jnp.int32, logits.shape, 1)

            def pick(col):
                return jnp.sum(jnp.where(lane == col, logits, 0.0), axis=1, keepdims=True)

            gl = jnp.where(lane < N_GROUPS, logits, -jnp.inf)
            g_max = jnp.max(gl, axis=1, keepdims=True)
            g_gate = jnp.exp(pick(group[k]) - g_max) / jnp.sum(jnp.exp(gl - g_max), axis=1, keepdims=True)
            sel = [pick(e_g * (group[k] + 1) + e) for e in experts[k]]
            top = jnp.maximum(sel[0], sel[1])
            p = [jnp.exp(s - top) for s in sel]
            scale = g_gate / (p[0] + p[1])
            weights.append([scale * p[0], scale * p[1]])
        ys = []
        for k in tiles:
            y = None
            for m, e in enumerate(experts[k]):
                a, u = gate_up[k][m]
                yk = weights[k][m] * jnp.dot((_silu(a) * u).astype(bf16), wd_ref[e], preferred_element_type=f32)
                y = yk if y is None else y + yk
            ys.append(y)
        for k in tiles:
            _store_slabs(y_ref.at[pl.ds(k * rows, rows)], ys[k])

    @pl.when(j * MOE_PAIR >= nt_ref[0])
    def _():
        y_ref[...] = jnp.zeros_like(y_ref)


def _moe(tb, nt, hs, wr, br, wg, wu, wd):
    e_g = EXPERTS_PER_GROUP
    d = wg.shape[2]
    n = d // LANES
    step_rows = MOE_PAIR * MOE_TM * n
    n_steps = hs.shape[0] // step_rows
    last = lambda j, tb, nt: jnp.minimum(j, lax.shift_right_logical(nt[0] - 1, PAIR_BITS))
    group = lambda j, tb, nt: (tb[last(j, tb, nt) * MOE_PAIR] >> (2 * EXPERT_BITS), 0, 0, 0)
    const = lambda j, tb, nt: (0, 0)
    grid_spec = pltpu.PrefetchScalarGridSpec(
        num_scalar_prefetch=2, grid=(n_steps,),
        in_specs=[pl.BlockSpec((step_rows, LANES), lambda j, tb, nt: (last(j, tb, nt), 0)),
                  pl.BlockSpec(wr.shape, const),
                  pl.BlockSpec(br.shape, const),
                  pl.BlockSpec((None, e_g, d, D_EXPERT), group),
                  pl.BlockSpec((None, e_g, d, D_EXPERT), group),
                  pl.BlockSpec((None, e_g, D_EXPERT, d), group)],
        out_specs=pl.BlockSpec((step_rows, LANES), lambda j, tb, nt: (j, 0)))
    return pl.pallas_call(
        _moe_kernel,
        grid_spec=grid_spec,
        out_shape=jax.ShapeDtypeStruct(hs.shape, f32),
        compiler_params=_cparams(1),
        name="moe",
    )(tb, nt, hs, wr, br, wg, wu, wd)


def _final_kernel(dest_ref, x1_ref, mod_ref, fg_ref, ys_ref, o_ref, ybuf_ref, sem):
    i = pl.program_id(0)
    tm = x1_ref.shape[0]
    n = ybuf_ref.shape[1] // tm

    def row_copy(tile, slot, r):
        return pltpu.make_async_copy(_slab(ys_ref, dest_ref[tile * tm + r], n), _slab(ybuf_ref.at[slot], r, n),
                                     sem.at[slot])

    def gather(tile, slot):
        def issue(r8, c):
            for j in range(ROW_DMA_UNROLL):
                row_copy(tile, slot, r8 * ROW_DMA_UNROLL + j).start(priority=j % 2)
            return c
        lax.fori_loop(0, tm // ROW_DMA_UNROLL, issue, 0)

    @pl.when(i == 0)
    def _():
        gather(0, 0)

    @pl.when(i + 1 < pl.num_programs(0))
    def _():
        gather(i + 1, (i + 1) % 2)

    slot = i % 2

    def drain(r8, c):
        for j in range(ROW_DMA_UNROLL):
            row_copy(i, slot, r8 * ROW_DMA_UNROLL + j).wait()
        return c

    lax.fori_loop(0, tm // ROW_DMA_UNROLL, drain, 0)

    x2 = x1_ref[...] + mod_ref[5:6, :] * _load_slabs(ybuf_ref.at[slot], tm)
    o_ref[...] = _rmsnorm_rows(x2) * fg_ref[...]


def _final(dest, x1, mod3, fg, ys, seq, tm):
    t, d = x1.shape
    per_b = seq // tm
    grid_spec = pltpu.PrefetchScalarGridSpec(
        num_scalar_prefetch=1, grid=(t // tm,),
        in_specs=[pl.BlockSpec((tm, d), lambda i, dest: (i, 0)),
                  pl.BlockSpec((None, 6, d), lambda i, dest: (i // per_b, 0, 0)),
                  pl.BlockSpec((1, d), lambda i, dest: (0, 0)),
                  pl.BlockSpec(memory_space=pl.ANY)],
        out_specs=pl.BlockSpec((tm, d), lambda i, dest: (i, 0)),
        scratch_shapes=[pltpu.VMEM((2, tm * d // LANES, LANES), ys.dtype), pltpu.SemaphoreType.DMA((2,))])
    return pl.pallas_call(
        _final_kernel,
        grid_spec=grid_spec,
        out_shape=jax.ShapeDtypeStruct((t, d), f32),
        compiler_params=_cparams(1),
        name="final",
    )(dest, x1, mod3, fg, ys)


def _prep_w_in(w_in_l):
    sizes = (GLA_QK, GLA_QK, GLA_W, GLA_W, GLA_LOWRANK, SWA_W, SWA_KV, SWA_KV)
    off = [int(o) for o in np.concatenate([[0], np.cumsum(sizes)])]
    pad = jnp.zeros((w_in_l.shape[0], IN_END - off[8]), w_in_l.dtype)
    return jnp.concatenate([w_in_l[:, off[0]:off[4]], w_in_l[:, off[5]:off[8]], w_in_l[:, off[4]:off[5]], pad],
                           axis=1).astype(bf16)


def _prep_w_out(w_out_l):
    return w_out_l.astype(bf16)


def _prep_router(w_grp_l, b_grp_l, w_exp_l, b_exp_l):
    d = w_grp_l.shape[0]
    n_e = N_GROUPS * EXPERTS_PER_GROUP
    gap, tail = EXPERTS_PER_GROUP - N_GROUPS, ROUTER_ROWS - EXPERTS_PER_GROUP - n_e
    wr = jnp.concatenate([w_grp_l.T, jnp.zeros((gap, d), f32), w_exp_l.T, jnp.zeros((tail, d), f32)], axis=0)
    br = jnp.concatenate([b_grp_l, jnp.zeros((gap,), f32), b_exp_l, jnp.zeros((tail,), f32)]).reshape(ROUTER_ROWS, 1)
    wr_hi = wr.astype(bf16)
    wr_lo = (wr - wr_hi.astype(f32)).astype(bf16)
    return wr_hi, wr_lo, br


def _tiles(seq):
    return dict(tm=min(seq, 512), tm_proj=min(seq, 1024), ts_gla=min(seq, 512), ts_swa=min(seq, 1024),
                tl=min(seq, 2048), rows=min(seq, 1024))


def _layer(x2, mod3, tab, e_mat, bsz, seq, norm_mix_g, w_in, gla_w_a2, gla_b_a2, gla_norm_g, swa_sinks, w_out,
           norm_ffn_g, w_grp, b_grp, w_exp, b_exp, w_gate, w_up, w_down, out_g):
    d = x2.shape[1]
    tiles = _tiles(seq)
    tm = tiles["tm"]
    wa = jnp.pad(gla_w_a2, ((0, LANES - GLA_LOWRANK), (0, 0)))
    wa_hi = wa.astype(bf16)
    wa_lo = (wa - wa_hi.astype(f32)).astype(bf16)
    qk, vg, la, sq, skv = _inproj(x2, mod3, norm_mix_g.reshape(1, d), _prep_w_in(w_in), tab, e_mat,
                                  wa_hi, wa_lo, gla_b_a2.reshape(1, GLA_QK), seq, tiles["tm_proj"])
    o_gla, o_swa, wg_b, wu_b, wd_b = _mixers(swa_sinks.reshape(1, SWA_HEADS), qk, vg, la,
                                             gla_norm_g.reshape(1, GLA_W), sq, skv, w_gate, w_up, w_down, bsz, seq,
                                             tiles["ts_gla"], tiles["ts_swa"])
    wr_hi, wr_lo, br = _prep_router(w_grp, b_grp, w_exp, b_exp)
    x1, hrow, route, cnt = _outproj(o_gla, o_swa, x2, mod3, _prep_w_out(w_out), norm_ffn_g.reshape(1, d),
                                    wr_hi, wr_lo, br, seq, tiles["tm_proj"])
    t = x2.shape[0]
    n_tiles_max = -(-(t // MOE_TM + N_PAIRS + (N_GROUPS - 1) * (MOE_PAIR - 1)) // MOE_PAIR) * MOE_PAIR
    assert n_tiles_max < 2 * LANES and t % MOE_TM == 0
    dest, tile_bucket, tile_owned, n_tiles = _plan(cnt, route, tiles["tl"])
    dest = dest.reshape(t)
    tile_bucket = tile_bucket.reshape(2 * LANES)
    n_tiles = n_tiles.reshape(LANES)[0:1]
    hs = _dispatch(dest, tile_bucket, tile_owned.reshape(2 * LANES), hrow, n_tiles_max, tiles["rows"])
    ys = _moe(tile_bucket, n_tiles, hs, wr_hi.T, br.reshape(1, ROUTER_ROWS), wg_b, wu_b, wd_b)
    return _final(dest, x1, mod3, out_g.reshape(1, d), ys, seq, tm)


def kernel(x, c, positions, ada_w, ada_b, norm_mix_g, w_in, gla_w_a2, gla_b_a2, gla_norm_g, swa_sinks, w_out,
           norm_ffn_g, w_grp, b_grp, w_exp, b_exp, w_gate, w_up, w_down, final_norm_g):
    bsz, seq, d = x.shape
    depth = ada_w.shape[0]
    assert depth == 1, "the final rmsnorm is fused into the last layer's combine kernel"
    tab = _rope_tab(positions)
    e_mat = _rope_expand_matrix()
    x2 = x.reshape(bsz * seq, d)
    for l in range(depth):
        mod3 = _adaln(c, ada_w[l], ada_b[l]).reshape(bsz, 6, d)
        x2 = _layer(x2, mod3, tab, e_mat, bsz, seq, norm_mix_g[l], w_in[l], gla_w_a2[l], gla_b_a2[l],
                    gla_norm_g[l], swa_sinks[l], w_out[l], norm_ffn_g[l], w_grp[l], b_grp[l], w_exp[l], b_exp[l],
                    w_gate[l], w_up[l], w_down[l], final_norm_g)
    return x2.reshape(bsz, seq, d)
```

```python
import functools

import numpy as np
import jax
import jax.numpy as jnp
from jax import lax
from jax.experimental import pallas as pl
from jax.experimental.pallas import tpu as pltpu

f32 = jnp.float32
bf16 = jnp.bfloat16

GLA_HEADS = 4
GLA_DK = 64
GLA_DV = 128
GLA_LOWRANK = 16
GLA_GATE_NORM = 16.0
GLA_CHUNK = 64
SWA_HEADS = 8
SWA_KV_HEADS = 2
SWA_HD = 64
WINDOW = 128
ROPE_THETA = 500000.0
ROPE_DIMS = SWA_HD // 4
N_GROUPS = 4
EXPERTS_PER_GROUP = 8
D_EXPERT = 256
EPS = 1e-6

LANES = 128
BF16_SUBLANES = 16
VMEM_LIMIT = 52 * 1024 * 1024

GLA_QK = GLA_HEADS * GLA_DK
GLA_W = GLA_HEADS * GLA_DV
SWA_W = SWA_HEADS * SWA_HD
SWA_KV = SWA_KV_HEADS * SWA_HD
ROUTE_ROWS = 8
ROUTER_ROWS = 128
EXPERT_BITS = 3
N_BUCKETS = N_GROUPS << (2 * EXPERT_BITS)
N_PAIRS = N_GROUPS * EXPERTS_PER_GROUP * (EXPERTS_PER_GROUP - 1) // 2
MOE_TM = 128
ROW_DMA_UNROLL = 8

TN = (((0,), (0,)), ((), ()))
NT = (((1,), (1,)), ((), ()))


def _cparams(n_axes):
    return pltpu.CompilerParams(dimension_semantics=("arbitrary",) * n_axes, vmem_limit_bytes=VMEM_LIMIT)


def _split_bf16(v):
    hi = v.astype(bf16)
    lo = (v - hi.astype(f32)).astype(bf16)
    return hi, lo


def _split_stack_bf16(v):
    hi = v.astype(bf16).astype(f32)
    return jnp.concatenate([hi, v - hi], axis=0).astype(bf16)


def _rmsnorm_rows(v):
    return v * lax.rsqrt(jnp.mean(v * v, axis=-1, keepdims=True) + EPS)


def _silu(v):
    return v * jax.nn.sigmoid(v)


def _adaln_kernel(c_ref, w_ref, b_ref, o_ref):
    c_hi, c_lo = _split_bf16(_silu(c_ref[...]))
    w_hi, w_lo = _split_bf16(w_ref[...])
    o_ref[...] = (jnp.dot(c_hi, w_hi, preferred_element_type=f32) + jnp.dot(c_lo, w_hi, preferred_element_type=f32)
                  + jnp.dot(c_hi, w_lo, preferred_element_type=f32)) + b_ref[...]


def _adaln(c, w, b):
    bsz, d = c.shape
    n = w.shape[1]
    return pl.pallas_call(
        _adaln_kernel,
        grid=(n // d,),
        in_specs=[pl.BlockSpec((bsz, d), lambda j: (0, 0)),
                  pl.BlockSpec((d, d), lambda j: (0, j)),
                  pl.BlockSpec((1, d), lambda j: (0, j))],
        out_specs=pl.BlockSpec((bsz, d), lambda j: (0, j)),
        out_shape=jax.ShapeDtypeStruct((bsz, n), f32),
        compiler_params=_cparams(1),
        name="adaln",
    )(c, w, b.reshape(1, n))


def _rope_tab_kernel(pos_ref, invf_ref, o_ref):
    half = ROPE_DIMS // 2
    for b in range(pos_ref.shape[0]):
        ang = pos_ref[b:b + 1, :].astype(f32) * invf_ref[...]
        o_ref[b, 0:half, :] = jnp.cos(ang)
        o_ref[b, half:2 * half, :] = jnp.sin(ang)


def _rope_tab(positions):
    bsz, s = positions.shape
    half = ROPE_DIMS // 2
    inv_freq = (np.float32(ROPE_THETA) ** (-np.arange(0, ROPE_DIMS, 2, dtype=np.float32) / np.float32(ROPE_DIMS)))
    invf = jnp.asarray(inv_freq.astype(np.float32).reshape(half, 1))
    return pl.pallas_call(
        _rope_tab_kernel,
        out_shape=jax.ShapeDtypeStruct((bsz, 2 * half, s), f32),
        name="rope_tab",
    )(positions, invf)


def _rope_expand_matrix():
    half = ROPE_DIMS // 2
    e = np.zeros((2 * half, 3 * LANES), np.float32)
    for j in range(LANES):
        jj = j % SWA_HD
        if jj < half:
            e[jj, j] = 1.0
            e[half + jj, LANES + j] = -1.0
        elif jj < 2 * half:
            e[jj - half, j] = 1.0
            e[half + jj - half, 2 * LANES + j] = 1.0
    return jnp.asarray(np.concatenate([e, e], axis=0), dtype=bf16)


IN_QK0, IN_VG0, IN_SQ0, IN_SKV0, IN_A0, IN_END = 0, 512, 1536, 2048, 2304, 2432
INPROJ_SUBS = 4
OUTPROJ_SUBS = 8


def _inproj_kernel(x_ref, mod_ref, g_ref, w_ref, tab_ref, e_ref, wa_hi_ref, wa_lo_ref, ba_ref,
                   qk_ref, vg_ref, la_ref, sq_ref, skv_ref):
    tm = x_ref.shape[0]
    sub = tm // INPROJ_SUBS
    groups = [slice(s * sub, (s + 1) * sub) for s in range(INPROJ_SUBS)]
    hb = []
    for rows in groups:
        h = (_rmsnorm_rows(x_ref[rows, :]) * g_ref[...]) * (1.0 + mod_ref[1:2, :]) + mod_ref[0:1, :]
        hb.append(h.astype(bf16))

    def proj(s, lo, hi):
        return jnp.dot(hb[s], w_ref[:, lo:hi], preferred_element_type=f32)

    a_low = [_split_bf16(proj(s, IN_A0, IN_END)) for s in range(INPROJ_SUBS)]

    for s, rows in enumerate(groups):
        qk = proj(s, IN_QK0, IN_VG0)
        lane = lax.broadcasted_iota(jnp.int32, qk.shape, 1)
        qk_ref[rows, :] = jnp.where(lane < GLA_QK, qk * (GLA_DK ** -0.5), qk).astype(bf16)

    log_a = []
    for al_hi, al_lo in a_low:
        z = (jnp.dot(al_hi, wa_hi_ref[...], preferred_element_type=f32)
             + jnp.dot(al_lo, wa_hi_ref[...], preferred_element_type=f32)
             + jnp.dot(al_hi, wa_lo_ref[...], preferred_element_type=f32)) + ba_ref[...]
        log_a.append(_split_bf16((jnp.minimum(z, 0.0) - jnp.log1p(jnp.exp(-jnp.abs(z)))) * (1.0 / GLA_GATE_NORM)))

    for s, rows in enumerate(groups):
        vg_ref[rows, :] = proj(s, IN_VG0, IN_SQ0).astype(bf16)

    c_r = lax.broadcasted_iota(jnp.int32, (sub, sub), 0)
    c_c = lax.broadcasted_iota(jnp.int32, (sub, sub), 1)
    tri = jnp.where((c_r >= c_c) & ((c_r ^ c_c) < GLA_CHUNK), 1.0, 0.0).astype(bf16)
    for rows, (la_hi, la_lo) in zip(groups, log_a):
        la_ref[rows, :] = (jnp.dot(tri, la_hi, preferred_element_type=f32)
                           + jnp.dot(tri, la_lo, preferred_element_type=f32))

    lane1 = lax.broadcasted_iota(jnp.int32, (1, LANES), 1)
    tables = []
    for rows in groups:
        tabs = lax.dot_general(_split_stack_bf16(tab_ref[:, rows]), e_ref[...], TN, preferred_element_type=f32)
        tables.append((tabs[:, 0:LANES] + jnp.where((lane1 & (SWA_HD - 1)) < ROPE_DIMS, 0.0, 1.0),
                       tabs[:, LANES:2 * LANES], tabs[:, 2 * LANES:3 * LANES]))

    def rope(s, v):
        cos_t, sa_t, sb_t = tables[s]
        return (v * cos_t + pltpu.roll(v, LANES - ROPE_DIMS // 2, 1) * sa_t
                + pltpu.roll(v, ROPE_DIMS // 2, 1) * sb_t)

    for s, rows in enumerate(groups):
        sq = proj(s, IN_SQ0, IN_SKV0)
        for p in range(SWA_W // LANES):
            cols = slice(p * LANES, (p + 1) * LANES)
            sq_ref[rows, cols] = (rope(s, sq[:, cols]) * (SWA_HD ** -0.5)).astype(bf16)
    for s, rows in enumerate(groups):
        skv = proj(s, IN_SKV0, IN_A0)
        first = lax.broadcasted_iota(jnp.int32, (sub, LANES), 1) < SWA_HD
        for part, val in enumerate((rope(s, skv[:, 0:LANES]), skv[:, LANES:2 * LANES])):
            swapped = pltpu.roll(val, SWA_HD, 1)
            skv_ref[rows, (2 * part) * LANES:(2 * part + 1) * LANES] = jnp.where(first, val, swapped).astype(bf16)
            skv_ref[rows, (2 * part + 1) * LANES:(2 * part + 2) * LANES] = jnp.where(first, swapped, val).astype(bf16)


def _inproj(x2, mod3, g, w, tab, e_mat, wa_hi, wa_lo, ba, seq, tm):
    t, d = x2.shape
    assert (tm // INPROJ_SUBS) % LANES == 0, "row groups are also lane groups of the rotary table block"
    per_b = seq // tm
    row = lambda i: (i, 0)
    const = lambda i: (0, 0)
    return pl.pallas_call(
        _inproj_kernel,
        grid=(t // tm,),
        in_specs=[pl.BlockSpec((tm, d), row),
                  pl.BlockSpec((None, 6, d), lambda i: (i // per_b, 0, 0)),
                  pl.BlockSpec((1, d), const),
                  pl.BlockSpec(w.shape, const),
                  pl.BlockSpec((None, ROPE_DIMS, tm), lambda i: (i // per_b, 0, i % per_b)),
                  pl.BlockSpec(e_mat.shape, const),
                  pl.BlockSpec(wa_hi.shape, const),
                  pl.BlockSpec(wa_lo.shape, const),
                  pl.BlockSpec(ba.shape, const)],
        out_specs=[pl.BlockSpec((tm, 2 * GLA_QK), row),
                   pl.BlockSpec((tm, 2 * GLA_W), row),
                   pl.BlockSpec((tm, GLA_QK), row),
                   pl.BlockSpec((tm, SWA_W), row),
                   pl.BlockSpec((tm, 4 * SWA_KV), row)],
        out_shape=[jax.ShapeDtypeStruct((t, 2 * GLA_QK), bf16),
                   jax.ShapeDtypeStruct((t, 2 * GLA_W), bf16),
                   jax.ShapeDtypeStruct((t, GLA_QK), f32),
                   jax.ShapeDtypeStruct((t, SWA_W), bf16),
                   jax.ShapeDtypeStruct((t, 4 * SWA_KV), bf16)],
        compiler_params=_cparams(1),
        name="inproj",
    )(x2, mod3, g, w, tab, e_mat, wa_hi, wa_lo, ba)


def _gla_chunk(c, qk_ref, vg_ref, la_ref, gn_ref, o_ref, st_ref, consts):
    causal, ones, first = consts
    c_len = GLA_CHUNK
    rows = slice(c * c_len, (c + 1) * c_len)
    b = la_ref[rows, :]
    b_last = b[c_len - 1:c_len, :]
    p1 = b_last.astype(bf16).astype(f32)
    p2 = (b_last - p1).astype(bf16).astype(f32)
    r_i = lax.broadcasted_iota(jnp.int32, (ones.shape[0], GLA_QK), 0)
    stack = jnp.where(r_i == 0, p1, jnp.where(r_i == 1, p2, jnp.where(r_i == 2, b_last - p1 - p2, 0.0)))
    b_tot_t = lax.dot_general(stack.astype(bf16), ones, TN, preferred_element_type=f32)
    q = qk_ref[rows, 0:GLA_QK].astype(f32)
    k = qk_ref[rows, GLA_QK:2 * GLA_QK].astype(f32)
    q_dec = q * jnp.exp(b)
    k_dec = (k * jnp.exp(-b)).astype(bf16)
    k_rem = (k * jnp.exp(b_last - b)).astype(bf16)
    decay = jnp.exp(b_tot_t)
    for p in range(GLA_HEADS // 2):
        ls = slice(p * LANES, (p + 1) * LANES)
        s_prev = st_ref[ls, :]
        s_prev_b = s_prev.astype(bf16)
        qd = q_dec[:, ls]
        kv_halves = []
        for hh in range(2):
            h = 2 * p + hh
            vs = slice(h * GLA_DV, (h + 1) * GLA_DV)
            qm = jnp.where(first if hh == 0 else ~first, qd, 0.0).astype(bf16)
            v = vg_ref[rows, vs]
            scores = lax.dot_general(qm, k_dec[:, ls], NT, preferred_element_type=f32)
            scores = jnp.where(causal, scores, 0.0).astype(bf16)
            o = (jnp.dot(scores, v, preferred_element_type=f32)
                 + jnp.dot(qm, s_prev_b, preferred_element_type=f32))
            o = _rmsnorm_rows(o) * gn_ref[:, vs]
            gate = vg_ref[rows, GLA_W + h * GLA_DV:GLA_W + (h + 1) * GLA_DV].astype(f32)
            o_ref[rows, vs] = (o * _silu(gate)).astype(bf16)
            kv = lax.dot_general(k_rem[:, ls], v, TN, preferred_element_type=f32)
            kv_halves.append(kv[hh * GLA_DK:(hh + 1) * GLA_DK, :])
        st_ref[ls, :] = decay[ls, :] * s_prev + jnp.concatenate(kv_halves, axis=0)


def _swa_block(blk, j, sink_ref, q_ref, kvp_ref, kvc_ref, o_ref, consts):
    band, s_i, first = consts
    w = WINDOW
    rows = slice(blk * w, (blk + 1) * w)
    prev = kvp_ref if blk == 0 else kvc_ref.at[pl.ds((blk - 1) * w, w)]
    k2 = [jnp.concatenate([prev[:, g * LANES:(g + 1) * LANES], kvc_ref[rows, g * LANES:(g + 1) * LANES]], axis=0)
          for g in range(SWA_KV_HEADS)]
    v2 = [jnp.concatenate([prev[:, (SWA_KV_HEADS + g) * LANES:(SWA_KV_HEADS + g + 1) * LANES],
                           kvc_ref[rows, (SWA_KV_HEADS + g) * LANES:(SWA_KV_HEADS + g + 1) * LANES]], axis=0)
          for g in range(SWA_KV_HEADS)]
    valid = band & ((j > 0) | (s_i >= w)) if blk == 0 else band
    heads = range(SWA_HEADS)
    kv_of = [h // (SWA_HEADS // SWA_KV_HEADS) for h in heads]
    scores = []
    for h in heads:
        p, hh = divmod(h, 2)
        qp = q_ref[rows, p * LANES:(p + 1) * LANES].astype(f32)
        qm = jnp.where(first if hh == 0 else ~first, qp, 0.0).astype(bf16)
        s = lax.dot_general(qm, k2[kv_of[h]], NT, preferred_element_type=f32)
        scores.append(jnp.where(valid, s, -jnp.inf))
    probs, denoms = [], []
    for h in heads:
        sink = sink_ref[0, h]
        m = jnp.maximum(jnp.max(scores[h], axis=-1, keepdims=True), sink)
        pr = jnp.exp(scores[h] - m)
        denoms.append(jnp.sum(pr, axis=-1, keepdims=True) + jnp.exp(sink - m))
        probs.append(pr.astype(bf16))
    outs = [jnp.dot(probs[h], v2[kv_of[h]], preferred_element_type=f32) / denoms[h] for h in heads]
    for p in range(SWA_W // LANES):
        o_ref[rows, p * LANES:(p + 1) * LANES] = jnp.where(first, outs[2 * p], outs[2 * p + 1]).astype(bf16)


def _gla_kernel(qk_ref, vg_ref, la_ref, gn_ref, o_ref, st_ref):
    @pl.when(pl.program_id(1) == 0)
    def _():
        st_ref[...] = jnp.zeros_like(st_ref)

    c_len = GLA_CHUNK
    causal = lax.broadcasted_iota(jnp.int32, (c_len, c_len), 0) >= lax.broadcasted_iota(jnp.int32, (c_len, c_len), 1)
    consts = (causal, jnp.ones((BF16_SUBLANES, LANES), bf16),
              lax.broadcasted_iota(jnp.int32, (c_len, LANES), 1) < GLA_DK)
    for c in range(qk_ref.shape[0] // c_len):
        _gla_chunk(c, qk_ref, vg_ref, la_ref, gn_ref, o_ref, st_ref, consts)


def _swa_kernel(sink_ref, q_ref, kvp_ref, kvc_ref, wg_ref, wu_ref, wd_ref, o_ref, wg_out, wu_out, wd_out):
    wg_out[...] = wg_ref[...].astype(bf16)
    wu_out[...] = wu_ref[...].astype(bf16)
    wd_out[...] = wd_ref[...].astype(bf16)
    w = WINDOW
    s_i = lax.broadcasted_iota(jnp.int32, (w, 2 * w), 1)
    rel = lax.broadcasted_iota(jnp.int32, (w, 2 * w), 0) + w - s_i
    consts = ((rel >= 0) & (rel < w), s_i, lax.broadcasted_iota(jnp.int32, (w, LANES), 1) < SWA_HD)
    for blk in range(q_ref.shape[0] // w):
        _swa_block(blk, pl.program_id(1), sink_ref, q_ref, kvp_ref, kvc_ref, o_ref, consts)


def _mixers(sinks, qk, vg, la, gn, sq, skv, wg, wu, wd, bsz, seq, ts_gla, ts):
    t = qk.shape[0]
    ng = seq // ts_gla
    tile_g = lambda b, j: (b * ng + j, 0)
    o_gla = pl.pallas_call(
        _gla_kernel,
        grid=(bsz, ng),
        in_specs=[pl.BlockSpec((ts_gla, 2 * GLA_QK), tile_g),
                  pl.BlockSpec((ts_gla, 2 * GLA_W), tile_g),
                  pl.BlockSpec((ts_gla, GLA_QK), tile_g),
                  pl.BlockSpec((1, GLA_W), lambda b, j: (0, 0))],
        out_specs=pl.BlockSpec((ts_gla, GLA_W), tile_g),
        out_shape=jax.ShapeDtypeStruct((t, GLA_W), bf16),
        scratch_shapes=[pltpu.VMEM((GLA_HEADS * GLA_DK, GLA_DV), f32)],
        compiler_params=_cparams(2),
        name="gla",
    )(qk, vg, la, gn)
    ns = seq // ts
    blocks = ts // WINDOW
    tile = lambda b, j: (b * ns + j, 0)
    n_exp = wg.shape[0] * wg.shape[1]
    assert n_exp % (bsz * ns) == 0
    per = n_exp // (bsz * ns)
    flat = [w.reshape((n_exp,) + w.shape[2:]) for w in (wg, wu, wd)]
    w_spec = lambda w: pl.BlockSpec((per,) + w.shape[1:], lambda b, j: (b * ns + j, 0, 0))
    o_swa, *w_bf16 = pl.pallas_call(
        _swa_kernel,
        grid=(bsz, ns),
        in_specs=[pl.BlockSpec(memory_space=pltpu.SMEM),
                  pl.BlockSpec((ts, SWA_W), tile),
                  pl.BlockSpec((WINDOW, 4 * SWA_KV), lambda b, j: ((b * ns + j) * blocks - jnp.minimum(j, 1), 0)),
                  pl.BlockSpec((ts, 4 * SWA_KV), tile)] + [w_spec(w) for w in flat],
        out_specs=[pl.BlockSpec((ts, SWA_W), tile)] + [w_spec(w) for w in flat],
        out_shape=[jax.ShapeDtypeStruct((t, SWA_W), bf16)] + [jax.ShapeDtypeStruct(w.shape, bf16) for w in flat],
        compiler_params=_cparams(2),
        name="swa",
    )(sinks, sq, skv, skv, *flat)
    return (o_gla, o_swa) + tuple(o.reshape(w.shape) for o, w in zip(w_bf16, (wg, wu, wd)))


def _store_slabs(ref, v):
    tm, n = v.shape[0], v.shape[1] // LANES
    for c in range(n):
        ref[pl.ds(c, tm, stride=n), :] = v[:, c * LANES:(c + 1) * LANES]


def _load_slabs(ref, tm):
    n = ref.shape[0] // tm
    return jnp.concatenate([ref[pl.ds(c, tm, stride=n), :] for c in range(n)], axis=1)


def _outproj_kernel(og_ref, os_ref, x_ref, mod_ref, wo_ref, g2_ref, wr_hi_ref, wr_lo_ref, br_ref,
                    x1_ref, hrow_ref, route_ref, cnt_ref, run_ref):
    @pl.when(pl.program_id(0) == 0)
    def _():
        run_ref[...] = jnp.zeros_like(run_ref)

    tm = x_ref.shape[0]
    sub = tm // OUTPROJ_SUBS
    n = hrow_ref.shape[0] // tm
    groups = [slice(s * sub, (s + 1) * sub) for s in range(OUTPROJ_SUBS)]
    hs = []
    for rows in groups:
        mix = (jnp.dot(og_ref[rows, :], wo_ref[0:GLA_W, :], preferred_element_type=f32)
               + jnp.dot(os_ref[rows, :], wo_ref[GLA_W:GLA_W + SWA_W, :], preferred_element_type=f32))
        x1 = x_ref[rows, :] + mod_ref[2:3, :] * mix
        x1_ref[rows, :] = x1
        hs.append((_rmsnorm_rows(x1) * g2_ref[...]) * (1.0 + mod_ref[4:5, :]) + mod_ref[3:4, :])

    lts = []
    for s, h in enumerate(hs):
        _store_slabs(hrow_ref.at[pl.ds(s * sub * n, sub * n)], h)
        h_hi, h_lo = _split_bf16(h)
        lts.append((lax.dot_general(wr_hi_ref[...], h_hi, NT, preferred_element_type=f32)
                    + lax.dot_general(wr_hi_ref[...], h_lo, NT, preferred_element_type=f32)
                    + lax.dot_general(wr_lo_ref[...], h_hi, NT, preferred_element_type=f32)) + br_ref[...])

    e_g = EXPERTS_PER_GROUP
    row = lax.broadcasted_iota(jnp.int32, (e_g, sub), 0)
    neg = -jnp.inf
    routed = []
    for lt in lts:
        gl = jnp.where(row < N_GROUPS, lt[0:e_g, :], neg)
        g_max = jnp.max(gl, axis=0, keepdims=True)
        g_gate = 1.0 / jnp.sum(jnp.exp(gl - g_max), axis=0, keepdims=True)
        g_idx = jnp.min(jnp.where(gl == g_max, row, e_g), axis=0, keepdims=True)
        sel = lt[e_g * N_GROUPS:e_g * (N_GROUPS + 1), :]
        for g in range(N_GROUPS - 2, -1, -1):
            sel = jnp.where(g_idx == g, lt[e_g * (g + 1):e_g * (g + 2), :], sel)
        t1 = jnp.max(sel, axis=0, keepdims=True)
        i1 = jnp.min(jnp.where(sel == t1, row, e_g), axis=0, keepdims=True)
        sel2 = jnp.where(row == i1, neg, sel)
        t2 = jnp.max(sel2, axis=0, keepdims=True)
        i2 = jnp.min(jnp.where(sel2 == t2, row, e_g), axis=0, keepdims=True)
        ex = jnp.exp(t2 - t1)
        w1 = g_gate / (1.0 + ex)
        w2 = g_gate * ex / (1.0 + ex)
        first_lo = i1 < i2
        bucket = ((g_idx << (2 * EXPERT_BITS)) | (jnp.minimum(i1, i2) << EXPERT_BITS) | jnp.maximum(i1, i2))
        routed.append((bucket, jnp.where(first_lo, w1, w2), jnp.where(first_lo, w2, w1)))

    t_r = lax.broadcasted_iota(jnp.int32, (sub, sub), 0)
    t_c = lax.broadcasted_iota(jnp.int32, (sub, sub), 1)
    earlier = jnp.where(t_r < t_c, 1.0, 0.0).astype(bf16)
    ones = jnp.ones((sub, sub), bf16)
    run = run_ref[...]
    for rows, (bucket, w_lo, w_hi) in zip(groups, routed):
        onehot = lax.broadcasted_iota(jnp.int32, (N_BUCKETS, sub), 0) == bucket
        oh_b = jnp.where(onehot, 1.0, 0.0).astype(bf16)
        prefix = jnp.dot(oh_b, earlier, preferred_element_type=f32) + run
        rank = jnp.sum(jnp.where(onehot, prefix, 0.0), axis=0, keepdims=True)
        run = run + jnp.dot(oh_b, ones, preferred_element_type=f32)
        route_ref[:, rows] = jnp.where(row == 0, bucket.astype(f32), jnp.where(row == 1, rank, jnp.where(
            row == 2, w_lo, jnp.where(row == 3, w_hi, 0.0))))
    run_ref[...] = run
    cnt_ref[...] = run[:, 0:LANES]


def _outproj(og, osw, x2, mod3, wo, g2, wr_hi, wr_lo, br, seq, tm):
    t, d = x2.shape
    assert (tm // OUTPROJ_SUBS) % LANES == 0, "row groups are also lane groups of the routing table"
    per_b = seq // tm
    row = lambda i: (i, 0)
    const = lambda i: (0, 0)
    return pl.pallas_call(
        _outproj_kernel,
        grid=(t // tm,),
        in_specs=[pl.BlockSpec((tm, GLA_W), row),
                  pl.BlockSpec((tm, SWA_W), row),
                  pl.BlockSpec((tm, d), row),
                  pl.BlockSpec((None, 6, d), lambda i: (i // per_b, 0, 0)),
                  pl.BlockSpec(wo.shape, const),
                  pl.BlockSpec((1, d), const),
                  pl.BlockSpec(wr_hi.shape, const),
                  pl.BlockSpec(wr_lo.shape, const),
                  pl.BlockSpec(br.shape, const)],
        out_specs=[pl.BlockSpec((tm, d), row),
                   pl.BlockSpec((tm * d // LANES, LANES), row),
                   pl.BlockSpec((ROUTE_ROWS, tm), lambda i: (0, i)),
                   pl.BlockSpec((N_BUCKETS, LANES), const)],
        out_shape=[jax.ShapeDtypeStruct((t, d), f32),
                   jax.ShapeDtypeStruct((t * d // LANES, LANES), f32),
                   jax.ShapeDtypeStruct((ROUTE_ROWS, t), f32),
                   jax.ShapeDtypeStruct((N_BUCKETS, LANES), f32)],
        scratch_shapes=[pltpu.VMEM((N_BUCKETS, tm // OUTPROJ_SUBS), f32)],
        compiler_params=_cparams(1),
        name="outproj",
    )(og, osw, x2, mod3, wo, g2, wr_hi, wr_lo, br)


def _plan_kernel(cnt_ref, route_ref, dest_ref, tb_ref, own_ref, nt_ref):
    nb = N_BUCKETS
    tiles = jnp.floor((cnt_ref[...] + (MOE_TM - 1)) * (1.0 / MOE_TM))
    b_r = lax.broadcasted_iota(jnp.int32, (nb, nb), 0)
    b_c = lax.broadcasted_iota(jnp.int32, (nb, nb), 1)
    before = jnp.where(b_c < b_r, 1.0, 0.0).astype(bf16)
    tiles_b = tiles.astype(bf16)
    g_r = b_r >> (2 * EXPERT_BITS)
    g_c = b_c >> (2 * EXPERT_BITS)
    g_tiles = jnp.dot(jnp.where(g_r == g_c, 1.0, 0.0).astype(bf16), tiles_b, preferred_element_type=f32)
    fill = MOE_PAIR * jnp.floor((g_tiles + (MOE_PAIR - 1)) * (1.0 / MOE_PAIR)) - g_tiles
    pads = jnp.dot(jnp.where(g_c < g_r, 1.0, 0.0).astype(bf16), fill.astype(bf16), preferred_element_type=f32)
    t_start = (jnp.dot(before, tiles_b, preferred_element_type=f32)
               + pads * (1.0 / (1 << (2 * EXPERT_BITS))))
    t_end = t_start + tiles
    tile_i = lax.broadcasted_iota(jnp.int32, (nb, 2 * LANES), 1).astype(f32)
    ended = jnp.where(jnp.concatenate([t_end, t_end], axis=1) <= tile_i, 1.0, 0.0)
    tb_ref[...] = jnp.sum(ended, axis=0, keepdims=True).astype(jnp.int32)
    started = jnp.where(jnp.concatenate([t_start, t_start], axis=1) <= tile_i, 1.0, 0.0)
    own_ref[...] = jnp.sum(started - ended, axis=0, keepdims=True).astype(jnp.int32)
    nt_ref[...] = t_end[nb - 1:nb, :].astype(jnp.int32)
    tl = route_ref.shape[1]
    onehot = lax.broadcasted_iota(jnp.int32, (nb, tl), 0) == route_ref[0:1, :].astype(jnp.int32)
    start = lax.dot_general(t_start.astype(bf16), jnp.where(onehot, 1.0, 0.0).astype(bf16), TN,
                            preferred_element_type=f32)
    dest_ref[...] = (start[0:1, :] * MOE_TM + route_ref[1:2, :]).astype(jnp.int32)


def _plan(cnt, route, tl):
    t = route.shape[1]
    return pl.pallas_call(
        _plan_kernel,
        grid=(t // tl,),
        in_specs=[pl.BlockSpec(cnt.shape, lambda i: (0, 0)),
                  pl.BlockSpec((ROUTE_ROWS, tl), lambda i: (0, i))],
        out_specs=[pl.BlockSpec((1, tl), lambda i: (0, i)),
                   pl.BlockSpec((1, 2 * LANES), lambda i: (0, 0)),
                   pl.BlockSpec((1, 2 * LANES), lambda i: (0, 0)),
                   pl.BlockSpec((1, LANES), lambda i: (0, 0))],
        out_shape=[jax.ShapeDtypeStruct((1, t), jnp.int32),
                   jax.ShapeDtypeStruct((1, 2 * LANES), jnp.int32),
                   jax.ShapeDtypeStruct((1, 2 * LANES), jnp.int32),
                   jax.ShapeDtypeStruct((1, LANES), jnp.int32)],
        compiler_params=_cparams(1),
        name="plan",
    )(cnt, route)


def _slab(ref, token, n):
    return ref.at[pl.ds(pl.multiple_of(token * n, n), n)]


def _dispatch_kernel(dest_ref, tb_ref, own_ref, src_ref, out_ref, zero_ref, sem_z, sem_r, *, rows, n_tiles_max):
    i = pl.program_id(0)
    n = src_ref.shape[0] // rows
    tile_rows = zero_ref.shape[0]

    def last_tile(k):
        return (own_ref[k] == 0) | (tb_ref[k] != tb_ref[k + 1])

    def zero_copy(k):
        return pltpu.make_async_copy(zero_ref, out_ref.at[pl.ds(pl.multiple_of(k * tile_rows, tile_rows), tile_rows)],
                                     sem_z)

    @pl.when(i == 0)
    def _():
        zero_ref[...] = jnp.zeros_like(zero_ref)

        def start(k, c):
            @pl.when(last_tile(k))
            def _():
                zero_copy(k).start()
            return c

        def wait(k, c):
            @pl.when(last_tile(k))
            def _():
                zero_copy(k).wait()
            return c

        lax.fori_loop(0, n_tiles_max, start, 0)
        lax.fori_loop(0, n_tiles_max, wait, 0)

    base = i * rows

    def row_copy(r):
        return pltpu.make_async_copy(_slab(src_ref, r, n), _slab(out_ref, dest_ref[base + r], n), sem_r)

    def issue(r8, c):
        for j in range(ROW_DMA_UNROLL):
            row_copy(r8 * ROW_DMA_UNROLL + j).start(priority=j % 2)
        return c

    def drain(r8, c):
        for j in range(ROW_DMA_UNROLL):
            row_copy(r8 * ROW_DMA_UNROLL + j).wait()
        return c

    lax.fori_loop(0, rows // ROW_DMA_UNROLL, issue, 0)
    lax.fori_loop(0, rows // ROW_DMA_UNROLL, drain, 0)


def _dispatch(dest, tb, own, hrow, n_tiles_max, rows):
    t = dest.shape[0]
    n = hrow.shape[0] // t
    grid_spec = pltpu.PrefetchScalarGridSpec(
        num_scalar_prefetch=3, grid=(t // rows,),
        in_specs=[pl.BlockSpec((rows * n, LANES), lambda i, dest, tb, own: (i, 0))],
        out_specs=pl.BlockSpec(memory_space=pl.ANY),
        scratch_shapes=[pltpu.VMEM((MOE_TM * n, LANES), hrow.dtype), pltpu.SemaphoreType.DMA(()),
                        pltpu.SemaphoreType.DMA(())])
    return pl.pallas_call(
        functools.partial(_dispatch_kernel, rows=rows, n_tiles_max=n_tiles_max),
        grid_spec=grid_spec,
        out_shape=jax.ShapeDtypeStruct((n_tiles_max * MOE_TM * n, LANES), hrow.dtype),
        compiler_params=_cparams(1),
        name="dispatch",
    )(dest, tb, own, hrow)


PAIR_BITS = 2
MOE_PAIR = 1 << PAIR_BITS


def _moe_kernel(tb_ref, nt_ref, hs_ref, wr_ref, br_ref, wg_ref, wu_ref, wd_ref, y_ref):
    j = pl.program_id(0)
    rows = hs_ref.shape[0] // MOE_PAIR

    @pl.when(j * MOE_PAIR < nt_ref[0])
    def _():
        e_g = EXPERTS_PER_GROUP
        tiles = range(MOE_PAIR)
        bucket = [tb_ref[j * MOE_PAIR + k] for k in tiles]
        group = [b >> (2 * EXPERT_BITS) for b in bucket]
        experts = [((b >> EXPERT_BITS) & (e_g - 1), b & (e_g - 1)) for b in bucket]
        h = [_load_slabs(hs_ref.at[pl.ds(k * rows, rows)], MOE_TM).astype(bf16) for k in tiles]
        gate_up = [[(jnp.dot(h[k], wg_ref[e], preferred_element_type=f32),
                     jnp.dot(h[k], wu_ref[e], preferred_element_type=f32)) for e in experts[k]] for k in tiles]
        weights = []
        for k in tiles:
            logits = jnp.dot(h[k], wr_ref[...], preferred_element_type=f32) + br_ref[...]
            lane = lax.broadcasted_iota(jnp.int32, logits.shape, 1)

            def pick(col):
                return jnp.sum(jnp.where(lane == col, logits, 0.0), axis=1, keepdims=True)

            gl = jnp.where(lane < N_GROUPS, logits, -jnp.inf)
            g_max = jnp.max(gl, axis=1, keepdims=True)
            g_gate = jnp.exp(pick(group[k]) - g_max) / jnp.sum(jnp.exp(gl - g_max), axis=1, keepdims=True)
            sel = [pick(e_g * (group[k] + 1) + e) for e in experts[k]]
            top = jnp.maximum(sel[0], sel[1])
            p = [jnp.exp(s - top) for s in sel]
            scale = g_gate / (p[0] + p[1])
            weights.append([scale * p[0], scale * p[1]])
        ys = []
        for k in tiles:
            y = None
            for m, e in enumerate(experts[k]):
                a, u = gate_up[k][m]
                yk = weights[k][m] * jnp.dot((_silu(a) * u).astype(bf16), wd_ref[e], preferred_element_type=f32)
                y = yk if y is None else y + yk
            ys.append(y)
        for k in tiles:
            _store_slabs(y_ref.at[pl.ds(k * rows, rows)], ys[k])

    @pl.when(j * MOE_PAIR >= nt_ref[0])
    def _():
        y_ref[...] = jnp.zeros_like(y_ref)


def _moe(tb, nt, hs, wr, br, wg, wu, wd):
    e_g = EXPERTS_PER_GROUP
    d = wg.shape[2]
    n = d // LANES
    step_rows = MOE_PAIR * MOE_TM * n
    n_steps = hs.shape[0] // step_rows
    last = lambda j, tb, nt: jnp.minimum(j, lax.shift_right_logical(nt[0] - 1, PAIR_BITS))
    group = lambda j, tb, nt: (tb[last(j, tb, nt) * MOE_PAIR] >> (2 * EXPERT_BITS), 0, 0, 0)
    const = lambda j, tb, nt: (0, 0)
    grid_spec = pltpu.PrefetchScalarGridSpec(
        num_scalar_prefetch=2, grid=(n_steps,),
        in_specs=[pl.BlockSpec((step_rows, LANES), lambda j, tb, nt: (last(j, tb, nt), 0)),
                  pl.BlockSpec(wr.shape, const),
                  pl.BlockSpec(br.shape, const),
                  pl.BlockSpec((None, e_g, d, D_EXPERT), group),
                  pl.BlockSpec((None, e_g, d, D_EXPERT), group),
                  pl.BlockSpec((None, e_g, D_EXPERT, d), group)],
        out_specs=pl.BlockSpec((step_rows, LANES), lambda j, tb, nt: (j, 0)))
    return pl.pallas_call(
        _moe_kernel,
        grid_spec=grid_spec,
        out_shape=jax.ShapeDtypeStruct(hs.shape, f32),
        compiler_params=_cparams(1),
        name="moe",
    )(tb, nt, hs, wr, br, wg, wu, wd)


def _final_kernel(dest_ref, x1_ref, mod_ref, fg_ref, ys_ref, o_ref, ybuf_ref, sem):
    i = pl.program_id(0)
    tm = x1_ref.shape[0]
    n = ybuf_ref.shape[1] // tm

    def row_copy(tile, slot, r):
        return pltpu.make_async_copy(_slab(ys_ref, dest_ref[tile * tm + r], n), _slab(ybuf_ref.at[slot], r, n),
                                     sem.at[slot])

    def gather(tile, slot):
        def issue(r8, c):
            for j in range(ROW_DMA_UNROLL):
                row_copy(tile, slot, r8 * ROW_DMA_UNROLL + j).start(priority=j % 2)
            return c
        lax.fori_loop(0, tm // ROW_DMA_UNROLL, issue, 0)

    @pl.when(i == 0)
    def _():
        gather(0, 0)

    @pl.when(i + 1 < pl.num_programs(0))
    def _():
        gather(i + 1, (i + 1) % 2)

    slot = i % 2

    def drain(r8, c):
        for j in range(ROW_DMA_UNROLL):
            row_copy(i, slot, r8 * ROW_DMA_UNROLL + j).wait()
        return c

    lax.fori_loop(0, tm // ROW_DMA_UNROLL, drain, 0)

    x2 = x1_ref[...] + mod_ref[5:6, :] * _load_slabs(ybuf_ref.at[slot], tm)
    o_ref[...] = _rmsnorm_rows(x2) * fg_ref[...]


def _final(dest, x1, mod3, fg, ys, seq, tm):
    t, d = x1.shape
    per_b = seq // tm
    grid_spec = pltpu.PrefetchScalarGridSpec(
        num_scalar_prefetch=1, grid=(t // tm,),
        in_specs=[pl.BlockSpec((tm, d), lambda i, dest: (i, 0)),
                  pl.BlockSpec((None, 6, d), lambda i, dest: (i // per_b, 0, 0)),
                  pl.BlockSpec((1, d), lambda i, dest: (0, 0)),
                  pl.BlockSpec(memory_space=pl.ANY)],
        out_specs=pl.BlockSpec((tm, d), lambda i, dest: (i, 0)),
        scratch_shapes=[pltpu.VMEM((2, tm * d // LANES, LANES), ys.dtype), pltpu.SemaphoreType.DMA((2,))])
    return pl.pallas_call(
        _final_kernel,
        grid_spec=grid_spec,
        out_shape=jax.ShapeDtypeStruct((t, d), f32),
        compiler_params=_cparams(1),
        name="final",
    )(dest, x1, mod3, fg, ys)


def _prep_w_in(w_in_l):
    sizes = (GLA_QK, GLA_QK, GLA_W, GLA_W, GLA_LOWRANK, SWA_W, SWA_KV, SWA_KV)
    off = [int(o) for o in np.concatenate([[0], np.cumsum(sizes)])]
    pad = jnp.zeros((w_in_l.shape[0], IN_END - off[8]), w_in_l.dtype)
    return jnp.concatenate([w_in_l[:, off[0]:off[4]], w_in_l[:, off[5]:off[8]], w_in_l[:, off[4]:off[5]], pad],
                           axis=1).astype(bf16)


def _prep_w_out(w_out_l):
    return w_out_l.astype(bf16)


def _prep_router(w_grp_l, b_grp_l, w_exp_l, b_exp_l):
    d = w_grp_l.shape[0]
    n_e = N_GROUPS * EXPERTS_PER_GROUP
    gap, tail = EXPERTS_PER_GROUP - N_GROUPS, ROUTER_ROWS - EXPERTS_PER_GROUP - n_e
    wr = jnp.concatenate([w_grp_l.T, jnp.zeros((gap, d), f32), w_exp_l.T, jnp.zeros((tail, d), f32)], axis=0)
    br = jnp.concatenate([b_grp_l, jnp.zeros((gap,), f32), b_exp_l, jnp.zeros((tail,), f32)]).reshape(ROUTER_ROWS, 1)
    wr_hi = wr.astype(bf16)
    wr_lo = (wr - wr_hi.astype(f32)).astype(bf16)
    return wr_hi, wr_lo, br


def _tiles(seq):
    return dict(tm=min(seq, 512), tm_proj=min(seq, 1024), ts_gla=min(seq, 512), ts_swa=min(seq, 1024),
                tl=min(seq, 2048), rows=min(seq, 2048))


def _layer(x2, mod3, tab, e_mat, bsz, seq, norm_mix_g, w_in, gla_w_a2, gla_b_a2, gla_norm_g, swa_sinks, w_out,
           norm_ffn_g, w_grp, b_grp, w_exp, b_exp, w_gate, w_up, w_down, out_g):
    d = x2.shape[1]
    tiles = _tiles(seq)
    tm = tiles["tm"]
    wa = jnp.pad(gla_w_a2, ((0, LANES - GLA_LOWRANK), (0, 0)))
    wa_hi = wa.astype(bf16)
    wa_lo = (wa - wa_hi.astype(f32)).astype(bf16)
    qk, vg, la, sq, skv = _inproj(x2, mod3, norm_mix_g.reshape(1, d), _prep_w_in(w_in), tab, e_mat,
                                  wa_hi, wa_lo, gla_b_a2.reshape(1, GLA_QK), seq, tiles["tm_proj"])
    o_gla, o_swa, wg_b, wu_b, wd_b = _mixers(swa_sinks.reshape(1, SWA_HEADS), qk, vg, la,
                                             gla_norm_g.reshape(1, GLA_W), sq, skv, w_gate, w_up, w_down, bsz, seq,
                                             tiles["ts_gla"], tiles["ts_swa"])
    wr_hi, wr_lo, br = _prep_router(w_grp, b_grp, w_exp, b_exp)
    x1, hrow, route, cnt = _outproj(o_gla, o_swa, x2, mod3, _prep_w_out(w_out), norm_ffn_g.reshape(1, d),
                                    wr_hi, wr_lo, br, seq, tiles["tm_proj"])
    t = x2.shape[0]
    n_tiles_max = -(-(t // MOE_TM + N_PAIRS + (N_GROUPS - 1) * (MOE_PAIR - 1)) // MOE_PAIR) * MOE_PAIR
    assert n_tiles_max < 2 * LANES and t % MOE_TM == 0
    dest, tile_bucket, tile_owned, n_tiles = _plan(cnt, route, tiles["tl"])
    dest = dest.reshape(t)
    tile_bucket = tile_bucket.reshape(2 * LANES)
    n_tiles = n_tiles.reshape(LANES)[0:1]
    hs = _dispatch(dest, tile_bucket, tile_owned.reshape(2 * LANES), hrow, n_tiles_max, tiles["rows"])
    ys = _moe(tile_bucket, n_tiles, hs, wr_hi.T, br.reshape(1, ROUTER_ROWS), wg_b, wu_b, wd_b)
    return _final(dest, x1, mod3, out_g.reshape(1, d), ys, seq, tm)


def kernel(x, c, positions, ada_w, ada_b, norm_mix_g, w_in, gla_w_a2, gla_b_a2, gla_norm_g, swa_sinks, w_out,
           norm_ffn_g, w_grp, b_grp, w_exp, b_exp, w_gate, w_up, w_down, final_norm_g):
    bsz, seq, d = x.shape
    depth = ada_w.shape[0]
    assert depth == 1, "the final rmsnorm is fused into the last layer's combine kernel"
    tab = _rope_tab(positions)
    e_mat = _rope_expand_matrix()
    x2 = x.reshape(bsz * seq, d)
    for l in range(depth):
        mod3 = _adaln(c, ada_w[l], ada_b[l]).reshape(bsz, 6, d)
        x2 = _layer(x2, mod3, tab, e_mat, bsz, seq, norm_mix_g[l], w_in[l], gla_w_a2[l], gla_b_a2[l],
                    gla_norm_g[l], swa_sinks[l], w_out[l], norm_ffn_g[l], w_grp[l], b_grp[l], w_exp[l], b_exp[l],
                    w_gate[l], w_up[l], w_down[l], final_norm_g)
    return x2.reshape(bsz, seq, d)
```
